```python
import math
import jax, jax.numpy as jnp
from jax import lax
import numpy as np

D_MODEL = 1024
BATCH = 16
SEQ = 4096
DEPTH = 4

MEM_LEN = 256
EPS = 1e-6
BRANCH_WIDTH = 512
N_BRANCH = 3

A_WIDTH = BRANCH_WIDTH
A_HEADS = 8
A_HEAD_DIM = A_WIDTH // A_HEADS
A_CONV = 4
RG_C = 8.0

B_HEADS = 4
B_DK = BRANCH_WIDTH // B_HEADS
B_DV = BRANCH_WIDTH // B_HEADS
B_CONV = 4
B_CHUNK = 64

C_WIDTH = BRANCH_WIDTH
C_GROUP = 16
C_GROUPS = C_WIDTH // C_GROUP
C_STATE = 64

X_HEADS = 4
X_HEAD_DIM = D_MODEL // X_HEADS

D_FF = 3 * D_MODEL
FFN_CONV = 3

IN_SPLITS = (A_WIDTH, A_WIDTH,
             B_HEADS * B_DK, B_HEADS * B_DK,
             B_HEADS * B_DV, B_HEADS * B_DV,
             B_HEADS, B_HEADS,
             C_WIDTH,
             N_BRANCH * D_MODEL)
D_IN = sum(IN_SPLITS)

kernel_name = "hybrid_rglru_deltanet_s5_block"


def rmsnorm(x, g):
    xf = x.astype(jnp.float32)
    var = jnp.mean(xf * xf, axis=-1, keepdims=True)
    return (xf * lax.rsqrt(var + EPS) * g.astype(jnp.float32)).astype(x.dtype)


def l2norm(x):
    return x * lax.rsqrt(jnp.sum(x * x, axis=-1, keepdims=True) + EPS)


def causal_dwconv(x, w):
    k, ch = w.shape
    return lax.conv_general_dilated(x, w[:, None, :].astype(x.dtype), (1,), [(k - 1, 0)],
                                    dimension_numbers=('NWC', 'WIO', 'NWC'),
                                    feature_group_count=ch)


def _linear_combine(e1, e2):
    a1, b1 = e1
    a2, b2 = e2
    return a1 * a2, a2 * b1 + b2


def _complex_linear_combine(e1, e2):
    ar1, ai1, br1, bi1 = e1
    ar2, ai2, br2, bi2 = e2
    return (ar1 * ar2 - ai1 * ai2,
            ar1 * ai2 + ai1 * ar2,
            ar2 * br1 - ai2 * bi1 + br2,
            ar2 * bi1 + ai2 * br1 + bi2)


def rg_lru(x, w_r, b_r, w_i, b_i, lam):
    bsz, s, _ = x.shape
    xf = x.astype(jnp.float32)
    xh = xf.reshape(bsz, s, A_HEADS, A_HEAD_DIM)
    r = jax.nn.sigmoid(jnp.einsum('bshi,hij->bshj', xh, w_r.astype(jnp.float32)).reshape(bsz, s, A_WIDTH) + b_r.astype(jnp.float32))
    ig = jax.nn.sigmoid(jnp.einsum('bshi,hij->bshj', xh, w_i.astype(jnp.float32)).reshape(bsz, s, A_WIDTH) + b_i.astype(jnp.float32))
    log_a = -RG_C * r * jax.nn.softplus(-lam.astype(jnp.float32))
    a = jnp.exp(log_a)
    first = (jnp.arange(s) == 0)[None, :, None]
    mult = jnp.where(first, 1.0, jnp.sqrt(-jnp.expm1(2.0 * log_a)))
    b = mult * ig * xf
    _, h = lax.associative_scan(_linear_combine, (a, b), axis=1)
    return h.astype(x.dtype)


def chunk_gated_delta_rule(q, k, v, g, beta):
    bsz, s, h, dk = q.shape
    dv = v.shape[-1]
    c = B_CHUNK
    n = s // c

    def chunks(t):
        return t.reshape(bsz, n, c, h, t.shape[-1]).transpose(0, 3, 1, 2, 4)

    qc = chunks(q * dk ** -0.5)
    kc = chunks(k)
    vc = chunks(v)
    gc = jnp.cumsum(g.reshape(bsz, n, c, h).transpose(0, 3, 1, 2), axis=-1)
    bc = beta.reshape(bsz, n, c, h).transpose(0, 3, 1, 2)
    incl = jnp.tril(jnp.ones((c, c), dtype=bool))
    strict = jnp.tril(jnp.ones((c, c), dtype=bool), k=-1)
    decay = jnp.exp(jnp.where(incl, gc[..., :, None] - gc[..., None, :], -jnp.inf))
    kb = kc * bc[..., None]
    a_mat = jnp.where(strict, jnp.einsum('bhnik,bhnjk->bhnij', kb, kc) * decay, 0.0)
    lower = a_mat + jnp.eye(c, dtype=a_mat.dtype)
    rhs = jnp.concatenate([vc * bc[..., None], kb * jnp.exp(gc)[..., None]], axis=-1)
    sol = lax.linalg.triangular_solve(lower, rhs, left_side=True, lower=True, unit_diagonal=True)
    u, w = sol[..., :dv], sol[..., dv:]
    qk = jnp.where(incl, jnp.einsum('bhnik,bhnjk->bhnij', qc, kc) * decay, 0.0)
    q_dec = qc * jnp.exp(gc)[..., None]
    k_dec = kc * jnp.exp(gc[..., -1:] - gc)[..., None]
    g_tot = jnp.exp(gc[..., -1])
    xs = tuple(jnp.moveaxis(t, 2, 0) for t in (u, w, qk, q_dec, k_dec, g_tot))

    def step(state, inp):
        u_n, w_n, qk_n, qd_n, kd_n, gt_n = inp
        v_new = u_n - jnp.einsum('bhck,bhkv->bhcv', w_n, state)
        o = jnp.einsum('bhck,bhkv->bhcv', qd_n, state) + jnp.einsum('bhij,bhjv->bhiv', qk_n, v_new)
        state = state * gt_n[..., None, None] + jnp.einsum('bhck,bhcv->bhkv', kd_n, v_new)
        return state, o

    s0 = jnp.zeros((bsz, h, dk, dv), jnp.float32)
    _, o = lax.scan(step, s0, xs)
    return o.transpose(1, 0, 3, 2, 4).reshape(bsz, s, h, dv)


def s5_layer(u, lam_re, lam_im, log_dt, b_re, b_im, c_re, c_im, d):
    bsz, s, _ = u.shape
    f32 = jnp.float32
    uf = u.astype(f32)
    ug = uf.reshape(bsz, s, C_GROUPS, C_GROUP)
    lr, li = lam_re.astype(f32), lam_im.astype(f32)
    dt = jnp.exp(log_dt.astype(f32))[:, None]
    mag = jnp.exp(lr * dt)
    ar, ai = mag * jnp.cos(li * dt), mag * jnp.sin(li * dt)
    den = lr * lr + li * li
    fr = ((ar - 1.0) * lr + ai * li) / den
    fi = (ai * lr - (ar - 1.0) * li) / den
    br, bi = b_re.astype(f32), b_im.astype(f32)
    bbr = fr[..., None] * br - fi[..., None] * bi
    bbi = fr[..., None] * bi + fi[..., None] * br
    bu_r = jnp.einsum('bsgc,gpc->bsgp', ug, bbr)
    bu_i = jnp.einsum('bsgc,gpc->bsgp', ug, bbi)
    a_r = jnp.broadcast_to(ar, (1, s, C_GROUPS, C_STATE))
    a_i = jnp.broadcast_to(ai, (1, s, C_GROUPS, C_STATE))
    _, _, hr, hi = lax.associative_scan(_complex_linear_combine, (a_r, a_i, bu_r, bu_i), axis=1)
    y = (jnp.einsum('bsgp,gcp->bsgc', hr, c_re.astype(f32))
         - jnp.einsum('bsgp,gcp->bsgc', hi, c_im.astype(f32))).reshape(bsz, s, C_WIDTH)
    return y + d.astype(f32) * uf


def hybrid_mixer(h, w_in, b_gate, a_conv_w, a_conv_b, a_w_r, a_b_r, a_w_i, a_b_i, a_lam,
                 b_conv_w, b_a_log, b_dt_bias, b_norm,
                 c_lam_re, c_lam_im, c_log_dt, c_b_re, c_b_im, c_c_re, c_c_im, c_d, c_glu_w, c_glu_b,
                 w_branch, w_out):
    bsz, s, _ = h.shape
    f32 = jnp.float32
    offs = np.cumsum(IN_SPLITS)[:-1].tolist()
    xa, ga, q, k, v, z, beta_raw, alpha_raw, uc, gates = jnp.split(h @ w_in, offs, axis=-1)

    ya = rg_lru(causal_dwconv(xa, a_conv_w) + a_conv_b, a_w_r, a_b_r, a_w_i, a_b_i, a_lam) * jax.nn.gelu(ga)

    qkv = jax.nn.silu(causal_dwconv(jnp.concatenate([q, k, v], axis=-1), b_conv_w)).astype(f32)
    q, k, v = jnp.split(qkv, [B_HEADS * B_DK, 2 * B_HEADS * B_DK], axis=-1)
    q = l2norm(q.reshape(bsz, s, B_HEADS, B_DK))
    k = l2norm(k.reshape(bsz, s, B_HEADS, B_DK))
    v = v.reshape(bsz, s, B_HEADS, B_DV)
    beta = jax.nn.sigmoid(beta_raw.astype(f32))
    g = -jnp.exp(b_a_log.astype(f32)) * jax.nn.softplus(alpha_raw.astype(f32) + b_dt_bias.astype(f32))
    o = chunk_gated_delta_rule(q, k, v, g, beta)
    o = rmsnorm(o, b_norm) * jax.nn.silu(z.astype(f32).reshape(bsz, s, B_HEADS, B_DV))
    yb = o.reshape(bsz, s, B_HEADS * B_DV).astype(h.dtype)

    yc = jax.nn.gelu(s5_layer(uc, c_lam_re, c_lam_im, c_log_dt, c_b_re, c_b_im, c_c_re, c_c_im, c_d))
    yc = (yc * jax.nn.sigmoid(yc @ c_glu_w.astype(f32) + c_glu_b.astype(f32))).astype(h.dtype)

    branches = jnp.stack([ya, yb, yc], axis=2)
    proj = jnp.einsum('bskc,kcd->bskd', branches, w_branch)
    gate = jax.nn.sigmoid((gates + b_gate).reshape(bsz, s, N_BRANCH, D_MODEL))
    return jnp.sum(gate * proj, axis=2) @ w_out


def cross_attention(h, mem_n, w_q, w_kv, w_o):
    bsz, s, _ = h.shape
    m = mem_n.shape[1]
    q = (h @ w_q).reshape(bsz, s, X_HEADS, X_HEAD_DIM)
    k, v = jnp.split(mem_n @ w_kv, 2, axis=-1)
    k = k.reshape(bsz, m, X_HEADS, X_HEAD_DIM)
    v = v.reshape(bsz, m, X_HEADS, X_HEAD_DIM)
    sc = jnp.einsum('bshd,bmhd->bhsm', q, k).astype(jnp.float32) * (X_HEAD_DIM ** -0.5)
    p = jax.nn.softmax(sc, axis=-1).astype(v.dtype)
    o = jnp.einsum('bhsm,bmhd->bshd', p, v).reshape(bsz, s, D_MODEL)
    return o @ w_o


def conv_ffn(h, w_up, conv_w, conv_b, w_down):
    u = causal_dwconv(h @ w_up, conv_w) + conv_b
    gate, val = jnp.split(u, 2, axis=-1)
    return (jax.nn.gelu(gate) * val) @ w_down


def _fwd_setup_inputs(seed: int = 0) -> dict:
    key = jax.random.key(seed)
    ks = iter(jax.random.split(key, 64))
    L = DEPTH
    f32 = jnp.float32

    def nrm(shape, scale):
        return jax.random.normal(next(ks), shape, f32) * scale

    def gain(shape):
        return 1.0 + 0.02 * jax.random.normal(next(ks), shape, f32)

    def unif(shape, lo, hi):
        return jax.random.uniform(next(ks), shape, f32, minval=lo, maxval=hi)

    x = nrm((BATCH, SEQ, D_MODEL), 1.0)
    mem = nrm((BATCH, MEM_LEN, D_MODEL), 1.0)
    mix_norm = gain((L, D_MODEL))
    w_in = nrm((L, D_MODEL, D_IN), D_MODEL ** -0.5)
    b_gate = nrm((L, N_BRANCH * D_MODEL), 0.01)
    a_conv_w = nrm((L, A_CONV, A_WIDTH), A_CONV ** -0.5)
    a_conv_b = nrm((L, A_WIDTH), 0.01)
    a_w_r = nrm((L, A_HEADS, A_HEAD_DIM, A_HEAD_DIM), A_HEAD_DIM ** -0.5)
    a_b_r = nrm((L, A_WIDTH), 0.01)
    a_w_i = nrm((L, A_HEADS, A_HEAD_DIM, A_HEAD_DIM), A_HEAD_DIM ** -0.5)
    a_b_i = nrm((L, A_WIDTH), 0.01)
    a_pow = unif((L, A_WIDTH), 0.9, 0.999) ** (1.0 / RG_C)
    a_lam = jnp.log(a_pow) - jnp.log1p(-a_pow)
    b_conv_w = nrm((L, B_CONV, 2 * B_HEADS * B_DK + B_HEADS * B_DV), B_CONV ** -0.5)
    b_a_log = jnp.log(unif((L, B_HEADS), 1.0, 16.0))
    dt = jnp.exp(unif((L, B_HEADS), math.log(0.001), math.log(0.1)))
    b_dt_bias = dt + jnp.log(-jnp.expm1(-dt))
    b_norm = gain((L, B_DV))
    c_lam_re = -0.5 + 0.01 * jax.random.normal(next(ks), (L, C_GROUPS, C_STATE), f32)
    c_lam_im = math.pi * jnp.arange(C_STATE, dtype=f32) + 0.01 * jax.random.normal(next(ks), (L, C_GROUPS, C_STATE), f32)
    c_log_dt = unif((L, C_GROUPS), math.log(0.001), math.log(0.1))
    c_b_re = nrm((L, C_GROUPS, C_STATE, C_GROUP), (2 * C_GROUP) ** -0.5)
    c_b_im = nrm((L, C_GROUPS, C_STATE, C_GROUP), (2 * C_GROUP) ** -0.5)
    c_c_re = nrm((L, C_GROUPS, C_GROUP, C_STATE), C_STATE ** -0.5)
    c_c_im = nrm((L, C_GROUPS, C_GROUP, C_STATE), C_STATE ** -0.5)
    c_d = nrm((L, C_WIDTH), 1.0)
    c_glu_w = nrm((L, C_WIDTH, C_WIDTH), C_WIDTH ** -0.5)
    c_glu_b = nrm((L, C_WIDTH), 0.01)
    w_branch = nrm((L, N_BRANCH, BRANCH_WIDTH, D_MODEL), BRANCH_WIDTH ** -0.5)
    w_out = nrm((L, D_MODEL, D_MODEL), D_MODEL ** -0.5)
    xa_norm = gain((L, D_MODEL))
    mem_norm = gain((L, D_MODEL))
    xa_w_q = nrm((L, D_MODEL, D_MODEL), D_MODEL ** -0.5)
    xa_w_kv = nrm((L, D_MODEL, 2 * D_MODEL), D_MODEL ** -0.5)
    xa_w_o = nrm((L, D_MODEL, D_MODEL), D_MODEL ** -0.5)
    ffn_norm = gain((L, D_MODEL))
    ffn_w_up = nrm((L, D_MODEL, 2 * D_FF), D_MODEL ** -0.5)
    ffn_conv_w = nrm((L, FFN_CONV, 2 * D_FF), FFN_CONV ** -0.5)
    ffn_conv_b = nrm((L, 2 * D_FF), 0.01)
    ffn_w_down = nrm((L, D_FF, D_MODEL), D_FF ** -0.5)
    final_norm = gain((D_MODEL,))
    return {"x": x, "mem": mem, "mix_norm": mix_norm, "w_in": w_in, "b_gate": b_gate,
            "a_conv_w": a_conv_w, "a_conv_b": a_conv_b, "a_w_r": a_w_r, "a_b_r": a_b_r,
            "a_w_i": a_w_i, "a_b_i": a_b_i, "a_lam": a_lam,
            "b_conv_w": b_conv_w, "b_a_log": b_a_log, "b_dt_bias": b_dt_bias, "b_norm": b_norm,
            "c_lam_re": c_lam_re, "c_lam_im": c_lam_im, "c_log_dt": c_log_dt,
            "c_b_re": c_b_re, "c_b_im": c_b_im, "c_c_re": c_c_re, "c_c_im": c_c_im, "c_d": c_d,
            "c_glu_w": c_glu_w, "c_glu_b": c_glu_b, "w_branch": w_branch, "w_out": w_out,
            "xa_norm": xa_norm, "mem_norm": mem_norm, "xa_w_q": xa_w_q, "xa_w_kv": xa_w_kv, "xa_w_o": xa_w_o,
            "ffn_norm": ffn_norm, "ffn_w_up": ffn_w_up, "ffn_conv_w": ffn_conv_w, "ffn_conv_b": ffn_conv_b,
            "ffn_w_down": ffn_w_down, "final_norm": final_norm}


def _fwd_reference(x, mem, mix_norm, w_in, b_gate, a_conv_w, a_conv_b, a_w_r, a_b_r, a_w_i, a_b_i, a_lam,
              b_conv_w, b_a_log, b_dt_bias, b_norm, c_lam_re, c_lam_im, c_log_dt, c_b_re, c_b_im,
              c_c_re, c_c_im, c_d, c_glu_w, c_glu_b, w_branch, w_out, xa_norm, mem_norm, xa_w_q,
              xa_w_kv, xa_w_o, ffn_norm, ffn_w_up, ffn_conv_w, ffn_conv_b, ffn_w_down, final_norm):
    for l in range(DEPTH):
        h = rmsnorm(x, mix_norm[l])
        x = x + hybrid_mixer(h, w_in[l], b_gate[l], a_conv_w[l], a_conv_b[l], a_w_r[l], a_b_r[l],
                             a_w_i[l], a_b_i[l], a_lam[l], b_conv_w[l], b_a_log[l], b_dt_bias[l], b_norm[l],
                             c_lam_re[l], c_lam_im[l], c_log_dt[l], c_b_re[l], c_b_im[l], c_c_re[l],
                             c_c_im[l], c_d[l], c_glu_w[l], c_glu_b[l], w_branch[l], w_out[l])
        h = rmsnorm(x, xa_norm[l])
        x = x + cross_attention(h, rmsnorm(mem, mem_norm[l]), xa_w_q[l], xa_w_kv[l], xa_w_o[l])
        h = rmsnorm(x, ffn_norm[l])
        x = x + conv_ffn(h, ffn_w_up[l], ffn_conv_w[l], ffn_conv_b[l], ffn_w_down[l])
    return rmsnorm(x, final_norm)


import jax as _jax
import jax.numpy as _jnp

TWIN_FORMAT = 'train_step'
FWD_PARAMS = ['x', 'mem', 'mix_norm', 'w_in', 'b_gate', 'a_conv_w', 'a_conv_b', 'a_w_r', 'a_b_r', 'a_w_i', 'a_b_i', 'a_lam', 'b_conv_w', 'b_a_log', 'b_dt_bias', 'b_norm', 'c_lam_re', 'c_lam_im', 'c_log_dt', 'c_b_re', 'c_b_im', 'c_c_re', 'c_c_im', 'c_d', 'c_glu_w', 'c_glu_b', 'w_branch', 'w_out', 'xa_norm', 'mem_norm', 'xa_w_q', 'xa_w_kv', 'xa_w_o', 'ffn_norm', 'ffn_w_up', 'ffn_conv_w', 'ffn_conv_b', 'ffn_w_down', 'final_norm']
TWIN_WEIGHTS = ['mix_norm', 'w_in', 'b_gate', 'a_conv_w', 'a_conv_b', 'a_w_r', 'a_b_r', 'a_w_i', 'a_b_i', 'a_lam', 'b_conv_w', 'b_a_log', 'b_dt_bias', 'b_norm', 'c_lam_re', 'c_lam_im', 'c_log_dt', 'c_b_re', 'c_b_im', 'c_c_re', 'c_c_im', 'c_d', 'c_glu_w', 'c_glu_b', 'w_branch', 'w_out', 'xa_norm', 'mem_norm', 'xa_w_q', 'xa_w_kv', 'xa_w_o', 'ffn_norm', 'ffn_w_up', 'ffn_conv_w', 'ffn_conv_b', 'ffn_w_down', 'final_norm']
TWIN_DIFF_INPUT = 'x'
TWIN_INPUTS = ['x', 'mem', 'mix_norm', 'w_in', 'b_gate', 'a_conv_w', 'a_conv_b', 'a_w_r', 'a_b_r', 'a_w_i', 'a_b_i', 'a_lam', 'b_conv_w', 'b_a_log', 'b_dt_bias', 'b_norm', 'c_lam_re', 'c_lam_im', 'c_log_dt', 'c_b_re', 'c_b_im', 'c_c_re', 'c_c_im', 'c_d', 'c_glu_w', 'c_glu_b', 'w_branch', 'w_out', 'xa_norm', 'mem_norm', 'xa_w_q', 'xa_w_kv', 'xa_w_o', 'ffn_norm', 'ffn_w_up', 'ffn_conv_w', 'ffn_conv_b', 'ffn_w_down', 'final_norm', 'loss_target', 'm_mix_norm', 'm_w_in', 'm_b_gate', 'm_a_conv_w', 'm_a_conv_b', 'm_a_w_r', 'm_a_b_r', 'm_a_w_i', 'm_a_b_i', 'm_a_lam', 'm_b_conv_w', 'm_b_a_log', 'm_b_dt_bias', 'm_b_norm', 'm_c_lam_re', 'm_c_lam_im', 'm_c_log_dt', 'm_c_b_re', 'm_c_b_im', 'm_c_c_re', 'm_c_c_im', 'm_c_d', 'm_c_glu_w', 'm_c_glu_b', 'm_w_branch', 'm_w_out', 'm_xa_norm', 'm_mem_norm', 'm_xa_w_q', 'm_xa_w_kv', 'm_xa_w_o', 'm_ffn_norm', 'm_ffn_w_up', 'm_ffn_conv_w', 'm_ffn_conv_b', 'm_ffn_w_down', 'm_final_norm', 'v_mix_norm', 'v_w_in', 'v_b_gate', 'v_a_conv_w', 'v_a_conv_b', 'v_a_w_r', 'v_a_b_r', 'v_a_w_i', 'v_a_b_i', 'v_a_lam', 'v_b_conv_w', 'v_b_a_log', 'v_b_dt_bias', 'v_b_norm', 'v_c_lam_re', 'v_c_lam_im', 'v_c_log_dt', 'v_c_b_re', 'v_c_b_im', 'v_c_c_re', 'v_c_c_im', 'v_c_d', 'v_c_glu_w', 'v_c_glu_b', 'v_w_branch', 'v_w_out', 'v_xa_norm', 'v_mem_norm', 'v_xa_w_q', 'v_xa_w_kv', 'v_xa_w_o', 'v_ffn_norm', 'v_ffn_w_up', 'v_ffn_conv_w', 'v_ffn_conv_b', 'v_ffn_w_down', 'v_final_norm']
TWIN_OUTPUTS = ['loss', 'grad_x', 'grad_mix_norm', 'grad_w_in', 'grad_b_gate', 'grad_a_conv_w', 'grad_a_conv_b', 'grad_a_w_r', 'grad_a_b_r', 'grad_a_w_i', 'grad_a_b_i', 'grad_a_lam', 'grad_b_conv_w', 'grad_b_a_log', 'grad_b_dt_bias', 'grad_b_norm', 'grad_c_lam_re', 'grad_c_lam_im', 'grad_c_log_dt', 'grad_c_b_re', 'grad_c_b_im', 'grad_c_c_re', 'grad_c_c_im', 'grad_c_d', 'grad_c_glu_w', 'grad_c_glu_b', 'grad_w_branch', 'grad_w_out', 'grad_xa_norm', 'grad_mem_norm', 'grad_xa_w_q', 'grad_xa_w_kv', 'grad_xa_w_o', 'grad_ffn_norm', 'grad_ffn_w_up', 'grad_ffn_conv_w', 'grad_ffn_conv_b', 'grad_ffn_w_down', 'grad_final_norm', 'delta_mix_norm', 'delta_w_in', 'delta_b_gate', 'delta_a_conv_w', 'delta_a_conv_b', 'delta_a_w_r', 'delta_a_b_r', 'delta_a_w_i', 'delta_a_b_i', 'delta_a_lam', 'delta_b_conv_w', 'delta_b_a_log', 'delta_b_dt_bias', 'delta_b_norm', 'delta_c_lam_re', 'delta_c_lam_im', 'delta_c_log_dt', 'delta_c_b_re', 'delta_c_b_im', 'delta_c_c_re', 'delta_c_c_im', 'delta_c_d', 'delta_c_glu_w', 'delta_c_glu_b', 'delta_w_branch', 'delta_w_out', 'delta_xa_norm', 'delta_mem_norm', 'delta_xa_w_q', 'delta_xa_w_kv', 'delta_xa_w_o', 'delta_ffn_norm', 'delta_ffn_w_up', 'delta_ffn_conv_w', 'delta_ffn_conv_b', 'delta_ffn_w_down', 'delta_final_norm', 'new_m_mix_norm', 'new_m_w_in', 'new_m_b_gate', 'new_m_a_conv_w', 'new_m_a_conv_b', 'new_m_a_w_r', 'new_m_a_b_r', 'new_m_a_w_i', 'new_m_a_b_i', 'new_m_a_lam', 'new_m_b_conv_w', 'new_m_b_a_log', 'new_m_b_dt_bias', 'new_m_b_norm', 'new_m_c_lam_re', 'new_m_c_lam_im', 'new_m_c_log_dt', 'new_m_c_b_re', 'new_m_c_b_im', 'new_m_c_c_re', 'new_m_c_c_im', 'new_m_c_d', 'new_m_c_glu_w', 'new_m_c_glu_b', 'new_m_w_branch', 'new_m_w_out', 'new_m_xa_norm', 'new_m_mem_norm', 'new_m_xa_w_q', 'new_m_xa_w_kv', 'new_m_xa_w_o', 'new_m_ffn_norm', 'new_m_ffn_w_up', 'new_m_ffn_conv_w', 'new_m_ffn_conv_b', 'new_m_ffn_w_down', 'new_m_final_norm', 'new_v_mix_norm', 'new_v_w_in', 'new_v_b_gate', 'new_v_a_conv_w', 'new_v_a_conv_b', 'new_v_a_w_r', 'new_v_a_b_r', 'new_v_a_w_i', 'new_v_a_b_i', 'new_v_a_lam', 'new_v_b_conv_w', 'new_v_b_a_log', 'new_v_b_dt_bias', 'new_v_b_norm', 'new_v_c_lam_re', 'new_v_c_lam_im', 'new_v_c_log_dt', 'new_v_c_b_re', 'new_v_c_b_im', 'new_v_c_c_re', 'new_v_c_c_im', 'new_v_c_d', 'new_v_c_glu_w', 'new_v_c_glu_b', 'new_v_w_branch', 'new_v_w_out', 'new_v_xa_norm', 'new_v_mem_norm', 'new_v_xa_w_q', 'new_v_xa_w_kv', 'new_v_xa_w_o', 'new_v_ffn_norm', 'new_v_ffn_w_up', 'new_v_ffn_conv_w', 'new_v_ffn_conv_b', 'new_v_ffn_w_down', 'new_v_final_norm']
TWIN_LEAF_KINDS = {'loss': 'loss', 'grad_x': 'grad_x', 'grad_mix_norm': 'grad_w', 'grad_w_in': 'grad_w', 'grad_b_gate': 'grad_w', 'grad_a_conv_w': 'grad_w', 'grad_a_conv_b': 'grad_w', 'grad_a_w_r': 'grad_w', 'grad_a_b_r': 'grad_w', 'grad_a_w_i': 'grad_w', 'grad_a_b_i': 'grad_w', 'grad_a_lam': 'grad_w', 'grad_b_conv_w': 'grad_w', 'grad_b_a_log': 'grad_w', 'grad_b_dt_bias': 'grad_w', 'grad_b_norm': 'grad_w', 'grad_c_lam_re': 'grad_w', 'grad_c_lam_im': 'grad_w', 'grad_c_log_dt': 'grad_w', 'grad_c_b_re': 'grad_w', 'grad_c_b_im': 'grad_w', 'grad_c_c_re': 'grad_w', 'grad_c_c_im': 'grad_w', 'grad_c_d': 'grad_w', 'grad_c_glu_w': 'grad_w', 'grad_c_glu_b': 'grad_w', 'grad_w_branch': 'grad_w', 'grad_w_out': 'grad_w', 'grad_xa_norm': 'grad_w', 'grad_mem_norm': 'grad_w', 'grad_xa_w_q': 'grad_w', 'grad_xa_w_kv': 'grad_w', 'grad_xa_w_o': 'grad_w', 'grad_ffn_norm': 'grad_w', 'grad_ffn_w_up': 'grad_w', 'grad_ffn_conv_w': 'grad_w', 'grad_ffn_conv_b': 'grad_w', 'grad_ffn_w_down': 'grad_w', 'grad_final_norm': 'grad_w', 'delta_mix_norm': 'delta_w', 'delta_w_in': 'delta_w', 'delta_b_gate': 'delta_w', 'delta_a_conv_w': 'delta_w', 'delta_a_conv_b': 'delta_w', 'delta_a_w_r': 'delta_w', 'delta_a_b_r': 'delta_w', 'delta_a_w_i': 'delta_w', 'delta_a_b_i': 'delta_w', 'delta_a_lam': 'delta_w', 'delta_b_conv_w': 'delta_w', 'delta_b_a_log': 'delta_w', 'delta_b_dt_bias': 'delta_w', 'delta_b_norm': 'delta_w', 'delta_c_lam_re': 'delta_w', 'delta_c_lam_im': 'delta_w', 'delta_c_log_dt': 'delta_w', 'delta_c_b_re': 'delta_w', 'delta_c_b_im': 'delta_w', 'delta_c_c_re': 'delta_w', 'delta_c_c_im': 'delta_w', 'delta_c_d': 'delta_w', 'delta_c_glu_w': 'delta_w', 'delta_c_glu_b': 'delta_w', 'delta_w_branch': 'delta_w', 'delta_w_out': 'delta_w', 'delta_xa_norm': 'delta_w', 'delta_mem_norm': 'delta_w', 'delta_xa_w_q': 'delta_w', 'delta_xa_w_kv': 'delta_w', 'delta_xa_w_o': 'delta_w', 'delta_ffn_norm': 'delta_w', 'delta_ffn_w_up': 'delta_w', 'delta_ffn_conv_w': 'delta_w', 'delta_ffn_conv_b': 'delta_w', 'delta_ffn_w_down': 'delta_w', 'delta_final_norm': 'delta_w', 'new_m_mix_norm': 'new_m', 'new_m_w_in': 'new_m', 'new_m_b_gate': 'new_m', 'new_m_a_conv_w': 'new_m', 'new_m_a_conv_b': 'new_m', 'new_m_a_w_r': 'new_m', 'new_m_a_b_r': 'new_m', 'new_m_a_w_i': 'new_m', 'new_m_a_b_i': 'new_m', 'new_m_a_lam': 'new_m', 'new_m_b_conv_w': 'new_m', 'new_m_b_a_log': 'new_m', 'new_m_b_dt_bias': 'new_m', 'new_m_b_norm': 'new_m', 'new_m_c_lam_re': 'new_m', 'new_m_c_lam_im': 'new_m', 'new_m_c_log_dt': 'new_m', 'new_m_c_b_re': 'new_m', 'new_m_c_b_im': 'new_m', 'new_m_c_c_re': 'new_m', 'new_m_c_c_im': 'new_m', 'new_m_c_d': 'new_m', 'new_m_c_glu_w': 'new_m', 'new_m_c_glu_b': 'new_m', 'new_m_w_branch': 'new_m', 'new_m_w_out': 'new_m', 'new_m_xa_norm': 'new_m', 'new_m_mem_norm': 'new_m', 'new_m_xa_w_q': 'new_m', 'new_m_xa_w_kv': 'new_m', 'new_m_xa_w_o': 'new_m', 'new_m_ffn_norm': 'new_m', 'new_m_ffn_w_up': 'new_m', 'new_m_ffn_conv_w': 'new_m', 'new_m_ffn_conv_b': 'new_m', 'new_m_ffn_w_down': 'new_m', 'new_m_final_norm': 'new_m', 'new_v_mix_norm': 'new_v', 'new_v_w_in': 'new_v', 'new_v_b_gate': 'new_v', 'new_v_a_conv_w': 'new_v', 'new_v_a_conv_b': 'new_v', 'new_v_a_w_r': 'new_v', 'new_v_a_b_r': 'new_v', 'new_v_a_w_i': 'new_v', 'new_v_a_b_i': 'new_v', 'new_v_a_lam': 'new_v', 'new_v_b_conv_w': 'new_v', 'new_v_b_a_log': 'new_v', 'new_v_b_dt_bias': 'new_v', 'new_v_b_norm': 'new_v', 'new_v_c_lam_re': 'new_v', 'new_v_c_lam_im': 'new_v', 'new_v_c_log_dt': 'new_v', 'new_v_c_b_re': 'new_v', 'new_v_c_b_im': 'new_v', 'new_v_c_c_re': 'new_v', 'new_v_c_c_im': 'new_v', 'new_v_c_d': 'new_v', 'new_v_c_glu_w': 'new_v', 'new_v_c_glu_b': 'new_v', 'new_v_w_branch': 'new_v', 'new_v_w_out': 'new_v', 'new_v_xa_norm': 'new_v', 'new_v_mem_norm': 'new_v', 'new_v_xa_w_q': 'new_v', 'new_v_xa_w_kv': 'new_v', 'new_v_xa_w_o': 'new_v', 'new_v_ffn_norm': 'new_v', 'new_v_ffn_w_up': 'new_v', 'new_v_ffn_conv_w': 'new_v', 'new_v_ffn_conv_b': 'new_v', 'new_v_ffn_w_down': 'new_v', 'new_v_final_norm': 'new_v'}


def _forward(args):
    return _fwd_reference(*[args[k] for k in FWD_PARAMS])


def _output_shape():
    out = _jax.eval_shape(lambda: _forward(_fwd_setup_inputs(0)))
    return out.shape, out.dtype

N_MICROBATCH = 1
ADAM_LR = 0.001
ADAM_B1 = 0.9
ADAM_B2 = 0.999
ADAM_EPS = 1e-08
ADAM_WD = 0.01
ADAM_STEP = 10
PER_EXAMPLE_BATCH_AXIS = {'x': 0, 'mem': 0, 'loss_target': 0}
SHARED_INPUTS = []
_WEIGHT_DTYPES = {'mix_norm': _jnp.float32, 'w_in': _jnp.float32, 'b_gate': _jnp.float32, 'a_conv_w': _jnp.float32, 'a_conv_b': _jnp.float32, 'a_w_r': _jnp.float32, 'a_b_r': _jnp.float32, 'a_w_i': _jnp.float32, 'a_b_i': _jnp.float32, 'a_lam': _jnp.float32, 'b_conv_w': _jnp.float32, 'b_a_log': _jnp.float32, 'b_dt_bias': _jnp.float32, 'b_norm': _jnp.float32, 'c_lam_re': _jnp.float32, 'c_lam_im': _jnp.float32, 'c_log_dt': _jnp.float32, 'c_b_re': _jnp.float32, 'c_b_im': _jnp.float32, 'c_c_re': _jnp.float32, 'c_c_im': _jnp.float32, 'c_d': _jnp.float32, 'c_glu_w': _jnp.float32, 'c_glu_b': _jnp.float32, 'w_branch': _jnp.float32, 'w_out': _jnp.float32, 'xa_norm': _jnp.float32, 'mem_norm': _jnp.float32, 'xa_w_q': _jnp.float32, 'xa_w_kv': _jnp.float32, 'xa_w_o': _jnp.float32, 'ffn_norm': _jnp.float32, 'ffn_w_up': _jnp.float32, 'ffn_conv_w': _jnp.float32, 'ffn_conv_b': _jnp.float32, 'ffn_w_down': _jnp.float32, 'final_norm': _jnp.float32}
MOMENT_SCALE = {'mix_norm': 2.073832e-01, 'w_in': 8.066412e-02, 'b_gate': 3.416235e-02, 'a_conv_w': 1.588101e-01, 'a_conv_b': 1.304699e+00, 'a_w_r': 4.849041e-02, 'a_b_r': 4.531801e-02, 'a_w_i': 8.905779e-02, 'a_b_i': 5.944496e-02, 'a_lam': 8.810113e-02, 'b_conv_w': 9.106457e-02, 'b_a_log': 4.095892e-01, 'b_dt_bias': 4.014176e-01, 'b_norm': 2.457990e-01, 'c_lam_re': 7.724974e-03, 'c_lam_im': 7.800092e-03, 'c_log_dt': 6.410093e+00, 'c_b_re': 3.977172e-03, 'c_b_im': 3.864616e-03, 'c_c_re': 5.625222e-03, 'c_c_im': 5.808840e-03, 'c_d': 9.867533e-02, 'c_glu_w': 2.174762e-02, 'c_glu_b': 3.447033e-02, 'w_branch': 8.740426e-02, 'w_out': 1.523595e-01, 'xa_norm': 2.846210e-02, 'mem_norm': 4.444650e-02, 'xa_w_q': 2.722597e-02, 'xa_w_kv': 3.242356e-02, 'xa_w_o': 3.776562e-02, 'ffn_norm': 1.946337e-01, 'ffn_w_up': 7.956238e-02, 'ffn_conv_w': 7.996772e-02, 'ffn_conv_b': 9.289982e-02, 'ffn_w_down': 1.357071e-01, 'final_norm': 6.398687e+01}


def _to_microbatches(a, axis):
    t = _jnp.moveaxis(a, axis, 0)
    t = t.reshape((N_MICROBATCH, t.shape[0] // N_MICROBATCH) + t.shape[1:])
    return _jnp.moveaxis(t, 1, axis + 1)


def setup_inputs(seed: int = 0) -> dict:
    inp = _fwd_setup_inputs(seed)
    key = _jax.random.fold_in(_jax.random.key(seed), 7919)
    shape, _ = _output_shape()
    out = dict(inp)
    out["loss_target"] = _jax.random.normal(_jax.random.fold_in(key, 0), shape, _jnp.float32)
    for i, name in enumerate(TWIN_WEIGHTS):
        w = inp[name].astype(_jnp.float32)
        if MOMENT_SCALE is None:
            s = _jnp.sqrt(_jnp.mean(_jnp.square(w)) + 1e-30)
        else:
            s = MOMENT_SCALE[name]
        km, kv = _jax.random.split(_jax.random.fold_in(key, i + 1))
        out[name] = w
        out["m_" + name] = s * _jax.random.normal(km, w.shape, _jnp.float32)
        out["v_" + name] = (s * s) * _jax.random.uniform(kv, w.shape, _jnp.float32, 0.5, 1.5)
    if N_MICROBATCH > 1:
        for name, axis in PER_EXAMPLE_BATCH_AXIS.items():
            out[name] = _to_microbatches(out[name], axis)
    return {'x': out['x'], 'mem': out['mem'], 'mix_norm': out['mix_norm'], 'w_in': out['w_in'], 'b_gate': out['b_gate'], 'a_conv_w': out['a_conv_w'], 'a_conv_b': out['a_conv_b'], 'a_w_r': out['a_w_r'], 'a_b_r': out['a_b_r'], 'a_w_i': out['a_w_i'], 'a_b_i': out['a_b_i'], 'a_lam': out['a_lam'], 'b_conv_w': out['b_conv_w'], 'b_a_log': out['b_a_log'], 'b_dt_bias': out['b_dt_bias'], 'b_norm': out['b_norm'], 'c_lam_re': out['c_lam_re'], 'c_lam_im': out['c_lam_im'], 'c_log_dt': out['c_log_dt'], 'c_b_re': out['c_b_re'], 'c_b_im': out['c_b_im'], 'c_c_re': out['c_c_re'], 'c_c_im': out['c_c_im'], 'c_d': out['c_d'], 'c_glu_w': out['c_glu_w'], 'c_glu_b': out['c_glu_b'], 'w_branch': out['w_branch'], 'w_out': out['w_out'], 'xa_norm': out['xa_norm'], 'mem_norm': out['mem_norm'], 'xa_w_q': out['xa_w_q'], 'xa_w_kv': out['xa_w_kv'], 'xa_w_o': out['xa_w_o'], 'ffn_norm': out['ffn_norm'], 'ffn_w_up': out['ffn_w_up'], 'ffn_conv_w': out['ffn_conv_w'], 'ffn_conv_b': out['ffn_conv_b'], 'ffn_w_down': out['ffn_w_down'], 'final_norm': out['final_norm'], 'loss_target': out['loss_target'], 'm_mix_norm': out['m_mix_norm'], 'm_w_in': out['m_w_in'], 'm_b_gate': out['m_b_gate'], 'm_a_conv_w': out['m_a_conv_w'], 'm_a_conv_b': out['m_a_conv_b'], 'm_a_w_r': out['m_a_w_r'], 'm_a_b_r': out['m_a_b_r'], 'm_a_w_i': out['m_a_w_i'], 'm_a_b_i': out['m_a_b_i'], 'm_a_lam': out['m_a_lam'], 'm_b_conv_w': out['m_b_conv_w'], 'm_b_a_log': out['m_b_a_log'], 'm_b_dt_bias': out['m_b_dt_bias'], 'm_b_norm': out['m_b_norm'], 'm_c_lam_re': out['m_c_lam_re'], 'm_c_lam_im': out['m_c_lam_im'], 'm_c_log_dt': out['m_c_log_dt'], 'm_c_b_re': out['m_c_b_re'], 'm_c_b_im': out['m_c_b_im'], 'm_c_c_re': out['m_c_c_re'], 'm_c_c_im': out['m_c_c_im'], 'm_c_d': out['m_c_d'], 'm_c_glu_w': out['m_c_glu_w'], 'm_c_glu_b': out['m_c_glu_b'], 'm_w_branch': out['m_w_branch'], 'm_w_out': out['m_w_out'], 'm_xa_norm': out['m_xa_norm'], 'm_mem_norm': out['m_mem_norm'], 'm_xa_w_q': out['m_xa_w_q'], 'm_xa_w_kv': out['m_xa_w_kv'], 'm_xa_w_o': out['m_xa_w_o'], 'm_ffn_norm': out['m_ffn_norm'], 'm_ffn_w_up': out['m_ffn_w_up'], 'm_ffn_conv_w': out['m_ffn_conv_w'], 'm_ffn_conv_b': out['m_ffn_conv_b'], 'm_ffn_w_down': out['m_ffn_w_down'], 'm_final_norm': out['m_final_norm'], 'v_mix_norm': out['v_mix_norm'], 'v_w_in': out['v_w_in'], 'v_b_gate': out['v_b_gate'], 'v_a_conv_w': out['v_a_conv_w'], 'v_a_conv_b': out['v_a_conv_b'], 'v_a_w_r': out['v_a_w_r'], 'v_a_b_r': out['v_a_b_r'], 'v_a_w_i': out['v_a_w_i'], 'v_a_b_i': out['v_a_b_i'], 'v_a_lam': out['v_a_lam'], 'v_b_conv_w': out['v_b_conv_w'], 'v_b_a_log': out['v_b_a_log'], 'v_b_dt_bias': out['v_b_dt_bias'], 'v_b_norm': out['v_b_norm'], 'v_c_lam_re': out['v_c_lam_re'], 'v_c_lam_im': out['v_c_lam_im'], 'v_c_log_dt': out['v_c_log_dt'], 'v_c_b_re': out['v_c_b_re'], 'v_c_b_im': out['v_c_b_im'], 'v_c_c_re': out['v_c_c_re'], 'v_c_c_im': out['v_c_c_im'], 'v_c_d': out['v_c_d'], 'v_c_glu_w': out['v_c_glu_w'], 'v_c_glu_b': out['v_c_glu_b'], 'v_w_branch': out['v_w_branch'], 'v_w_out': out['v_w_out'], 'v_xa_norm': out['v_xa_norm'], 'v_mem_norm': out['v_mem_norm'], 'v_xa_w_q': out['v_xa_w_q'], 'v_xa_w_kv': out['v_xa_w_kv'], 'v_xa_w_o': out['v_xa_w_o'], 'v_ffn_norm': out['v_ffn_norm'], 'v_ffn_w_up': out['v_ffn_w_up'], 'v_ffn_conv_w': out['v_ffn_conv_w'], 'v_ffn_conv_b': out['v_ffn_conv_b'], 'v_ffn_w_down': out['v_ffn_w_down'], 'v_final_norm': out['v_final_norm']}


def _loss(weights, diff, rest, loss_target):
    with _jax.named_scope("forward"):
        args = {**rest, TWIN_DIFF_INPUT: diff, **{k: w.astype(_WEIGHT_DTYPES[k]) for k, w in weights.items()}}
        y = _forward(args)
    with _jax.named_scope("loss_head"):
        err = _jnp.square(y.astype(_jnp.float32) - loss_target)
        return 0.5 * _jnp.sum(_jnp.mean(err, axis=-1)) if err.ndim else 0.5 * err


def _adamw(w, g, m, v):
    m = ADAM_B1 * m + (1.0 - ADAM_B1) * g
    v = ADAM_B2 * v + (1.0 - ADAM_B2) * _jnp.square(g)
    m_hat = m / (1.0 - ADAM_B1 ** ADAM_STEP)
    v_hat = v / (1.0 - ADAM_B2 ** ADAM_STEP)
    delta = -ADAM_LR * (m_hat / (_jnp.sqrt(v_hat) + ADAM_EPS) + ADAM_WD * w)
    return delta, m, v


def reference(x, mem, mix_norm, w_in, b_gate, a_conv_w, a_conv_b, a_w_r, a_b_r, a_w_i, a_b_i, a_lam, b_conv_w, b_a_log, b_dt_bias, b_norm, c_lam_re, c_lam_im, c_log_dt, c_b_re, c_b_im, c_c_re, c_c_im, c_d, c_glu_w, c_glu_b, w_branch, w_out, xa_norm, mem_norm, xa_w_q, xa_w_kv, xa_w_o, ffn_norm, ffn_w_up, ffn_conv_w, ffn_conv_b, ffn_w_down, final_norm, loss_target, m_mix_norm, m_w_in, m_b_gate, m_a_conv_w, m_a_conv_b, m_a_w_r, m_a_b_r, m_a_w_i, m_a_b_i, m_a_lam, m_b_conv_w, m_b_a_log, m_b_dt_bias, m_b_norm, m_c_lam_re, m_c_lam_im, m_c_log_dt, m_c_b_re, m_c_b_im, m_c_c_re, m_c_c_im, m_c_d, m_c_glu_w, m_c_glu_b, m_w_branch, m_w_out, m_xa_norm, m_mem_norm, m_xa_w_q, m_xa_w_kv, m_xa_w_o, m_ffn_norm, m_ffn_w_up, m_ffn_conv_w, m_ffn_conv_b, m_ffn_w_down, m_final_norm, v_mix_norm, v_w_in, v_b_gate, v_a_conv_w, v_a_conv_b, v_a_w_r, v_a_b_r, v_a_w_i, v_a_b_i, v_a_lam, v_b_conv_w, v_b_a_log, v_b_dt_bias, v_b_norm, v_c_lam_re, v_c_lam_im, v_c_log_dt, v_c_b_re, v_c_b_im, v_c_c_re, v_c_c_im, v_c_d, v_c_glu_w, v_c_glu_b, v_w_branch, v_w_out, v_xa_norm, v_mem_norm, v_xa_w_q, v_xa_w_kv, v_xa_w_o, v_ffn_norm, v_ffn_w_up, v_ffn_conv_w, v_ffn_conv_b, v_ffn_w_down, v_final_norm):
    given = dict(x=x, mem=mem, mix_norm=mix_norm, w_in=w_in, b_gate=b_gate, a_conv_w=a_conv_w, a_conv_b=a_conv_b, a_w_r=a_w_r, a_b_r=a_b_r, a_w_i=a_w_i, a_b_i=a_b_i, a_lam=a_lam, b_conv_w=b_conv_w, b_a_log=b_a_log, b_dt_bias=b_dt_bias, b_norm=b_norm, c_lam_re=c_lam_re, c_lam_im=c_lam_im, c_log_dt=c_log_dt, c_b_re=c_b_re, c_b_im=c_b_im, c_c_re=c_c_re, c_c_im=c_c_im, c_d=c_d, c_glu_w=c_glu_w, c_glu_b=c_glu_b, w_branch=w_branch, w_out=w_out, xa_norm=xa_norm, mem_norm=mem_norm, xa_w_q=xa_w_q, xa_w_kv=xa_w_kv, xa_w_o=xa_w_o, ffn_norm=ffn_norm, ffn_w_up=ffn_w_up, ffn_conv_w=ffn_conv_w, ffn_conv_b=ffn_conv_b, ffn_w_down=ffn_w_down, final_norm=final_norm, loss_target=loss_target, m_mix_norm=m_mix_norm, m_w_in=m_w_in, m_b_gate=m_b_gate, m_a_conv_w=m_a_conv_w, m_a_conv_b=m_a_conv_b, m_a_w_r=m_a_w_r, m_a_b_r=m_a_b_r, m_a_w_i=m_a_w_i, m_a_b_i=m_a_b_i, m_a_lam=m_a_lam, m_b_conv_w=m_b_conv_w, m_b_a_log=m_b_a_log, m_b_dt_bias=m_b_dt_bias, m_b_norm=m_b_norm, m_c_lam_re=m_c_lam_re, m_c_lam_im=m_c_lam_im, m_c_log_dt=m_c_log_dt, m_c_b_re=m_c_b_re, m_c_b_im=m_c_b_im, m_c_c_re=m_c_c_re, m_c_c_im=m_c_c_im, m_c_d=m_c_d, m_c_glu_w=m_c_glu_w, m_c_glu_b=m_c_glu_b, m_w_branch=m_w_branch, m_w_out=m_w_out, m_xa_norm=m_xa_norm, m_mem_norm=m_mem_norm, m_xa_w_q=m_xa_w_q, m_xa_w_kv=m_xa_w_kv, m_xa_w_o=m_xa_w_o, m_ffn_norm=m_ffn_norm, m_ffn_w_up=m_ffn_w_up, m_ffn_conv_w=m_ffn_conv_w, m_ffn_conv_b=m_ffn_conv_b, m_ffn_w_down=m_ffn_w_down, m_final_norm=m_final_norm, v_mix_norm=v_mix_norm, v_w_in=v_w_in, v_b_gate=v_b_gate, v_a_conv_w=v_a_conv_w, v_a_conv_b=v_a_conv_b, v_a_w_r=v_a_w_r, v_a_b_r=v_a_b_r, v_a_w_i=v_a_w_i, v_a_b_i=v_a_b_i, v_a_lam=v_a_lam, v_b_conv_w=v_b_conv_w, v_b_a_log=v_b_a_log, v_b_dt_bias=v_b_dt_bias, v_b_norm=v_b_norm, v_c_lam_re=v_c_lam_re, v_c_lam_im=v_c_lam_im, v_c_log_dt=v_c_log_dt, v_c_b_re=v_c_b_re, v_c_b_im=v_c_b_im, v_c_c_re=v_c_c_re, v_c_c_im=v_c_c_im, v_c_d=v_c_d, v_c_glu_w=v_c_glu_w, v_c_glu_b=v_c_glu_b, v_w_branch=v_w_branch, v_w_out=v_w_out, v_xa_norm=v_xa_norm, v_mem_norm=v_mem_norm, v_xa_w_q=v_xa_w_q, v_xa_w_kv=v_xa_w_kv, v_xa_w_o=v_xa_w_o, v_ffn_norm=v_ffn_norm, v_ffn_w_up=v_ffn_w_up, v_ffn_conv_w=v_ffn_conv_w, v_ffn_conv_b=v_ffn_conv_b, v_ffn_w_down=v_ffn_w_down, v_final_norm=v_final_norm)
    weights = {n: given[n] for n in TWIN_WEIGHTS}
    shared = {n: given[n] for n in SHARED_INPUTS}
    per_example = {n: given[n] for n in ['x', 'mem']}
    grad_fn = _jax.value_and_grad(_loss, argnums=(0, 1))

    def one_microbatch(ex, loss_target):
        ex = dict(ex)
        diff = ex.pop(TWIN_DIFF_INPUT)
        return grad_fn(weights, diff, {**shared, **ex}, loss_target)

    if N_MICROBATCH == 1:
        loss, (grad_w, grad_x) = one_microbatch(per_example, given["loss_target"])
    else:
        def body(carry, xs):
            loss_sum, grad_sum = carry
            l_k, (gw_k, gx_k) = one_microbatch(xs[0], xs[1])
            with _jax.named_scope("update"):
                return (loss_sum + l_k, _jax.tree.map(_jnp.add, grad_sum, gw_k)), gx_k

        init = (_jnp.zeros((), _jnp.float32), _jax.tree.map(_jnp.zeros_like, weights))
        (loss, grad_w), grad_x = _jax.lax.scan(body, init, (per_example, given["loss_target"]))
    with _jax.named_scope("update"):
        delta_w, new_m, new_v = {}, {}, {}
        for n in TWIN_WEIGHTS:
            delta_w[n], new_m[n], new_v[n] = _adamw(weights[n], grad_w[n], given["m_" + n], given["v_" + n])
    return (loss, grad_x, *[grad_w[n] for n in TWIN_WEIGHTS], *[delta_w[n] for n in TWIN_WEIGHTS],
            *[new_m[n] for n in TWIN_WEIGHTS], *[new_v[n] for n in TWIN_WEIGHTS])
```

```python
import functools
import math

import jax
import jax.numpy as jnp
from jax import lax
from jax.experimental import pallas as pl
from jax.experimental.pallas import tpu as pltpu

F32, BF16 = jnp.float32, jnp.bfloat16
MESH = pl.DeviceIdType.MESH
ANY = pl.BlockSpec(memory_space=pl.ANY)

VMEM_LIMIT_V7X = 56 * 1024 * 1024
LANES, SUBLANES = 128, 8
EPS = 1e-6
RG_C = 8.0
DN_CHUNK = 64
N_CHIPS = 4
ADAM_LR, ADAM_B1, ADAM_B2, ADAM_EPS, ADAM_WD, ADAM_STEP = 0.001, 0.9, 0.999, 1e-08, 0.01, 10

SHARDED_MATS = (("w_in", 2), ("c_glu_w", 1), ("w_branch", 3), ("w_out", 1), ("xa_w_q", 1), ("xa_w_kv", 2),
                ("xa_w_o", 1), ("ffn_w_up", 2), ("ffn_w_down", 1))
SHARDED_CONVS = (("a_conv_w", 2), ("b_conv_w", 2), ("ffn_conv_w", 2))
SHARDED = SHARDED_MATS + SHARDED_CONVS
WEIGHTS = ("mix_norm", "w_in", "b_gate", "a_conv_w", "a_conv_b", "a_w_r", "a_b_r", "a_w_i", "a_b_i", "a_lam",
           "b_conv_w", "b_a_log", "b_dt_bias", "b_norm", "c_lam_re", "c_lam_im", "c_log_dt", "c_b_re", "c_b_im",
           "c_c_re", "c_c_im", "c_d", "c_glu_w", "c_glu_b", "w_branch", "w_out", "xa_norm", "mem_norm", "xa_w_q",
           "xa_w_kv", "xa_w_o", "ffn_norm", "ffn_w_up", "ffn_conv_w", "ffn_conv_b", "ffn_w_down", "final_norm")
REPLICATED = tuple(n for n in WEIGHTS if n not in dict(SHARDED))


def _cparams(n_axes):
    return pltpu.CompilerParams(dimension_semantics=("arbitrary",) * n_axes, vmem_limit_bytes=VMEM_LIMIT_V7X)


def _tile(n, pref, off=0):
    t = pref
    while t >= LANES:
        if n % t == 0 and off % t == 0:
            return t
        t //= 2
    assert off == 0, (n, pref, off)
    return n


def _full_spec(shape, n_grid):
    nd = len(shape)
    if n_grid == 1:
        return pl.BlockSpec(shape, lambda i: (0,) * nd)
    return pl.BlockSpec(shape, lambda i, j: (0,) * nd)


_NN, _NT, _TN = ((1,), (0,)), ((1,), (1,)), ((0,), (0,))


def _dot(a, b, dims, hp):
    if hp:
        return lax.dot_general(a, b, (dims, ((), ())), precision=lax.Precision.HIGHEST, preferred_element_type=F32)
    return lax.dot_general(a.astype(BF16), b.astype(BF16), (dims, ((), ())), preferred_element_type=F32)


@functools.partial(jax.custom_vjp, nondiff_argnums=(2,))
def mm_nn(a, b, hp=False):
    return _dot(a, b, _NN, hp)


@functools.partial(jax.custom_vjp, nondiff_argnums=(2,))
def mm_nt(a, b, hp=False):
    return _dot(a, b, _NT, hp)


@functools.partial(jax.custom_vjp, nondiff_argnums=(2,))
def mm_tn(a, b, hp=False):
    return _dot(a, b, _TN, hp)


mm_nn.defvjp(lambda a, b, hp: (_dot(a, b, _NN, hp), (a, b)),
             lambda hp, r, g: (mm_nt(g, r[1], hp), mm_tn(r[0], g, hp)))
mm_nt.defvjp(lambda a, b, hp: (_dot(a, b, _NT, hp), (a, b)),
             lambda hp, r, g: (mm_nn(g, r[1], hp), mm_tn(g, r[0], hp)))
mm_tn.defvjp(lambda a, b, hp: (_dot(a, b, _TN, hp), (a, b)),
             lambda hp, r, g: (mm_nt(r[1], g, hp), mm_nn(r[0], g, hp)))


@functools.partial(jax.custom_vjp, nondiff_argnums=(1,))
def taps(xext, k):
    return tuple((xext if s == 0 else pltpu.roll(xext, s, 0))[SUBLANES:] for s in range(k))


def _taps_fwd(xext, k):
    return taps(xext, k), None


def _taps_bwd(k, _, gs):
    tot = None
    for s, g in enumerate(gs):
        gp = jnp.concatenate([jnp.zeros((SUBLANES, g.shape[1]), g.dtype), g], axis=0)
        if s:
            gp = pltpu.roll(gp, gp.shape[0] - s, 0)
        tot = gp if tot is None else tot + gp
    return (tot,)


taps.defvjp(_taps_fwd, _taps_bwd)


def _gelu(x):
    return x * (0.5 * (1.0 + jnp.tanh(0.7978845608028654 * (x + 0.044715 * (x * x * x)))))


def _sigmoid(x):
    return jax.nn.sigmoid(x)


def _softplus(x):
    return jnp.maximum(x, 0.0) + jnp.log1p(jnp.exp(-jnp.abs(x)))


def _rmsnorm(x, g):
    return x * lax.rsqrt(jnp.mean(x * x, axis=-1, keepdims=True) + EPS) * g


def _lane_pick(x, lane):
    sel = lax.broadcasted_iota(jnp.int32, x.shape, 1) == lane
    return jnp.sum(jnp.where(sel, x, 0.0), axis=1, keepdims=True)


def _load_ext(ref, t0, n):
    main = ref[pl.ds(t0, n), :].astype(F32)
    hstart = pl.multiple_of(jnp.maximum(t0 - SUBLANES, 0), SUBLANES)
    halo = ref[pl.ds(hstart, SUBLANES), :].astype(F32)
    halo = jnp.where(t0 > 0, halo, 0.0)
    return jnp.concatenate([halo, main], axis=0)


def mm(name, a, b, mode, *, out_dtype=F32, res=None, a_cols=None, b_cols=None):
    a0, aw = a_cols if a_cols else (0, a.shape[1])
    b0, bw = b_cols if b_cols else (0, b.shape[1])
    if mode == "nn":
        m, k, n = a.shape[0], aw, bw
        assert b.shape[0] == k
        tm, tk, tn = _tile(m, 512), _tile(k, 512, a0), _tile(n, 1024, b0)
        a_spec = pl.BlockSpec((tm, tk), lambda i, j, kk: (i, kk + a0 // tk))
        b_spec = pl.BlockSpec((tk, tn), lambda i, j, kk: (kk, j + b0 // tn))
        dims = _NN
    elif mode == "nt":
        m, k, n = a.shape[0], aw, b.shape[0]
        assert bw == k
        tm, tn = _tile(m, 512), _tile(n, 1024)
        tk = _tile(k, 512, math.gcd(a0, b0) if (a0 or b0) else 0)
        assert a0 % tk == 0 and b0 % tk == 0
        a_spec = pl.BlockSpec((tm, tk), lambda i, j, kk: (i, kk + a0 // tk))
        b_spec = pl.BlockSpec((tn, tk), lambda i, j, kk: (j, kk + b0 // tk))
        dims = _NT
    else:
        k, m, n = a.shape[0], aw, bw
        assert b.shape[0] == k
        tk, tm, tn = _tile(k, 512), _tile(m, 512, a0), _tile(n, 1024, b0)
        a_spec = pl.BlockSpec((tk, tm), lambda i, j, kk: (kk, i + a0 // tm))
        b_spec = pl.BlockSpec((tk, tn), lambda i, j, kk: (kk, j + b0 // tn))
        dims = _TN
    nk = k // tk
    has_res = res is not None

    def body(*refs):
        a_ref, b_ref = refs[0], refs[1]
        o_ref, acc_ref = refs[-2], refs[-1]
        kk = pl.program_id(2)

        @pl.when(kk == 0)
        def _():
            acc_ref[...] = jnp.zeros_like(acc_ref)

        acc_ref[...] += lax.dot_general(a_ref[...].astype(BF16), b_ref[...].astype(BF16), (dims, ((), ())),
                                        preferred_element_type=F32)

        @pl.when(kk == nk - 1)
        def _():
            out = acc_ref[...]
            if has_res:
                out = out + refs[2][...].astype(F32)
            o_ref[...] = out.astype(o_ref.dtype)

    o_spec = pl.BlockSpec((tm, tn), lambda i, j, kk: (i, j))
    return pl.pallas_call(
        body, name=name, grid=(m // tm, n // tn, nk),
        in_specs=[a_spec, b_spec] + ([o_spec] if has_res else []), out_specs=o_spec,
        out_shape=jax.ShapeDtypeStruct((m, n), out_dtype),
        scratch_shapes=[pltpu.VMEM((tm, tn), F32)], compiler_params=_cparams(3),
    )(*((a, b, res) if has_res else (a, b)))


def _row_specs(rows, tm):
    return [pl.BlockSpec((tm, cw), lambda i, c=c0 // cw: (i, c)) for (_, c0, cw) in rows]


def rowop_fwd(name, f, rows, params, outs, tm=256):
    t = rows[0][0].shape[0]
    tm = min(tm, t)
    n_r, n_p = len(rows), len(params)

    def body(*refs):
        vals = f(*[r[...].astype(F32) for r in refs[:n_r]], *[p[...] for p in refs[n_r:n_r + n_p]])
        for o, v in zip(refs[n_r + n_p:], vals):
            o[...] = v.astype(o.dtype)

    res = pl.pallas_call(
        body, name=name, grid=(t // tm,),
        in_specs=_row_specs(rows, tm) + [_full_spec(p.shape, 1) for p in params],
        out_specs=[pl.BlockSpec((tm, cw), lambda i: (i, 0)) for cw, _ in outs],
        out_shape=[jax.ShapeDtypeStruct((t, cw), dt) for cw, dt in outs],
        compiler_params=_cparams(1),
    )(*[r[0] for r in rows], *params)
    return list(res)


def rowop_bwd(name, f, rows, params, couts, need, tm=256, add=None):
    t = rows[0][0].shape[0]
    tm = min(tm, t)
    n_r, n_p, n_c = len(rows), len(params), len(couts)
    want = [k for k in range(n_r) if need[k]]
    adds = [add] if add is not None else []

    def body(*refs):
        xs = [r[...].astype(F32) for r in refs[:n_r]]
        ps = [p[...] for p in refs[n_r:n_r + n_p]]
        cs = tuple(c[...].astype(F32) for c in refs[n_r + n_p:n_r + n_p + n_c])
        outs = refs[n_r + n_p + n_c + len(adds):]
        _, vjp = jax.vjp(f, *xs, *ps)
        gs = vjp(cs)
        for o, k in zip(outs[:len(want)], want):
            o[...] = gs[k] + refs[n_r + n_p + n_c][...] if (adds and k == want[0]) else gs[k]

        @pl.when(pl.program_id(0) == 0)
        def _():
            for o in outs[len(want):]:
                o[...] = jnp.zeros_like(o)

        for o, g in zip(outs[len(want):], gs[n_r:]):
            o[...] += g

    res = pl.pallas_call(
        body, name=name, grid=(t // tm,),
        in_specs=_row_specs(rows, tm) + [_full_spec(p.shape, 1) for p in params] + _row_specs(couts, tm)
        + _row_specs(adds, tm),
        out_specs=[pl.BlockSpec((tm, rows[k][2]), lambda i: (i, 0)) for k in want]
        + [_full_spec(p.shape, 1) for p in params],
        out_shape=[jax.ShapeDtypeStruct((t, rows[k][2]), F32) for k in want]
        + [jax.ShapeDtypeStruct(p.shape, F32) for p in params],
        compiler_params=_cparams(1),
    )(*[r[0] for r in rows], *params, *[c[0] for c in couts], *[a[0] for a in adds])
    return list(res[:len(want)]), list(res[len(want):])


def _seq_specs(seqs, s, cb, order):
    if order == "bj":
        return [pl.BlockSpec((s, cb), lambda b, j, c=c0 // cb: (b, c + j)) for (_, c0) in seqs]
    return [pl.BlockSpec((s, cb), lambda j, b, c=c0 // cb: (b, c + j)) for (_, c0) in seqs]


def _param_specs(params, order):
    if order == "bj":
        return [pl.BlockSpec(bs, lambda b, j, fn=fn: fn(j)) for (_, bs, fn) in params]
    return [pl.BlockSpec(bs, lambda j, b, fn=fn: fn(j)) for (_, bs, fn) in params]


def seqop_fwd(name, f, seqs, params, out_dtypes, *, nb, s, nblk, cb=LANES, n=256):
    n = min(n, s)
    n_s, n_p = len(seqs), len(params)

    def body(*refs):
        seq_refs, par_refs, out_refs = refs[:n_s], refs[n_s:n_s + n_p], refs[n_s + n_p:]

        def step(i, carry):
            t0 = pl.multiple_of(i * n, n)
            vals = f(t0, *[_load_ext(r, t0, n) for r in seq_refs], *[p[...] for p in par_refs])
            for o, v in zip(out_refs, vals):
                o[pl.ds(t0, n), :] = v.astype(o.dtype)
            return carry

        lax.fori_loop(0, s // n, step, 0)

    res = pl.pallas_call(
        body, name=name, grid=(nb, nblk),
        in_specs=_seq_specs(seqs, s, cb, "bj") + _param_specs(params, "bj"),
        out_specs=[pl.BlockSpec((s, cb), lambda b, j: (b, j)) for _ in out_dtypes],
        out_shape=[jax.ShapeDtypeStruct((nb * s, nblk * cb), dt) for dt in out_dtypes],
        compiler_params=_cparams(2),
    )(*[q[0] for q in seqs], *[p[0] for p in params])
    return list(res)


def seqop_bwd(name, f, seqs, params, cot_seqs, cot_fn, *, nb, s, nblk, cb=LANES, n=256):
    n = min(n, s)
    n_s, n_p, n_c = len(seqs), len(params), len(cot_seqs)
    nchunk = s // n

    def body(*refs):
        seq_refs, par_refs = refs[:n_s], refs[n_s:n_s + n_p]
        cot_refs = refs[n_s + n_p:n_s + n_p + n_c]
        dseq_refs = refs[n_s + n_p + n_c:n_s + n_p + n_c + n_s]
        dpar_refs = refs[n_s + n_p + n_c + n_s:]

        @pl.when(pl.program_id(1) == 0)
        def _():
            for o in dpar_refs:
                o[...] = jnp.zeros_like(o)

        def step(ii, halos):
            t0 = pl.multiple_of((nchunk - 1 - ii) * n, n)
            exts = [_load_ext(r, t0, n) for r in seq_refs]
            ps = [p[...] for p in par_refs]
            cots = cot_fn(t0, *[_load_ext(r, t0, n) for r in cot_refs])
            _, vjp = jax.vjp(lambda *args: f(t0, *args), *exts, *ps)
            gs = vjp(tuple(cots))
            tail = pl.multiple_of(t0 + n - SUBLANES, SUBLANES)
            new_halos = []
            for o, g, h in zip(dseq_refs, gs[:n_s], halos):
                o[pl.ds(t0, n), :] = g[SUBLANES:]
                o[pl.ds(tail, SUBLANES), :] += h
                new_halos.append(g[:SUBLANES])
            for o, g in zip(dpar_refs, gs[n_s:]):
                o[...] += g
            return tuple(new_halos)

        lax.fori_loop(0, nchunk, step, tuple(jnp.zeros((SUBLANES, cb), F32) for _ in range(n_s)))

    res = pl.pallas_call(
        body, name=name, grid=(nblk, nb),
        in_specs=_seq_specs(seqs, s, cb, "jb") + _param_specs(params, "jb") + _seq_specs(cot_seqs, s, cb, "jb"),
        out_specs=[pl.BlockSpec((s, cb), lambda j, b: (b, j)) for _ in seqs] + _param_specs(params, "jb"),
        out_shape=[jax.ShapeDtypeStruct((nb * s, nblk * cb), F32) for _ in seqs]
        + [jax.ShapeDtypeStruct(p[0].shape, F32) for p in params],
        compiler_params=_cparams(2),
    )(*[q[0] for q in seqs], *[p[0] for p in params], *[q[0] for q in cot_seqs])
    return list(res[:n_s]), list(res[n_s:])


def rscan(name, a, b, reverse, *, nb, s, cb=256):
    c = a.shape[1]
    cb = min(cb, c)

    def body(a_ref, b_ref, h_ref):
        def step(ii, car):
            i = (s // SUBLANES - 1 - ii) if reverse else ii
            base = pl.multiple_of(i * SUBLANES, SUBLANES)
            for r in (reversed(range(SUBLANES)) if reverse else range(SUBLANES)):
                row = pl.ds(base + r, 1)
                if reverse:
                    g = b_ref[row, :] + car
                    h_ref[row, :] = g
                    car = a_ref[row, :] * g
                else:
                    car = a_ref[row, :] * car + b_ref[row, :]
                    h_ref[row, :] = car
            return car

        lax.fori_loop(0, s // SUBLANES, step, jnp.zeros((1, cb), F32))

    spec = pl.BlockSpec((s, cb), lambda bb, j: (bb, j))
    return pl.pallas_call(
        body, name=name, grid=(nb, c // cb), in_specs=[spec, spec], out_specs=spec,
        out_shape=jax.ShapeDtypeStruct(a.shape, F32), compiler_params=_cparams(2),
    )(a, b)


def cscan(name, bu, ar, ai, reverse, *, nb, s, cb=256):
    p = ar.shape[1]
    cb = min(cb, p)
    nj = p // cb

    def body(br_ref, bi_ref, ar_ref, ai_ref, hr_ref, hi_ref):
        lr = ar_ref[...]
        li = -ai_ref[...] if reverse else ai_ref[...]

        def step(ii, car):
            hr, hi = car
            i = (s // SUBLANES - 1 - ii) if reverse else ii
            base = pl.multiple_of(i * SUBLANES, SUBLANES)
            for r in (reversed(range(SUBLANES)) if reverse else range(SUBLANES)):
                row = pl.ds(base + r, 1)
                nr = lr * hr - li * hi + br_ref[row, :]
                ni = lr * hi + li * hr + bi_ref[row, :]
                hr, hi = nr, ni
                hr_ref[row, :] = hr
                hi_ref[row, :] = hi
            return hr, hi

        z = jnp.zeros((1, cb), F32)
        lax.fori_loop(0, s // SUBLANES, step, (z, z))

    o_spec = pl.BlockSpec((s, cb), lambda bb, j: (bb, j))
    l_spec = pl.BlockSpec((1, cb), lambda bb, j: (0, j))
    res = pl.pallas_call(
        body, name=name, grid=(nb, nj),
        in_specs=[o_spec, pl.BlockSpec((s, cb), lambda bb, j: (bb, j + nj)), l_spec, l_spec],
        out_specs=[o_spec, o_spec],
        out_shape=[jax.ShapeDtypeStruct((nb * s, p), F32)] * 2, compiler_params=_cparams(2),
    )(bu, bu, ar, ai)
    return res[0], res[1]


def _norm_f(x, g):
    return (_rmsnorm(x, g),)


def _rglru_pre_f(t0, xext, w0, w1, w2, w3, cb, wr, wi, br, bi, lam):
    x0, x1, x2, x3 = taps(xext, 4)
    xc = w3 * x0 + w2 * x1 + w1 * x2 + w0 * x3 + cb
    r = _sigmoid(mm_nn(xc, wr[0]) + br)
    ig = _sigmoid(mm_nn(xc, wi[0]) + bi)
    log_a = -RG_C * r * _softplus(-lam)
    a = jnp.exp(log_a)
    first = (lax.broadcasted_iota(jnp.int32, a.shape, 0) + t0) == 0
    mult = jnp.where(first, 1.0, jnp.sqrt(1.0 - jnp.exp(2.0 * log_a)))
    return a, mult * ig * xc


def _ffn_act_f(t0, gext, vext, g0, g1, g2, v0, v1, v2, bg, bv):
    ga, gb, gc = taps(gext, 3)
    va, vb, vc = taps(vext, 3)
    ug = g2 * ga + g1 * gb + g0 * gc + bg
    uv = v2 * va + v1 * vb + v0 * vc + bv
    return (_gelu(ug) * uv,)


def _s5_post_f(ys, uc, d, gw, gb):
    yc = _gelu(ys + d * uc)
    return (yc * _sigmoid(mm_nn(yc, gw) + gb),)


def _merge_f(ha, ga, yb, yc, g0, g1, g2, wb0, wb1, wb2, bg0, bg1, bg2):
    ya = ha * _gelu(ga)
    out = _sigmoid(g0 + bg0) * mm_nn(ya, wb0)
    out = out + _sigmoid(g1 + bg1) * mm_nn(yb, wb1)
    out = out + _sigmoid(g2 + bg2) * mm_nn(yc, wb2)
    return (out,)


def _s5_param_f(lr, li, ldt, btr, bti):
    dt = jnp.exp(ldt)
    mag = jnp.exp(lr * dt)
    ar, ai = mag * jnp.cos(li * dt), mag * jnp.sin(li * dt)
    den = lr * lr + li * li
    fr = ((ar - 1.0) * lr + ai * li) / den
    fi = (ai * lr - (ar - 1.0) * li) / den
    return ar, ai, fr * btr - fi * bti, fr * bti + fi * btr


def _tri_inv(a):
    c = a.shape[0]
    ri = lax.broadcasted_iota(jnp.int32, (c, c), 0)
    ci = lax.broadcasted_iota(jnp.int32, (c, c), 1)
    eye = jnp.where(ri == ci, 1.0, 0.0).astype(F32)
    ad = jnp.where((ri >> 4) == (ci >> 4), a, 0.0)
    ao = a - ad
    a2 = mm_nn(ad, ad, True)
    a4 = mm_nn(a2, a2, True)
    a8 = mm_nn(a4, a4, True)
    td = mm_nn(mm_nn(eye - ad, eye + a2, True), mm_nn(eye + a4, eye + a8, True), True)
    nn_ = mm_nn(td, ao, True)
    n2 = mm_nn(nn_, nn_, True)
    return mm_nn(mm_nn(eye - nn_, eye + n2, True), td, True)


def _dn_chunk_f(state, qe, ke, ve, z, ba, q0, q1, q2, q3, k0, k1, k2, k3, v0, v1, v2, v3, alog, dtb, nw):
    c = DN_CHUNK
    dk = qe.shape[1]

    def conv_silu(xe, w0, w1, w2, w3):
        x0, x1, x2, x3 = taps(xe, 4)
        u = w3 * x0 + w2 * x1 + w1 * x2 + w0 * x3
        return u * _sigmoid(u)

    def l2n(x):
        return x * lax.rsqrt(jnp.sum(x * x, axis=-1, keepdims=True) + EPS)

    q = l2n(conv_silu(qe, q0, q1, q2, q3)) * (dk ** -0.5)
    k = l2n(conv_silu(ke, k0, k1, k2, k3))
    v = conv_silu(ve, v0, v1, v2, v3)
    beta = _sigmoid(_lane_pick(ba, 0))
    g = -jnp.exp(_lane_pick(alog, 0)) * _softplus(_lane_pick(ba, 1) + _lane_pick(dtb, 0))
    ri = lax.broadcasted_iota(jnp.int32, (c, c), 0)
    ci = lax.broadcasted_iota(jnp.int32, (c, c), 1)
    incl, strict = ri >= ci, ri > ci
    ltri = jnp.where(incl, 1.0, 0.0).astype(F32)
    gc_w = mm_nn(ltri, jnp.broadcast_to(g, (c, dk)), True)
    gc_c = mm_nn(ltri, jnp.broadcast_to(g, (c, c)), True)
    gc_r = mm_nt(jnp.full((c, dk), 1.0 / dk, F32), gc_w, True)
    gtot = jnp.sum(g, axis=0, keepdims=True)
    decay = jnp.exp(jnp.where(incl, gc_c - gc_r, -1e30))
    e_gc = jnp.exp(gc_w)
    kb = k * beta
    a_mat = jnp.where(strict, mm_nt(kb, k) * decay, 0.0)
    t_inv = _tri_inv(a_mat)
    u = mm_nn(t_inv, v * beta, True)
    w = mm_nn(t_inv, kb * e_gc, True)
    qk = jnp.where(incl, mm_nt(q, k) * decay, 0.0)
    v_new = u - mm_nn(w, state)
    o = mm_nn(q * e_gc, state) + mm_nn(qk, v_new)
    new_state = state * jnp.exp(gtot) + mm_tn(k * jnp.exp(gtot - gc_w), v_new)
    o = _rmsnorm(o, nw) * (z * _sigmoid(z))
    return o, new_state


def _attn_f(q, k, v):
    sc = mm_nt(q, k) * (q.shape[1] ** -0.5)
    sc = sc - jnp.max(sc, axis=-1, keepdims=True)
    e = jnp.exp(sc)
    p = e / jnp.sum(e, axis=-1, keepdims=True)
    return (mm_nn(p, v),)


def _dn_specs(proj_cols, s, nh, order):
    if order == "hb":
        return [pl.BlockSpec((s, LANES), lambda h, b, c=c0 // LANES: (b, c + h)) for c0 in proj_cols]
    raise ValueError(order)


def deltanet_fwd(proj, cols, taps12, alog, dtb, nw, *, nb, s, nh):
    nc = s // DN_CHUNK
    head_par = pl.BlockSpec((1, LANES), lambda h, b: (0, h))

    def body(*refs):
        q_ref, k_ref, v_ref, z_ref, ba_ref = refs[:5]
        par_refs = refs[5:5 + 15]
        o_ref, st_ref = refs[20], refs[21]

        def step(i, state):
            t0 = pl.multiple_of(i * DN_CHUNK, DN_CHUNK)
            st_ref[0, 0, i] = state
            rows = pl.ds(t0, DN_CHUNK)
            o, new_state = _dn_chunk_f(
                state, _load_ext(q_ref, t0, DN_CHUNK), _load_ext(k_ref, t0, DN_CHUNK), _load_ext(v_ref, t0, DN_CHUNK),
                z_ref[rows, :], ba_ref[rows, :], *[p[...] for p in par_refs])
            o_ref[rows, :] = o
            return new_state

        lax.fori_loop(0, nc, step, jnp.zeros((LANES, LANES), F32))

    res = pl.pallas_call(
        body, name="deltanet_fwd", grid=(nh, nb),
        in_specs=_dn_specs(cols, s, nh, "hb") + [head_par] * 14 + [pl.BlockSpec((1, LANES), lambda h, b: (0, 0))],
        out_specs=[pl.BlockSpec((s, LANES), lambda h, b: (b, h)),
                   pl.BlockSpec((1, 1, nc, LANES, LANES), lambda h, b: (b, h, 0, 0, 0))],
        out_shape=[jax.ShapeDtypeStruct((nb * s, nh * LANES), F32),
                   jax.ShapeDtypeStruct((nb, nh, nc, LANES, LANES), F32)],
        compiler_params=_cparams(2),
    )(proj, proj, proj, proj, proj, *taps12, alog, dtb, nw)
    return res[0], res[1]


def deltanet_bwd(proj, cols, taps12, alog, dtb, nw, states, dyb, *, nb, s, nh):
    nc = s // DN_CHUNK
    head_par = pl.BlockSpec((1, LANES), lambda h, b: (0, h))
    norm_par = pl.BlockSpec((1, LANES), lambda h, b: (0, 0))
    seq_out = pl.BlockSpec((s, LANES), lambda h, b: (b, h))

    def body(*refs):
        q_ref, k_ref, v_ref, z_ref, ba_ref = refs[:5]
        par_refs = refs[5:20]
        st_ref, do_ref = refs[20], refs[21]
        dq_ref, dk_ref, dv_ref, dz_ref, dba_ref = refs[22:27]
        dpar_refs = refs[27:42]
        h_id, b_id = pl.program_id(0), pl.program_id(1)

        @pl.when(b_id == 0)
        def _():
            for o in dpar_refs[:14]:
                o[...] = jnp.zeros_like(o)

        @pl.when((b_id == 0) & (h_id == 0))
        def _():
            dpar_refs[14][...] = jnp.zeros_like(dpar_refs[14])

        def step(ii, car):
            d_state, halos = car[0], car[1:]
            i = nc - 1 - ii
            t0 = pl.multiple_of(i * DN_CHUNK, DN_CHUNK)
            rows = pl.ds(t0, DN_CHUNK)
            args = [st_ref[0, 0, i], _load_ext(q_ref, t0, DN_CHUNK), _load_ext(k_ref, t0, DN_CHUNK),
                    _load_ext(v_ref, t0, DN_CHUNK), z_ref[rows, :], ba_ref[rows, :]] + [p[...] for p in par_refs]
            _, vjp = jax.vjp(_dn_chunk_f, *args)
            gs = vjp((do_ref[rows, :], d_state))
            tail = pl.multiple_of(t0 + DN_CHUNK - SUBLANES, SUBLANES)
            new_halos = []
            for o, g, hl in zip((dq_ref, dk_ref, dv_ref), gs[1:4], halos):
                o[rows, :] = g[SUBLANES:]
                o[pl.ds(tail, SUBLANES), :] += hl
                new_halos.append(g[:SUBLANES])
            dz_ref[rows, :] = gs[4]
            dba_ref[rows, :] = gs[5]
            for o, g in zip(dpar_refs, gs[6:]):
                o[...] += g
            return (gs[0], *new_halos)

        zero_halo = jnp.zeros((SUBLANES, LANES), F32)
        lax.fori_loop(0, nc, step, (jnp.zeros((LANES, LANES), F32), zero_halo, zero_halo, zero_halo))

    res = pl.pallas_call(
        body, name="deltanet_bwd", grid=(nh, nb),
        in_specs=_dn_specs(cols, s, nh, "hb") + [head_par] * 14 + [norm_par]
        + [pl.BlockSpec((1, 1, nc, LANES, LANES), lambda h, b: (b, h, 0, 0, 0)), seq_out],
        out_specs=[seq_out] * 5 + [head_par] * 14 + [norm_par],
        out_shape=[jax.ShapeDtypeStruct((nb * s, nh * LANES), F32)] * 5
        + [jax.ShapeDtypeStruct((1, nh * LANES), F32)] * 14 + [jax.ShapeDtypeStruct((1, LANES), F32)],
        compiler_params=_cparams(2),
    )(proj, proj, proj, proj, proj, *taps12, alog, dtb, nw, states, dyb)
    return list(res[:5]), list(res[5:])


def attn_fwd(q, kv, *, nb, s, m, nh, tq=512):
    hd = q.shape[1] // nh
    tq = min(tq, s)
    nq = s // tq

    def body(q_ref, k_ref, v_ref, o_ref):
        o_ref[...] = _attn_f(q_ref[...].astype(F32), k_ref[...], v_ref[...])[0].astype(o_ref.dtype)

    return pl.pallas_call(
        body, name="attn_fwd", grid=(nb, nh, nq),
        in_specs=[pl.BlockSpec((tq, hd), lambda b, h, i: (b * nq + i, h)),
                  pl.BlockSpec((m, hd), lambda b, h, i: (b, h)),
                  pl.BlockSpec((m, hd), lambda b, h, i: (b, nh + h))],
        out_specs=pl.BlockSpec((tq, hd), lambda b, h, i: (b * nq + i, h)),
        out_shape=jax.ShapeDtypeStruct(q.shape, BF16), compiler_params=_cparams(3),
    )(q, kv, kv)


def attn_bwd(q, kv, do, *, nb, s, m, nh, tq=512):
    hd = q.shape[1] // nh
    tq = min(tq, s)
    nq = s // tq

    def body(q_ref, k_ref, v_ref, do_ref, dq_ref, dk_ref, dv_ref):
        _, vjp = jax.vjp(_attn_f, q_ref[...].astype(F32), k_ref[...], v_ref[...])
        dq, dk, dv = vjp((do_ref[...].astype(F32),))
        dq_ref[...] = dq

        @pl.when(pl.program_id(2) == 0)
        def _():
            dk_ref[...] = jnp.zeros_like(dk_ref)
            dv_ref[...] = jnp.zeros_like(dv_ref)

        dk_ref[...] += dk
        dv_ref[...] += dv

    q_spec = pl.BlockSpec((tq, hd), lambda b, h, i: (b * nq + i, h))
    m_spec = pl.BlockSpec((m, hd), lambda b, h, i: (b, h))
    res = pl.pallas_call(
        body, name="attn_bwd", grid=(nb, nh, nq),
        in_specs=[q_spec, m_spec, pl.BlockSpec((m, hd), lambda b, h, i: (b, nh + h)), q_spec],
        out_specs=[q_spec, m_spec, m_spec],
        out_shape=[jax.ShapeDtypeStruct(q.shape, F32), jax.ShapeDtypeStruct((nb * m, nh * hd), F32),
                   jax.ShapeDtypeStruct((nb * m, nh * hd), F32)],
        compiler_params=_cparams(3),
    )(q, kv, kv, do)
    return res[0], res[1], res[2]


def loss_head(x, target, gain, tm=256):
    t, d = x.shape
    tm = min(tm, t)

    def body(x_ref, t_ref, g_ref, l_ref, dx_ref, dg_ref):
        tgt = t_ref[...]

        def lf(xv, gv):
            e = _rmsnorm(xv, gv) - tgt
            return 0.5 * jnp.sum(jnp.mean(e * e, axis=-1, keepdims=True), axis=0, keepdims=True)

        lv, vjp = jax.vjp(lf, x_ref[...], g_ref[...])
        dx, dg = vjp(jnp.ones((1, 1), F32))
        dx_ref[...] = dx

        @pl.when(pl.program_id(0) == 0)
        def _():
            l_ref[...] = jnp.zeros_like(l_ref)
            dg_ref[...] = jnp.zeros_like(dg_ref)

        l_ref[...] += jnp.broadcast_to(lv, l_ref.shape)
        dg_ref[...] += dg

    row = pl.BlockSpec((tm, d), lambda i: (i, 0))
    res = pl.pallas_call(
        body, name="loss_head", grid=(t // tm,),
        in_specs=[row, row, _full_spec((1, d), 1)],
        out_specs=[_full_spec((1, LANES), 1), row, _full_spec((1, d), 1)],
        out_shape=[jax.ShapeDtypeStruct((1, LANES), F32), jax.ShapeDtypeStruct((t, d), F32),
                   jax.ShapeDtypeStruct((1, d), F32)],
        compiler_params=_cparams(1),
    )(x, target, gain)
    return res[0], res[1], res[2]


def s5_dlam(gr, gi, hr, hi, *, nb, s, cb=256, n=256):
    p = gr.shape[1]
    cb, n = min(cb, p), min(n, s)

    def body(gr_ref, gi_ref, hr_ref, hi_ref, dar_ref, dai_ref):
        @pl.when(pl.program_id(1) == 0)
        def _():
            dar_ref[...] = jnp.zeros_like(dar_ref)
            dai_ref[...] = jnp.zeros_like(dai_ref)

        def step(i, carry):
            t0 = pl.multiple_of(i * n, n)
            g_r, g_i = gr_ref[pl.ds(t0, n), :], gi_ref[pl.ds(t0, n), :]
            p_r = taps(_load_ext(hr_ref, t0, n), 2)[1]
            p_i = taps(_load_ext(hi_ref, t0, n), 2)[1]
            dar_ref[...] += jnp.sum(g_r * p_r + g_i * p_i, axis=0, keepdims=True)
            dai_ref[...] += jnp.sum(g_i * p_r - g_r * p_i, axis=0, keepdims=True)
            return carry

        lax.fori_loop(0, s // n, step, 0)

    blk = pl.BlockSpec((s, cb), lambda j, b: (b, j))
    acc = pl.BlockSpec((1, cb), lambda j, b: (0, j))
    res = pl.pallas_call(
        body, name="s5_dlam", grid=(p // cb, nb), in_specs=[blk] * 4, out_specs=[acc, acc],
        out_shape=[jax.ShapeDtypeStruct((1, p), F32)] * 2, compiler_params=_cparams(2),
    )(gr, gi, hr, hi)
    return res[0], res[1]


def s5_param_fwd(lr, li, ldt, btr, bti):
    def body(*refs):
        vals = _s5_param_f(*[r[...] for r in refs[:5]])
        for o, v in zip(refs[5:], vals):
            o[...] = v

    return pl.pallas_call(
        body, name="s5_param_fwd",
        out_shape=[jax.ShapeDtypeStruct(lr.shape, F32)] * 2 + [jax.ShapeDtypeStruct(btr.shape, F32)] * 2,
    )(lr, li, ldt, btr, bti)


def s5_param_bwd(lr, li, ldt, btr, bti, cots):
    def body(*refs):
        _, vjp = jax.vjp(_s5_param_f, *[r[...] for r in refs[:5]])
        gs = vjp(tuple(r[...] for r in refs[5:9]))
        for o, g in zip(refs[9:], gs):
            o[...] = g

    return pl.pallas_call(
        body, name="s5_param_bwd",
        out_shape=[jax.ShapeDtypeStruct(a.shape, F32) for a in (lr, li, ldt, btr, bti)],
    )(lr, li, ldt, btr, bti, *cots)


def sumk(name, buf, rb=512):
    k, r, _ = buf.shape
    rb = _tile(r, rb) if r % LANES == 0 else r

    def body(b_ref, o_ref):
        acc = b_ref[0]
        for i in range(1, k):
            acc = acc + b_ref[i]
        o_ref[...] = acc

    return pl.pallas_call(
        body, name=name, grid=(r // rb,),
        in_specs=[pl.BlockSpec((k, rb, LANES), lambda i: (0, i, 0))],
        out_specs=pl.BlockSpec((rb, LANES), lambda i: (i, 0)),
        out_shape=jax.ShapeDtypeStruct((r, LANES), F32), compiler_params=_cparams(1),
    )(buf)


def adamw(name, w, g, m, v, rb=1024):
    r = w.shape[0]
    rb = _tile(r, rb) if r % LANES == 0 else r

    def body(w_ref, g_ref, m_ref, v_ref, d_ref, nm_ref, nv_ref):
        gv = g_ref[...]
        nm = ADAM_B1 * m_ref[...] + (1.0 - ADAM_B1) * gv
        nv = ADAM_B2 * v_ref[...] + (1.0 - ADAM_B2) * (gv * gv)
        m_hat = nm / (1.0 - ADAM_B1 ** ADAM_STEP)
        v_hat = nv / (1.0 - ADAM_B2 ** ADAM_STEP)
        d_ref[...] = -ADAM_LR * (m_hat / (jnp.sqrt(v_hat) + ADAM_EPS) + ADAM_WD * w_ref[...])
        nm_ref[...] = nm
        nv_ref[...] = nv

    spec = pl.BlockSpec((rb, LANES), lambda i: (i, 0))
    res = pl.pallas_call(
        body, name=name, grid=(r // rb,), in_specs=[spec] * 4, out_specs=[spec] * 3,
        out_shape=[jax.ShapeDtypeStruct(w.shape, F32)] * 3, compiler_params=_cparams(1),
    )(w, g, m, v)
    return res[0], res[1], res[2]


def _place():
    x, y, c = lax.axis_index("x"), lax.axis_index("y"), lax.axis_index("c")
    chips = [(1 - x, y), (x, 1 - y), (1 - x, 1 - y)]
    return x, y, c, chips


def _chip_id(px, py):
    return 2 * px + py


def _rcopy(src, dst, send_sems, recv_sems, k, to):
    return pltpu.make_async_remote_copy(src_ref=src, dst_ref=dst, send_sem=send_sems.at[k], recv_sem=recv_sems.at[k],
                                        device_id=to, device_id_type=MESH)


def _comm_call(name, body, out_shape, n_sems, *args):
    return pl.pallas_call(
        body, name=name, out_shape=out_shape, in_specs=[ANY] * len(args), out_specs=ANY,
        scratch_shapes=[pltpu.SemaphoreType.DMA((n_sems,)), pltpu.SemaphoreType.DMA((n_sems,)),
                        pltpu.SemaphoreType.DMA],
    )(*args)


def gather_chips(name, slab):
    def body(x_ref, out_ref, send_sems, recv_sems, local_sem):
        x, y, c, chips = _place()
        me, sib = _chip_id(x, y), (x, y, 1 - c)
        mine = pltpu.make_async_copy(x_ref, out_ref.at[me], local_sem)
        mine.start()
        first = [_rcopy(x_ref.at[c], out_ref.at[me, c], send_sems, recv_sems, j, (*chip, c))
                 for j, chip in enumerate(chips)]
        for cp in first:
            cp.start()
        passed = []
        for j, chip in enumerate(chips):
            landed = out_ref.at[_chip_id(*chip), c]
            _rcopy(x_ref.at[c], landed, send_sems, recv_sems, j, sib).wait_recv()
            passed.append(_rcopy(landed, landed, send_sems, recv_sems, 3 + j, sib))
            passed[-1].start()
        for j, chip in enumerate(chips):
            _rcopy(x_ref.at[c], out_ref.at[_chip_id(*chip), 1 - c], send_sems, recv_sems, 3 + j, sib).wait_recv()
        for cp in first + passed:
            cp.wait_send()
        mine.wait()

    return _comm_call(name, body, jax.ShapeDtypeStruct((N_CHIPS,) + slab.shape, slab.dtype), 6, slab)


def swap_halves(name, g):
    def body(g_ref, out_ref, send_sems, recv_sems, local_sem):
        x, y, c, _ = _place()
        mine = pltpu.make_async_copy(g_ref.at[c], out_ref.at[c], local_sem)
        mine.start()
        cp = _rcopy(g_ref.at[1 - c], out_ref.at[c], send_sems, recv_sems, 0, (x, y, 1 - c))
        cp.start()
        _rcopy(g_ref.at[1 - c], out_ref.at[1 - c], send_sems, recv_sems, 0, (x, y, 1 - c)).wait_recv()
        cp.wait_send()
        mine.wait()

    return _comm_call(name, body, jax.ShapeDtypeStruct(g.shape, g.dtype), 1, g)


def scatter_chips(name, p):
    def body(p_ref, out_ref, send_sems, recv_sems, local_sem):
        x, y, c, chips = _place()
        me = _chip_id(x, y)
        mine = pltpu.make_async_copy(p_ref.at[me], out_ref.at[me], local_sem)
        mine.start()
        sent = [_rcopy(p_ref.at[_chip_id(*chip)], out_ref.at[me], send_sems, recv_sems, j, (*chip, c))
                for j, chip in enumerate(chips)]
        for cp in sent:
            cp.start()
        for j, chip in enumerate(chips):
            _rcopy(p_ref.at[me], out_ref.at[_chip_id(*chip)], send_sems, recv_sems, j, (*chip, c)).wait_recv()
        for cp in sent:
            cp.wait_send()
        mine.wait()

    return _comm_call(name, body, jax.ShapeDtypeStruct(p.shape, p.dtype), 3, p)


def join_halves(name, r):
    def body(r_ref, out_ref, send_sems, recv_sems, local_sem):
        x, y, c, _ = _place()
        mine = pltpu.make_async_copy(r_ref, out_ref.at[c], local_sem)
        mine.start()
        cp = _rcopy(r_ref, out_ref.at[c], send_sems, recv_sems, 0, (x, y, 1 - c))
        cp.start()
        _rcopy(r_ref, out_ref.at[1 - c], send_sems, recv_sems, 0, (x, y, 1 - c)).wait_recv()
        cp.wait_send()
        mine.wait()

    return _comm_call(name, body, jax.ShapeDtypeStruct((2,) + r.shape, r.dtype), 1, r)


def gather_all(name, slab):
    def body(x_ref, out_ref, send_sems, recv_sems, local_sem):
        x, y, c, chips = _place()
        me, sib = (x, y, c), (x, y, 1 - c)

        def rows(px, py, pc):
            return out_ref.at[4 * px + 2 * py + pc]

        mine = pltpu.make_async_copy(x_ref, rows(*me), local_sem)
        mine.start()
        first = [_rcopy(x_ref, rows(*me), send_sems, recv_sems, 0, sib)]
        first += [_rcopy(x_ref, rows(*me), send_sems, recv_sems, 1 + j, (*chip, c)) for j, chip in enumerate(chips)]
        for cp in first:
            cp.start()
        passed = []
        for j, chip in enumerate(chips):
            landed = rows(*chip, c)
            _rcopy(x_ref, landed, send_sems, recv_sems, 1 + j, sib).wait_recv()
            passed.append(_rcopy(landed, landed, send_sems, recv_sems, 4 + j, sib))
            passed[-1].start()
        _rcopy(x_ref, rows(*sib), send_sems, recv_sems, 0, sib).wait_recv()
        for j, chip in enumerate(chips):
            _rcopy(x_ref, rows(*chip, 1 - c), send_sems, recv_sems, 4 + j, sib).wait_recv()
        for cp in first + passed:
            cp.wait_send()
        mine.wait()

    return _comm_call(name, body, jax.ShapeDtypeStruct((8,) + slab.shape, slab.dtype), 7, slab)


D, BW, NH_B, NG, NP_S5, CG = 1024, 512, 4, 32, 64, 16
P_S5 = NG * NP_S5
C_GATES, C_XA, C_GA, C_Q, C_K, C_V, C_Z, C_UC, C_BA, N_PROJ = 0, 3072, 3584, 4096, 4608, 5120, 5632, 6144, 6656, 7168
_IN_OFFS = (0, 512, 1024, 1536, 2048, 2560, 3072, 3076, 3080, 3592, 6664)


def _blockdiag2(w):
    z = jnp.zeros((w.shape[0] // 2, 2 * w.shape[1], 2 * w.shape[2]), w.dtype)
    return z.at[:, :64, :64].set(w[0::2]).at[:, 64:, 64:].set(w[1::2])


def _unblockdiag2(d):
    return jnp.stack([d[:, :64, :64], d[:, 64:, 64:]], axis=1).reshape(-1, 64, 64)


def _layer_params(w, l):
    p = {}
    wi = w["w_in"][l]
    xa, ga, q, k, v, z, beta, alpha, uc, gates = [wi[:, a:b] for a, b in zip(_IN_OFFS[:-1], _IN_OFFS[1:])]
    ba = jnp.zeros((D, NH_B, LANES), wi.dtype).at[:, :, 0].set(beta).at[:, :, 1].set(alpha).reshape(D, NH_B * LANES)
    p["w_in_pad"] = jnp.concatenate([gates, xa, ga, q, k, v, z, uc, ba], axis=1)
    row = lambda a: a.reshape(1, -1).astype(F32)
    acw, bcw, fcw = w["a_conv_w"][l], w["b_conv_w"][l], w["ffn_conv_w"][l]
    p["a_taps"] = [row(acw[i]) for i in range(4)]
    p["a_cb"], p["a_br"], p["a_bi"], p["a_lam"] = (row(w[n][l]) for n in ("a_conv_b", "a_b_r", "a_b_i", "a_lam"))
    p["a_wr"], p["a_wi"] = _blockdiag2(w["a_w_r"][l]), _blockdiag2(w["a_w_i"][l])
    p["b_taps"] = [row(bcw[i, j * BW:(j + 1) * BW]) for j in range(3) for i in range(4)]
    p["b_alog"] = row(jnp.repeat(w["b_a_log"][l], LANES))
    p["b_dtb"] = row(jnp.repeat(w["b_dt_bias"][l], LANES))
    p["b_nw"] = row(w["b_norm"][l])
    p["c_lr"], p["c_li"] = w["c_lam_re"][l][:, None, :], w["c_lam_im"][l][:, None, :]
    p["c_ldt"] = w["c_log_dt"][l][:, None, None]
    p["c_btr"], p["c_bti"] = w["c_b_re"][l].transpose(0, 2, 1), w["c_b_im"][l].transpose(0, 2, 1)
    eye = jnp.eye(NG, dtype=F32)
    p["eye"] = eye
    expand = lambda t: (t[:, :, None, :] * eye[:, None, :, None]).reshape(t.shape[0] * t.shape[1], -1)
    p["expand"] = expand
    p["c_cre"] = expand(w["c_c_re"][l].transpose(0, 2, 1))
    p["c_cimn"] = expand(-w["c_c_im"][l].transpose(0, 2, 1))
    p["c_d"], p["c_glu_b"] = row(w["c_d"][l]), row(w["c_glu_b"][l])
    p["c_glu_w"] = w["c_glu_w"][l].astype(F32)
    p["wb"] = [w["w_branch"][l, k].astype(F32) for k in range(3)]
    p["bg"] = [row(w["b_gate"][l, k * D:(k + 1) * D]) for k in range(3)]
    for n in ("w_out", "xa_w_q", "xa_w_kv", "xa_w_o", "ffn_w_up", "ffn_w_down"):
        p[n] = w[n][l]
    for n in ("mix_norm", "xa_norm", "mem_norm", "ffn_norm"):
        p[n] = row(w[n][l])
    dff = fcw.shape[1] // 2
    p["f_taps"] = [row(fcw[i, :dff]) for i in range(3)] + [row(fcw[i, dff:]) for i in range(3)]
    p["f_b"] = [row(w["ffn_conv_b"][l, :dff]), row(w["ffn_conv_b"][l, dff:])]
    return p


def _chan(arr, cb=LANES):
    return (arr, (1, cb), lambda j: (0, j))


def _a_params(p):
    blk = lambda a: (a, (1, LANES, LANES), lambda j: (j, 0, 0))
    return [_chan(t) for t in p["a_taps"]] + [_chan(p["a_cb"]), blk(p["a_wr"]), blk(p["a_wi"]),
                                              _chan(p["a_br"]), _chan(p["a_bi"]), _chan(p["a_lam"])]


def _f_params(p):
    return [_chan(t) for t in p["f_taps"]] + [_chan(b) for b in p["f_b"]]


def _layer_fwd(x, mem, p, nb, s, m):
    sv = {"x": x}
    h = rowop_fwd("norm_mix", _norm_f, [(x, 0, D)], [p["mix_norm"]], [(D, BF16)])[0]
    proj = mm("in_proj", h, p["w_in_pad"], "nn")
    a, b = seqop_fwd("rglru_pre", _rglru_pre_f, [(proj, C_XA)], _a_params(p), [F32, F32], nb=nb, s=s, nblk=BW // LANES)
    ha = rscan("rglru_scan", a, b, False, nb=nb, s=s)
    yb, states = deltanet_fwd(proj, (C_Q, C_K, C_V, C_Z, C_BA), p["b_taps"], p["b_alog"], p["b_dtb"], p["b_nw"],
                              nb=nb, s=s, nh=NH_B)
    ar, ai, bbr, bbi = s5_param_fwd(p["c_lr"], p["c_li"], p["c_ldt"], p["c_btr"], p["c_bti"])
    bd = jnp.concatenate([p["expand"](bbr), p["expand"](bbi)], axis=1)
    ar2, ai2 = ar.reshape(1, P_S5), ai.reshape(1, P_S5)
    bu = mm("s5_bu", proj, bd, "nn", a_cols=(C_UC, BW))
    hr, hi = cscan("s5_scan", bu, ar2, ai2, False, nb=nb, s=s)
    ys = mm("s5_y_im", hi, p["c_cimn"], "nn", res=mm("s5_y_re", hr, p["c_cre"], "nn"))
    yc = rowop_fwd("s5_post", _s5_post_f, [(ys, 0, BW), (proj, C_UC, BW)], [p["c_d"], p["c_glu_w"], p["c_glu_b"]],
                   [(BW, F32)])[0]
    merge_rows = [(ha, 0, BW), (proj, C_GA, BW), (yb, 0, BW), (yc, 0, BW),
                  (proj, 0, D), (proj, D, D), (proj, 2 * D, D)]
    merged = rowop_fwd("merge", _merge_f, merge_rows, p["wb"] + p["bg"], [(D, BF16)])[0]
    x1 = mm("out_proj", merged, p["w_out"], "nn", res=x)
    hq = rowop_fwd("norm_xa", _norm_f, [(x1, 0, D)], [p["xa_norm"]], [(D, BF16)])[0]
    q = mm("xa_q", hq, p["xa_w_q"], "nn", out_dtype=BF16)
    memn = rowop_fwd("norm_mem", _norm_f, [(mem, 0, D)], [p["mem_norm"]], [(D, BF16)])[0]
    kv = mm("xa_kv", memn, p["xa_w_kv"], "nn")
    o = attn_fwd(q, kv, nb=nb, s=s, m=m, nh=4)
    x2 = mm("xa_o", o, p["xa_w_o"], "nn", res=x1)
    hf = rowop_fwd("norm_ffn", _norm_f, [(x2, 0, D)], [p["ffn_norm"]], [(D, BF16)])[0]
    up = mm("ffn_up", hf, p["ffn_w_up"], "nn")
    dff = up.shape[1] // 2
    act = seqop_fwd("ffn_act", _ffn_act_f, [(up, 0), (up, dff)], _f_params(p), [BF16], nb=nb, s=s,
                    nblk=dff // LANES)[0]
    x3 = mm("ffn_down", act, p["ffn_w_down"], "nn", res=x2)
    sv.update(h=h, proj=proj, a=a, ha=ha, yb=yb, states=states, ar=ar2, ai=ai2, bd=bd, hr=hr, hi=hi, ys=ys, yc=yc,
              merged=merged, x1=x1, hq=hq, q=q, memn=memn, kv=kv, o=o, x2=x2, hf=hf, up=up, act=act)
    return x3, sv


def _layer_bwd(dx3, mem, sv, p, nb, s, m):
    g = {}
    up, dff = sv["up"], sv["up"].shape[1] // 2
    dact = mm("d_ffn_down_x", dx3, p["ffn_w_down"], "nt")
    g["ffn_w_down"] = mm("d_ffn_down_w", sv["act"], dx3, "tn")
    (dug, duv), dfp = seqop_bwd("ffn_act_bwd", _ffn_act_f, [(up, 0), (up, dff)], _f_params(p), [(dact, 0)],
                                lambda t0, e: (e[SUBLANES:],), nb=nb, s=s, nblk=dff // LANES)
    dhf = mm("d_ffn_up_x1", duv, p["ffn_w_up"], "nt", b_cols=(dff, dff),
             res=mm("d_ffn_up_x0", dug, p["ffn_w_up"], "nt", b_cols=(0, dff)))
    g["ffn_w_up"] = jnp.concatenate([mm("d_ffn_up_w0", sv["hf"], dug, "tn"), mm("d_ffn_up_w1", sv["hf"], duv, "tn")],
                                    axis=1)
    g["ffn_conv_w"] = jnp.concatenate([jnp.concatenate(dfp[0:3], axis=0), jnp.concatenate(dfp[3:6], axis=0)], axis=1)
    g["ffn_conv_b"] = jnp.concatenate([dfp[6], dfp[7]], axis=1)[0]
    (dx2,), (gn,) = rowop_bwd("norm_ffn_bwd", _norm_f, [(sv["x2"], 0, D)], [p["ffn_norm"]], [(dhf, 0, D)], [True],
                              add=(dx3, 0, D))
    g["ffn_norm"] = gn[0]
    do = mm("d_xa_o_x", dx2, p["xa_w_o"], "nt")
    g["xa_w_o"] = mm("d_xa_o_w", sv["o"], dx2, "tn")
    dq, dk, dv = attn_bwd(sv["q"], sv["kv"], do, nb=nb, s=s, m=m, nh=4)
    dkv = jnp.concatenate([dk, dv], axis=1)
    dhq = mm("d_xa_q_x", dq, p["xa_w_q"], "nt")
    g["xa_w_q"] = mm("d_xa_q_w", sv["hq"], dq, "tn")
    dmemn = mm("d_xa_kv_x", dkv, p["xa_w_kv"], "nt")
    g["xa_w_kv"] = mm("d_xa_kv_w", sv["memn"], dkv, "tn")
    (dx1,), (gn,) = rowop_bwd("norm_xa_bwd", _norm_f, [(sv["x1"], 0, D)], [p["xa_norm"]], [(dhq, 0, D)], [True],
                              add=(dx2, 0, D))
    g["xa_norm"] = gn[0]
    _, (gn,) = rowop_bwd("norm_mem_bwd", _norm_f, [(mem, 0, D)], [p["mem_norm"]], [(dmemn, 0, D)], [False])
    g["mem_norm"] = gn[0]
    proj = sv["proj"]
    dmerged = mm("d_out_proj_x", dx1, p["w_out"], "nt")
    g["w_out"] = mm("d_out_proj_w", sv["merged"], dx1, "tn")
    merge_rows = [(sv["ha"], 0, BW), (proj, C_GA, BW), (sv["yb"], 0, BW), (sv["yc"], 0, BW),
                  (proj, 0, D), (proj, D, D), (proj, 2 * D, D)]
    (dha, dga, dyb, dyc, dg0, dg1, dg2), dmp = rowop_bwd("merge_bwd", _merge_f, merge_rows, p["wb"] + p["bg"],
                                                          [(dmerged, 0, D)], [True] * 7)
    g["w_branch"] = jnp.stack(dmp[0:3])
    g["b_gate"] = jnp.concatenate(dmp[3:6], axis=1)[0]
    (dys, duc), (gd, ggw, ggb) = rowop_bwd("s5_post_bwd", _s5_post_f, [(sv["ys"], 0, BW), (proj, C_UC, BW)],
                                           [p["c_d"], p["c_glu_w"], p["c_glu_b"]], [(dyc, 0, BW)], [True, True])
    g["c_d"], g["c_glu_w"], g["c_glu_b"] = gd[0], ggw, ggb[0]
    cd = jnp.concatenate([p["c_cre"], p["c_cimn"]], axis=0)
    dhs = mm("d_s5_y_x", dys, cd, "nt")
    dcre = mm("d_s5_y_wre", sv["hr"], dys, "tn")
    dcimn = mm("d_s5_y_wim", sv["hi"], dys, "tn")
    gr, gi = cscan("s5_scan_bwd", dhs, sv["ar"], sv["ai"], True, nb=nb, s=s)
    dar, dai = s5_dlam(gr, gi, sv["hr"], sv["hi"], nb=nb, s=s)
    duc = mm("d_s5_bu_x1", gi, sv["bd"], "nt", b_cols=(P_S5, P_S5),
             res=mm("d_s5_bu_x0", gr, sv["bd"], "nt", b_cols=(0, P_S5), res=duc))
    dbd_re = mm("d_s5_bu_wre", proj, gr, "tn", a_cols=(C_UC, BW))
    dbd_im = mm("d_s5_bu_wim", proj, gi, "tn", a_cols=(C_UC, BW))
    eye = p["eye"]
    pick = lambda dmat, r, c: jnp.einsum("grhc,gh->grc", dmat.reshape(NG, r, NG, c), eye)
    glr, gli, gldt, gbtr, gbti = s5_param_bwd(
        p["c_lr"], p["c_li"], p["c_ldt"], p["c_btr"], p["c_bti"],
        (dar.reshape(NG, 1, NP_S5), dai.reshape(NG, 1, NP_S5), pick(dbd_re, CG, NP_S5), pick(dbd_im, CG, NP_S5)))
    g["c_lam_re"], g["c_lam_im"], g["c_log_dt"] = glr[:, 0, :], gli[:, 0, :], gldt[:, 0, 0]
    g["c_b_re"], g["c_b_im"] = gbtr.transpose(0, 2, 1), gbti.transpose(0, 2, 1)
    g["c_c_re"] = pick(dcre, NP_S5, CG).transpose(0, 2, 1)
    g["c_c_im"] = -pick(dcimn, NP_S5, CG).transpose(0, 2, 1)
    (dq_b, dk_b, dv_b, dz_b, dba), dbp = deltanet_bwd(proj, (C_Q, C_K, C_V, C_Z, C_BA), p["b_taps"], p["b_alog"],
                                                      p["b_dtb"], p["b_nw"], sv["states"], dyb, nb=nb, s=s, nh=NH_B)
    g["b_conv_w"] = jnp.concatenate([jnp.concatenate(dbp[4 * j:4 * j + 4], axis=0) for j in range(3)], axis=1)
    g["b_a_log"] = dbp[12].reshape(NH_B, LANES)[:, 0]
    g["b_dt_bias"] = dbp[13].reshape(NH_B, LANES)[:, 0]
    g["b_norm"] = dbp[14][0]
    gsc = rscan("rglru_scan_bwd", sv["a"], dha, True, nb=nb, s=s)
    (dxa,), dap = seqop_bwd("rglru_pre_bwd", _rglru_pre_f, [(proj, C_XA)], _a_params(p), [(gsc, 0), (sv["ha"], 0)],
                            lambda t0, ge, he: (ge[SUBLANES:] * taps(he, 2)[1], ge[SUBLANES:]),
                            nb=nb, s=s, nblk=BW // LANES)
    g["a_conv_w"] = jnp.concatenate(dap[0:4], axis=0)
    g["a_conv_b"], g["a_b_r"], g["a_b_i"], g["a_lam"] = dap[4][0], dap[7][0], dap[8][0], dap[9][0]
    g["a_w_r"], g["a_w_i"] = _unblockdiag2(dap[5]), _unblockdiag2(dap[6])
    dproj = jnp.concatenate([dg0, dg1, dg2, dxa, dga, dq_b, dk_b, dv_b, dz_b, duc, dba], axis=1)
    dh = mm("d_in_proj_x", dproj, p["w_in_pad"], "nt")
    dwp = mm("d_in_proj_w", sv["h"], dproj, "tn")
    dba_w = dwp[:, C_BA:].reshape(D, NH_B, LANES)
    pieces = [dwp[:, c0:c0 + BW] for c0 in (C_XA, C_GA, C_Q, C_K, C_V, C_Z)]
    g["w_in"] = jnp.concatenate(pieces + [dba_w[:, :, 0], dba_w[:, :, 1], dwp[:, C_UC:C_UC + BW], dwp[:, :3 * D]],
                                axis=1)
    (dx,), (gn,) = rowop_bwd("norm_mix_bwd", _norm_f, [(sv["x"], 0, D)], [p["mix_norm"]], [(dh, 0, D)], [True],
                             add=(dx1, 0, D))
    g["mix_norm"] = gn[0]
    return dx, g


def _local_step(x, mem, target, w):
    nb, s, _ = x.shape
    m = mem.shape[1]
    depth = w["mix_norm"].shape[0]
    x2d, mem2d = x.reshape(nb * s, D), mem.reshape(nb * m, D)
    params, saved = [], []
    for l in range(depth):
        params.append(_layer_params(w, l))
        x2d, sv = _layer_fwd(x2d, mem2d, params[l], nb, s, m)
        saved.append(sv)
    loss_row, dx, gfn = loss_head(x2d, target.reshape(nb * s, D), w["final_norm"].reshape(1, D))
    grads = [None] * depth
    for l in reversed(range(depth)):
        dx, grads[l] = _layer_bwd(dx, mem2d, saved[l], params[l], nb, s, m)
    out = {n: jnp.stack([grads[l][n] for l in range(depth)]) for n in grads[0]}
    out["final_norm"] = gfn[0]
    return loss_row, dx.reshape(x.shape), out


SLAB_UNIT = 2 * 1024 * LANES


def _slab(parts, dtype, unit):
    flat = jnp.concatenate([a.astype(dtype).reshape(-1) for a in parts])
    pad = (-flat.shape[0]) % unit
    return jnp.pad(flat, (0, pad)) if pad else flat


def _unslab(flat, like):
    out, off = [], 0
    for a in like:
        out.append(flat[off:off + a.size].reshape(a.shape))
        off += a.size
    return out


def _gather_weights(name, local, names_axes, dtype):
    slab = _slab([local[n] for n, _ in names_axes], dtype, SLAB_UNIT).reshape(2, -1, LANES)
    got = gather_chips(name, slab).reshape(N_CHIPS, -1)
    full, off = {}, 0
    for n, ax in names_axes:
        a = local[n]
        pieces = got[:, off:off + a.size].reshape((N_CHIPS,) + a.shape)
        full[n] = jnp.concatenate([pieces[j] for j in range(N_CHIPS)], axis=ax)
        off += a.size
    return full


def kernel(x, mem, mix_norm, w_in, b_gate, a_conv_w, a_conv_b, a_w_r, a_b_r, a_w_i, a_b_i, a_lam, b_conv_w, b_a_log, b_dt_bias, b_norm, c_lam_re, c_lam_im, c_log_dt, c_b_re, c_b_im, c_c_re, c_c_im, c_d, c_glu_w, c_glu_b, w_branch, w_out, xa_norm, mem_norm, xa_w_q, xa_w_kv, xa_w_o, ffn_norm, ffn_w_up, ffn_conv_w, ffn_conv_b, ffn_w_down, final_norm, loss_target, m_mix_norm, m_w_in, m_b_gate, m_a_conv_w, m_a_conv_b, m_a_w_r, m_a_b_r, m_a_w_i, m_a_b_i, m_a_lam, m_b_conv_w, m_b_a_log, m_b_dt_bias, m_b_norm, m_c_lam_re, m_c_lam_im, m_c_log_dt, m_c_b_re, m_c_b_im, m_c_c_re, m_c_c_im, m_c_d, m_c_glu_w, m_c_glu_b, m_w_branch, m_w_out, m_xa_norm, m_mem_norm, m_xa_w_q, m_xa_w_kv, m_xa_w_o, m_ffn_norm, m_ffn_w_up, m_ffn_conv_w, m_ffn_conv_b, m_ffn_w_down, m_final_norm, v_mix_norm, v_w_in, v_b_gate, v_a_conv_w, v_a_conv_b, v_a_w_r, v_a_b_r, v_a_w_i, v_a_b_i, v_a_lam, v_b_conv_w, v_b_a_log, v_b_dt_bias, v_b_norm, v_c_lam_re, v_c_lam_im, v_c_log_dt, v_c_b_re, v_c_b_im, v_c_c_re, v_c_c_im, v_c_d, v_c_glu_w, v_c_glu_b, v_w_branch, v_w_out, v_xa_norm, v_mem_norm, v_xa_w_q, v_xa_w_kv, v_xa_w_o, v_ffn_norm, v_ffn_w_up, v_ffn_conv_w, v_ffn_conv_b, v_ffn_w_down, v_final_norm):
    env = dict(locals())
    w = {n: env[n] for n in WEIGHTS}
    mom = {n: env["m_" + n] for n in WEIGHTS}
    var = {n: env["v_" + n] for n in WEIGHTS}

    full = dict(w)
    full.update(_gather_weights("gather_mats", w, SHARDED_MATS, BF16))
    full.update(_gather_weights("gather_convs", w, SHARDED_CONVS, F32))
    loss_row, grad_x, gfull = _local_step(x, mem, loss_target, full)

    per_chip = []
    for j in range(N_CHIPS):
        parts = []
        for n, ax in SHARDED:
            width = w[n].shape[ax]
            parts.append(lax.slice_in_dim(gfull[n], j * width, (j + 1) * width, axis=ax))
        per_chip.append(_slab(parts, F32, SLAB_UNIT))
    rows = per_chip[0].shape[0] // (2 * LANES)
    gsl = jnp.stack(per_chip).reshape(N_CHIPS, 2, rows, LANES).transpose(1, 0, 2, 3)
    both = swap_halves("swap_halves", gsl)
    part = sumk("add_cores", both.reshape(2, N_CHIPS * rows, LANES)).reshape(N_CHIPS, rows, LANES)
    mine = sumk("add_chips", scatter_chips("scatter_chips", part))
    g_sh = join_halves("join_halves", mine).reshape(2 * rows, LANES)
    sh_names = [n for n, _ in SHARDED]
    slab2 = lambda d: _slab([d[n] for n in sh_names], F32, SLAB_UNIT).reshape(2 * rows, LANES)
    d_sh, m_sh, v_sh = adamw("adamw_sharded", slab2(w), g_sh, slab2(mom), slab2(var))

    unit = SUBLANES * LANES
    small = _slab([gfull[n] for n in REPLICATED] + [loss_row[:, :1]], F32, unit).reshape(-1, LANES)
    g_rp = sumk("add_devices", gather_all("gather_all", small))
    slab1 = lambda d: _slab([d[n] for n in REPLICATED] + [jnp.zeros((1, 1), F32)], F32, unit).reshape(-1, LANES)
    d_rp, m_rp, v_rp = adamw("adamw_replicated", slab1(w), g_rp, slab1(mom), slab1(var))

    res = {}
    like_sh = [w[n] for n in sh_names]
    like_rp = [w[n] for n in REPLICATED] + [jnp.zeros((1, 1), F32)]
    for kind, sh, rp in (("grad", g_sh, g_rp), ("delta", d_sh, d_rp), ("new_m", m_sh, m_rp), ("new_v", v_sh, v_rp)):
        for n, a in zip(sh_names, _unslab(sh.reshape(-1), like_sh)):
            res[kind + "_" + n] = a
        for n, a in zip(list(REPLICATED) + ["loss"], _unslab(rp.reshape(-1), like_rp)):
            res[kind + "_" + n] = a
    loss = res["grad_loss"].reshape(())
    return (loss, grad_x, *[res["grad_" + n] for n in WEIGHTS], *[res["delta_" + n] for n in WEIGHTS],
            *[res["new_m_" + n] for n in WEIGHTS], *[res["new_v_" + n] for n in WEIGHTS])
```

```python
import functools
import math

import jax
import jax.numpy as jnp
from jax import lax
from jax.experimental import pallas as pl
from jax.experimental.pallas import tpu as pltpu

F32, BF16 = jnp.float32, jnp.bfloat16
MESH = pl.DeviceIdType.MESH
ANY = pl.BlockSpec(memory_space=pl.ANY)

VMEM_LIMIT_V7X = 56 * 1024 * 1024
MM_TILE = 1024
LANES, SUBLANES = 128, 8
EPS = 1e-6
RG_C = 8.0
DN_CHUNK = 64
N_CHIPS = 4
ADAM_LR, ADAM_B1, ADAM_B2, ADAM_EPS, ADAM_WD, ADAM_STEP = 0.001, 0.9, 0.999, 1e-08, 0.01, 10

SHARDED_MATS = (("w_in", 2), ("c_glu_w", 1), ("w_branch", 3), ("w_out", 1), ("xa_w_q", 1), ("xa_w_kv", 2),
                ("xa_w_o", 1), ("ffn_w_up", 2), ("ffn_w_down", 1))
SHARDED_CONVS = (("a_conv_w", 2), ("b_conv_w", 2), ("ffn_conv_w", 2))
SHARDED = SHARDED_MATS + SHARDED_CONVS
WEIGHTS = ("mix_norm", "w_in", "b_gate", "a_conv_w", "a_conv_b", "a_w_r", "a_b_r", "a_w_i", "a_b_i", "a_lam",
           "b_conv_w", "b_a_log", "b_dt_bias", "b_norm", "c_lam_re", "c_lam_im", "c_log_dt", "c_b_re", "c_b_im",
           "c_c_re", "c_c_im", "c_d", "c_glu_w", "c_glu_b", "w_branch", "w_out", "xa_norm", "mem_norm", "xa_w_q",
           "xa_w_kv", "xa_w_o", "ffn_norm", "ffn_w_up", "ffn_conv_w", "ffn_conv_b", "ffn_w_down", "final_norm")
REPLICATED = tuple(n for n in WEIGHTS if n not in dict(SHARDED))


def _cparams(n_axes):
    return pltpu.CompilerParams(dimension_semantics=("arbitrary",) * n_axes, vmem_limit_bytes=VMEM_LIMIT_V7X)


def _tile(n, pref, off=0):
    t = pref
    while t >= LANES:
        if n % t == 0 and off % t == 0:
            return t
        t //= 2
    assert off == 0, (n, pref, off)
    return n


def _full_spec(shape, n_grid):
    nd = len(shape)
    if n_grid == 1:
        return pl.BlockSpec(shape, lambda i: (0,) * nd)
    return pl.BlockSpec(shape, lambda i, j: (0,) * nd)


_NN, _NT, _TN = ((1,), (0,)), ((1,), (1,)), ((0,), (0,))


def _dot(a, b, dims, hp):
    if hp:
        return lax.dot_general(a, b, (dims, ((), ())), precision=lax.Precision.HIGHEST, preferred_element_type=F32)
    return lax.dot_general(a.astype(BF16), b.astype(BF16), (dims, ((), ())), preferred_element_type=F32)


@functools.partial(jax.custom_vjp, nondiff_argnums=(2,))
def mm_nn(a, b, hp=False):
    return _dot(a, b, _NN, hp)


@functools.partial(jax.custom_vjp, nondiff_argnums=(2,))
def mm_nt(a, b, hp=False):
    return _dot(a, b, _NT, hp)


@functools.partial(jax.custom_vjp, nondiff_argnums=(2,))
def mm_tn(a, b, hp=False):
    return _dot(a, b, _TN, hp)


mm_nn.defvjp(lambda a, b, hp: (_dot(a, b, _NN, hp), (a, b)),
             lambda hp, r, g: (mm_nt(g, r[1], hp), mm_tn(r[0], g, hp)))
mm_nt.defvjp(lambda a, b, hp: (_dot(a, b, _NT, hp), (a, b)),
             lambda hp, r, g: (mm_nn(g, r[1], hp), mm_tn(g, r[0], hp)))
mm_tn.defvjp(lambda a, b, hp: (_dot(a, b, _TN, hp), (a, b)),
             lambda hp, r, g: (mm_nt(r[1], g, hp), mm_nn(r[0], g, hp)))


@functools.partial(jax.custom_vjp, nondiff_argnums=(1,))
def taps(xext, k):
    return tuple((xext if s == 0 else pltpu.roll(xext, s, 0))[SUBLANES:] for s in range(k))


def _taps_fwd(xext, k):
    return taps(xext, k), None


def _taps_bwd(k, _, gs):
    tot = None
    for s, g in enumerate(gs):
        gp = jnp.concatenate([jnp.zeros((SUBLANES, g.shape[1]), g.dtype), g], axis=0)
        if s:
            gp = pltpu.roll(gp, gp.shape[0] - s, 0)
        tot = gp if tot is None else tot + gp
    return (tot,)


taps.defvjp(_taps_fwd, _taps_bwd)


def _gelu(x):
    return x * (0.5 * (1.0 + jnp.tanh(0.7978845608028654 * (x + 0.044715 * (x * x * x)))))


def _sigmoid(x):
    return jax.nn.sigmoid(x)


def _softplus(x):
    return jnp.maximum(x, 0.0) + jnp.log1p(jnp.exp(-jnp.abs(x)))


def _rmsnorm(x, g):
    return x * lax.rsqrt(jnp.mean(x * x, axis=-1, keepdims=True) + EPS) * g


def _lane_pick(x, lane):
    sel = lax.broadcasted_iota(jnp.int32, x.shape, 1) == lane
    return jnp.sum(jnp.where(sel, x, 0.0), axis=1, keepdims=True)


def _load_ext(ref, t0, n):
    main = ref[pl.ds(t0, n), :].astype(F32)
    hstart = pl.multiple_of(jnp.maximum(t0 - SUBLANES, 0), SUBLANES)
    halo = ref[pl.ds(hstart, SUBLANES), :].astype(F32)
    halo = jnp.where(t0 > 0, halo, 0.0)
    return jnp.concatenate([halo, main], axis=0)


def mm(name, a, b, mode, *, out_dtype=F32, res=None, a_cols=None, b_cols=None):
    a0, aw = a_cols if a_cols else (0, a.shape[1])
    b0, bw = b_cols if b_cols else (0, b.shape[1])
    big = MM_TILE
    if mode == "nn":
        m, k, n = a.shape[0], aw, bw
        assert b.shape[0] == k
        tm, tk, tn = _tile(m, big), _tile(k, big, a0), _tile(n, big, b0)
        a_spec = pl.BlockSpec((tm, tk), lambda i, j, kk: (i, kk + a0 // tk))
        b_spec = pl.BlockSpec((tk, tn), lambda i, j, kk: (kk, j + b0 // tn))
        dims = _NN
    elif mode == "nt":
        m, k, n = a.shape[0], aw, b.shape[0]
        assert bw == k
        tm, tn = _tile(m, big), _tile(n, big)
        tk = _tile(k, big, math.gcd(a0, b0) if (a0 or b0) else 0)
        assert a0 % tk == 0 and b0 % tk == 0
        a_spec = pl.BlockSpec((tm, tk), lambda i, j, kk: (i, kk + a0 // tk))
        b_spec = pl.BlockSpec((tn, tk), lambda i, j, kk: (j, kk + b0 // tk))
        dims = _NT
    else:
        k, m, n = a.shape[0], aw, bw
        assert b.shape[0] == k
        tk, tm, tn = _tile(k, big), _tile(m, big, a0), _tile(n, big, b0)
        a_spec = pl.BlockSpec((tk, tm), lambda i, j, kk: (kk, i + a0 // tm))
        b_spec = pl.BlockSpec((tk, tn), lambda i, j, kk: (kk, j + b0 // tn))
        dims = _TN
    nk = k // tk
    has_res = res is not None

    def tile_dot(a_ref, b_ref):
        return lax.dot_general(a_ref[...].astype(BF16), b_ref[...].astype(BF16), (dims, ((), ())),
                               preferred_element_type=F32)

    def finish(out, refs, o_ref):
        if has_res:
            out = out + refs[2][...].astype(F32)
        o_ref[...] = out.astype(o_ref.dtype)

    def body_single(*refs):
        finish(tile_dot(refs[0], refs[1]), refs, refs[-1])

    def body_acc(*refs):
        o_ref, acc_ref = refs[-2], refs[-1]
        kk = pl.program_id(2)

        @pl.when(kk == 0)
        def _():
            acc_ref[...] = tile_dot(refs[0], refs[1])

        @pl.when((kk > 0) & (kk < nk - 1))
        def _():
            acc_ref[...] += tile_dot(refs[0], refs[1])

        @pl.when(kk == nk - 1)
        def _():
            finish(acc_ref[...] + tile_dot(refs[0], refs[1]), refs, o_ref)

    o_spec = pl.BlockSpec((tm, tn), lambda i, j, kk: (i, j))
    return pl.pallas_call(
        body_single if nk == 1 else body_acc, name=name, grid=(m // tm, n // tn, nk),
        in_specs=[a_spec, b_spec] + ([o_spec] if has_res else []), out_specs=o_spec,
        out_shape=jax.ShapeDtypeStruct((m, n), out_dtype),
        scratch_shapes=[] if nk == 1 else [pltpu.VMEM((tm, tn), F32)], compiler_params=_cparams(3),
    )(*((a, b, res) if has_res else (a, b)))


def _row_specs(rows, tm):
    return [pl.BlockSpec((tm, cw), lambda i, c=c0 // cw: (i, c)) for (_, c0, cw) in rows]


def rowop_fwd(name, f, rows, params, outs, tm=256):
    t = rows[0][0].shape[0]
    tm = min(tm, t)
    n_r, n_p = len(rows), len(params)

    def body(*refs):
        vals = f(*[r[...].astype(F32) for r in refs[:n_r]], *[p[...] for p in refs[n_r:n_r + n_p]])
        for o, v in zip(refs[n_r + n_p:], vals):
            o[...] = v.astype(o.dtype)

    res = pl.pallas_call(
        body, name=name, grid=(t // tm,),
        in_specs=_row_specs(rows, tm) + [_full_spec(p.shape, 1) for p in params],
        out_specs=[pl.BlockSpec((tm, cw), lambda i: (i, 0)) for cw, _ in outs],
        out_shape=[jax.ShapeDtypeStruct((t, cw), dt) for cw, dt in outs],
        compiler_params=_cparams(1),
    )(*[r[0] for r in rows], *params)
    return list(res)


def rowop_bwd(name, f, rows, params, couts, need, tm=256, add=None):
    t = rows[0][0].shape[0]
    tm = min(tm, t)
    n_r, n_p, n_c = len(rows), len(params), len(couts)
    want = [k for k in range(n_r) if need[k]]
    adds = [add] if add is not None else []

    def body(*refs):
        xs = [r[...].astype(F32) for r in refs[:n_r]]
        ps = [p[...] for p in refs[n_r:n_r + n_p]]
        cs = tuple(c[...].astype(F32) for c in refs[n_r + n_p:n_r + n_p + n_c])
        outs = refs[n_r + n_p + n_c + len(adds):]
        _, vjp = jax.vjp(f, *xs, *ps)
        gs = vjp(cs)
        for o, k in zip(outs[:len(want)], want):
            o[...] = gs[k] + refs[n_r + n_p + n_c][...] if (adds and k == want[0]) else gs[k]

        @pl.when(pl.program_id(0) == 0)
        def _():
            for o in outs[len(want):]:
                o[...] = jnp.zeros_like(o)

        for o, g in zip(outs[len(want):], gs[n_r:]):
            o[...] += g

    res = pl.pallas_call(
        body, name=name, grid=(t // tm,),
        in_specs=_row_specs(rows, tm) + [_full_spec(p.shape, 1) for p in params] + _row_specs(couts, tm)
        + _row_specs(adds, tm),
        out_specs=[pl.BlockSpec((tm, rows[k][2]), lambda i: (i, 0)) for k in want]
        + [_full_spec(p.shape, 1) for p in params],
        out_shape=[jax.ShapeDtypeStruct((t, rows[k][2]), F32) for k in want]
        + [jax.ShapeDtypeStruct(p.shape, F32) for p in params],
        compiler_params=_cparams(1),
    )(*[r[0] for r in rows], *params, *[c[0] for c in couts], *[a[0] for a in adds])
    return list(res[:len(want)]), list(res[len(want):])


def _seq_specs(seqs, s, cb, order):
    if order == "bj":
        return [pl.BlockSpec((s, cb), lambda b, j, c=c0 // cb: (b, c + j)) for (_, c0) in seqs]
    return [pl.BlockSpec((s, cb), lambda j, b, c=c0 // cb: (b, c + j)) for (_, c0) in seqs]


def _param_specs(params, order):
    if order == "bj":
        return [pl.BlockSpec(bs, lambda b, j, fn=fn: fn(j)) for (_, bs, fn) in params]
    return [pl.BlockSpec(bs, lambda j, b, fn=fn: fn(j)) for (_, bs, fn) in params]


def seqop_fwd(name, f, seqs, params, out_dtypes, *, nb, s, nblk, cb=LANES, n=256):
    n = min(n, s)
    n_s, n_p = len(seqs), len(params)

    def body(*refs):
        seq_refs, par_refs, out_refs = refs[:n_s], refs[n_s:n_s + n_p], refs[n_s + n_p:]

        def step(i, carry):
            t0 = pl.multiple_of(i * n, n)
            vals = f(t0, *[_load_ext(r, t0, n) for r in seq_refs], *[p[...] for p in par_refs])
            for o, v in zip(out_refs, vals):
                o[pl.ds(t0, n), :] = v.astype(o.dtype)
            return carry

        lax.fori_loop(0, s // n, step, 0)

    res = pl.pallas_call(
        body, name=name, grid=(nb, nblk),
        in_specs=_seq_specs(seqs, s, cb, "bj") + _param_specs(params, "bj"),
        out_specs=[pl.BlockSpec((s, cb), lambda b, j: (b, j)) for _ in out_dtypes],
        out_shape=[jax.ShapeDtypeStruct((nb * s, nblk * cb), dt) for dt in out_dtypes],
        compiler_params=_cparams(2),
    )(*[q[0] for q in seqs], *[p[0] for p in params])
    return list(res)


def seqop_bwd(name, f, seqs, params, cot_seqs, cot_fn, *, nb, s, nblk, cb=LANES, n=256):
    n = min(n, s)
    n_s, n_p, n_c = len(seqs), len(params), len(cot_seqs)
    nchunk = s // n

    def body(*refs):
        seq_refs, par_refs = refs[:n_s], refs[n_s:n_s + n_p]
        cot_refs = refs[n_s + n_p:n_s + n_p + n_c]
        dseq_refs = refs[n_s + n_p + n_c:n_s + n_p + n_c + n_s]
        dpar_refs = refs[n_s + n_p + n_c + n_s:]

        @pl.when(pl.program_id(1) == 0)
        def _():
            for o in dpar_refs:
                o[...] = jnp.zeros_like(o)

        def step(ii, halos):
            t0 = pl.multiple_of((nchunk - 1 - ii) * n, n)
            exts = [_load_ext(r, t0, n) for r in seq_refs]
            ps = [p[...] for p in par_refs]
            cots = cot_fn(t0, *[_load_ext(r, t0, n) for r in cot_refs])
            _, vjp = jax.vjp(lambda *args: f(t0, *args), *exts, *ps)
            gs = vjp(tuple(cots))
            tail = pl.multiple_of(t0 + n - SUBLANES, SUBLANES)
            new_halos = []
            for o, g, h in zip(dseq_refs, gs[:n_s], halos):
                o[pl.ds(t0, n), :] = g[SUBLANES:]
                o[pl.ds(tail, SUBLANES), :] += h
                new_halos.append(g[:SUBLANES])
            for o, g in zip(dpar_refs, gs[n_s:]):
                o[...] += g
            return tuple(new_halos)

        lax.fori_loop(0, nchunk, step, tuple(jnp.zeros((SUBLANES, cb), F32) for _ in range(n_s)))

    res = pl.pallas_call(
        body, name=name, grid=(nblk, nb),
        in_specs=_seq_specs(seqs, s, cb, "jb") + _param_specs(params, "jb") + _seq_specs(cot_seqs, s, cb, "jb"),
        out_specs=[pl.BlockSpec((s, cb), lambda j, b: (b, j)) for _ in seqs] + _param_specs(params, "jb"),
        out_shape=[jax.ShapeDtypeStruct((nb * s, nblk * cb), F32) for _ in seqs]
        + [jax.ShapeDtypeStruct(p[0].shape, F32) for p in params],
        compiler_params=_cparams(2),
    )(*[q[0] for q in seqs], *[p[0] for p in params], *[q[0] for q in cot_seqs])
    return list(res[:n_s]), list(res[n_s:])


def rscan(name, a, b, reverse, *, nb, s, cb=256):
    c = a.shape[1]
    cb = min(cb, c)

    def body(a_ref, b_ref, h_ref):
        def step(ii, car):
            i = (s // SUBLANES - 1 - ii) if reverse else ii
            base = pl.multiple_of(i * SUBLANES, SUBLANES)
            for r in (reversed(range(SUBLANES)) if reverse else range(SUBLANES)):
                row = pl.ds(base + r, 1)
                if reverse:
                    g = b_ref[row, :] + car
                    h_ref[row, :] = g
                    car = a_ref[row, :] * g
                else:
                    car = a_ref[row, :] * car + b_ref[row, :]
                    h_ref[row, :] = car
            return car

        lax.fori_loop(0, s // SUBLANES, step, jnp.zeros((1, cb), F32))

    spec = pl.BlockSpec((s, cb), lambda bb, j: (bb, j))
    return pl.pallas_call(
        body, name=name, grid=(nb, c // cb), in_specs=[spec, spec], out_specs=spec,
        out_shape=jax.ShapeDtypeStruct(a.shape, F32), compiler_params=_cparams(2),
    )(a, b)


def cscan(name, bu, ar, ai, reverse, *, nb, s, cb=256):
    p = ar.shape[1]
    cb = min(cb, p)
    nj = p // cb

    def body(br_ref, bi_ref, ar_ref, ai_ref, hr_ref, hi_ref):
        lr = ar_ref[...]
        li = -ai_ref[...] if reverse else ai_ref[...]

        def step(ii, car):
            hr, hi = car
            i = (s // SUBLANES - 1 - ii) if reverse else ii
            base = pl.multiple_of(i * SUBLANES, SUBLANES)
            for r in (reversed(range(SUBLANES)) if reverse else range(SUBLANES)):
                row = pl.ds(base + r, 1)
                nr = lr * hr - li * hi + br_ref[row, :]
                ni = lr * hi + li * hr + bi_ref[row, :]
                hr, hi = nr, ni
                hr_ref[row, :] = hr
                hi_ref[row, :] = hi
            return hr, hi

        z = jnp.zeros((1, cb), F32)
        lax.fori_loop(0, s // SUBLANES, step, (z, z))

    o_spec = pl.BlockSpec((s, cb), lambda bb, j: (bb, j))
    l_spec = pl.BlockSpec((1, cb), lambda bb, j: (0, j))
    res = pl.pallas_call(
        body, name=name, grid=(nb, nj),
        in_specs=[o_spec, pl.BlockSpec((s, cb), lambda bb, j: (bb, j + nj)), l_spec, l_spec],
        out_specs=[o_spec, o_spec],
        out_shape=[jax.ShapeDtypeStruct((nb * s, p), F32)] * 2, compiler_params=_cparams(2),
    )(bu, bu, ar, ai)
    return res[0], res[1]


def _norm_f(x, g):
    return (_rmsnorm(x, g),)


def _rglru_pre_f(t0, xext, w0, w1, w2, w3, cb, wr, wi, br, bi, lam):
    x0, x1, x2, x3 = taps(xext, 4)
    xc = w3 * x0 + w2 * x1 + w1 * x2 + w0 * x3 + cb
    r = _sigmoid(mm_nn(xc, wr[0]) + br)
    ig = _sigmoid(mm_nn(xc, wi[0]) + bi)
    log_a = -RG_C * r * _softplus(-lam)
    a = jnp.exp(log_a)
    first = (lax.broadcasted_iota(jnp.int32, a.shape, 0) + t0) == 0
    mult = jnp.where(first, 1.0, jnp.sqrt(1.0 - jnp.exp(2.0 * log_a)))
    return a, mult * ig * xc


def _ffn_act_f(t0, gext, vext, g0, g1, g2, v0, v1, v2, bg, bv):
    ga, gb, gc = taps(gext, 3)
    va, vb, vc = taps(vext, 3)
    ug = g2 * ga + g1 * gb + g0 * gc + bg
    uv = v2 * va + v1 * vb + v0 * vc + bv
    return (_gelu(ug) * uv,)


def _s5_post_f(ys, uc, d, gw, gb):
    yc = _gelu(ys + d * uc)
    return (yc * _sigmoid(mm_nn(yc, gw) + gb),)


def _merge_f(ha, ga, yb, yc, g0, g1, g2, wb0, wb1, wb2, bg0, bg1, bg2):
    ya = ha * _gelu(ga)
    out = _sigmoid(g0 + bg0) * mm_nn(ya, wb0)
    out = out + _sigmoid(g1 + bg1) * mm_nn(yb, wb1)
    out = out + _sigmoid(g2 + bg2) * mm_nn(yc, wb2)
    return (out,)


def _s5_param_f(lr, li, ldt, btr, bti):
    dt = jnp.exp(ldt)
    mag = jnp.exp(lr * dt)
    ar, ai = mag * jnp.cos(li * dt), mag * jnp.sin(li * dt)
    den = lr * lr + li * li
    fr = ((ar - 1.0) * lr + ai * li) / den
    fi = (ai * lr - (ar - 1.0) * li) / den
    return ar, ai, fr * btr - fi * bti, fr * bti + fi * btr


def _tri_inv(a):
    c = a.shape[0]
    ri = lax.broadcasted_iota(jnp.int32, (c, c), 0)
    ci = lax.broadcasted_iota(jnp.int32, (c, c), 1)
    eye = jnp.where(ri == ci, 1.0, 0.0).astype(F32)
    ad = jnp.where((ri >> 4) == (ci >> 4), a, 0.0)
    ao = a - ad
    a2 = mm_nn(ad, ad, True)
    a4 = mm_nn(a2, a2, True)
    a8 = mm_nn(a4, a4, True)
    td = mm_nn(mm_nn(eye - ad, eye + a2, True), mm_nn(eye + a4, eye + a8, True), True)
    nn_ = mm_nn(td, ao, True)
    n2 = mm_nn(nn_, nn_, True)
    return mm_nn(mm_nn(eye - nn_, eye + n2, True), td, True)


def _dn_chunk_f(state, qe, ke, ve, z, ba, q0, q1, q2, q3, k0, k1, k2, k3, v0, v1, v2, v3, alog, dtb, nw):
    c = DN_CHUNK
    dk = qe.shape[1]

    def conv_silu(xe, w0, w1, w2, w3):
        x0, x1, x2, x3 = taps(xe, 4)
        u = w3 * x0 + w2 * x1 + w1 * x2 + w0 * x3
        return u * _sigmoid(u)

    def l2n(x):
        return x * lax.rsqrt(jnp.sum(x * x, axis=-1, keepdims=True) + EPS)

    q = l2n(conv_silu(qe, q0, q1, q2, q3)) * (dk ** -0.5)
    k = l2n(conv_silu(ke, k0, k1, k2, k3))
    v = conv_silu(ve, v0, v1, v2, v3)
    beta = _sigmoid(_lane_pick(ba, 0))
    g = -jnp.exp(_lane_pick(alog, 0)) * _softplus(_lane_pick(ba, 1) + _lane_pick(dtb, 0))
    ri = lax.broadcasted_iota(jnp.int32, (c, c), 0)
    ci = lax.broadcasted_iota(jnp.int32, (c, c), 1)
    incl, strict = ri >= ci, ri > ci
    ltri = jnp.where(incl, 1.0, 0.0).astype(F32)
    gc_w = mm_nn(ltri, jnp.broadcast_to(g, (c, dk)), True)
    gc_c = mm_nn(ltri, jnp.broadcast_to(g, (c, c)), True)
    gc_r = mm_nt(jnp.full((c, dk), 1.0 / dk, F32), gc_w, True)
    gtot = jnp.sum(g, axis=0, keepdims=True)
    decay = jnp.exp(jnp.where(incl, gc_c - gc_r, -1e30))
    e_gc = jnp.exp(gc_w)
    kb = k * beta
    a_mat = jnp.where(strict, mm_nt(kb, k) * decay, 0.0)
    t_inv = _tri_inv(a_mat)
    u = mm_nn(t_inv, v * beta, True)
    w = mm_nn(t_inv, kb * e_gc, True)
    qk = jnp.where(incl, mm_nt(q, k) * decay, 0.0)
    v_new = u - mm_nn(w, state)
    o = mm_nn(q * e_gc, state) + mm_nn(qk, v_new)
    new_state = state * jnp.exp(gtot) + mm_tn(k * jnp.exp(gtot - gc_w), v_new)
    o = _rmsnorm(o, nw) * (z * _sigmoid(z))
    return o, new_state


def _attn_f(q, k, v):
    sc = mm_nt(q, k) * (q.shape[1] ** -0.5)
    sc = sc - jnp.max(sc, axis=-1, keepdims=True)
    e = jnp.exp(sc)
    p = e / jnp.sum(e, axis=-1, keepdims=True)
    return (mm_nn(p, v),)


def _dn_specs(proj_cols, s, nh, order):
    if order == "hb":
        return [pl.BlockSpec((s, LANES), lambda h, b, c=c0 // LANES: (b, c + h)) for c0 in proj_cols]
    raise ValueError(order)


def deltanet_fwd(proj, cols, taps12, alog, dtb, nw, *, nb, s, nh):
    nc = s // DN_CHUNK
    head_par = pl.BlockSpec((1, LANES), lambda h, b: (0, h))

    def body(*refs):
        q_ref, k_ref, v_ref, z_ref, ba_ref = refs[:5]
        par_refs = refs[5:5 + 15]
        o_ref, st_ref = refs[20], refs[21]

        def step(i, state):
            t0 = pl.multiple_of(i * DN_CHUNK, DN_CHUNK)
            st_ref[0, 0, i] = state
            rows = pl.ds(t0, DN_CHUNK)
            o, new_state = _dn_chunk_f(
                state, _load_ext(q_ref, t0, DN_CHUNK), _load_ext(k_ref, t0, DN_CHUNK), _load_ext(v_ref, t0, DN_CHUNK),
                z_ref[rows, :], ba_ref[rows, :], *[p[...] for p in par_refs])
            o_ref[rows, :] = o
            return new_state

        lax.fori_loop(0, nc, step, jnp.zeros((LANES, LANES), F32))

    res = pl.pallas_call(
        body, name="deltanet_fwd", grid=(nh, nb),
        in_specs=_dn_specs(cols, s, nh, "hb") + [head_par] * 14 + [pl.BlockSpec((1, LANES), lambda h, b: (0, 0))],
        out_specs=[pl.BlockSpec((s, LANES), lambda h, b: (b, h)),
                   pl.BlockSpec((1, 1, nc, LANES, LANES), lambda h, b: (b, h, 0, 0, 0))],
        out_shape=[jax.ShapeDtypeStruct((nb * s, nh * LANES), F32),
                   jax.ShapeDtypeStruct((nb, nh, nc, LANES, LANES), F32)],
        compiler_params=_cparams(2),
    )(proj, proj, proj, proj, proj, *taps12, alog, dtb, nw)
    return res[0], res[1]


def deltanet_bwd(proj, cols, taps12, alog, dtb, nw, states, dyb, *, nb, s, nh):
    nc = s // DN_CHUNK
    head_par = pl.BlockSpec((1, LANES), lambda h, b: (0, h))
    norm_par = pl.BlockSpec((1, LANES), lambda h, b: (0, 0))
    seq_out = pl.BlockSpec((s, LANES), lambda h, b: (b, h))

    def body(*refs):
        q_ref, k_ref, v_ref, z_ref, ba_ref = refs[:5]
        par_refs = refs[5:20]
        st_ref, do_ref = refs[20], refs[21]
        dq_ref, dk_ref, dv_ref, dz_ref, dba_ref = refs[22:27]
        dpar_refs = refs[27:42]
        h_id, b_id = pl.program_id(0), pl.program_id(1)

        @pl.when(b_id == 0)
        def _():
            for o in dpar_refs[:14]:
                o[...] = jnp.zeros_like(o)

        @pl.when((b_id == 0) & (h_id == 0))
        def _():
            dpar_refs[14][...] = jnp.zeros_like(dpar_refs[14])

        def step(ii, car):
            d_state, halos = car[0], car[1:]
            i = nc - 1 - ii
            t0 = pl.multiple_of(i * DN_CHUNK, DN_CHUNK)
            rows = pl.ds(t0, DN_CHUNK)
            args = [st_ref[0, 0, i], _load_ext(q_ref, t0, DN_CHUNK), _load_ext(k_ref, t0, DN_CHUNK),
                    _load_ext(v_ref, t0, DN_CHUNK), z_ref[rows, :], ba_ref[rows, :]] + [p[...] for p in par_refs]
            _, vjp = jax.vjp(_dn_chunk_f, *args)
            gs = vjp((do_ref[rows, :], d_state))
            tail = pl.multiple_of(t0 + DN_CHUNK - SUBLANES, SUBLANES)
            new_halos = []
            for o, g, hl in zip((dq_ref, dk_ref, dv_ref), gs[1:4], halos):
                o[rows, :] = g[SUBLANES:]
                o[pl.ds(tail, SUBLANES), :] += hl
                new_halos.append(g[:SUBLANES])
            dz_ref[rows, :] = gs[4]
            dba_ref[rows, :] = gs[5]
            for o, g in zip(dpar_refs, gs[6:]):
                o[...] += g
            return (gs[0], *new_halos)

        zero_halo = jnp.zeros((SUBLANES, LANES), F32)
        lax.fori_loop(0, nc, step, (jnp.zeros((LANES, LANES), F32), zero_halo, zero_halo, zero_halo))

    res = pl.pallas_call(
        body, name="deltanet_bwd", grid=(nh, nb),
        in_specs=_dn_specs(cols, s, nh, "hb") + [head_par] * 14 + [norm_par]
        + [pl.BlockSpec((1, 1, nc, LANES, LANES), lambda h, b: (b, h, 0, 0, 0)), seq_out],
        out_specs=[seq_out] * 5 + [head_par] * 14 + [norm_par],
        out_shape=[jax.ShapeDtypeStruct((nb * s, nh * LANES), F32)] * 5
        + [jax.ShapeDtypeStruct((1, nh * LANES), F32)] * 14 + [jax.ShapeDtypeStruct((1, LANES), F32)],
        compiler_params=_cparams(2),
    )(proj, proj, proj, proj, proj, *taps12, alog, dtb, nw, states, dyb)
    return list(res[:5]), list(res[5:])


def attn_fwd(q, kv, *, nb, s, m, nh, tq=512):
    hd = q.shape[1] // nh
    tq = min(tq, s)
    nq = s // tq

    def body(q_ref, k_ref, v_ref, o_ref):
        o_ref[...] = _attn_f(q_ref[...].astype(F32), k_ref[...], v_ref[...])[0].astype(o_ref.dtype)

    return pl.pallas_call(
        body, name="attn_fwd", grid=(nb, nh, nq),
        in_specs=[pl.BlockSpec((tq, hd), lambda b, h, i: (b * nq + i, h)),
                  pl.BlockSpec((m, hd), lambda b, h, i: (b, h)),
                  pl.BlockSpec((m, hd), lambda b, h, i: (b, nh + h))],
        out_specs=pl.BlockSpec((tq, hd), lambda b, h, i: (b * nq + i, h)),
        out_shape=jax.ShapeDtypeStruct(q.shape, BF16), compiler_params=_cparams(3),
    )(q, kv, kv)


def attn_bwd(q, kv, do, *, nb, s, m, nh, tq=512):
    hd = q.shape[1] // nh
    tq = min(tq, s)
    nq = s // tq

    def body(q_ref, k_ref, v_ref, do_ref, dq_ref, dk_ref, dv_ref):
        _, vjp = jax.vjp(_attn_f, q_ref[...].astype(F32), k_ref[...], v_ref[...])
        dq, dk, dv = vjp((do_ref[...].astype(F32),))
        dq_ref[...] = dq

        @pl.when(pl.program_id(2) == 0)
        def _():
            dk_ref[...] = jnp.zeros_like(dk_ref)
            dv_ref[...] = jnp.zeros_like(dv_ref)

        dk_ref[...] += dk
        dv_ref[...] += dv

    q_spec = pl.BlockSpec((tq, hd), lambda b, h, i: (b * nq + i, h))
    m_spec = pl.BlockSpec((m, hd), lambda b, h, i: (b, h))
    res = pl.pallas_call(
        body, name="attn_bwd", grid=(nb, nh, nq),
        in_specs=[q_spec, m_spec, pl.BlockSpec((m, hd), lambda b, h, i: (b, nh + h)), q_spec],
        out_specs=[q_spec, m_spec, m_spec],
        out_shape=[jax.ShapeDtypeStruct(q.shape, F32), jax.ShapeDtypeStruct((nb * m, nh * hd), F32),
                   jax.ShapeDtypeStruct((nb * m, nh * hd), F32)],
        compiler_params=_cparams(3),
    )(q, kv, kv, do)
    return res[0], res[1], res[2]


def loss_head(x, target, gain, tm=256):
    t, d = x.shape
    tm = min(tm, t)

    def body(x_ref, t_ref, g_ref, l_ref, dx_ref, dg_ref):
        tgt = t_ref[...]

        def lf(xv, gv):
            e = _rmsnorm(xv, gv) - tgt
            return 0.5 * jnp.sum(jnp.mean(e * e, axis=-1, keepdims=True), axis=0, keepdims=True)

        lv, vjp = jax.vjp(lf, x_ref[...], g_ref[...])
        dx, dg = vjp(jnp.ones((1, 1), F32))
        dx_ref[...] = dx

        @pl.when(pl.program_id(0) == 0)
        def _():
            l_ref[...] = jnp.zeros_like(l_ref)
            dg_ref[...] = jnp.zeros_like(dg_ref)

        l_ref[...] += jnp.broadcast_to(lv, l_ref.shape)
        dg_ref[...] += dg

    row = pl.BlockSpec((tm, d), lambda i: (i, 0))
    res = pl.pallas_call(
        body, name="loss_head", grid=(t // tm,),
        in_specs=[row, row, _full_spec((1, d), 1)],
        out_specs=[_full_spec((1, LANES), 1), row, _full_spec((1, d), 1)],
        out_shape=[jax.ShapeDtypeStruct((1, LANES), F32), jax.ShapeDtypeStruct((t, d), F32),
                   jax.ShapeDtypeStruct((1, d), F32)],
        compiler_params=_cparams(1),
    )(x, target, gain)
    return res[0], res[1], res[2]


def s5_dlam(gr, gi, hr, hi, *, nb, s, cb=256, n=256):
    p = gr.shape[1]
    cb, n = min(cb, p), min(n, s)

    def body(gr_ref, gi_ref, hr_ref, hi_ref, dar_ref, dai_ref):
        @pl.when(pl.program_id(1) == 0)
        def _():
            dar_ref[...] = jnp.zeros_like(dar_ref)
            dai_ref[...] = jnp.zeros_like(dai_ref)

        def step(i, carry):
            t0 = pl.multiple_of(i * n, n)
            g_r, g_i = gr_ref[pl.ds(t0, n), :], gi_ref[pl.ds(t0, n), :]
            p_r = taps(_load_ext(hr_ref, t0, n), 2)[1]
            p_i = taps(_load_ext(hi_ref, t0, n), 2)[1]
            dar_ref[...] += jnp.sum(g_r * p_r + g_i * p_i, axis=0, keepdims=True)
            dai_ref[...] += jnp.sum(g_i * p_r - g_r * p_i, axis=0, keepdims=True)
            return carry

        lax.fori_loop(0, s // n, step, 0)

    blk = pl.BlockSpec((s, cb), lambda j, b: (b, j))
    acc = pl.BlockSpec((1, cb), lambda j, b: (0, j))
    res = pl.pallas_call(
        body, name="s5_dlam", grid=(p // cb, nb), in_specs=[blk] * 4, out_specs=[acc, acc],
        out_shape=[jax.ShapeDtypeStruct((1, p), F32)] * 2, compiler_params=_cparams(2),
    )(gr, gi, hr, hi)
    return res[0], res[1]


def s5_param_fwd(lr, li, ldt, btr, bti):
    def body(*refs):
        vals = _s5_param_f(*[r[...] for r in refs[:5]])
        for o, v in zip(refs[5:], vals):
            o[...] = v

    return pl.pallas_call(
        body, name="s5_param_fwd",
        out_shape=[jax.ShapeDtypeStruct(lr.shape, F32)] * 2 + [jax.ShapeDtypeStruct(btr.shape, F32)] * 2,
    )(lr, li, ldt, btr, bti)


def s5_param_bwd(lr, li, ldt, btr, bti, cots):
    def body(*refs):
        _, vjp = jax.vjp(_s5_param_f, *[r[...] for r in refs[:5]])
        gs = vjp(tuple(r[...] for r in refs[5:9]))
        for o, g in zip(refs[9:], gs):
            o[...] = g

    return pl.pallas_call(
        body, name="s5_param_bwd",
        out_shape=[jax.ShapeDtypeStruct(a.shape, F32) for a in (lr, li, ldt, btr, bti)],
    )(lr, li, ldt, btr, bti, *cots)


def _row_block(r, c, copies):
    lanes = -(-c // LANES) * LANES
    rb = 2048
    while rb > SUBLANES and (r % rb or copies * rb * lanes * 4 > 12 * 1024 * 1024):
        rb //= 2
    return rb if r % rb == 0 else r


def _as2d(a, lead=0):
    return a.reshape(a.shape[:lead] + (-1, a.shape[-1]))


def sumk(name, buf):
    b3 = _as2d(buf, 1)
    k, r, c = b3.shape
    rb = _row_block(r, c, 2 * (k + 1))

    def body(b_ref, o_ref):
        acc = b_ref[0]
        for i in range(1, k):
            acc = acc + b_ref[i]
        o_ref[...] = acc

    out = pl.pallas_call(
        body, name=name, grid=(r // rb,),
        in_specs=[pl.BlockSpec((k, rb, c), lambda i: (0, i, 0))],
        out_specs=pl.BlockSpec((rb, c), lambda i: (i, 0)),
        out_shape=jax.ShapeDtypeStruct((r, c), F32), compiler_params=_cparams(1),
    )(b3)
    return out.reshape(buf.shape[1:])


def adamw(name, w, g, m, v):
    shape = w.shape
    w, g, m, v = (_as2d(a) for a in (w, g, m, v))
    r, c = w.shape
    rb = _row_block(r, c, 14)

    def body(w_ref, g_ref, m_ref, v_ref, d_ref, nm_ref, nv_ref):
        gv = g_ref[...]
        nm = ADAM_B1 * m_ref[...] + (1.0 - ADAM_B1) * gv
        nv = ADAM_B2 * v_ref[...] + (1.0 - ADAM_B2) * (gv * gv)
        m_hat = nm / (1.0 - ADAM_B1 ** ADAM_STEP)
        v_hat = nv / (1.0 - ADAM_B2 ** ADAM_STEP)
        d_ref[...] = -ADAM_LR * (m_hat / (jnp.sqrt(v_hat) + ADAM_EPS) + ADAM_WD * w_ref[...])
        nm_ref[...] = nm
        nv_ref[...] = nv

    spec = pl.BlockSpec((rb, c), lambda i: (i, 0))
    res = pl.pallas_call(
        body, name=name, grid=(r // rb,), in_specs=[spec] * 4, out_specs=[spec] * 3,
        out_shape=[jax.ShapeDtypeStruct(w.shape, F32)] * 3, compiler_params=_cparams(1),
    )(w, g, m, v)
    return tuple(a.reshape(shape) for a in res)


def _place():
    x, y, c = lax.axis_index("x"), lax.axis_index("y"), lax.axis_index("c")
    chips = [(1 - x, y), (x, 1 - y), (1 - x, 1 - y)]
    return x, y, c, chips


def _chip_id(px, py):
    return 2 * px + py


def _rcopy(src, dst, send_sems, recv_sems, k, to):
    return pltpu.make_async_remote_copy(src_ref=src, dst_ref=dst, send_sem=send_sems.at[k], recv_sem=recv_sems.at[k],
                                        device_id=to, device_id_type=MESH)


def _comm_call(name, body, out_shapes, n_sems, n_local, args):
    return pl.pallas_call(
        body, name=name, out_shape=out_shapes, in_specs=[ANY] * len(args), out_specs=[ANY] * len(out_shapes),
        scratch_shapes=[pltpu.SemaphoreType.DMA((n_sems,)), pltpu.SemaphoreType.DMA((n_sems,)),
                        pltpu.SemaphoreType.DMA((n_local,))],
    )(*args)


def _finish(remote, local):
    for cp in remote:
        cp.wait_send()
    for cp in local:
        cp.wait()


def gather_chips(name, arrays):
    n = len(arrays)
    halves = [a.shape[0] // 2 for a in arrays]

    def body(*refs):
        ins, outs = refs[:n], refs[n:2 * n]
        send_sems, recv_sems, local_sems = refs[2 * n:]
        x, y, c, chips = _place()
        me, sib = _chip_id(x, y), (x, y, 1 - c)
        half = [pl.ds(c * h, h) for h in halves]
        other = [pl.ds((1 - c) * h, h) for h in halves]
        local, remote = [], []
        for a in range(n):
            local.append(pltpu.make_async_copy(ins[a], outs[a].at[me], local_sems.at[a]))
            local[-1].start()
            for j, chip in enumerate(chips):
                remote.append(_rcopy(ins[a].at[half[a]], outs[a].at[me, half[a]], send_sems, recv_sems, 6 * a + j,
                                     (*chip, c)))
                remote[-1].start()
        for a in range(n):
            for j, chip in enumerate(chips):
                landed = outs[a].at[_chip_id(*chip), half[a]]
                _rcopy(ins[a].at[half[a]], landed, send_sems, recv_sems, 6 * a + j, sib).wait_recv()
                remote.append(_rcopy(landed, landed, send_sems, recv_sems, 6 * a + 3 + j, sib))
                remote[-1].start()
        for a in range(n):
            for j, chip in enumerate(chips):
                _rcopy(ins[a].at[half[a]], outs[a].at[_chip_id(*chip), other[a]], send_sems, recv_sems, 6 * a + 3 + j,
                       sib).wait_recv()
        _finish(remote, local)

    shapes = [jax.ShapeDtypeStruct((N_CHIPS,) + a.shape, a.dtype) for a in arrays]
    return _comm_call(name, body, shapes, 6 * n, n, arrays)


def swap_halves(name, arrays):
    n = len(arrays)

    def body(*refs):
        ins, outs = refs[:n], refs[n:2 * n]
        send_sems, recv_sems, local_sems = refs[2 * n:]
        x, y, c, _ = _place()
        sib = (x, y, 1 - c)
        local, remote = [], []
        for a in range(n):
            local.append(pltpu.make_async_copy(ins[a].at[c], outs[a].at[c], local_sems.at[a]))
            local[-1].start()
            for k in range(N_CHIPS):
                remote.append(_rcopy(ins[a].at[1 - c, k], outs[a].at[c, k], send_sems, recv_sems, N_CHIPS * a + k, sib))
                remote[-1].start()
        for a in range(n):
            for k in range(N_CHIPS):
                _rcopy(ins[a].at[1 - c, k], outs[a].at[1 - c, k], send_sems, recv_sems, N_CHIPS * a + k, sib).wait_recv()
        _finish(remote, local)

    shapes = [jax.ShapeDtypeStruct(a.shape, a.dtype) for a in arrays]
    return _comm_call(name, body, shapes, N_CHIPS * n, n, arrays)


def scatter_chips(name, arrays):
    n = len(arrays)

    def body(*refs):
        ins, outs = refs[:n], refs[n:2 * n]
        send_sems, recv_sems, local_sems = refs[2 * n:]
        x, y, c, chips = _place()
        me = _chip_id(x, y)
        local, remote = [], []
        for a in range(n):
            local.append(pltpu.make_async_copy(ins[a].at[me], outs[a].at[me], local_sems.at[a]))
            local[-1].start()
            for j, chip in enumerate(chips):
                remote.append(_rcopy(ins[a].at[_chip_id(*chip)], outs[a].at[me], send_sems, recv_sems, 3 * a + j,
                                     (*chip, c)))
                remote[-1].start()
        for a in range(n):
            for j, chip in enumerate(chips):
                _rcopy(ins[a].at[me], outs[a].at[_chip_id(*chip)], send_sems, recv_sems, 3 * a + j,
                       (*chip, c)).wait_recv()
        _finish(remote, local)

    shapes = [jax.ShapeDtypeStruct(a.shape, a.dtype) for a in arrays]
    return _comm_call(name, body, shapes, 3 * n, n, arrays)


def join_halves(name, arrays):
    n = len(arrays)
    offs = [sum(a.shape[0] for a in arrays[:i]) for i in range(n)]

    def body(*refs):
        ins, outs = refs[:n], refs[n:2 * n]
        send_sems, recv_sems, local_sems = refs[2 * n:]
        x, y, c, _ = _place()
        sib = (x, y, 1 - c)
        local, remote = [], []
        for a in range(n):
            h = arrays[a].shape[0]
            local.append(pltpu.make_async_copy(ins[a], outs[a].at[pl.ds(c * h, h)], local_sems.at[a]))
            local[-1].start()
            for k in range(h):
                remote.append(_rcopy(ins[a].at[k], outs[a].at[c * h + k], send_sems, recv_sems, offs[a] + k, sib))
                remote[-1].start()
        for a in range(n):
            h = arrays[a].shape[0]
            for k in range(h):
                _rcopy(ins[a].at[k], outs[a].at[(1 - c) * h + k], send_sems, recv_sems, offs[a] + k, sib).wait_recv()
        _finish(remote, local)

    shapes = [jax.ShapeDtypeStruct((2 * a.shape[0],) + a.shape[1:], a.dtype) for a in arrays]
    return _comm_call(name, body, shapes, offs[-1] + arrays[-1].shape[0], n, arrays)


def gather_all(name, slab):
    def body(x_ref, out_ref, send_sems, recv_sems, local_sems):
        x, y, c, chips = _place()
        me, sib = (x, y, c), (x, y, 1 - c)

        def rows(px, py, pc):
            return out_ref.at[4 * px + 2 * py + pc]

        mine = pltpu.make_async_copy(x_ref, rows(*me), local_sems.at[0])
        mine.start()
        first = [_rcopy(x_ref, rows(*me), send_sems, recv_sems, 0, sib)]
        first += [_rcopy(x_ref, rows(*me), send_sems, recv_sems, 1 + j, (*chip, c)) for j, chip in enumerate(chips)]
        for cp in first:
            cp.start()
        passed = []
        for j, chip in enumerate(chips):
            landed = rows(*chip, c)
            _rcopy(x_ref, landed, send_sems, recv_sems, 1 + j, sib).wait_recv()
            passed.append(_rcopy(landed, landed, send_sems, recv_sems, 4 + j, sib))
            passed[-1].start()
        _rcopy(x_ref, rows(*sib), send_sems, recv_sems, 0, sib).wait_recv()
        for j, chip in enumerate(chips):
            _rcopy(x_ref, rows(*chip, 1 - c), send_sems, recv_sems, 4 + j, sib).wait_recv()
        for cp in first + passed:
            cp.wait_send()
        mine.wait()

    return _comm_call(name, body, [jax.ShapeDtypeStruct((8,) + slab.shape, slab.dtype)], 7, 1, [slab])[0]


D, BW, NH_B, NG, NP_S5, CG = 1024, 512, 4, 32, 64, 16
P_S5 = NG * NP_S5
C_GATES, C_XA, C_GA, C_Q, C_K, C_V, C_Z, C_UC, C_BA, N_PROJ = 0, 3072, 3584, 4096, 4608, 5120, 5632, 6144, 6656, 7168
_IN_OFFS = (0, 512, 1024, 1536, 2048, 2560, 3072, 3076, 3080, 3592, 6664)


def _blockdiag2(w):
    z = jnp.zeros((w.shape[0] // 2, 2 * w.shape[1], 2 * w.shape[2]), w.dtype)
    return z.at[:, :64, :64].set(w[0::2]).at[:, 64:, 64:].set(w[1::2])


def _unblockdiag2(d):
    return jnp.stack([d[:, :64, :64], d[:, 64:, 64:]], axis=1).reshape(-1, 64, 64)


def _layer_params(w, l):
    p = {}
    wi = w["w_in"][l]
    xa, ga, q, k, v, z, beta, alpha, uc, gates = [wi[:, a:b] for a, b in zip(_IN_OFFS[:-1], _IN_OFFS[1:])]
    ba = jnp.zeros((D, NH_B, LANES), wi.dtype).at[:, :, 0].set(beta).at[:, :, 1].set(alpha).reshape(D, NH_B * LANES)
    p["w_in_pad"] = jnp.concatenate([gates, xa, ga, q, k, v, z, uc, ba], axis=1)
    row = lambda a: a.reshape(1, -1).astype(F32)
    acw, bcw, fcw = w["a_conv_w"][l], w["b_conv_w"][l], w["ffn_conv_w"][l]
    p["a_taps"] = [row(acw[i]) for i in range(4)]
    p["a_cb"], p["a_br"], p["a_bi"], p["a_lam"] = (row(w[n][l]) for n in ("a_conv_b", "a_b_r", "a_b_i", "a_lam"))
    p["a_wr"], p["a_wi"] = _blockdiag2(w["a_w_r"][l]), _blockdiag2(w["a_w_i"][l])
    p["b_taps"] = [row(bcw[i, j * BW:(j + 1) * BW]) for j in range(3) for i in range(4)]
    p["b_alog"] = row(jnp.repeat(w["b_a_log"][l], LANES))
    p["b_dtb"] = row(jnp.repeat(w["b_dt_bias"][l], LANES))
    p["b_nw"] = row(w["b_norm"][l])
    p["c_lr"], p["c_li"] = w["c_lam_re"][l][:, None, :], w["c_lam_im"][l][:, None, :]
    p["c_ldt"] = w["c_log_dt"][l][:, None, None]
    p["c_btr"], p["c_bti"] = w["c_b_re"][l].transpose(0, 2, 1), w["c_b_im"][l].transpose(0, 2, 1)
    eye = jnp.eye(NG, dtype=F32)
    p["eye"] = eye
    expand = lambda t: (t[:, :, None, :] * eye[:, None, :, None]).reshape(t.shape[0] * t.shape[1], -1)
    p["expand"] = expand
    p["c_cre"] = expand(w["c_c_re"][l].transpose(0, 2, 1))
    p["c_cimn"] = expand(-w["c_c_im"][l].transpose(0, 2, 1))
    p["c_d"], p["c_glu_b"] = row(w["c_d"][l]), row(w["c_glu_b"][l])
    p["c_glu_w"] = w["c_glu_w"][l].astype(F32)
    p["wb"] = [w["w_branch"][l, k].astype(F32) for k in range(3)]
    p["bg"] = [row(w["b_gate"][l, k * D:(k + 1) * D]) for k in range(3)]
    for n in ("w_out", "xa_w_q", "xa_w_kv", "xa_w_o", "ffn_w_up", "ffn_w_down"):
        p[n] = w[n][l]
    for n in ("mix_norm", "xa_norm", "mem_norm", "ffn_norm"):
        p[n] = row(w[n][l])
    dff = fcw.shape[1] // 2
    p["f_taps"] = [row(fcw[i, :dff]) for i in range(3)] + [row(fcw[i, dff:]) for i in range(3)]
    p["f_b"] = [row(w["ffn_conv_b"][l, :dff]), row(w["ffn_conv_b"][l, dff:])]
    return p


def _chan(arr, cb=LANES):
    return (arr, (1, cb), lambda j: (0, j))


def _a_params(p):
    blk = lambda a: (a, (1, LANES, LANES), lambda j: (j, 0, 0))
    return [_chan(t) for t in p["a_taps"]] + [_chan(p["a_cb"]), blk(p["a_wr"]), blk(p["a_wi"]),
                                              _chan(p["a_br"]), _chan(p["a_bi"]), _chan(p["a_lam"])]


def _f_params(p):
    return [_chan(t) for t in p["f_taps"]] + [_chan(b) for b in p["f_b"]]


def _layer_fwd(x, mem, p, nb, s, m):
    sv = {"x": x}
    h = rowop_fwd("norm_mix", _norm_f, [(x, 0, D)], [p["mix_norm"]], [(D, BF16)])[0]
    proj = mm("in_proj", h, p["w_in_pad"], "nn")
    a, b = seqop_fwd("rglru_pre", _rglru_pre_f, [(proj, C_XA)], _a_params(p), [F32, F32], nb=nb, s=s, nblk=BW // LANES)
    ha = rscan("rglru_scan", a, b, False, nb=nb, s=s)
    yb, states = deltanet_fwd(proj, (C_Q, C_K, C_V, C_Z, C_BA), p["b_taps"], p["b_alog"], p["b_dtb"], p["b_nw"],
                              nb=nb, s=s, nh=NH_B)
    ar, ai, bbr, bbi = s5_param_fwd(p["c_lr"], p["c_li"], p["c_ldt"], p["c_btr"], p["c_bti"])
    bd = jnp.concatenate([p["expand"](bbr), p["expand"](bbi)], axis=1)
    ar2, ai2 = ar.reshape(1, P_S5), ai.reshape(1, P_S5)
    bu = mm("s5_bu", proj, bd, "nn", a_cols=(C_UC, BW))
    hr, hi = cscan("s5_scan", bu, ar2, ai2, False, nb=nb, s=s)
    ys = mm("s5_y_im", hi, p["c_cimn"], "nn", res=mm("s5_y_re", hr, p["c_cre"], "nn"))
    yc = rowop_fwd("s5_post", _s5_post_f, [(ys, 0, BW), (proj, C_UC, BW)], [p["c_d"], p["c_glu_w"], p["c_glu_b"]],
                   [(BW, F32)])[0]
    merge_rows = [(ha, 0, BW), (proj, C_GA, BW), (yb, 0, BW), (yc, 0, BW),
                  (proj, 0, D), (proj, D, D), (proj, 2 * D, D)]
    merged = rowop_fwd("merge", _merge_f, merge_rows, p["wb"] + p["bg"], [(D, BF16)])[0]
    x1 = mm("out_proj", merged, p["w_out"], "nn", res=x)
    hq = rowop_fwd("norm_xa", _norm_f, [(x1, 0, D)], [p["xa_norm"]], [(D, BF16)])[0]
    q = mm("xa_q", hq, p["xa_w_q"], "nn", out_dtype=BF16)
    memn = rowop_fwd("norm_mem", _norm_f, [(mem, 0, D)], [p["mem_norm"]], [(D, BF16)])[0]
    kv = mm("xa_kv", memn, p["xa_w_kv"], "nn")
    o = attn_fwd(q, kv, nb=nb, s=s, m=m, nh=4)
    x2 = mm("xa_o", o, p["xa_w_o"], "nn", res=x1)
    hf = rowop_fwd("norm_ffn", _norm_f, [(x2, 0, D)], [p["ffn_norm"]], [(D, BF16)])[0]
    up = mm("ffn_up", hf, p["ffn_w_up"], "nn")
    dff = up.shape[1] // 2
    act = seqop_fwd("ffn_act", _ffn_act_f, [(up, 0), (up, dff)], _f_params(p), [BF16], nb=nb, s=s,
                    nblk=dff // LANES)[0]
    x3 = mm("ffn_down", act, p["ffn_w_down"], "nn", res=x2)
    sv.update(h=h, proj=proj, a=a, ha=ha, yb=yb, states=states, ar=ar2, ai=ai2, bd=bd, hr=hr, hi=hi, ys=ys, yc=yc,
              merged=merged, x1=x1, hq=hq, q=q, memn=memn, kv=kv, o=o, x2=x2, hf=hf, up=up, act=act)
    return x3, sv


def _layer_bwd(dx3, mem, sv, p, nb, s, m):
    g = {}
    up, dff = sv["up"], sv["up"].shape[1] // 2
    dact = mm("d_ffn_down_x", dx3, p["ffn_w_down"], "nt")
    g["ffn_w_down"] = mm("d_ffn_down_w", sv["act"], dx3, "tn")
    (dug, duv), dfp = seqop_bwd("ffn_act_bwd", _ffn_act_f, [(up, 0), (up, dff)], _f_params(p), [(dact, 0)],
                                lambda t0, e: (e[SUBLANES:],), nb=nb, s=s, nblk=dff // LANES)
    dhf = mm("d_ffn_up_x1", duv, p["ffn_w_up"], "nt", b_cols=(dff, dff),
             res=mm("d_ffn_up_x0", dug, p["ffn_w_up"], "nt", b_cols=(0, dff)))
    g["ffn_w_up"] = jnp.concatenate([mm("d_ffn_up_w0", sv["hf"], dug, "tn"), mm("d_ffn_up_w1", sv["hf"], duv, "tn")],
                                    axis=1)
    g["ffn_conv_w"] = jnp.concatenate([jnp.concatenate(dfp[0:3], axis=0), jnp.concatenate(dfp[3:6], axis=0)], axis=1)
    g["ffn_conv_b"] = jnp.concatenate([dfp[6], dfp[7]], axis=1)[0]
    (dx2,), (gn,) = rowop_bwd("norm_ffn_bwd", _norm_f, [(sv["x2"], 0, D)], [p["ffn_norm"]], [(dhf, 0, D)], [True],
                              add=(dx3, 0, D))
    g["ffn_norm"] = gn[0]
    do = mm("d_xa_o_x", dx2, p["xa_w_o"], "nt")
    g["xa_w_o"] = mm("d_xa_o_w", sv["o"], dx2, "tn")
    dq, dk, dv = attn_bwd(sv["q"], sv["kv"], do, nb=nb, s=s, m=m, nh=4)
    dkv = jnp.concatenate([dk, dv], axis=1)
    dhq = mm("d_xa_q_x", dq, p["xa_w_q"], "nt")
    g["xa_w_q"] = mm("d_xa_q_w", sv["hq"], dq, "tn")
    dmemn = mm("d_xa_kv_x", dkv, p["xa_w_kv"], "nt")
    g["xa_w_kv"] = mm("d_xa_kv_w", sv["memn"], dkv, "tn")
    (dx1,), (gn,) = rowop_bwd("norm_xa_bwd", _norm_f, [(sv["x1"], 0, D)], [p["xa_norm"]], [(dhq, 0, D)], [True],
                              add=(dx2, 0, D))
    g["xa_norm"] = gn[0]
    _, (gn,) = rowop_bwd("norm_mem_bwd", _norm_f, [(mem, 0, D)], [p["mem_norm"]], [(dmemn, 0, D)], [False])
    g["mem_norm"] = gn[0]
    proj = sv["proj"]
    dmerged = mm("d_out_proj_x", dx1, p["w_out"], "nt")
    g["w_out"] = mm("d_out_proj_w", sv["merged"], dx1, "tn")
    merge_rows = [(sv["ha"], 0, BW), (proj, C_GA, BW), (sv["yb"], 0, BW), (sv["yc"], 0, BW),
                  (proj, 0, D), (proj, D, D), (proj, 2 * D, D)]
    (dha, dga, dyb, dyc, dg0, dg1, dg2), dmp = rowop_bwd("merge_bwd", _merge_f, merge_rows, p["wb"] + p["bg"],
                                                          [(dmerged, 0, D)], [True] * 7)
    g["w_branch"] = jnp.stack(dmp[0:3])
    g["b_gate"] = jnp.concatenate(dmp[3:6], axis=1)[0]
    (dys, duc), (gd, ggw, ggb) = rowop_bwd("s5_post_bwd", _s5_post_f, [(sv["ys"], 0, BW), (proj, C_UC, BW)],
                                           [p["c_d"], p["c_glu_w"], p["c_glu_b"]], [(dyc, 0, BW)], [True, True])
    g["c_d"], g["c_glu_w"], g["c_glu_b"] = gd[0], ggw, ggb[0]
    cd = jnp.concatenate([p["c_cre"], p["c_cimn"]], axis=0)
    dhs = mm("d_s5_y_x", dys, cd, "nt")
    dcre = mm("d_s5_y_wre", sv["hr"], dys, "tn")
    dcimn = mm("d_s5_y_wim", sv["hi"], dys, "tn")
    gr, gi = cscan("s5_scan_bwd", dhs, sv["ar"], sv["ai"], True, nb=nb, s=s)
    dar, dai = s5_dlam(gr, gi, sv["hr"], sv["hi"], nb=nb, s=s)
    duc = mm("d_s5_bu_x1", gi, sv["bd"], "nt", b_cols=(P_S5, P_S5),
             res=mm("d_s5_bu_x0", gr, sv["bd"], "nt", b_cols=(0, P_S5), res=duc))
    dbd_re = mm("d_s5_bu_wre", proj, gr, "tn", a_cols=(C_UC, BW))
    dbd_im = mm("d_s5_bu_wim", proj, gi, "tn", a_cols=(C_UC, BW))
    eye = p["eye"]
    pick = lambda dmat, r, c: jnp.einsum("grhc,gh->grc", dmat.reshape(NG, r, NG, c), eye)
    glr, gli, gldt, gbtr, gbti = s5_param_bwd(
        p["c_lr"], p["c_li"], p["c_ldt"], p["c_btr"], p["c_bti"],
        (dar.reshape(NG, 1, NP_S5), dai.reshape(NG, 1, NP_S5), pick(dbd_re, CG, NP_S5), pick(dbd_im, CG, NP_S5)))
    g["c_lam_re"], g["c_lam_im"], g["c_log_dt"] = glr[:, 0, :], gli[:, 0, :], gldt[:, 0, 0]
    g["c_b_re"], g["c_b_im"] = gbtr.transpose(0, 2, 1), gbti.transpose(0, 2, 1)
    g["c_c_re"] = pick(dcre, NP_S5, CG).transpose(0, 2, 1)
    g["c_c_im"] = -pick(dcimn, NP_S5, CG).transpose(0, 2, 1)
    (dq_b, dk_b, dv_b, dz_b, dba), dbp = deltanet_bwd(proj, (C_Q, C_K, C_V, C_Z, C_BA), p["b_taps"], p["b_alog"],
                                                      p["b_dtb"], p["b_nw"], sv["states"], dyb, nb=nb, s=s, nh=NH_B)
    g["b_conv_w"] = jnp.concatenate([jnp.concatenate(dbp[4 * j:4 * j + 4], axis=0) for j in range(3)], axis=1)
    g["b_a_log"] = dbp[12].reshape(NH_B, LANES)[:, 0]
    g["b_dt_bias"] = dbp[13].reshape(NH_B, LANES)[:, 0]
    g["b_norm"] = dbp[14][0]
    gsc = rscan("rglru_scan_bwd", sv["a"], dha, True, nb=nb, s=s)
    (dxa,), dap = seqop_bwd("rglru_pre_bwd", _rglru_pre_f, [(proj, C_XA)], _a_params(p), [(gsc, 0), (sv["ha"], 0)],
                            lambda t0, ge, he: (ge[SUBLANES:] * taps(he, 2)[1], ge[SUBLANES:]),
                            nb=nb, s=s, nblk=BW // LANES)
    g["a_conv_w"] = jnp.concatenate(dap[0:4], axis=0)
    g["a_conv_b"], g["a_b_r"], g["a_b_i"], g["a_lam"] = dap[4][0], dap[7][0], dap[8][0], dap[9][0]
    g["a_w_r"], g["a_w_i"] = _unblockdiag2(dap[5]), _unblockdiag2(dap[6])
    dproj = jnp.concatenate([dg0, dg1, dg2, dxa, dga, dq_b, dk_b, dv_b, dz_b, duc, dba], axis=1)
    dh = mm("d_in_proj_x", dproj, p["w_in_pad"], "nt")
    dwp = mm("d_in_proj_w", sv["h"], dproj, "tn")
    dba_w = dwp[:, C_BA:].reshape(D, NH_B, LANES)
    pieces = [dwp[:, c0:c0 + BW] for c0 in (C_XA, C_GA, C_Q, C_K, C_V, C_Z)]
    g["w_in"] = jnp.concatenate(pieces + [dba_w[:, :, 0], dba_w[:, :, 1], dwp[:, C_UC:C_UC + BW], dwp[:, :3 * D]],
                                axis=1)
    (dx,), (gn,) = rowop_bwd("norm_mix_bwd", _norm_f, [(sv["x"], 0, D)], [p["mix_norm"]], [(dh, 0, D)], [True],
                             add=(dx1, 0, D))
    g["mix_norm"] = gn[0]
    return dx, g


def _local_step(x, mem, target, w):
    nb, s, _ = x.shape
    m = mem.shape[1]
    depth = w["mix_norm"].shape[0]
    x2d, mem2d = x.reshape(nb * s, D), mem.reshape(nb * m, D)
    params, saved = [], []
    for l in range(depth):
        params.append(_layer_params(w, l))
        x2d, sv = _layer_fwd(x2d, mem2d, params[l], nb, s, m)
        saved.append(sv)
    loss_row, dx, gfn = loss_head(x2d, target.reshape(nb * s, D), w["final_norm"].reshape(1, D))
    grads = [None] * depth
    for l in reversed(range(depth)):
        dx, grads[l] = _layer_bwd(dx, mem2d, saved[l], params[l], nb, s, m)
    out = {n: jnp.stack([grads[l][n] for l in range(depth)]) for n in grads[0]}
    out["final_norm"] = gfn[0]
    return loss_row, dx.reshape(x.shape), out


SLAB_UNIT = 2 * 1024 * LANES


def _slab(parts, dtype, unit):
    flat = jnp.concatenate([a.astype(dtype).reshape(-1) for a in parts])
    pad = (-flat.shape[0]) % unit
    return jnp.pad(flat, (0, pad)) if pad else flat


def _unslab(flat, like):
    out, off = [], 0
    for a in like:
        out.append(flat[off:off + a.size].reshape(a.shape))
        off += a.size
    return out


def _gather_weights(name, local, names_axes, dtype):
    got = gather_chips(name, [local[n].astype(dtype) for n, _ in names_axes])
    return {n: jnp.concatenate([g[j] for j in range(N_CHIPS)], axis=ax) for (n, ax), g in zip(names_axes, got)}


def kernel(x, mem, mix_norm, w_in, b_gate, a_conv_w, a_conv_b, a_w_r, a_b_r, a_w_i, a_b_i, a_lam, b_conv_w, b_a_log, b_dt_bias, b_norm, c_lam_re, c_lam_im, c_log_dt, c_b_re, c_b_im, c_c_re, c_c_im, c_d, c_glu_w, c_glu_b, w_branch, w_out, xa_norm, mem_norm, xa_w_q, xa_w_kv, xa_w_o, ffn_norm, ffn_w_up, ffn_conv_w, ffn_conv_b, ffn_w_down, final_norm, loss_target, m_mix_norm, m_w_in, m_b_gate, m_a_conv_w, m_a_conv_b, m_a_w_r, m_a_b_r, m_a_w_i, m_a_b_i, m_a_lam, m_b_conv_w, m_b_a_log, m_b_dt_bias, m_b_norm, m_c_lam_re, m_c_lam_im, m_c_log_dt, m_c_b_re, m_c_b_im, m_c_c_re, m_c_c_im, m_c_d, m_c_glu_w, m_c_glu_b, m_w_branch, m_w_out, m_xa_norm, m_mem_norm, m_xa_w_q, m_xa_w_kv, m_xa_w_o, m_ffn_norm, m_ffn_w_up, m_ffn_conv_w, m_ffn_conv_b, m_ffn_w_down, m_final_norm, v_mix_norm, v_w_in, v_b_gate, v_a_conv_w, v_a_conv_b, v_a_w_r, v_a_b_r, v_a_w_i, v_a_b_i, v_a_lam, v_b_conv_w, v_b_a_log, v_b_dt_bias, v_b_norm, v_c_lam_re, v_c_lam_im, v_c_log_dt, v_c_b_re, v_c_b_im, v_c_c_re, v_c_c_im, v_c_d, v_c_glu_w, v_c_glu_b, v_w_branch, v_w_out, v_xa_norm, v_mem_norm, v_xa_w_q, v_xa_w_kv, v_xa_w_o, v_ffn_norm, v_ffn_w_up, v_ffn_conv_w, v_ffn_conv_b, v_ffn_w_down, v_final_norm):
    env = dict(locals())
    w = {n: env[n] for n in WEIGHTS}
    mom = {n: env["m_" + n] for n in WEIGHTS}
    var = {n: env["v_" + n] for n in WEIGHTS}

    full = dict(w)
    full.update(_gather_weights("gather_mats", w, SHARDED_MATS, BF16))
    full.update(_gather_weights("gather_convs", w, SHARDED_CONVS, F32))
    loss_row, grad_x, gfull = _local_step(x, mem, loss_target, full)

    depth = mix_norm.shape[0]
    assert depth % 2 == 0
    hl = depth // 2
    res = {}
    by_chip = []
    for n, ax in SHARDED:
        width = w[n].shape[ax]
        by_chip.append(jnp.stack([jnp.stack([
            lax.slice_in_dim(gfull[n][hf * hl:(hf + 1) * hl], j * width, (j + 1) * width, axis=ax)
            for j in range(N_CHIPS)]) for hf in range(2)]))
    both = swap_halves("swap_halves", by_chip)
    part = [sumk("add_cores_" + n, b) for (n, _), b in zip(SHARDED, both)]
    recv = scatter_chips("scatter_chips", part)
    mine = [sumk("add_chips_" + n, r) for (n, _), r in zip(SHARDED, recv)]
    for (n, _), g in zip(SHARDED, join_halves("join_halves", mine)):
        res["grad_" + n] = g
        res["delta_" + n], res["new_m_" + n], res["new_v_" + n] = adamw("adamw_" + n, w[n], g, mom[n], var[n])

    unit = SUBLANES * LANES
    small = _slab([gfull[n] for n in REPLICATED] + [loss_row[:, :1]], F32, unit).reshape(-1, LANES)
    g_rp = sumk("add_devices", gather_all("gather_all", small))
    slab1 = lambda d: _slab([d[n] for n in REPLICATED] + [jnp.zeros((1, 1), F32)], F32, unit).reshape(-1, LANES)
    d_rp, m_rp, v_rp = adamw("adamw_replicated", slab1(w), g_rp, slab1(mom), slab1(var))
    like_rp = [w[n] for n in REPLICATED] + [jnp.zeros((1, 1), F32)]
    for kind, rp in (("grad", g_rp), ("delta", d_rp), ("new_m", m_rp), ("new_v", v_rp)):
        for n, a in zip(list(REPLICATED) + ["loss"], _unslab(rp.reshape(-1), like_rp)):
            res[kind + "_" + n] = a
    loss = res["grad_loss"].reshape(())
    return (loss, grad_x, *[res["grad_" + n] for n in WEIGHTS], *[res["delta_" + n] for n in WEIGHTS],
            *[res["new_m_" + n] for n in WEIGHTS], *[res["new_v_" + n] for n in WEIGHTS])
```

```python
import functools
import math

import jax
import jax.numpy as jnp
from jax import lax
from jax.experimental import pallas as pl
from jax.experimental.pallas import tpu as pltpu

F32, BF16 = jnp.float32, jnp.bfloat16
MESH = pl.DeviceIdType.MESH
ANY = pl.BlockSpec(memory_space=pl.ANY)

VMEM_LIMIT_V7X = 56 * 1024 * 1024
MM_TILE = 1024
LANES, SUBLANES = 128, 8
EPS = 1e-6
RG_C = 8.0
DN_CHUNK = 64
N_CHIPS = 4
ADAM_LR, ADAM_B1, ADAM_B2, ADAM_EPS, ADAM_WD, ADAM_STEP = 0.001, 0.9, 0.999, 1e-08, 0.01, 10

SHARDED_MATS = (("w_in", 2), ("c_glu_w", 1), ("w_branch", 3), ("w_out", 1), ("xa_w_q", 1), ("xa_w_kv", 2),
                ("xa_w_o", 1), ("ffn_w_up", 2), ("ffn_w_down", 1))
SHARDED_CONVS = (("a_conv_w", 2), ("b_conv_w", 2), ("ffn_conv_w", 2))
SHARDED = SHARDED_MATS + SHARDED_CONVS
WEIGHTS = ("mix_norm", "w_in", "b_gate", "a_conv_w", "a_conv_b", "a_w_r", "a_b_r", "a_w_i", "a_b_i", "a_lam",
           "b_conv_w", "b_a_log", "b_dt_bias", "b_norm", "c_lam_re", "c_lam_im", "c_log_dt", "c_b_re", "c_b_im",
           "c_c_re", "c_c_im", "c_d", "c_glu_w", "c_glu_b", "w_branch", "w_out", "xa_norm", "mem_norm", "xa_w_q",
           "xa_w_kv", "xa_w_o", "ffn_norm", "ffn_w_up", "ffn_conv_w", "ffn_conv_b", "ffn_w_down", "final_norm")
REPLICATED = tuple(n for n in WEIGHTS if n not in dict(SHARDED))


def _cparams(n_axes):
    return pltpu.CompilerParams(dimension_semantics=("arbitrary",) * n_axes, vmem_limit_bytes=VMEM_LIMIT_V7X)


def _tile(n, pref, off=0):
    t = pref
    while t >= LANES:
        if n % t == 0 and off % t == 0:
            return t
        t //= 2
    assert off == 0, (n, pref, off)
    return n


def _full_spec(shape, n_grid):
    nd = len(shape)
    if n_grid == 1:
        return pl.BlockSpec(shape, lambda i: (0,) * nd)
    return pl.BlockSpec(shape, lambda i, j: (0,) * nd)


_NN, _NT, _TN = ((1,), (0,)), ((1,), (1,)), ((0,), (0,))


def _dot(a, b, dims, hp):
    if hp:
        return lax.dot_general(a, b, (dims, ((), ())), precision=lax.Precision.HIGHEST, preferred_element_type=F32)
    return lax.dot_general(a.astype(BF16), b.astype(BF16), (dims, ((), ())), preferred_element_type=F32)


@functools.partial(jax.custom_vjp, nondiff_argnums=(2,))
def mm_nn(a, b, hp=False):
    return _dot(a, b, _NN, hp)


@functools.partial(jax.custom_vjp, nondiff_argnums=(2,))
def mm_nt(a, b, hp=False):
    return _dot(a, b, _NT, hp)


@functools.partial(jax.custom_vjp, nondiff_argnums=(2,))
def mm_tn(a, b, hp=False):
    return _dot(a, b, _TN, hp)


mm_nn.defvjp(lambda a, b, hp: (_dot(a, b, _NN, hp), (a, b)),
             lambda hp, r, g: (mm_nt(g, r[1], hp), mm_tn(r[0], g, hp)))
mm_nt.defvjp(lambda a, b, hp: (_dot(a, b, _NT, hp), (a, b)),
             lambda hp, r, g: (mm_nn(g, r[1], hp), mm_tn(g, r[0], hp)))
mm_tn.defvjp(lambda a, b, hp: (_dot(a, b, _TN, hp), (a, b)),
             lambda hp, r, g: (mm_nt(r[1], g, hp), mm_nn(r[0], g, hp)))


@functools.partial(jax.custom_vjp, nondiff_argnums=(1,))
def taps(xext, k):
    return tuple((xext if s == 0 else pltpu.roll(xext, s, 0))[SUBLANES:] for s in range(k))


def _taps_fwd(xext, k):
    return taps(xext, k), None


def _taps_bwd(k, _, gs):
    tot = None
    for s, g in enumerate(gs):
        gp = jnp.concatenate([jnp.zeros((SUBLANES, g.shape[1]), g.dtype), g], axis=0)
        if s:
            gp = pltpu.roll(gp, gp.shape[0] - s, 0)
        tot = gp if tot is None else tot + gp
    return (tot,)


taps.defvjp(_taps_fwd, _taps_bwd)


def _gelu(x):
    return x * (0.5 * (1.0 + jnp.tanh(0.7978845608028654 * (x + 0.044715 * (x * x * x)))))


def _sigmoid(x):
    return jax.nn.sigmoid(x)


def _softplus(x):
    return jnp.maximum(x, 0.0) + jnp.log1p(jnp.exp(-jnp.abs(x)))


def _rmsnorm(x, g):
    return x * lax.rsqrt(jnp.mean(x * x, axis=-1, keepdims=True) + EPS) * g


def _lane_pick(x, lane):
    sel = lax.broadcasted_iota(jnp.int32, x.shape, 1) == lane
    return jnp.sum(jnp.where(sel, x, 0.0), axis=1, keepdims=True)


def _load_ext(ref, t0, n):
    main = ref[pl.ds(t0, n), :].astype(F32)
    hstart = pl.multiple_of(jnp.maximum(t0 - SUBLANES, 0), SUBLANES)
    halo = ref[pl.ds(hstart, SUBLANES), :].astype(F32)
    halo = jnp.where(t0 > 0, halo, 0.0)
    return jnp.concatenate([halo, main], axis=0)


def mm(name, a, b, mode, *, out_dtype=F32, res=None, a_cols=None, b_cols=None):
    a0, aw = a_cols if a_cols else (0, a.shape[1])
    b0, bw = b_cols if b_cols else (0, b.shape[1])
    big = MM_TILE
    if mode == "nn":
        m, k, n = a.shape[0], aw, bw
        assert b.shape[0] == k
        tm, tk, tn = _tile(m, big), _tile(k, big, a0), _tile(n, big, b0)
        a_spec = pl.BlockSpec((tm, tk), lambda i, j, kk: (i, kk + a0 // tk))
        b_spec = pl.BlockSpec((tk, tn), lambda i, j, kk: (kk, j + b0 // tn))
        dims = _NN
    elif mode == "nt":
        m, k, n = a.shape[0], aw, b.shape[0]
        assert bw == k
        tm, tn = _tile(m, big), _tile(n, big)
        tk = _tile(k, big, math.gcd(a0, b0) if (a0 or b0) else 0)
        assert a0 % tk == 0 and b0 % tk == 0
        a_spec = pl.BlockSpec((tm, tk), lambda i, j, kk: (i, kk + a0 // tk))
        b_spec = pl.BlockSpec((tn, tk), lambda i, j, kk: (j, kk + b0 // tk))
        dims = _NT
    else:
        k, m, n = a.shape[0], aw, bw
        assert b.shape[0] == k
        tk, tm, tn = _tile(k, big), _tile(m, big, a0), _tile(n, big, b0)
        a_spec = pl.BlockSpec((tk, tm), lambda i, j, kk: (kk, i + a0 // tm))
        b_spec = pl.BlockSpec((tk, tn), lambda i, j, kk: (kk, j + b0 // tn))
        dims = _TN
    nk = k // tk
    has_res = res is not None

    def tile_dot(a_ref, b_ref):
        return lax.dot_general(a_ref[...].astype(BF16), b_ref[...].astype(BF16), (dims, ((), ())),
                               preferred_element_type=F32)

    def finish(out, refs, o_ref):
        if has_res:
            out = out + refs[2][...].astype(F32)
        o_ref[...] = out.astype(o_ref.dtype)

    def body_single(*refs):
        finish(tile_dot(refs[0], refs[1]), refs, refs[-1])

    def body_acc(*refs):
        o_ref, acc_ref = refs[-2], refs[-1]
        kk = pl.program_id(2)

        @pl.when(kk == 0)
        def _():
            acc_ref[...] = tile_dot(refs[0], refs[1])

        @pl.when((kk > 0) & (kk < nk - 1))
        def _():
            acc_ref[...] += tile_dot(refs[0], refs[1])

        @pl.when(kk == nk - 1)
        def _():
            finish(acc_ref[...] + tile_dot(refs[0], refs[1]), refs, o_ref)

    o_spec = pl.BlockSpec((tm, tn), lambda i, j, kk: (i, j))
    return pl.pallas_call(
        body_single if nk == 1 else body_acc, name=name, grid=(m // tm, n // tn, nk),
        in_specs=[a_spec, b_spec] + ([o_spec] if has_res else []), out_specs=o_spec,
        out_shape=jax.ShapeDtypeStruct((m, n), out_dtype),
        scratch_shapes=[] if nk == 1 else [pltpu.VMEM((tm, tn), F32)], compiler_params=_cparams(3),
    )(*((a, b, res) if has_res else (a, b)))


def _row_specs(rows, tm):
    return [pl.BlockSpec((tm, cw), lambda i, c=c0 // cw: (i, c)) for (_, c0, cw) in rows]


def rowop_fwd(name, f, rows, params, outs, tm=256):
    t = rows[0][0].shape[0]
    tm = min(tm, t)
    n_r, n_p = len(rows), len(params)

    def body(*refs):
        vals = f(*[r[...].astype(F32) for r in refs[:n_r]], *[p[...] for p in refs[n_r:n_r + n_p]])
        for o, v in zip(refs[n_r + n_p:], vals):
            o[...] = v.astype(o.dtype)

    res = pl.pallas_call(
        body, name=name, grid=(t // tm,),
        in_specs=_row_specs(rows, tm) + [_full_spec(p.shape, 1) for p in params],
        out_specs=[pl.BlockSpec((tm, cw), lambda i: (i, 0)) for cw, _ in outs],
        out_shape=[jax.ShapeDtypeStruct((t, cw), dt) for cw, dt in outs],
        compiler_params=_cparams(1),
    )(*[r[0] for r in rows], *params)
    return list(res)


def rowop_bwd(name, f, rows, params, couts, need, tm=256, add=None):
    t = rows[0][0].shape[0]
    tm = min(tm, t)
    n_r, n_p, n_c = len(rows), len(params), len(couts)
    want = [k for k in range(n_r) if need[k]]
    adds = [add] if add is not None else []

    def body(*refs):
        xs = [r[...].astype(F32) for r in refs[:n_r]]
        ps = [p[...] for p in refs[n_r:n_r + n_p]]
        cs = tuple(c[...].astype(F32) for c in refs[n_r + n_p:n_r + n_p + n_c])
        outs = refs[n_r + n_p + n_c + len(adds):]
        _, vjp = jax.vjp(f, *xs, *ps)
        gs = vjp(cs)
        for o, k in zip(outs[:len(want)], want):
            o[...] = gs[k] + refs[n_r + n_p + n_c][...] if (adds and k == want[0]) else gs[k]

        @pl.when(pl.program_id(0) == 0)
        def _():
            for o in outs[len(want):]:
                o[...] = jnp.zeros_like(o)

        for o, g in zip(outs[len(want):], gs[n_r:]):
            o[...] += g

    res = pl.pallas_call(
        body, name=name, grid=(t // tm,),
        in_specs=_row_specs(rows, tm) + [_full_spec(p.shape, 1) for p in params] + _row_specs(couts, tm)
        + _row_specs(adds, tm),
        out_specs=[pl.BlockSpec((tm, rows[k][2]), lambda i: (i, 0)) for k in want]
        + [_full_spec(p.shape, 1) for p in params],
        out_shape=[jax.ShapeDtypeStruct((t, rows[k][2]), F32) for k in want]
        + [jax.ShapeDtypeStruct(p.shape, F32) for p in params],
        compiler_params=_cparams(1),
    )(*[r[0] for r in rows], *params, *[c[0] for c in couts], *[a[0] for a in adds])
    return list(res[:len(want)]), list(res[len(want):])


def _seq_specs(seqs, s, cb, order):
    if order == "bj":
        return [pl.BlockSpec((s, cb), lambda b, j, c=c0 // cb: (b, c + j)) for (_, c0) in seqs]
    return [pl.BlockSpec((s, cb), lambda j, b, c=c0 // cb: (b, c + j)) for (_, c0) in seqs]


def _param_specs(params, order):
    if order == "bj":
        return [pl.BlockSpec(bs, lambda b, j, fn=fn: fn(j)) for (_, bs, fn) in params]
    return [pl.BlockSpec(bs, lambda j, b, fn=fn: fn(j)) for (_, bs, fn) in params]


def seqop_fwd(name, f, seqs, params, out_dtypes, *, nb, s, nblk, cb=LANES, n=256):
    n = min(n, s)
    n_s, n_p = len(seqs), len(params)

    def body(*refs):
        seq_refs, par_refs, out_refs = refs[:n_s], refs[n_s:n_s + n_p], refs[n_s + n_p:]

        def step(i, carry):
            t0 = pl.multiple_of(i * n, n)
            vals = f(t0, *[_load_ext(r, t0, n) for r in seq_refs], *[p[...] for p in par_refs])
            for o, v in zip(out_refs, vals):
                o[pl.ds(t0, n), :] = v.astype(o.dtype)
            return carry

        lax.fori_loop(0, s // n, step, 0)

    res = pl.pallas_call(
        body, name=name, grid=(nb, nblk),
        in_specs=_seq_specs(seqs, s, cb, "bj") + _param_specs(params, "bj"),
        out_specs=[pl.BlockSpec((s, cb), lambda b, j: (b, j)) for _ in out_dtypes],
        out_shape=[jax.ShapeDtypeStruct((nb * s, nblk * cb), dt) for dt in out_dtypes],
        compiler_params=_cparams(2),
    )(*[q[0] for q in seqs], *[p[0] for p in params])
    return list(res)


def seqop_bwd(name, f, seqs, params, cot_seqs, cot_fn, *, nb, s, nblk, cb=LANES, n=256):
    n = min(n, s)
    n_s, n_p, n_c = len(seqs), len(params), len(cot_seqs)
    nchunk = s // n

    def body(*refs):
        seq_refs, par_refs = refs[:n_s], refs[n_s:n_s + n_p]
        cot_refs = refs[n_s + n_p:n_s + n_p + n_c]
        dseq_refs = refs[n_s + n_p + n_c:n_s + n_p + n_c + n_s]
        dpar_refs = refs[n_s + n_p + n_c + n_s:]

        @pl.when(pl.program_id(1) == 0)
        def _():
            for o in dpar_refs:
                o[...] = jnp.zeros_like(o)

        def step(ii, halos):
            t0 = pl.multiple_of((nchunk - 1 - ii) * n, n)
            exts = [_load_ext(r, t0, n) for r in seq_refs]
            ps = [p[...] for p in par_refs]
            cots = cot_fn(t0, *[_load_ext(r, t0, n) for r in cot_refs])
            _, vjp = jax.vjp(lambda *args: f(t0, *args), *exts, *ps)
            gs = vjp(tuple(cots))
            tail = pl.multiple_of(t0 + n - SUBLANES, SUBLANES)
            new_halos = []
            for o, g, h in zip(dseq_refs, gs[:n_s], halos):
                o[pl.ds(t0, n), :] = g[SUBLANES:]
                o[pl.ds(tail, SUBLANES), :] += h
                new_halos.append(g[:SUBLANES])
            for o, g in zip(dpar_refs, gs[n_s:]):
                o[...] += g
            return tuple(new_halos)

        lax.fori_loop(0, nchunk, step, tuple(jnp.zeros((SUBLANES, cb), F32) for _ in range(n_s)))

    res = pl.pallas_call(
        body, name=name, grid=(nblk, nb),
        in_specs=_seq_specs(seqs, s, cb, "jb") + _param_specs(params, "jb") + _seq_specs(cot_seqs, s, cb, "jb"),
        out_specs=[pl.BlockSpec((s, cb), lambda j, b: (b, j)) for _ in seqs] + _param_specs(params, "jb"),
        out_shape=[jax.ShapeDtypeStruct((nb * s, nblk * cb), F32) for _ in seqs]
        + [jax.ShapeDtypeStruct(p[0].shape, F32) for p in params],
        compiler_params=_cparams(2),
    )(*[q[0] for q in seqs], *[p[0] for p in params], *[q[0] for q in cot_seqs])
    return list(res[:n_s]), list(res[n_s:])


def rscan(name, a, b, reverse, *, nb, s, cb=256):
    c = a.shape[1]
    cb = min(cb, c)

    def body(a_ref, b_ref, h_ref):
        def step(ii, car):
            i = (s // SUBLANES - 1 - ii) if reverse else ii
            base = pl.multiple_of(i * SUBLANES, SUBLANES)
            for r in (reversed(range(SUBLANES)) if reverse else range(SUBLANES)):
                row = pl.ds(base + r, 1)
                if reverse:
                    g = b_ref[row, :] + car
                    h_ref[row, :] = g
                    car = a_ref[row, :] * g
                else:
                    car = a_ref[row, :] * car + b_ref[row, :]
                    h_ref[row, :] = car
            return car

        lax.fori_loop(0, s // SUBLANES, step, jnp.zeros((1, cb), F32))

    spec = pl.BlockSpec((s, cb), lambda bb, j: (bb, j))
    return pl.pallas_call(
        body, name=name, grid=(nb, c // cb), in_specs=[spec, spec], out_specs=spec,
        out_shape=jax.ShapeDtypeStruct(a.shape, F32), compiler_params=_cparams(2),
    )(a, b)


def cscan(name, bu, ar, ai, reverse, *, nb, s, cb=256):
    p = ar.shape[1]
    cb = min(cb, p)
    nj = p // cb

    def body(br_ref, bi_ref, ar_ref, ai_ref, hr_ref, hi_ref):
        lr = ar_ref[...]
        li = -ai_ref[...] if reverse else ai_ref[...]

        def step(ii, car):
            hr, hi = car
            i = (s // SUBLANES - 1 - ii) if reverse else ii
            base = pl.multiple_of(i * SUBLANES, SUBLANES)
            for r in (reversed(range(SUBLANES)) if reverse else range(SUBLANES)):
                row = pl.ds(base + r, 1)
                nr = lr * hr - li * hi + br_ref[row, :]
                ni = lr * hi + li * hr + bi_ref[row, :]
                hr, hi = nr, ni
                hr_ref[row, :] = hr
                hi_ref[row, :] = hi
            return hr, hi

        z = jnp.zeros((1, cb), F32)
        lax.fori_loop(0, s // SUBLANES, step, (z, z))

    o_spec = pl.BlockSpec((s, cb), lambda bb, j: (bb, j))
    l_spec = pl.BlockSpec((1, cb), lambda bb, j: (0, j))
    res = pl.pallas_call(
        body, name=name, grid=(nb, nj),
        in_specs=[o_spec, pl.BlockSpec((s, cb), lambda bb, j: (bb, j + nj)), l_spec, l_spec],
        out_specs=[o_spec, o_spec],
        out_shape=[jax.ShapeDtypeStruct((nb * s, p), F32)] * 2, compiler_params=_cparams(2),
    )(bu, bu, ar, ai)
    return res[0], res[1]


def _norm_f(x, g):
    return (_rmsnorm(x, g),)


def _rglru_pre_f(t0, xext, w0, w1, w2, w3, cb, wr, wi, br, bi, lam):
    x0, x1, x2, x3 = taps(xext, 4)
    xc = w3 * x0 + w2 * x1 + w1 * x2 + w0 * x3 + cb
    r = _sigmoid(mm_nn(xc, wr[0]) + br)
    ig = _sigmoid(mm_nn(xc, wi[0]) + bi)
    log_a = -RG_C * r * _softplus(-lam)
    a = jnp.exp(log_a)
    first = (lax.broadcasted_iota(jnp.int32, a.shape, 0) + t0) == 0
    mult = jnp.where(first, 1.0, jnp.sqrt(1.0 - jnp.exp(2.0 * log_a)))
    return a, mult * ig * xc


def _ffn_act_f(t0, gext, vext, g0, g1, g2, v0, v1, v2, bg, bv):
    ga, gb, gc = taps(gext, 3)
    va, vb, vc = taps(vext, 3)
    ug = g2 * ga + g1 * gb + g0 * gc + bg
    uv = v2 * va + v1 * vb + v0 * vc + bv
    return (_gelu(ug) * uv,)


def _s5_post_f(ys, uc, d, gw, gb):
    yc = _gelu(ys + d * uc)
    return (yc * _sigmoid(mm_nn(yc, gw) + gb),)


def _merge_f(ha, ga, yb, yc, g0, g1, g2, wb0, wb1, wb2, bg0, bg1, bg2):
    ya = ha * _gelu(ga)
    out = _sigmoid(g0 + bg0) * mm_nn(ya, wb0)
    out = out + _sigmoid(g1 + bg1) * mm_nn(yb, wb1)
    out = out + _sigmoid(g2 + bg2) * mm_nn(yc, wb2)
    return (out,)


def _s5_param_f(lr, li, ldt, btr, bti):
    dt = jnp.exp(ldt)
    mag = jnp.exp(lr * dt)
    ar, ai = mag * jnp.cos(li * dt), mag * jnp.sin(li * dt)
    den = lr * lr + li * li
    fr = ((ar - 1.0) * lr + ai * li) / den
    fi = (ai * lr - (ar - 1.0) * li) / den
    return ar, ai, fr * btr - fi * bti, fr * bti + fi * btr


def _tri_inv(a):
    c = a.shape[0]
    ri = lax.broadcasted_iota(jnp.int32, (c, c), 0)
    ci = lax.broadcasted_iota(jnp.int32, (c, c), 1)
    eye = jnp.where(ri == ci, 1.0, 0.0).astype(F32)
    ad = jnp.where((ri >> 4) == (ci >> 4), a, 0.0)
    ao = a - ad
    a2 = mm_nn(ad, ad, True)
    a4 = mm_nn(a2, a2, True)
    a8 = mm_nn(a4, a4, True)
    td = mm_nn(mm_nn(eye - ad, eye + a2, True), mm_nn(eye + a4, eye + a8, True), True)
    nn_ = mm_nn(td, ao, True)
    n2 = mm_nn(nn_, nn_, True)
    return mm_nn(mm_nn(eye - nn_, eye + n2, True), td, True)


def _dn_chunk_f(state, qe, ke, ve, z, ba, q0, q1, q2, q3, k0, k1, k2, k3, v0, v1, v2, v3, alog, dtb, nw):
    c = DN_CHUNK
    dk = qe.shape[1]

    def conv_silu(xe, w0, w1, w2, w3):
        x0, x1, x2, x3 = taps(xe, 4)
        u = w3 * x0 + w2 * x1 + w1 * x2 + w0 * x3
        return u * _sigmoid(u)

    def l2n(x):
        return x * lax.rsqrt(jnp.sum(x * x, axis=-1, keepdims=True) + EPS)

    q = l2n(conv_silu(qe, q0, q1, q2, q3)) * (dk ** -0.5)
    k = l2n(conv_silu(ke, k0, k1, k2, k3))
    v = conv_silu(ve, v0, v1, v2, v3)
    beta = _sigmoid(_lane_pick(ba, 0))
    g = -jnp.exp(_lane_pick(alog, 0)) * _softplus(_lane_pick(ba, 1) + _lane_pick(dtb, 0))
    ri = lax.broadcasted_iota(jnp.int32, (c, c), 0)
    ci = lax.broadcasted_iota(jnp.int32, (c, c), 1)
    incl, strict = ri >= ci, ri > ci
    ltri = jnp.where(incl, 1.0, 0.0).astype(F32)
    gc_w = mm_nn(ltri, jnp.broadcast_to(g, (c, dk)), True)
    gc_c = mm_nn(ltri, jnp.broadcast_to(g, (c, c)), True)
    gc_r = mm_nt(jnp.full((c, dk), 1.0 / dk, F32), gc_w, True)
    gtot = jnp.sum(g, axis=0, keepdims=True)
    decay = jnp.exp(jnp.where(incl, gc_c - gc_r, -1e30))
    e_gc = jnp.exp(gc_w)
    kb = k * beta
    a_mat = jnp.where(strict, mm_nt(kb, k) * decay, 0.0)
    t_inv = _tri_inv(a_mat)
    u = mm_nn(t_inv, v * beta, True)
    w = mm_nn(t_inv, kb * e_gc, True)
    qk = jnp.where(incl, mm_nt(q, k) * decay, 0.0)
    v_new = u - mm_nn(w, state)
    o = mm_nn(q * e_gc, state) + mm_nn(qk, v_new)
    new_state = state * jnp.exp(gtot) + mm_tn(k * jnp.exp(gtot - gc_w), v_new)
    o = _rmsnorm(o, nw) * (z * _sigmoid(z))
    return o, new_state


def _attn_f(q, k, v):
    sc = mm_nt(q, k) * (q.shape[1] ** -0.5)
    sc = sc - jnp.max(sc, axis=-1, keepdims=True)
    e = jnp.exp(sc)
    p = e / jnp.sum(e, axis=-1, keepdims=True)
    return (mm_nn(p, v),)


def _dn_specs(proj_cols, s, nh, order):
    if order == "hb":
        return [pl.BlockSpec((s, LANES), lambda h, b, c=c0 // LANES: (b, c + h)) for c0 in proj_cols]
    raise ValueError(order)


def deltanet_fwd(proj, cols, taps12, alog, dtb, nw, *, nb, s, nh):
    nc = s // DN_CHUNK
    head_par = pl.BlockSpec((1, LANES), lambda h, b: (0, h))

    def body(*refs):
        q_ref, k_ref, v_ref, z_ref, ba_ref = refs[:5]
        par_refs = refs[5:5 + 15]
        o_ref, st_ref = refs[20], refs[21]

        def step(i, state):
            t0 = pl.multiple_of(i * DN_CHUNK, DN_CHUNK)
            st_ref[0, 0, i] = state
            rows = pl.ds(t0, DN_CHUNK)
            o, new_state = _dn_chunk_f(
                state, _load_ext(q_ref, t0, DN_CHUNK), _load_ext(k_ref, t0, DN_CHUNK), _load_ext(v_ref, t0, DN_CHUNK),
                z_ref[rows, :], ba_ref[rows, :], *[p[...] for p in par_refs])
            o_ref[rows, :] = o
            return new_state

        lax.fori_loop(0, nc, step, jnp.zeros((LANES, LANES), F32))

    res = pl.pallas_call(
        body, name="deltanet_fwd", grid=(nh, nb),
        in_specs=_dn_specs(cols, s, nh, "hb") + [head_par] * 14 + [pl.BlockSpec((1, LANES), lambda h, b: (0, 0))],
        out_specs=[pl.BlockSpec((s, LANES), lambda h, b: (b, h)),
                   pl.BlockSpec((1, 1, nc, LANES, LANES), lambda h, b: (b, h, 0, 0, 0))],
        out_shape=[jax.ShapeDtypeStruct((nb * s, nh * LANES), F32),
                   jax.ShapeDtypeStruct((nb, nh, nc, LANES, LANES), F32)],
        compiler_params=_cparams(2),
    )(proj, proj, proj, proj, proj, *taps12, alog, dtb, nw)
    return res[0], res[1]


def deltanet_bwd(proj, cols, taps12, alog, dtb, nw, states, dyb, *, nb, s, nh):
    nc = s // DN_CHUNK
    head_par = pl.BlockSpec((1, LANES), lambda h, b: (0, h))
    norm_par = pl.BlockSpec((1, LANES), lambda h, b: (0, 0))
    seq_out = pl.BlockSpec((s, LANES), lambda h, b: (b, h))

    def body(*refs):
        q_ref, k_ref, v_ref, z_ref, ba_ref = refs[:5]
        par_refs = refs[5:20]
        st_ref, do_ref = refs[20], refs[21]
        dq_ref, dk_ref, dv_ref, dz_ref, dba_ref = refs[22:27]
        dpar_refs = refs[27:42]
        h_id, b_id = pl.program_id(0), pl.program_id(1)

        @pl.when(b_id == 0)
        def _():
            for o in dpar_refs[:14]:
                o[...] = jnp.zeros_like(o)

        @pl.when((b_id == 0) & (h_id == 0))
        def _():
            dpar_refs[14][...] = jnp.zeros_like(dpar_refs[14])

        def step(ii, car):
            d_state, halos = car[0], car[1:]
            i = nc - 1 - ii
            t0 = pl.multiple_of(i * DN_CHUNK, DN_CHUNK)
            rows = pl.ds(t0, DN_CHUNK)
            args = [st_ref[0, 0, i], _load_ext(q_ref, t0, DN_CHUNK), _load_ext(k_ref, t0, DN_CHUNK),
                    _load_ext(v_ref, t0, DN_CHUNK), z_ref[rows, :], ba_ref[rows, :]] + [p[...] for p in par_refs]
            _, vjp = jax.vjp(_dn_chunk_f, *args)
            gs = vjp((do_ref[rows, :], d_state))
            tail = pl.multiple_of(t0 + DN_CHUNK - SUBLANES, SUBLANES)
            new_halos = []
            for o, g, hl in zip((dq_ref, dk_ref, dv_ref), gs[1:4], halos):
                o[rows, :] = g[SUBLANES:]
                o[pl.ds(tail, SUBLANES), :] += hl
                new_halos.append(g[:SUBLANES])
            dz_ref[rows, :] = gs[4]
            dba_ref[rows, :] = gs[5]
            for o, g in zip(dpar_refs, gs[6:]):
                o[...] += g
            return (gs[0], *new_halos)

        zero_halo = jnp.zeros((SUBLANES, LANES), F32)
        lax.fori_loop(0, nc, step, (jnp.zeros((LANES, LANES), F32), zero_halo, zero_halo, zero_halo))

    res = pl.pallas_call(
        body, name="deltanet_bwd", grid=(nh, nb),
        in_specs=_dn_specs(cols, s, nh, "hb") + [head_par] * 14 + [norm_par]
        + [pl.BlockSpec((1, 1, nc, LANES, LANES), lambda h, b: (b, h, 0, 0, 0)), seq_out],
        out_specs=[seq_out] * 5 + [head_par] * 14 + [norm_par],
        out_shape=[jax.ShapeDtypeStruct((nb * s, nh * LANES), F32)] * 5
        + [jax.ShapeDtypeStruct((1, nh * LANES), F32)] * 14 + [jax.ShapeDtypeStruct((1, LANES), F32)],
        compiler_params=_cparams(2),
    )(proj, proj, proj, proj, proj, *taps12, alog, dtb, nw, states, dyb)
    return list(res[:5]), list(res[5:])


def attn_fwd(q, kv, *, nb, s, m, nh, tq=512):
    hd = q.shape[1] // nh
    tq = min(tq, s)
    nq = s // tq

    def body(q_ref, k_ref, v_ref, o_ref):
        o_ref[...] = _attn_f(q_ref[...].astype(F32), k_ref[...], v_ref[...])[0].astype(o_ref.dtype)

    return pl.pallas_call(
        body, name="attn_fwd", grid=(nb, nh, nq),
        in_specs=[pl.BlockSpec((tq, hd), lambda b, h, i: (b * nq + i, h)),
                  pl.BlockSpec((m, hd), lambda b, h, i: (b, h)),
                  pl.BlockSpec((m, hd), lambda b, h, i: (b, nh + h))],
        out_specs=pl.BlockSpec((tq, hd), lambda b, h, i: (b * nq + i, h)),
        out_shape=jax.ShapeDtypeStruct(q.shape, BF16), compiler_params=_cparams(3),
    )(q, kv, kv)


def attn_bwd(q, kv, do, *, nb, s, m, nh, tq=512):
    hd = q.shape[1] // nh
    tq = min(tq, s)
    nq = s // tq

    def body(q_ref, k_ref, v_ref, do_ref, dq_ref, dk_ref, dv_ref):
        _, vjp = jax.vjp(_attn_f, q_ref[...].astype(F32), k_ref[...], v_ref[...])
        dq, dk, dv = vjp((do_ref[...].astype(F32),))
        dq_ref[...] = dq

        @pl.when(pl.program_id(2) == 0)
        def _():
            dk_ref[...] = jnp.zeros_like(dk_ref)
            dv_ref[...] = jnp.zeros_like(dv_ref)

        dk_ref[...] += dk
        dv_ref[...] += dv

    q_spec = pl.BlockSpec((tq, hd), lambda b, h, i: (b * nq + i, h))
    m_spec = pl.BlockSpec((m, hd), lambda b, h, i: (b, h))
    res = pl.pallas_call(
        body, name="attn_bwd", grid=(nb, nh, nq),
        in_specs=[q_spec, m_spec, pl.BlockSpec((m, hd), lambda b, h, i: (b, nh + h)), q_spec],
        out_specs=[q_spec, m_spec, m_spec],
        out_shape=[jax.ShapeDtypeStruct(q.shape, F32), jax.ShapeDtypeStruct((nb * m, nh * hd), F32),
                   jax.ShapeDtypeStruct((nb * m, nh * hd), F32)],
        compiler_params=_cparams(3),
    )(q, kv, kv, do)
    return res[0], res[1], res[2]


def loss_head(x, target, gain, tm=256):
    t, d = x.shape
    tm = min(tm, t)

    def body(x_ref, t_ref, g_ref, l_ref, dx_ref, dg_ref):
        tgt = t_ref[...]

        def lf(xv, gv):
            e = _rmsnorm(xv, gv) - tgt
            return 0.5 * jnp.sum(jnp.mean(e * e, axis=-1, keepdims=True), axis=0, keepdims=True)

        lv, vjp = jax.vjp(lf, x_ref[...], g_ref[...])
        dx, dg = vjp(jnp.ones((1, 1), F32))
        dx_ref[...] = dx

        @pl.when(pl.program_id(0) == 0)
        def _():
            l_ref[...] = jnp.zeros_like(l_ref)
            dg_ref[...] = jnp.zeros_like(dg_ref)

        l_ref[...] += jnp.broadcast_to(lv, l_ref.shape)
        dg_ref[...] += dg

    row = pl.BlockSpec((tm, d), lambda i: (i, 0))
    res = pl.pallas_call(
        body, name="loss_head", grid=(t // tm,),
        in_specs=[row, row, _full_spec((1, d), 1)],
        out_specs=[_full_spec((1, LANES), 1), row, _full_spec((1, d), 1)],
        out_shape=[jax.ShapeDtypeStruct((1, LANES), F32), jax.ShapeDtypeStruct((t, d), F32),
                   jax.ShapeDtypeStruct((1, d), F32)],
        compiler_params=_cparams(1),
    )(x, target, gain)
    return res[0], res[1], res[2]


def s5_dlam(gr, gi, hr, hi, *, nb, s, cb=256, n=256):
    p = gr.shape[1]
    cb, n = min(cb, p), min(n, s)

    def body(gr_ref, gi_ref, hr_ref, hi_ref, dar_ref, dai_ref):
        @pl.when(pl.program_id(1) == 0)
        def _():
            dar_ref[...] = jnp.zeros_like(dar_ref)
            dai_ref[...] = jnp.zeros_like(dai_ref)

        def step(i, carry):
            t0 = pl.multiple_of(i * n, n)
            g_r, g_i = gr_ref[pl.ds(t0, n), :], gi_ref[pl.ds(t0, n), :]
            p_r = taps(_load_ext(hr_ref, t0, n), 2)[1]
            p_i = taps(_load_ext(hi_ref, t0, n), 2)[1]
            dar_ref[...] += jnp.sum(g_r * p_r + g_i * p_i, axis=0, keepdims=True)
            dai_ref[...] += jnp.sum(g_i * p_r - g_r * p_i, axis=0, keepdims=True)
            return carry

        lax.fori_loop(0, s // n, step, 0)

    blk = pl.BlockSpec((s, cb), lambda j, b: (b, j))
    acc = pl.BlockSpec((1, cb), lambda j, b: (0, j))
    res = pl.pallas_call(
        body, name="s5_dlam", grid=(p // cb, nb), in_specs=[blk] * 4, out_specs=[acc, acc],
        out_shape=[jax.ShapeDtypeStruct((1, p), F32)] * 2, compiler_params=_cparams(2),
    )(gr, gi, hr, hi)
    return res[0], res[1]


def s5_param_fwd(lr, li, ldt, btr, bti):
    def body(*refs):
        vals = _s5_param_f(*[r[...] for r in refs[:5]])
        for o, v in zip(refs[5:], vals):
            o[...] = v

    return pl.pallas_call(
        body, name="s5_param_fwd",
        out_shape=[jax.ShapeDtypeStruct(lr.shape, F32)] * 2 + [jax.ShapeDtypeStruct(btr.shape, F32)] * 2,
    )(lr, li, ldt, btr, bti)


def s5_param_bwd(lr, li, ldt, btr, bti, cots):
    def body(*refs):
        _, vjp = jax.vjp(_s5_param_f, *[r[...] for r in refs[:5]])
        gs = vjp(tuple(r[...] for r in refs[5:9]))
        for o, g in zip(refs[9:], gs):
            o[...] = g

    return pl.pallas_call(
        body, name="s5_param_bwd",
        out_shape=[jax.ShapeDtypeStruct(a.shape, F32) for a in (lr, li, ldt, btr, bti)],
    )(lr, li, ldt, btr, bti, *cots)


def _row_block(r, c, copies):
    lanes = -(-c // LANES) * LANES
    rb = 2048
    while rb > SUBLANES and (r % rb or copies * rb * lanes * 4 > 12 * 1024 * 1024):
        rb //= 2
    return rb if r % rb == 0 else r


def _as2d(a, lead=0):
    return a.reshape(a.shape[:lead] + (-1, a.shape[-1]))


def sumk(name, buf):
    b3 = _as2d(buf, 1)
    k, r, c = b3.shape
    rb = _row_block(r, c, 2 * (k + 1))

    def body(b_ref, o_ref):
        acc = b_ref[0]
        for i in range(1, k):
            acc = acc + b_ref[i]
        o_ref[...] = acc

    out = pl.pallas_call(
        body, name=name, grid=(r // rb,),
        in_specs=[pl.BlockSpec((k, rb, c), lambda i: (0, i, 0))],
        out_specs=pl.BlockSpec((rb, c), lambda i: (i, 0)),
        out_shape=jax.ShapeDtypeStruct((r, c), F32), compiler_params=_cparams(1),
    )(b3)
    return out.reshape(buf.shape[1:])


def adamw(name, w, g, m, v):
    shape = w.shape
    w, g, m, v = (_as2d(a) for a in (w, g, m, v))
    r, c = w.shape
    rb = _row_block(r, c, 14)

    def body(w_ref, g_ref, m_ref, v_ref, d_ref, nm_ref, nv_ref):
        gv = g_ref[...]
        nm = ADAM_B1 * m_ref[...] + (1.0 - ADAM_B1) * gv
        nv = ADAM_B2 * v_ref[...] + (1.0 - ADAM_B2) * (gv * gv)
        m_hat = nm / (1.0 - ADAM_B1 ** ADAM_STEP)
        v_hat = nv / (1.0 - ADAM_B2 ** ADAM_STEP)
        d_ref[...] = -ADAM_LR * (m_hat / (jnp.sqrt(v_hat) + ADAM_EPS) + ADAM_WD * w_ref[...])
        nm_ref[...] = nm
        nv_ref[...] = nv

    spec = pl.BlockSpec((rb, c), lambda i: (i, 0))
    res = pl.pallas_call(
        body, name=name, grid=(r // rb,), in_specs=[spec] * 4, out_specs=[spec] * 3,
        out_shape=[jax.ShapeDtypeStruct(w.shape, F32)] * 3, compiler_params=_cparams(1),
    )(w, g, m, v)
    return tuple(a.reshape(shape) for a in res)


def _place():
    x, y, c = lax.axis_index("x"), lax.axis_index("y"), lax.axis_index("c")
    chips = [(1 - x, y), (x, 1 - y), (1 - x, 1 - y)]
    return x, y, c, chips


def _chip_id(px, py):
    return 2 * px + py


def _rcopy(src, dst, send_sems, recv_sems, k, to):
    return pltpu.make_async_remote_copy(src_ref=src, dst_ref=dst, send_sem=send_sems.at[k], recv_sem=recv_sems.at[k],
                                        device_id=to, device_id_type=MESH)


def _comm_call(name, body, out_shapes, n_sems, args):
    return pl.pallas_call(
        body, name=name, out_shape=out_shapes, in_specs=[ANY] * len(args), out_specs=[ANY] * len(out_shapes),
        scratch_shapes=[pltpu.SemaphoreType.DMA((n_sems,)), pltpu.SemaphoreType.DMA((n_sems,))],
    )(*args)


def _finish(remote):
    for cp in remote:
        cp.wait_send()


def _set_slot(buf, slot, val):
    if val.ndim < buf.ndim:
        val = val[None]
    return lax.dynamic_update_slice(buf, val.astype(buf.dtype), (slot,) + (0,) * (buf.ndim - 1))


def gather_chips(name, arrays):
    n = len(arrays)
    halves = [a.shape[0] // 2 for a in arrays]

    def body(*refs):
        ins, outs = refs[:n], refs[n:2 * n]
        send_sems, recv_sems = refs[2 * n:]
        x, y, c, chips = _place()
        me, sib = _chip_id(x, y), (x, y, 1 - c)
        half = [pl.ds(c * h, h) for h in halves]
        other = [pl.ds((1 - c) * h, h) for h in halves]
        remote = []
        for a in range(n):
            for j, chip in enumerate(chips):
                remote.append(_rcopy(ins[a].at[half[a]], outs[a].at[me, half[a]], send_sems, recv_sems, 6 * a + j,
                                     (*chip, c)))
                remote[-1].start()
        for a in range(n):
            for j, chip in enumerate(chips):
                landed = outs[a].at[_chip_id(*chip), half[a]]
                _rcopy(ins[a].at[half[a]], landed, send_sems, recv_sems, 6 * a + j, sib).wait_recv()
                remote.append(_rcopy(landed, landed, send_sems, recv_sems, 6 * a + 3 + j, sib))
                remote[-1].start()
        for a in range(n):
            for j, chip in enumerate(chips):
                _rcopy(ins[a].at[half[a]], outs[a].at[_chip_id(*chip), other[a]], send_sems, recv_sems, 6 * a + 3 + j,
                       sib).wait_recv()
        _finish(remote)

    shapes = [jax.ShapeDtypeStruct((N_CHIPS,) + a.shape, a.dtype) for a in arrays]
    return _comm_call(name, body, shapes, 6 * n, arrays)


def swap_halves(name, arrays):
    n = len(arrays)

    def body(*refs):
        ins, outs = refs[:n], refs[n:2 * n]
        send_sems, recv_sems = refs[2 * n:]
        x, y, c, _ = _place()
        sib = (x, y, 1 - c)
        remote = []
        for a in range(n):
            for k in range(N_CHIPS):
                remote.append(_rcopy(ins[a].at[1 - c, k], outs[a].at[k], send_sems, recv_sems, N_CHIPS * a + k, sib))
                remote[-1].start()
        for a in range(n):
            for k in range(N_CHIPS):
                _rcopy(ins[a].at[1 - c, k], outs[a].at[k], send_sems, recv_sems, N_CHIPS * a + k, sib).wait_recv()
        _finish(remote)

    shapes = [jax.ShapeDtypeStruct(a.shape[1:], a.dtype) for a in arrays]
    return _comm_call(name, body, shapes, N_CHIPS * n, arrays)


def scatter_chips(name, arrays):
    n = len(arrays)

    def body(*refs):
        ins, outs = refs[:n], refs[n:2 * n]
        send_sems, recv_sems = refs[2 * n:]
        x, y, c, chips = _place()
        remote = []
        for a in range(n):
            for j, chip in enumerate(chips):
                remote.append(_rcopy(ins[a].at[_chip_id(*chip)], outs[a].at[j], send_sems, recv_sems, 3 * a + j,
                                     (*chip, c)))
                remote[-1].start()
        for a in range(n):
            for j, chip in enumerate(chips):
                _rcopy(ins[a].at[0], outs[a].at[j], send_sems, recv_sems, 3 * a + j, (*chip, c)).wait_recv()
        _finish(remote)

    shapes = [jax.ShapeDtypeStruct((3,) + a.shape[1:], a.dtype) for a in arrays]
    return _comm_call(name, body, shapes, 3 * n, arrays)


def join_halves(name, arrays):
    n = len(arrays)
    offs = [sum(a.shape[0] for a in arrays[:i]) for i in range(n)]

    def body(*refs):
        ins, outs = refs[:n], refs[n:2 * n]
        send_sems, recv_sems = refs[2 * n:]
        x, y, c, _ = _place()
        sib = (x, y, 1 - c)
        remote = []
        for a in range(n):
            h = arrays[a].shape[0]
            for k in range(h):
                remote.append(_rcopy(ins[a].at[k], outs[a].at[c * h + k], send_sems, recv_sems, offs[a] + k, sib))
                remote[-1].start()
        for a in range(n):
            h = arrays[a].shape[0]
            for k in range(h):
                _rcopy(ins[a].at[k], outs[a].at[(1 - c) * h + k], send_sems, recv_sems, offs[a] + k, sib).wait_recv()
        _finish(remote)

    shapes = [jax.ShapeDtypeStruct((2 * a.shape[0],) + a.shape[1:], a.dtype) for a in arrays]
    return _comm_call(name, body, shapes, offs[-1] + arrays[-1].shape[0], arrays)


def add_pair(name, mine2, other, c):
    a3, o2 = _as2d(mine2, 1), _as2d(other)
    r, cols = o2.shape
    rb = _row_block(r, cols, 6)

    def body(c_ref, a_ref, o_ref, out_ref):
        out_ref[...] = a_ref[0] + o_ref[...]

    out = pl.pallas_call(
        body, name=name, out_shape=jax.ShapeDtypeStruct((r, cols), F32),
        grid_spec=pltpu.PrefetchScalarGridSpec(
            num_scalar_prefetch=1, grid=(r // rb,),
            in_specs=[pl.BlockSpec((1, rb, cols), lambda i, s: (s[0], i, 0)),
                      pl.BlockSpec((rb, cols), lambda i, s: (i, 0))],
            out_specs=pl.BlockSpec((rb, cols), lambda i, s: (i, 0))),
        compiler_params=_cparams(1),
    )(jnp.reshape(c, (1,)).astype(jnp.int32), a3, o2)
    return out.reshape(other.shape)


def add_chips(name, part, recv, me):
    p3, r3 = _as2d(part, 1), _as2d(recv, 1)
    _, r, cols = p3.shape
    rb = _row_block(r, cols, 10)

    def body(me_ref, p_ref, r_ref, out_ref):
        own, acc = p_ref[0], None
        for k in range(N_CHIPS):
            rel = jnp.full(own.shape, me_ref[0] ^ k, jnp.int32)
            term = jnp.where(rel == 0, own, jnp.where(rel == 2, r_ref[0], jnp.where(rel == 1, r_ref[1], r_ref[2])))
            acc = term if acc is None else acc + term
        out_ref[...] = acc

    out = pl.pallas_call(
        body, name=name, out_shape=jax.ShapeDtypeStruct((r, cols), F32),
        grid_spec=pltpu.PrefetchScalarGridSpec(
            num_scalar_prefetch=1, grid=(r // rb,),
            in_specs=[pl.BlockSpec((1, rb, cols), lambda i, s: (s[0], i, 0)),
                      pl.BlockSpec((3, rb, cols), lambda i, s: (0, i, 0))],
            out_specs=pl.BlockSpec((rb, cols), lambda i, s: (i, 0))),
        compiler_params=_cparams(1),
    )(jnp.reshape(me, (1,)).astype(jnp.int32), p3, r3)
    return out.reshape(part.shape[1:])


def gather_all(name, slab):
    def body(x_ref, out_ref, send_sems, recv_sems):
        x, y, c, chips = _place()
        me, sib = (x, y, c), (x, y, 1 - c)

        def rows(px, py, pc):
            return out_ref.at[4 * px + 2 * py + pc]

        first = [_rcopy(x_ref, rows(*me), send_sems, recv_sems, 0, sib)]
        first += [_rcopy(x_ref, rows(*me), send_sems, recv_sems, 1 + j, (*chip, c)) for j, chip in enumerate(chips)]
        for cp in first:
            cp.start()
        passed = []
        for j, chip in enumerate(chips):
            landed = rows(*chip, c)
            _rcopy(x_ref, landed, send_sems, recv_sems, 1 + j, sib).wait_recv()
            passed.append(_rcopy(landed, landed, send_sems, recv_sems, 4 + j, sib))
            passed[-1].start()
        _rcopy(x_ref, rows(*sib), send_sems, recv_sems, 0, sib).wait_recv()
        for j, chip in enumerate(chips):
            _rcopy(x_ref, rows(*chip, 1 - c), send_sems, recv_sems, 4 + j, sib).wait_recv()
        _finish(first + passed)

    return _comm_call(name, body, [jax.ShapeDtypeStruct((8,) + slab.shape, slab.dtype)], 7, [slab])[0]


D, BW, NH_B, NG, NP_S5, CG = 1024, 512, 4, 32, 64, 16
P_S5 = NG * NP_S5
C_GATES, C_XA, C_GA, C_Q, C_K, C_V, C_Z, C_UC, C_BA, N_PROJ = 0, 3072, 3584, 4096, 4608, 5120, 5632, 6144, 6656, 7168
_IN_OFFS = (0, 512, 1024, 1536, 2048, 2560, 3072, 3076, 3080, 3592, 6664)


def _blockdiag2(w):
    z = jnp.zeros((w.shape[0] // 2, 2 * w.shape[1], 2 * w.shape[2]), w.dtype)
    return z.at[:, :64, :64].set(w[0::2]).at[:, 64:, 64:].set(w[1::2])


def _unblockdiag2(d):
    return jnp.stack([d[:, :64, :64], d[:, 64:, 64:]], axis=1).reshape(-1, 64, 64)


def _layer_params(w, l):
    p = {}
    wi = w["w_in"][l]
    xa, ga, q, k, v, z, beta, alpha, uc, gates = [wi[:, a:b] for a, b in zip(_IN_OFFS[:-1], _IN_OFFS[1:])]
    ba = jnp.zeros((D, NH_B, LANES), wi.dtype).at[:, :, 0].set(beta).at[:, :, 1].set(alpha).reshape(D, NH_B * LANES)
    p["w_in_pad"] = jnp.concatenate([gates, xa, ga, q, k, v, z, uc, ba], axis=1)
    row = lambda a: a.reshape(1, -1).astype(F32)
    acw, bcw, fcw = w["a_conv_w"][l], w["b_conv_w"][l], w["ffn_conv_w"][l]
    p["a_taps"] = [row(acw[i]) for i in range(4)]
    p["a_cb"], p["a_br"], p["a_bi"], p["a_lam"] = (row(w[n][l]) for n in ("a_conv_b", "a_b_r", "a_b_i", "a_lam"))
    p["a_wr"], p["a_wi"] = _blockdiag2(w["a_w_r"][l]), _blockdiag2(w["a_w_i"][l])
    p["b_taps"] = [row(bcw[i, j * BW:(j + 1) * BW]) for j in range(3) for i in range(4)]
    p["b_alog"] = row(jnp.repeat(w["b_a_log"][l], LANES))
    p["b_dtb"] = row(jnp.repeat(w["b_dt_bias"][l], LANES))
    p["b_nw"] = row(w["b_norm"][l])
    p["c_lr"], p["c_li"] = w["c_lam_re"][l][:, None, :], w["c_lam_im"][l][:, None, :]
    p["c_ldt"] = w["c_log_dt"][l][:, None, None]
    p["c_btr"], p["c_bti"] = w["c_b_re"][l].transpose(0, 2, 1), w["c_b_im"][l].transpose(0, 2, 1)
    eye = jnp.eye(NG, dtype=F32)
    p["eye"] = eye
    expand = lambda t: (t[:, :, None, :] * eye[:, None, :, None]).reshape(t.shape[0] * t.shape[1], -1)
    p["expand"] = expand
    p["c_cre"] = expand(w["c_c_re"][l].transpose(0, 2, 1))
    p["c_cimn"] = expand(-w["c_c_im"][l].transpose(0, 2, 1))
    p["c_d"], p["c_glu_b"] = row(w["c_d"][l]), row(w["c_glu_b"][l])
    p["c_glu_w"] = w["c_glu_w"][l].astype(F32)
    p["wb"] = [w["w_branch"][l, k].astype(F32) for k in range(3)]
    p["bg"] = [row(w["b_gate"][l, k * D:(k + 1) * D]) for k in range(3)]
    for n in ("w_out", "xa_w_q", "xa_w_kv", "xa_w_o", "ffn_w_up", "ffn_w_down"):
        p[n] = w[n][l]
    for n in ("mix_norm", "xa_norm", "mem_norm", "ffn_norm"):
        p[n] = row(w[n][l])
    dff = fcw.shape[1] // 2
    p["f_taps"] = [row(fcw[i, :dff]) for i in range(3)] + [row(fcw[i, dff:]) for i in range(3)]
    p["f_b"] = [row(w["ffn_conv_b"][l, :dff]), row(w["ffn_conv_b"][l, dff:])]
    return p


def _chan(arr, cb=LANES):
    return (arr, (1, cb), lambda j: (0, j))


def _a_params(p):
    blk = lambda a: (a, (1, LANES, LANES), lambda j: (j, 0, 0))
    return [_chan(t) for t in p["a_taps"]] + [_chan(p["a_cb"]), blk(p["a_wr"]), blk(p["a_wi"]),
                                              _chan(p["a_br"]), _chan(p["a_bi"]), _chan(p["a_lam"])]


def _f_params(p):
    return [_chan(t) for t in p["f_taps"]] + [_chan(b) for b in p["f_b"]]


def _layer_fwd(x, mem, p, nb, s, m):
    sv = {"x": x}
    h = rowop_fwd("norm_mix", _norm_f, [(x, 0, D)], [p["mix_norm"]], [(D, BF16)])[0]
    proj = mm("in_proj", h, p["w_in_pad"], "nn")
    a, b = seqop_fwd("rglru_pre", _rglru_pre_f, [(proj, C_XA)], _a_params(p), [F32, F32], nb=nb, s=s, nblk=BW // LANES)
    ha = rscan("rglru_scan", a, b, False, nb=nb, s=s)
    yb, states = deltanet_fwd(proj, (C_Q, C_K, C_V, C_Z, C_BA), p["b_taps"], p["b_alog"], p["b_dtb"], p["b_nw"],
                              nb=nb, s=s, nh=NH_B)
    ar, ai, bbr, bbi = s5_param_fwd(p["c_lr"], p["c_li"], p["c_ldt"], p["c_btr"], p["c_bti"])
    bd = jnp.concatenate([p["expand"](bbr), p["expand"](bbi)], axis=1)
    ar2, ai2 = ar.reshape(1, P_S5), ai.reshape(1, P_S5)
    bu = mm("s5_bu", proj, bd, "nn", a_cols=(C_UC, BW))
    hr, hi = cscan("s5_scan", bu, ar2, ai2, False, nb=nb, s=s)
    ys = mm("s5_y_im", hi, p["c_cimn"], "nn", res=mm("s5_y_re", hr, p["c_cre"], "nn"))
    yc = rowop_fwd("s5_post", _s5_post_f, [(ys, 0, BW), (proj, C_UC, BW)], [p["c_d"], p["c_glu_w"], p["c_glu_b"]],
                   [(BW, F32)])[0]
    merge_rows = [(ha, 0, BW), (proj, C_GA, BW), (yb, 0, BW), (yc, 0, BW),
                  (proj, 0, D), (proj, D, D), (proj, 2 * D, D)]
    merged = rowop_fwd("merge", _merge_f, merge_rows, p["wb"] + p["bg"], [(D, BF16)])[0]
    x1 = mm("out_proj", merged, p["w_out"], "nn", res=x)
    hq = rowop_fwd("norm_xa", _norm_f, [(x1, 0, D)], [p["xa_norm"]], [(D, BF16)])[0]
    q = mm("xa_q", hq, p["xa_w_q"], "nn", out_dtype=BF16)
    memn = rowop_fwd("norm_mem", _norm_f, [(mem, 0, D)], [p["mem_norm"]], [(D, BF16)])[0]
    kv = mm("xa_kv", memn, p["xa_w_kv"], "nn")
    o = attn_fwd(q, kv, nb=nb, s=s, m=m, nh=4)
    x2 = mm("xa_o", o, p["xa_w_o"], "nn", res=x1)
    hf = rowop_fwd("norm_ffn", _norm_f, [(x2, 0, D)], [p["ffn_norm"]], [(D, BF16)])[0]
    up = mm("ffn_up", hf, p["ffn_w_up"], "nn")
    dff = up.shape[1] // 2
    act = seqop_fwd("ffn_act", _ffn_act_f, [(up, 0), (up, dff)], _f_params(p), [BF16], nb=nb, s=s,
                    nblk=dff // LANES)[0]
    x3 = mm("ffn_down", act, p["ffn_w_down"], "nn", res=x2)
    sv.update(h=h, proj=proj, a=a, ha=ha, yb=yb, states=states, ar=ar2, ai=ai2, bd=bd, hr=hr, hi=hi, ys=ys, yc=yc,
              merged=merged, x1=x1, hq=hq, q=q, memn=memn, kv=kv, o=o, x2=x2, hf=hf, up=up, act=act)
    return x3, sv


def _layer_bwd(dx3, mem, sv, p, nb, s, m):
    g = {}
    up, dff = sv["up"], sv["up"].shape[1] // 2
    dact = mm("d_ffn_down_x", dx3, p["ffn_w_down"], "nt")
    g["ffn_w_down"] = mm("d_ffn_down_w", sv["act"], dx3, "tn")
    (dug, duv), dfp = seqop_bwd("ffn_act_bwd", _ffn_act_f, [(up, 0), (up, dff)], _f_params(p), [(dact, 0)],
                                lambda t0, e: (e[SUBLANES:],), nb=nb, s=s, nblk=dff // LANES)
    dhf = mm("d_ffn_up_x1", duv, p["ffn_w_up"], "nt", b_cols=(dff, dff),
             res=mm("d_ffn_up_x0", dug, p["ffn_w_up"], "nt", b_cols=(0, dff)))
    g["ffn_w_up"] = jnp.concatenate([mm("d_ffn_up_w0", sv["hf"], dug, "tn"), mm("d_ffn_up_w1", sv["hf"], duv, "tn")],
                                    axis=1)
    g["ffn_conv_w"] = jnp.concatenate([jnp.concatenate(dfp[0:3], axis=0), jnp.concatenate(dfp[3:6], axis=0)], axis=1)
    g["ffn_conv_b"] = jnp.concatenate([dfp[6], dfp[7]], axis=1)[0]
    (dx2,), (gn,) = rowop_bwd("norm_ffn_bwd", _norm_f, [(sv["x2"], 0, D)], [p["ffn_norm"]], [(dhf, 0, D)], [True],
                              add=(dx3, 0, D))
    g["ffn_norm"] = gn[0]
    do = mm("d_xa_o_x", dx2, p["xa_w_o"], "nt")
    g["xa_w_o"] = mm("d_xa_o_w", sv["o"], dx2, "tn")
    dq, dk, dv = attn_bwd(sv["q"], sv["kv"], do, nb=nb, s=s, m=m, nh=4)
    dkv = jnp.concatenate([dk, dv], axis=1)
    dhq = mm("d_xa_q_x", dq, p["xa_w_q"], "nt")
    g["xa_w_q"] = mm("d_xa_q_w", sv["hq"], dq, "tn")
    dmemn = mm("d_xa_kv_x", dkv, p["xa_w_kv"], "nt")
    g["xa_w_kv"] = mm("d_xa_kv_w", sv["memn"], dkv, "tn")
    (dx1,), (gn,) = rowop_bwd("norm_xa_bwd", _norm_f, [(sv["x1"], 0, D)], [p["xa_norm"]], [(dhq, 0, D)], [True],
                              add=(dx2, 0, D))
    g["xa_norm"] = gn[0]
    _, (gn,) = rowop_bwd("norm_mem_bwd", _norm_f, [(mem, 0, D)], [p["mem_norm"]], [(dmemn, 0, D)], [False])
    g["mem_norm"] = gn[0]
    proj = sv["proj"]
    dmerged = mm("d_out_proj_x", dx1, p["w_out"], "nt")
    g["w_out"] = mm("d_out_proj_w", sv["merged"], dx1, "tn")
    merge_rows = [(sv["ha"], 0, BW), (proj, C_GA, BW), (sv["yb"], 0, BW), (sv["yc"], 0, BW),
                  (proj, 0, D), (proj, D, D), (proj, 2 * D, D)]
    (dha, dga, dyb, dyc, dg0, dg1, dg2), dmp = rowop_bwd("merge_bwd", _merge_f, merge_rows, p["wb"] + p["bg"],
                                                          [(dmerged, 0, D)], [True] * 7)
    g["w_branch"] = jnp.stack(dmp[0:3])
    g["b_gate"] = jnp.concatenate(dmp[3:6], axis=1)[0]
    (dys, duc), (gd, ggw, ggb) = rowop_bwd("s5_post_bwd", _s5_post_f, [(sv["ys"], 0, BW), (proj, C_UC, BW)],
                                           [p["c_d"], p["c_glu_w"], p["c_glu_b"]], [(dyc, 0, BW)], [True, True])
    g["c_d"], g["c_glu_w"], g["c_glu_b"] = gd[0], ggw, ggb[0]
    cd = jnp.concatenate([p["c_cre"], p["c_cimn"]], axis=0)
    dhs = mm("d_s5_y_x", dys, cd, "nt")
    dcre = mm("d_s5_y_wre", sv["hr"], dys, "tn")
    dcimn = mm("d_s5_y_wim", sv["hi"], dys, "tn")
    gr, gi = cscan("s5_scan_bwd", dhs, sv["ar"], sv["ai"], True, nb=nb, s=s)
    dar, dai = s5_dlam(gr, gi, sv["hr"], sv["hi"], nb=nb, s=s)
    duc = mm("d_s5_bu_x1", gi, sv["bd"], "nt", b_cols=(P_S5, P_S5),
             res=mm("d_s5_bu_x0", gr, sv["bd"], "nt", b_cols=(0, P_S5), res=duc))
    dbd_re = mm("d_s5_bu_wre", proj, gr, "tn", a_cols=(C_UC, BW))
    dbd_im = mm("d_s5_bu_wim", proj, gi, "tn", a_cols=(C_UC, BW))
    eye = p["eye"]
    pick = lambda dmat, r, c: jnp.einsum("grhc,gh->grc", dmat.reshape(NG, r, NG, c), eye)
    glr, gli, gldt, gbtr, gbti = s5_param_bwd(
        p["c_lr"], p["c_li"], p["c_ldt"], p["c_btr"], p["c_bti"],
        (dar.reshape(NG, 1, NP_S5), dai.reshape(NG, 1, NP_S5), pick(dbd_re, CG, NP_S5), pick(dbd_im, CG, NP_S5)))
    g["c_lam_re"], g["c_lam_im"], g["c_log_dt"] = glr[:, 0, :], gli[:, 0, :], gldt[:, 0, 0]
    g["c_b_re"], g["c_b_im"] = gbtr.transpose(0, 2, 1), gbti.transpose(0, 2, 1)
    g["c_c_re"] = pick(dcre, NP_S5, CG).transpose(0, 2, 1)
    g["c_c_im"] = -pick(dcimn, NP_S5, CG).transpose(0, 2, 1)
    (dq_b, dk_b, dv_b, dz_b, dba), dbp = deltanet_bwd(proj, (C_Q, C_K, C_V, C_Z, C_BA), p["b_taps"], p["b_alog"],
                                                      p["b_dtb"], p["b_nw"], sv["states"], dyb, nb=nb, s=s, nh=NH_B)
    g["b_conv_w"] = jnp.concatenate([jnp.concatenate(dbp[4 * j:4 * j + 4], axis=0) for j in range(3)], axis=1)
    g["b_a_log"] = dbp[12].reshape(NH_B, LANES)[:, 0]
    g["b_dt_bias"] = dbp[13].reshape(NH_B, LANES)[:, 0]
    g["b_norm"] = dbp[14][0]
    gsc = rscan("rglru_scan_bwd", sv["a"], dha, True, nb=nb, s=s)
    (dxa,), dap = seqop_bwd("rglru_pre_bwd", _rglru_pre_f, [(proj, C_XA)], _a_params(p), [(gsc, 0), (sv["ha"], 0)],
                            lambda t0, ge, he: (ge[SUBLANES:] * taps(he, 2)[1], ge[SUBLANES:]),
                            nb=nb, s=s, nblk=BW // LANES)
    g["a_conv_w"] = jnp.concatenate(dap[0:4], axis=0)
    g["a_conv_b"], g["a_b_r"], g["a_b_i"], g["a_lam"] = dap[4][0], dap[7][0], dap[8][0], dap[9][0]
    g["a_w_r"], g["a_w_i"] = _unblockdiag2(dap[5]), _unblockdiag2(dap[6])
    dproj = jnp.concatenate([dg0, dg1, dg2, dxa, dga, dq_b, dk_b, dv_b, dz_b, duc, dba], axis=1)
    dh = mm("d_in_proj_x", dproj, p["w_in_pad"], "nt")
    dwp = mm("d_in_proj_w", sv["h"], dproj, "tn")
    dba_w = dwp[:, C_BA:].reshape(D, NH_B, LANES)
    pieces = [dwp[:, c0:c0 + BW] for c0 in (C_XA, C_GA, C_Q, C_K, C_V, C_Z)]
    g["w_in"] = jnp.concatenate(pieces + [dba_w[:, :, 0], dba_w[:, :, 1], dwp[:, C_UC:C_UC + BW], dwp[:, :3 * D]],
                                axis=1)
    (dx,), (gn,) = rowop_bwd("norm_mix_bwd", _norm_f, [(sv["x"], 0, D)], [p["mix_norm"]], [(dh, 0, D)], [True],
                             add=(dx1, 0, D))
    g["mix_norm"] = gn[0]
    return dx, g


def _local_step(x, mem, target, w):
    nb, s, _ = x.shape
    m = mem.shape[1]
    depth = w["mix_norm"].shape[0]
    x2d, mem2d = x.reshape(nb * s, D), mem.reshape(nb * m, D)
    params, saved = [], []
    for l in range(depth):
        params.append(_layer_params(w, l))
        x2d, sv = _layer_fwd(x2d, mem2d, params[l], nb, s, m)
        saved.append(sv)
    loss_row, dx, gfn = loss_head(x2d, target.reshape(nb * s, D), w["final_norm"].reshape(1, D))
    grads = [None] * depth
    for l in reversed(range(depth)):
        dx, grads[l] = _layer_bwd(dx, mem2d, saved[l], params[l], nb, s, m)
    out = {n: jnp.stack([grads[l][n] for l in range(depth)]) for n in grads[0]}
    out["final_norm"] = gfn[0]
    return loss_row, dx.reshape(x.shape), out


SLAB_UNIT = 2 * 1024 * LANES


def _slab(parts, dtype, unit):
    flat = jnp.concatenate([a.astype(dtype).reshape(-1) for a in parts])
    pad = (-flat.shape[0]) % unit
    return jnp.pad(flat, (0, pad)) if pad else flat


def _unslab(flat, like):
    out, off = [], 0
    for a in like:
        out.append(flat[off:off + a.size].reshape(a.shape))
        off += a.size
    return out


def _gather_weights(name, local, names_axes, dtype):
    mine = [local[n].astype(dtype) for n, _ in names_axes]
    me = _chip_id(lax.axis_index("x"), lax.axis_index("y"))
    got = [_set_slot(g, me, a) for g, a in zip(gather_chips(name, mine), mine)]
    return {n: jnp.concatenate([g[j] for j in range(N_CHIPS)], axis=ax) for (n, ax), g in zip(names_axes, got)}


def kernel(x, mem, mix_norm, w_in, b_gate, a_conv_w, a_conv_b, a_w_r, a_b_r, a_w_i, a_b_i, a_lam, b_conv_w, b_a_log, b_dt_bias, b_norm, c_lam_re, c_lam_im, c_log_dt, c_b_re, c_b_im, c_c_re, c_c_im, c_d, c_glu_w, c_glu_b, w_branch, w_out, xa_norm, mem_norm, xa_w_q, xa_w_kv, xa_w_o, ffn_norm, ffn_w_up, ffn_conv_w, ffn_conv_b, ffn_w_down, final_norm, loss_target, m_mix_norm, m_w_in, m_b_gate, m_a_conv_w, m_a_conv_b, m_a_w_r, m_a_b_r, m_a_w_i, m_a_b_i, m_a_lam, m_b_conv_w, m_b_a_log, m_b_dt_bias, m_b_norm, m_c_lam_re, m_c_lam_im, m_c_log_dt, m_c_b_re, m_c_b_im, m_c_c_re, m_c_c_im, m_c_d, m_c_glu_w, m_c_glu_b, m_w_branch, m_w_out, m_xa_norm, m_mem_norm, m_xa_w_q, m_xa_w_kv, m_xa_w_o, m_ffn_norm, m_ffn_w_up, m_ffn_conv_w, m_ffn_conv_b, m_ffn_w_down, m_final_norm, v_mix_norm, v_w_in, v_b_gate, v_a_conv_w, v_a_conv_b, v_a_w_r, v_a_b_r, v_a_w_i, v_a_b_i, v_a_lam, v_b_conv_w, v_b_a_log, v_b_dt_bias, v_b_norm, v_c_lam_re, v_c_lam_im, v_c_log_dt, v_c_b_re, v_c_b_im, v_c_c_re, v_c_c_im, v_c_d, v_c_glu_w, v_c_glu_b, v_w_branch, v_w_out, v_xa_norm, v_mem_norm, v_xa_w_q, v_xa_w_kv, v_xa_w_o, v_ffn_norm, v_ffn_w_up, v_ffn_conv_w, v_ffn_conv_b, v_ffn_w_down, v_final_norm):
    env = dict(locals())
    w = {n: env[n] for n in WEIGHTS}
    mom = {n: env["m_" + n] for n in WEIGHTS}
    var = {n: env["v_" + n] for n in WEIGHTS}

    full = dict(w)
    full.update(_gather_weights("gather_mats", w, SHARDED_MATS, BF16))
    full.update(_gather_weights("gather_convs", w, SHARDED_CONVS, F32))
    loss_row, grad_x, gfull = _local_step(x, mem, loss_target, full)

    depth = mix_norm.shape[0]
    assert depth % 2 == 0
    hl = depth // 2
    res = {}
    by_chip = []
    for n, ax in SHARDED:
        width = w[n].shape[ax]
        by_chip.append(jnp.stack([jnp.stack([
            lax.slice_in_dim(gfull[n][hf * hl:(hf + 1) * hl], j * width, (j + 1) * width, axis=ax)
            for j in range(N_CHIPS)]) for hf in range(2)]))
    core = lax.axis_index("c")
    chip = _chip_id(lax.axis_index("x"), lax.axis_index("y"))
    sibs = swap_halves("swap_halves", by_chip)
    part = [add_pair("add_cores_" + n, g2, sb, core) for (n, _), g2, sb in zip(SHARDED, by_chip, sibs)]
    recv = scatter_chips("scatter_chips", part)
    mine = [add_chips("add_chips_" + n, p, r, chip) for (n, _), p, r in zip(SHARDED, part, recv)]
    for (n, _), own, joined in zip(SHARDED, mine, join_halves("join_halves", mine)):
        g = _set_slot(joined, core * hl, own)
        res["grad_" + n] = g
        res["delta_" + n], res["new_m_" + n], res["new_v_" + n] = adamw("adamw_" + n, w[n], g, mom[n], var[n])

    unit = SUBLANES * LANES
    small = _slab([gfull[n] for n in REPLICATED] + [loss_row[:, :1]], F32, unit).reshape(-1, LANES)
    g_rp = sumk("add_devices", _set_slot(gather_all("gather_all", small), 2 * chip + core, small))
    slab1 = lambda d: _slab([d[n] for n in REPLICATED] + [jnp.zeros((1, 1), F32)], F32, unit).reshape(-1, LANES)
    d_rp, m_rp, v_rp = adamw("adamw_replicated", slab1(w), g_rp, slab1(mom), slab1(var))
    like_rp = [w[n] for n in REPLICATED] + [jnp.zeros((1, 1), F32)]
    for kind, rp in (("grad", g_rp), ("delta", d_rp), ("new_m", m_rp), ("new_v", v_rp)):
        for n, a in zip(list(REPLICATED) + ["loss"], _unslab(rp.reshape(-1), like_rp)):
            res[kind + "_" + n] = a
    loss = res["grad_loss"].reshape(())
    return (loss, grad_x, *[res["grad_" + n] for n in WEIGHTS], *[res["delta_" + n] for n in WEIGHTS],
            *[res["new_m_" + n] for n in WEIGHTS], *[res["new_v_" + n] for n in WEIGHTS])
```

```python
import functools
import math

import jax
import jax.numpy as jnp
from jax import lax
from jax.experimental import pallas as pl
from jax.experimental.pallas import tpu as pltpu

F32, BF16 = jnp.float32, jnp.bfloat16
MESH = pl.DeviceIdType.MESH
ANY = pl.BlockSpec(memory_space=pl.ANY)

VMEM_LIMIT_V7X = 56 * 1024 * 1024
MM_TILE = 1024
LANES, SUBLANES = 128, 8
EPS = 1e-6
RG_C = 8.0
DN_CHUNK = 64
N_CHIPS = 4
ADAM_LR, ADAM_B1, ADAM_B2, ADAM_EPS, ADAM_WD, ADAM_STEP = 0.001, 0.9, 0.999, 1e-08, 0.01, 10

SHARDED_MATS = (("w_in", 2), ("c_glu_w", 1), ("w_branch", 3), ("w_out", 1), ("xa_w_q", 1), ("xa_w_kv", 2),
                ("xa_w_o", 1), ("ffn_w_up", 2), ("ffn_w_down", 1))
SHARDED_CONVS = (("a_conv_w", 2), ("b_conv_w", 2), ("ffn_conv_w", 2))
SHARDED = SHARDED_MATS + SHARDED_CONVS
WEIGHTS = ("mix_norm", "w_in", "b_gate", "a_conv_w", "a_conv_b", "a_w_r", "a_b_r", "a_w_i", "a_b_i", "a_lam",
           "b_conv_w", "b_a_log", "b_dt_bias", "b_norm", "c_lam_re", "c_lam_im", "c_log_dt", "c_b_re", "c_b_im",
           "c_c_re", "c_c_im", "c_d", "c_glu_w", "c_glu_b", "w_branch", "w_out", "xa_norm", "mem_norm", "xa_w_q",
           "xa_w_kv", "xa_w_o", "ffn_norm", "ffn_w_up", "ffn_conv_w", "ffn_conv_b", "ffn_w_down", "final_norm")
REPLICATED = tuple(n for n in WEIGHTS if n not in dict(SHARDED))


def _cparams(n_axes):
    return pltpu.CompilerParams(dimension_semantics=("arbitrary",) * n_axes, vmem_limit_bytes=VMEM_LIMIT_V7X)


def _tile(n, pref, off=0):
    t = pref
    while t >= LANES:
        if n % t == 0 and off % t == 0:
            return t
        t //= 2
    assert off == 0, (n, pref, off)
    return n


def _full_spec(shape, n_grid):
    nd = len(shape)
    if n_grid == 1:
        return pl.BlockSpec(shape, lambda i: (0,) * nd)
    return pl.BlockSpec(shape, lambda i, j: (0,) * nd)


_NN, _NT, _TN = ((1,), (0,)), ((1,), (1,)), ((0,), (0,))


def _dot(a, b, dims, hp):
    if hp:
        return lax.dot_general(a, b, (dims, ((), ())), precision=lax.Precision.HIGHEST, preferred_element_type=F32)
    return lax.dot_general(a.astype(BF16), b.astype(BF16), (dims, ((), ())), preferred_element_type=F32)


@functools.partial(jax.custom_vjp, nondiff_argnums=(2,))
def mm_nn(a, b, hp=False):
    return _dot(a, b, _NN, hp)


@functools.partial(jax.custom_vjp, nondiff_argnums=(2,))
def mm_nt(a, b, hp=False):
    return _dot(a, b, _NT, hp)


@functools.partial(jax.custom_vjp, nondiff_argnums=(2,))
def mm_tn(a, b, hp=False):
    return _dot(a, b, _TN, hp)


mm_nn.defvjp(lambda a, b, hp: (_dot(a, b, _NN, hp), (a, b)),
             lambda hp, r, g: (mm_nt(g, r[1], hp), mm_tn(r[0], g, hp)))
mm_nt.defvjp(lambda a, b, hp: (_dot(a, b, _NT, hp), (a, b)),
             lambda hp, r, g: (mm_nn(g, r[1], hp), mm_tn(g, r[0], hp)))
mm_tn.defvjp(lambda a, b, hp: (_dot(a, b, _TN, hp), (a, b)),
             lambda hp, r, g: (mm_nt(r[1], g, hp), mm_nn(r[0], g, hp)))


@functools.partial(jax.custom_vjp, nondiff_argnums=(1,))
def taps(xext, k):
    return tuple((xext if s == 0 else pltpu.roll(xext, s, 0))[SUBLANES:] for s in range(k))


def _taps_fwd(xext, k):
    return taps(xext, k), None


def _taps_bwd(k, _, gs):
    tot = None
    for s, g in enumerate(gs):
        gp = jnp.concatenate([jnp.zeros((SUBLANES, g.shape[1]), g.dtype), g], axis=0)
        if s:
            gp = pltpu.roll(gp, gp.shape[0] - s, 0)
        tot = gp if tot is None else tot + gp
    return (tot,)


taps.defvjp(_taps_fwd, _taps_bwd)


def _gelu(x):
    return x * (0.5 * (1.0 + jnp.tanh(0.7978845608028654 * (x + 0.044715 * (x * x * x)))))


def _sigmoid(x):
    return jax.nn.sigmoid(x)


def _softplus(x):
    return jnp.maximum(x, 0.0) + jnp.log1p(jnp.exp(-jnp.abs(x)))


def _rmsnorm(x, g):
    return x * lax.rsqrt(jnp.mean(x * x, axis=-1, keepdims=True) + EPS) * g


def _lane_pick(x, lane):
    sel = lax.broadcasted_iota(jnp.int32, x.shape, 1) == lane
    return jnp.sum(jnp.where(sel, x, 0.0), axis=1, keepdims=True)


def _load_ext(ref, t0, n):
    main = ref[pl.ds(t0, n), :].astype(F32)
    hstart = pl.multiple_of(jnp.maximum(t0 - SUBLANES, 0), SUBLANES)
    halo = ref[pl.ds(hstart, SUBLANES), :].astype(F32)
    halo = jnp.where(t0 > 0, halo, 0.0)
    return jnp.concatenate([halo, main], axis=0)


def mm(name, a, b, mode, *, out_dtype=F32, res=None, a_cols=None, b_cols=None):
    a0, aw = a_cols if a_cols else (0, a.shape[1])
    b0, bw = b_cols if b_cols else (0, b.shape[1])
    big = MM_TILE
    if mode == "nn":
        m, k, n = a.shape[0], aw, bw
        assert b.shape[0] == k
        tm, tk, tn = _tile(m, big), _tile(k, big, a0), _tile(n, big, b0)
        a_spec = pl.BlockSpec((tm, tk), lambda i, j, kk: (i, kk + a0 // tk))
        b_spec = pl.BlockSpec((tk, tn), lambda i, j, kk: (kk, j + b0 // tn))
        dims = _NN
    elif mode == "nt":
        m, k, n = a.shape[0], aw, b.shape[0]
        assert bw == k
        tm, tn = _tile(m, big), _tile(n, big)
        tk = _tile(k, big, math.gcd(a0, b0) if (a0 or b0) else 0)
        assert a0 % tk == 0 and b0 % tk == 0
        a_spec = pl.BlockSpec((tm, tk), lambda i, j, kk: (i, kk + a0 // tk))
        b_spec = pl.BlockSpec((tn, tk), lambda i, j, kk: (j, kk + b0 // tk))
        dims = _NT
    else:
        k, m, n = a.shape[0], aw, bw
        assert b.shape[0] == k
        tk, tm, tn = _tile(k, big), _tile(m, big, a0), _tile(n, big, b0)
        a_spec = pl.BlockSpec((tk, tm), lambda i, j, kk: (kk, i + a0 // tm))
        b_spec = pl.BlockSpec((tk, tn), lambda i, j, kk: (kk, j + b0 // tn))
        dims = _TN
    nk = k // tk
    has_res = res is not None

    def tile_dot(a_ref, b_ref):
        return lax.dot_general(a_ref[...].astype(BF16), b_ref[...].astype(BF16), (dims, ((), ())),
                               preferred_element_type=F32)

    def finish(out, refs, o_ref):
        if has_res:
            out = out + refs[2][...].astype(F32)
        o_ref[...] = out.astype(o_ref.dtype)

    def body_single(*refs):
        finish(tile_dot(refs[0], refs[1]), refs, refs[-1])

    def body_acc(*refs):
        o_ref, acc_ref = refs[-2], refs[-1]
        kk = pl.program_id(2)

        @pl.when(kk == 0)
        def _():
            acc_ref[...] = tile_dot(refs[0], refs[1])

        @pl.when((kk > 0) & (kk < nk - 1))
        def _():
            acc_ref[...] += tile_dot(refs[0], refs[1])

        @pl.when(kk == nk - 1)
        def _():
            finish(acc_ref[...] + tile_dot(refs[0], refs[1]), refs, o_ref)

    o_spec = pl.BlockSpec((tm, tn), lambda i, j, kk: (i, j))
    return pl.pallas_call(
        body_single if nk == 1 else body_acc, name=name, grid=(m // tm, n // tn, nk),
        in_specs=[a_spec, b_spec] + ([o_spec] if has_res else []), out_specs=o_spec,
        out_shape=jax.ShapeDtypeStruct((m, n), out_dtype),
        scratch_shapes=[] if nk == 1 else [pltpu.VMEM((tm, tn), F32)], compiler_params=_cparams(3),
    )(*((a, b, res) if has_res else (a, b)))


def _row_specs(rows, tm):
    return [pl.BlockSpec((tm, cw), lambda i, c=c0 // cw: (i, c)) for (_, c0, cw) in rows]


def rowop_fwd(name, f, rows, params, outs, tm=256):
    t = rows[0][0].shape[0]
    tm = min(tm, t)
    n_r, n_p = len(rows), len(params)

    def body(*refs):
        vals = f(*[r[...].astype(F32) for r in refs[:n_r]], *[p[...] for p in refs[n_r:n_r + n_p]])
        for o, v in zip(refs[n_r + n_p:], vals):
            o[...] = v.astype(o.dtype)

    res = pl.pallas_call(
        body, name=name, grid=(t // tm,),
        in_specs=_row_specs(rows, tm) + [_full_spec(p.shape, 1) for p in params],
        out_specs=[pl.BlockSpec((tm, cw), lambda i: (i, 0)) for cw, _ in outs],
        out_shape=[jax.ShapeDtypeStruct((t, cw), dt) for cw, dt in outs],
        compiler_params=_cparams(1),
    )(*[r[0] for r in rows], *params)
    return list(res)


def rowop_bwd(name, f, rows, params, couts, need, tm=256, add=None):
    t = rows[0][0].shape[0]
    tm = min(tm, t)
    n_r, n_p, n_c = len(rows), len(params), len(couts)
    want = [k for k in range(n_r) if need[k]]
    adds = [add] if add is not None else []

    def body(*refs):
        xs = [r[...].astype(F32) for r in refs[:n_r]]
        ps = [p[...] for p in refs[n_r:n_r + n_p]]
        cs = tuple(c[...].astype(F32) for c in refs[n_r + n_p:n_r + n_p + n_c])
        outs = refs[n_r + n_p + n_c + len(adds):]
        _, vjp = jax.vjp(f, *xs, *ps)
        gs = vjp(cs)
        for o, k in zip(outs[:len(want)], want):
            o[...] = gs[k] + refs[n_r + n_p + n_c][...] if (adds and k == want[0]) else gs[k]

        @pl.when(pl.program_id(0) == 0)
        def _():
            for o in outs[len(want):]:
                o[...] = jnp.zeros_like(o)

        for o, g in zip(outs[len(want):], gs[n_r:]):
            o[...] += g

    res = pl.pallas_call(
        body, name=name, grid=(t // tm,),
        in_specs=_row_specs(rows, tm) + [_full_spec(p.shape, 1) for p in params] + _row_specs(couts, tm)
        + _row_specs(adds, tm),
        out_specs=[pl.BlockSpec((tm, rows[k][2]), lambda i: (i, 0)) for k in want]
        + [_full_spec(p.shape, 1) for p in params],
        out_shape=[jax.ShapeDtypeStruct((t, rows[k][2]), F32) for k in want]
        + [jax.ShapeDtypeStruct(p.shape, F32) for p in params],
        compiler_params=_cparams(1),
    )(*[r[0] for r in rows], *params, *[c[0] for c in couts], *[a[0] for a in adds])
    return list(res[:len(want)]), list(res[len(want):])


def _seq_specs(seqs, s, cb, order):
    if order == "bj":
        return [pl.BlockSpec((s, cb), lambda b, j, c=c0 // cb: (b, c + j)) for (_, c0) in seqs]
    return [pl.BlockSpec((s, cb), lambda j, b, c=c0 // cb: (b, c + j)) for (_, c0) in seqs]


def _param_specs(params, order):
    if order == "bj":
        return [pl.BlockSpec(bs, lambda b, j, fn=fn: fn(j)) for (_, bs, fn) in params]
    return [pl.BlockSpec(bs, lambda j, b, fn=fn: fn(j)) for (_, bs, fn) in params]


def seqop_fwd(name, f, seqs, params, out_dtypes, *, nb, s, nblk, cb=LANES, n=256):
    n = min(n, s)
    n_s, n_p = len(seqs), len(params)

    def body(*refs):
        seq_refs, par_refs, out_refs = refs[:n_s], refs[n_s:n_s + n_p], refs[n_s + n_p:]

        def step(i, carry):
            t0 = pl.multiple_of(i * n, n)
            vals = f(t0, *[_load_ext(r, t0, n) for r in seq_refs], *[p[...] for p in par_refs])
            for o, v in zip(out_refs, vals):
                o[pl.ds(t0, n), :] = v.astype(o.dtype)
            return carry

        lax.fori_loop(0, s // n, step, 0)

    res = pl.pallas_call(
        body, name=name, grid=(nb, nblk),
        in_specs=_seq_specs(seqs, s, cb, "bj") + _param_specs(params, "bj"),
        out_specs=[pl.BlockSpec((s, cb), lambda b, j: (b, j)) for _ in out_dtypes],
        out_shape=[jax.ShapeDtypeStruct((nb * s, nblk * cb), dt) for dt in out_dtypes],
        compiler_params=_cparams(2),
    )(*[q[0] for q in seqs], *[p[0] for p in params])
    return list(res)


def seqop_bwd(name, f, seqs, params, cot_seqs, cot_fn, *, nb, s, nblk, cb=LANES, n=256):
    n = min(n, s)
    n_s, n_p, n_c = len(seqs), len(params), len(cot_seqs)
    nchunk = s // n

    def body(*refs):
        seq_refs, par_refs = refs[:n_s], refs[n_s:n_s + n_p]
        cot_refs = refs[n_s + n_p:n_s + n_p + n_c]
        dseq_refs = refs[n_s + n_p + n_c:n_s + n_p + n_c + n_s]
        dpar_refs = refs[n_s + n_p + n_c + n_s:]

        @pl.when(pl.program_id(1) == 0)
        def _():
            for o in dpar_refs:
                o[...] = jnp.zeros_like(o)

        def step(ii, halos):
            t0 = pl.multiple_of((nchunk - 1 - ii) * n, n)
            exts = [_load_ext(r, t0, n) for r in seq_refs]
            ps = [p[...] for p in par_refs]
            cots = cot_fn(t0, *[_load_ext(r, t0, n) for r in cot_refs])
            _, vjp = jax.vjp(lambda *args: f(t0, *args), *exts, *ps)
            gs = vjp(tuple(cots))
            tail = pl.multiple_of(t0 + n - SUBLANES, SUBLANES)
            new_halos = []
            for o, g, h in zip(dseq_refs, gs[:n_s], halos):
                o[pl.ds(t0, n), :] = g[SUBLANES:]
                o[pl.ds(tail, SUBLANES), :] += h
                new_halos.append(g[:SUBLANES])
            for o, g in zip(dpar_refs, gs[n_s:]):
                o[...] += g
            return tuple(new_halos)

        lax.fori_loop(0, nchunk, step, tuple(jnp.zeros((SUBLANES, cb), F32) for _ in range(n_s)))

    res = pl.pallas_call(
        body, name=name, grid=(nblk, nb),
        in_specs=_seq_specs(seqs, s, cb, "jb") + _param_specs(params, "jb") + _seq_specs(cot_seqs, s, cb, "jb"),
        out_specs=[pl.BlockSpec((s, cb), lambda j, b: (b, j)) for _ in seqs] + _param_specs(params, "jb"),
        out_shape=[jax.ShapeDtypeStruct((nb * s, nblk * cb), F32) for _ in seqs]
        + [jax.ShapeDtypeStruct(p[0].shape, F32) for p in params],
        compiler_params=_cparams(2),
    )(*[q[0] for q in seqs], *[p[0] for p in params], *[q[0] for q in cot_seqs])
    return list(res[:n_s]), list(res[n_s:])


def rscan(name, a, b, reverse, *, nb, s, cb=256):
    c = a.shape[1]
    cb = min(cb, c)

    def body(a_ref, b_ref, h_ref):
        def step(ii, car):
            i = (s // SUBLANES - 1 - ii) if reverse else ii
            base = pl.multiple_of(i * SUBLANES, SUBLANES)
            for r in (reversed(range(SUBLANES)) if reverse else range(SUBLANES)):
                row = pl.ds(base + r, 1)
                if reverse:
                    g = b_ref[row, :] + car
                    h_ref[row, :] = g
                    car = a_ref[row, :] * g
                else:
                    car = a_ref[row, :] * car + b_ref[row, :]
                    h_ref[row, :] = car
            return car

        lax.fori_loop(0, s // SUBLANES, step, jnp.zeros((1, cb), F32))

    spec = pl.BlockSpec((s, cb), lambda bb, j: (bb, j))
    return pl.pallas_call(
        body, name=name, grid=(nb, c // cb), in_specs=[spec, spec], out_specs=spec,
        out_shape=jax.ShapeDtypeStruct(a.shape, F32), compiler_params=_cparams(2),
    )(a, b)


def cscan(name, bu, ar, ai, reverse, *, nb, s, cb=256):
    p = ar.shape[1]
    cb = min(cb, p)
    nj = p // cb

    def body(br_ref, bi_ref, ar_ref, ai_ref, hr_ref, hi_ref):
        lr = ar_ref[...]
        li = -ai_ref[...] if reverse else ai_ref[...]

        def step(ii, car):
            hr, hi = car
            i = (s // SUBLANES - 1 - ii) if reverse else ii
            base = pl.multiple_of(i * SUBLANES, SUBLANES)
            for r in (reversed(range(SUBLANES)) if reverse else range(SUBLANES)):
                row = pl.ds(base + r, 1)
                nr = lr * hr - li * hi + br_ref[row, :]
                ni = lr * hi + li * hr + bi_ref[row, :]
                hr, hi = nr, ni
                hr_ref[row, :] = hr
                hi_ref[row, :] = hi
            return hr, hi

        z = jnp.zeros((1, cb), F32)
        lax.fori_loop(0, s // SUBLANES, step, (z, z))

    o_spec = pl.BlockSpec((s, cb), lambda bb, j: (bb, j))
    l_spec = pl.BlockSpec((1, cb), lambda bb, j: (0, j))
    res = pl.pallas_call(
        body, name=name, grid=(nb, nj),
        in_specs=[o_spec, pl.BlockSpec((s, cb), lambda bb, j: (bb, j + nj)), l_spec, l_spec],
        out_specs=[o_spec, o_spec],
        out_shape=[jax.ShapeDtypeStruct((nb * s, p), F32)] * 2, compiler_params=_cparams(2),
    )(bu, bu, ar, ai)
    return res[0], res[1]


def _norm_f(x, g):
    return (_rmsnorm(x, g),)


def _rglru_pre_f(t0, xext, w0, w1, w2, w3, cb, wr, wi, br, bi, lam):
    x0, x1, x2, x3 = taps(xext, 4)
    xc = w3 * x0 + w2 * x1 + w1 * x2 + w0 * x3 + cb
    r = _sigmoid(mm_nn(xc, wr[0]) + br)
    ig = _sigmoid(mm_nn(xc, wi[0]) + bi)
    log_a = -RG_C * r * _softplus(-lam)
    a = jnp.exp(log_a)
    first = (lax.broadcasted_iota(jnp.int32, a.shape, 0) + t0) == 0
    mult = jnp.where(first, 1.0, jnp.sqrt(1.0 - jnp.exp(2.0 * log_a)))
    return a, mult * ig * xc


def _ffn_act_f(t0, gext, vext, g0, g1, g2, v0, v1, v2, bg, bv):
    ga, gb, gc = taps(gext, 3)
    va, vb, vc = taps(vext, 3)
    ug = g2 * ga + g1 * gb + g0 * gc + bg
    uv = v2 * va + v1 * vb + v0 * vc + bv
    return (_gelu(ug) * uv,)


def _s5_post_f(ys, uc, d, gw, gb):
    yc = _gelu(ys + d * uc)
    return (yc * _sigmoid(mm_nn(yc, gw) + gb),)


def _merge_f(ha, ga, yb, yc, g0, g1, g2, wb0, wb1, wb2, bg0, bg1, bg2):
    ya = ha * _gelu(ga)
    out = _sigmoid(g0 + bg0) * mm_nn(ya, wb0)
    out = out + _sigmoid(g1 + bg1) * mm_nn(yb, wb1)
    out = out + _sigmoid(g2 + bg2) * mm_nn(yc, wb2)
    return (out,)


def _s5_param_f(lr, li, ldt, btr, bti):
    dt = jnp.exp(ldt)
    mag = jnp.exp(lr * dt)
    ar, ai = mag * jnp.cos(li * dt), mag * jnp.sin(li * dt)
    den = lr * lr + li * li
    fr = ((ar - 1.0) * lr + ai * li) / den
    fi = (ai * lr - (ar - 1.0) * li) / den
    return ar, ai, fr * btr - fi * bti, fr * bti + fi * btr


def _each(fn, *lists):
    return [fn(*xs) for xs in zip(*lists)]


def _tri_inv(mats):
    c = mats[0].shape[0]
    ri = lax.broadcasted_iota(jnp.int32, (c, c), 0)
    ci = lax.broadcasted_iota(jnp.int32, (c, c), 1)
    eye = jnp.where(ri == ci, 1.0, 0.0).astype(F32)
    hp = lambda a, b: mm_nn(a, b, True)
    ad = [jnp.where((ri >> 4) == (ci >> 4), a, 0.0) for a in mats]
    ao = _each(lambda a, d: a - d, mats, ad)
    a2 = _each(hp, ad, ad)
    a4 = _each(hp, a2, a2)
    a8 = _each(hp, a4, a4)
    lo = _each(lambda d, s: hp(eye - d, eye + s), ad, a2)
    hi = _each(lambda s, e: hp(eye + s, eye + e), a4, a8)
    td = _each(hp, lo, hi)
    nn_ = _each(hp, td, ao)
    n2 = _each(hp, nn_, nn_)
    return _each(hp, _each(lambda n, s: hp(eye - n, eye + s), nn_, n2), td)


def _dn_chunk_f(nh, *args):
    per = [args[20 * h:20 * (h + 1)] for h in range(nh)]
    nw = args[20 * nh]
    state, qe, ke, ve, z, ba = ([p[j] for p in per] for j in range(6))
    c = DN_CHUNK
    dk = qe[0].shape[1]

    def conv_silu(xes, first):
        tp = [taps(x, 4) for x in xes]
        u = [p[first + 3] * t[0] + p[first + 2] * t[1] + p[first + 1] * t[2] + p[first] * t[3] for p, t in zip(per, tp)]
        return [x * _sigmoid(x) for x in u]

    def l2n(xs):
        return [x * lax.rsqrt(jnp.sum(x * x, axis=-1, keepdims=True) + EPS) for x in xs]

    q = [x * (dk ** -0.5) for x in l2n(conv_silu(qe, 6))]
    k = l2n(conv_silu(ke, 10))
    v = conv_silu(ve, 14)
    beta = [_sigmoid(_lane_pick(b, 0)) for b in ba]
    g = [-jnp.exp(_lane_pick(p[18], 0)) * _softplus(_lane_pick(b, 1) + _lane_pick(p[19], 0)) for p, b in zip(per, ba)]
    ri = lax.broadcasted_iota(jnp.int32, (c, c), 0)
    ci = lax.broadcasted_iota(jnp.int32, (c, c), 1)
    incl, strict = ri >= ci, ri > ci
    ltri = jnp.where(incl, 1.0, 0.0).astype(F32)
    mean_row = jnp.full((c, dk), 1.0 / dk, F32)
    gc_w = [mm_nn(ltri, jnp.broadcast_to(x, (c, dk)), True) for x in g]
    gc_c = [mm_nn(ltri, jnp.broadcast_to(x, (c, c)), True) for x in g]
    gc_r = [mm_nt(mean_row, x, True) for x in gc_w]
    gtot = [jnp.sum(x, axis=0, keepdims=True) for x in g]
    decay = _each(lambda a, b: jnp.exp(jnp.where(incl, a - b, -1e30)), gc_c, gc_r)
    e_gc = [jnp.exp(x) for x in gc_w]
    kb = _each(lambda a, b: a * b, k, beta)
    a_mat = _each(lambda a, b, d: jnp.where(strict, mm_nt(a, b) * d, 0.0), kb, k, decay)
    t_inv = _tri_inv(a_mat)
    u = _each(lambda t, a, b: mm_nn(t, a * b, True), t_inv, v, beta)
    w = _each(lambda t, a, e: mm_nn(t, a * e, True), t_inv, kb, e_gc)
    qk = _each(lambda a, b, d: jnp.where(incl, mm_nt(a, b) * d, 0.0), q, k, decay)
    v_new = _each(lambda a, b, s: a - mm_nn(b, s), u, w, state)
    o = _each(lambda a, e, s, b, n: mm_nn(a * e, s) + mm_nn(b, n), q, e_gc, state, qk, v_new)
    new_state = _each(lambda s, t, a, gw, n: s * jnp.exp(t) + mm_tn(a * jnp.exp(t - gw), n), state, gtot, k, gc_w, v_new)
    o = _each(lambda x, zz: _rmsnorm(x, nw) * (zz * _sigmoid(zz)), o, z)
    return tuple(o) + tuple(new_state)


def _attn_f(q, k, v):
    sc = mm_nt(q, k) * (q.shape[1] ** -0.5)
    sc = sc - jnp.max(sc, axis=-1, keepdims=True)
    e = jnp.exp(sc)
    p = e / jnp.sum(e, axis=-1, keepdims=True)
    return (mm_nn(p, v),)


DN_SEG = 512


def _dn_ext(ref, edge, t0, i, h):
    lanes = pl.ds(h * LANES, LANES)
    hstart = pl.multiple_of(jnp.maximum(t0 - SUBLANES, 0), SUBLANES)
    halo = jnp.where(i > 0, ref[pl.ds(hstart, SUBLANES), lanes], edge[:, h * LANES:(h + 1) * LANES])
    return jnp.concatenate([halo, ref[pl.ds(t0, DN_CHUNK), lanes]], axis=0)


def _dn_layout(cols, nb, s, nh, reverse):
    seg = min(DN_SEG, s)
    nseg, w = s // seg, nh * LANES
    hb = seg // SUBLANES

    def sg_of(g):
        return (nseg - 1 - g) if reverse else g

    main = [pl.BlockSpec((seg, w), lambda b, g, c=c0 // w: (b * nseg + sg_of(g), c)) for c0 in cols]
    edge = [pl.BlockSpec((SUBLANES, w), lambda b, g, c=c0 // w: (jnp.maximum((b * nseg + sg_of(g)) * hb - 1, 0), c))
            for c0 in cols[:3]]
    par = [pl.BlockSpec((1, w), lambda b, g: (0, 0))] * 14 + [pl.BlockSpec((1, LANES), lambda b, g: (0, 0))]
    seq = pl.BlockSpec((seg, w), lambda b, g: (b * nseg + sg_of(g), 0))
    st = pl.BlockSpec((1, nh, seg // DN_CHUNK, LANES, LANES), lambda b, g: (b, 0, sg_of(g), 0, 0))
    return seg, nseg, main, edge, par, seq, st


def deltanet_fwd(proj, cols, taps12, alog, dtb, nw, *, nb, s, nh):
    seg, nseg, main, edge, par, seq, st = _dn_layout(cols, nb, s, nh, False)
    ncs = seg // DN_CHUNK

    def body(*refs):
        q_ref, k_ref, v_ref, z_ref, ba_ref = refs[:5]
        edges = [jnp.where(pl.program_id(1) > 0, r[...], 0.0) for r in refs[5:8]]
        pars = [p[...] for p in refs[8:23]]
        o_ref, st_ref, state_scr = refs[23], refs[24], refs[25]

        @pl.when(pl.program_id(1) == 0)
        def _():
            state_scr[...] = jnp.zeros_like(state_scr)

        def step(i, carry):
            t0 = pl.multiple_of(i * DN_CHUNK, DN_CHUNK)
            rows = pl.ds(t0, DN_CHUNK)
            states = [state_scr[h] for h in range(nh)]
            args = []
            for h in range(nh):
                lanes = pl.ds(h * LANES, LANES)
                args += ([states[h]] + [_dn_ext(r, e, t0, i, h) for r, e in zip((q_ref, k_ref, v_ref), edges)]
                         + [z_ref[rows, lanes], ba_ref[rows, lanes]]
                         + [p[:, h * LANES:(h + 1) * LANES] for p in pars[:14]])
            outs = _dn_chunk_f(nh, *args, pars[14])
            for h in range(nh):
                st_ref[0, h, i] = states[h]
                o_ref[rows, pl.ds(h * LANES, LANES)] = outs[h]
                state_scr[h] = outs[nh + h]
            return carry

        lax.fori_loop(0, ncs, step, 0)

    res = pl.pallas_call(
        body, name="deltanet_fwd", grid=(nb, nseg),
        in_specs=main + edge + par, out_specs=[seq, st],
        out_shape=[jax.ShapeDtypeStruct((nb * s, nh * LANES), F32),
                   jax.ShapeDtypeStruct((nb, nh, s // DN_CHUNK, LANES, LANES), F32)],
        scratch_shapes=[pltpu.VMEM((nh, LANES, LANES), F32)], compiler_params=_cparams(2),
    )(proj, proj, proj, proj, proj, proj, proj, proj, *taps12, alog, dtb, nw)
    return res[0], res[1]


def deltanet_bwd(proj, cols, taps12, alog, dtb, nw, states, dyb, *, nb, s, nh):
    seg, nseg, main, edge, par, seq, st = _dn_layout(cols, nb, s, nh, True)
    ncs = seg // DN_CHUNK

    def body(*refs):
        q_ref, k_ref, v_ref, z_ref, ba_ref = refs[:5]
        first_in_time = pl.program_id(1) == nseg - 1
        edges = [jnp.where(first_in_time, 0.0, r[...]) for r in refs[5:8]]
        pars = [p[...] for p in refs[8:23]]
        st_ref, do_ref = refs[23], refs[24]
        dseq_refs = refs[25:28]
        dz_ref, dba_ref = refs[28], refs[29]
        dpar_refs = refs[30:45]
        dstate_scr, dhalo_scr = refs[45], refs[46]

        @pl.when((pl.program_id(0) == 0) & (pl.program_id(1) == 0))
        def _():
            for o in dpar_refs:
                o[...] = jnp.zeros_like(o)

        @pl.when(pl.program_id(1) == 0)
        def _():
            dstate_scr[...] = jnp.zeros_like(dstate_scr)
            dhalo_scr[...] = jnp.zeros_like(dhalo_scr)

        def step(ii, carry):
            i = ncs - 1 - ii
            t0 = pl.multiple_of(i * DN_CHUNK, DN_CHUNK)
            rows = pl.ds(t0, DN_CHUNK)
            d_states = [dstate_scr[h] for h in range(nh)]
            d_halos = [dhalo_scr[k] for k in range(3)]
            args = []
            for h in range(nh):
                lanes = pl.ds(h * LANES, LANES)
                args += ([st_ref[0, h, i]] + [_dn_ext(r, e, t0, i, h) for r, e in zip((q_ref, k_ref, v_ref), edges)]
                         + [z_ref[rows, lanes], ba_ref[rows, lanes]]
                         + [p[:, h * LANES:(h + 1) * LANES] for p in pars[:14]])
            _, vjp = jax.vjp(functools.partial(_dn_chunk_f, nh), *args, pars[14])
            grads = vjp(tuple(do_ref[rows, pl.ds(h * LANES, LANES)] for h in range(nh)) + tuple(d_states))
            dpar_refs[14][...] += grads[20 * nh]
            for h in range(nh):
                lanes = pl.ds(h * LANES, LANES)
                gs = grads[20 * h:20 * (h + 1)]
                dstate_scr[h] = gs[0]
                for k, (o, g) in enumerate(zip(dseq_refs, gs[1:4])):
                    o[rows, lanes] = jnp.concatenate(
                        [g[SUBLANES:DN_CHUNK], g[DN_CHUNK:] + d_halos[k][:, h * LANES:(h + 1) * LANES]], axis=0)
                    dhalo_scr[k, :, lanes] = g[:SUBLANES]
                dz_ref[rows, lanes] = gs[4]
                dba_ref[rows, lanes] = gs[5]
                for o, g in zip(dpar_refs[:14], gs[6:20]):
                    o[:, lanes] += g
            return carry

        lax.fori_loop(0, ncs, step, 0)

    w = nh * LANES
    res = pl.pallas_call(
        body, name="deltanet_bwd", grid=(nb, nseg),
        in_specs=main + edge + par + [st, seq], out_specs=[seq] * 5 + par,
        out_shape=[jax.ShapeDtypeStruct((nb * s, w), F32)] * 5
        + [jax.ShapeDtypeStruct((1, w), F32)] * 14 + [jax.ShapeDtypeStruct((1, LANES), F32)],
        scratch_shapes=[pltpu.VMEM((nh, LANES, LANES), F32), pltpu.VMEM((3, SUBLANES, w), F32)],
        compiler_params=_cparams(2),
    )(proj, proj, proj, proj, proj, proj, proj, proj, *taps12, alog, dtb, nw, states, dyb)
    return list(res[:5]), list(res[5:])


def attn_fwd(q, kv, *, nb, s, m, nh, tq=512):
    hd = q.shape[1] // nh
    tq = min(tq, s)
    nq = s // tq

    def body(q_ref, k_ref, v_ref, o_ref):
        o_ref[...] = _attn_f(q_ref[...].astype(F32), k_ref[...], v_ref[...])[0].astype(o_ref.dtype)

    return pl.pallas_call(
        body, name="attn_fwd", grid=(nb, nh, nq),
        in_specs=[pl.BlockSpec((tq, hd), lambda b, h, i: (b * nq + i, h)),
                  pl.BlockSpec((m, hd), lambda b, h, i: (b, h)),
                  pl.BlockSpec((m, hd), lambda b, h, i: (b, nh + h))],
        out_specs=pl.BlockSpec((tq, hd), lambda b, h, i: (b * nq + i, h)),
        out_shape=jax.ShapeDtypeStruct(q.shape, BF16), compiler_params=_cparams(3),
    )(q, kv, kv)


def attn_bwd(q, kv, do, *, nb, s, m, nh, tq=512):
    hd = q.shape[1] // nh
    tq = min(tq, s)
    nq = s // tq

    def body(q_ref, k_ref, v_ref, do_ref, dq_ref, dk_ref, dv_ref):
        _, vjp = jax.vjp(_attn_f, q_ref[...].astype(F32), k_ref[...], v_ref[...])
        dq, dk, dv = vjp((do_ref[...].astype(F32),))
        dq_ref[...] = dq

        @pl.when(pl.program_id(2) == 0)
        def _():
            dk_ref[...] = jnp.zeros_like(dk_ref)
            dv_ref[...] = jnp.zeros_like(dv_ref)

        dk_ref[...] += dk
        dv_ref[...] += dv

    q_spec = pl.BlockSpec((tq, hd), lambda b, h, i: (b * nq + i, h))
    m_spec = pl.BlockSpec((m, hd), lambda b, h, i: (b, h))
    res = pl.pallas_call(
        body, name="attn_bwd", grid=(nb, nh, nq),
        in_specs=[q_spec, m_spec, pl.BlockSpec((m, hd), lambda b, h, i: (b, nh + h)), q_spec],
        out_specs=[q_spec, m_spec, m_spec],
        out_shape=[jax.ShapeDtypeStruct(q.shape, F32), jax.ShapeDtypeStruct((nb * m, nh * hd), F32),
                   jax.ShapeDtypeStruct((nb * m, nh * hd), F32)],
        compiler_params=_cparams(3),
    )(q, kv, kv, do)
    return res[0], res[1], res[2]


def loss_head(x, target, gain, tm=256):
    t, d = x.shape
    tm = min(tm, t)

    def body(x_ref, t_ref, g_ref, l_ref, dx_ref, dg_ref):
        tgt = t_ref[...]

        def lf(xv, gv):
            e = _rmsnorm(xv, gv) - tgt
            return 0.5 * jnp.sum(jnp.mean(e * e, axis=-1, keepdims=True), axis=0, keepdims=True)

        lv, vjp = jax.vjp(lf, x_ref[...], g_ref[...])
        dx, dg = vjp(jnp.ones((1, 1), F32))
        dx_ref[...] = dx

        @pl.when(pl.program_id(0) == 0)
        def _():
            l_ref[...] = jnp.zeros_like(l_ref)
            dg_ref[...] = jnp.zeros_like(dg_ref)

        l_ref[...] += jnp.broadcast_to(lv, l_ref.shape)
        dg_ref[...] += dg

    row = pl.BlockSpec((tm, d), lambda i: (i, 0))
    res = pl.pallas_call(
        body, name="loss_head", grid=(t // tm,),
        in_specs=[row, row, _full_spec((1, d), 1)],
        out_specs=[_full_spec((1, LANES), 1), row, _full_spec((1, d), 1)],
        out_shape=[jax.ShapeDtypeStruct((1, LANES), F32), jax.ShapeDtypeStruct((t, d), F32),
                   jax.ShapeDtypeStruct((1, d), F32)],
        compiler_params=_cparams(1),
    )(x, target, gain)
    return res[0], res[1], res[2]


def s5_dlam(gr, gi, hr, hi, *, nb, s, cb=256, n=256):
    p = gr.shape[1]
    cb, n = min(cb, p), min(n, s)

    def body(gr_ref, gi_ref, hr_ref, hi_ref, dar_ref, dai_ref):
        @pl.when(pl.program_id(1) == 0)
        def _():
            dar_ref[...] = jnp.zeros_like(dar_ref)
            dai_ref[...] = jnp.zeros_like(dai_ref)

        def step(i, carry):
            t0 = pl.multiple_of(i * n, n)
            g_r, g_i = gr_ref[pl.ds(t0, n), :], gi_ref[pl.ds(t0, n), :]
            p_r = taps(_load_ext(hr_ref, t0, n), 2)[1]
            p_i = taps(_load_ext(hi_ref, t0, n), 2)[1]
            dar_ref[...] += jnp.sum(g_r * p_r + g_i * p_i, axis=0, keepdims=True)
            dai_ref[...] += jnp.sum(g_i * p_r - g_r * p_i, axis=0, keepdims=True)
            return carry

        lax.fori_loop(0, s // n, step, 0)

    blk = pl.BlockSpec((s, cb), lambda j, b: (b, j))
    acc = pl.BlockSpec((1, cb), lambda j, b: (0, j))
    res = pl.pallas_call(
        body, name="s5_dlam", grid=(p // cb, nb), in_specs=[blk] * 4, out_specs=[acc, acc],
        out_shape=[jax.ShapeDtypeStruct((1, p), F32)] * 2, compiler_params=_cparams(2),
    )(gr, gi, hr, hi)
    return res[0], res[1]


def s5_param_fwd(lr, li, ldt, btr, bti):
    def body(*refs):
        vals = _s5_param_f(*[r[...] for r in refs[:5]])
        for o, v in zip(refs[5:], vals):
            o[...] = v

    return pl.pallas_call(
        body, name="s5_param_fwd",
        out_shape=[jax.ShapeDtypeStruct(lr.shape, F32)] * 2 + [jax.ShapeDtypeStruct(btr.shape, F32)] * 2,
    )(lr, li, ldt, btr, bti)


def s5_param_bwd(lr, li, ldt, btr, bti, cots):
    def body(*refs):
        _, vjp = jax.vjp(_s5_param_f, *[r[...] for r in refs[:5]])
        gs = vjp(tuple(r[...] for r in refs[5:9]))
        for o, g in zip(refs[9:], gs):
            o[...] = g

    return pl.pallas_call(
        body, name="s5_param_bwd",
        out_shape=[jax.ShapeDtypeStruct(a.shape, F32) for a in (lr, li, ldt, btr, bti)],
    )(lr, li, ldt, btr, bti, *cots)


def _row_block(r, c, copies):
    lanes = -(-c // LANES) * LANES
    rb = 2048
    while rb > SUBLANES and (r % rb or copies * rb * lanes * 4 > 12 * 1024 * 1024):
        rb //= 2
    return rb if r % rb == 0 else r


def _as2d(a, lead=0):
    return a.reshape(a.shape[:lead] + (-1, a.shape[-1]))


def sumk(name, buf):
    b3 = _as2d(buf, 1)
    k, r, c = b3.shape
    rb = _row_block(r, c, 2 * (k + 1))

    def body(b_ref, o_ref):
        acc = b_ref[0]
        for i in range(1, k):
            acc = acc + b_ref[i]
        o_ref[...] = acc

    out = pl.pallas_call(
        body, name=name, grid=(r // rb,),
        in_specs=[pl.BlockSpec((k, rb, c), lambda i: (0, i, 0))],
        out_specs=pl.BlockSpec((rb, c), lambda i: (i, 0)),
        out_shape=jax.ShapeDtypeStruct((r, c), F32), compiler_params=_cparams(1),
    )(b3)
    return out.reshape(buf.shape[1:])


def adamw(name, w, g, m, v):
    shape = w.shape
    w, g, m, v = (_as2d(a) for a in (w, g, m, v))
    r, c = w.shape
    rb = _row_block(r, c, 14)

    def body(w_ref, g_ref, m_ref, v_ref, d_ref, nm_ref, nv_ref):
        gv = g_ref[...]
        nm = ADAM_B1 * m_ref[...] + (1.0 - ADAM_B1) * gv
        nv = ADAM_B2 * v_ref[...] + (1.0 - ADAM_B2) * (gv * gv)
        m_hat = nm / (1.0 - ADAM_B1 ** ADAM_STEP)
        v_hat = nv / (1.0 - ADAM_B2 ** ADAM_STEP)
        d_ref[...] = -ADAM_LR * (m_hat / (jnp.sqrt(v_hat) + ADAM_EPS) + ADAM_WD * w_ref[...])
        nm_ref[...] = nm
        nv_ref[...] = nv

    spec = pl.BlockSpec((rb, c), lambda i: (i, 0))
    res = pl.pallas_call(
        body, name=name, grid=(r // rb,), in_specs=[spec] * 4, out_specs=[spec] * 3,
        out_shape=[jax.ShapeDtypeStruct(w.shape, F32)] * 3, compiler_params=_cparams(1),
    )(w, g, m, v)
    return tuple(a.reshape(shape) for a in res)


def _place():
    x, y, c = lax.axis_index("x"), lax.axis_index("y"), lax.axis_index("c")
    chips = [(1 - x, y), (x, 1 - y), (1 - x, 1 - y)]
    return x, y, c, chips


def _chip_id(px, py):
    return 2 * px + py


def _rcopy(src, dst, send_sems, recv_sems, k, to):
    return pltpu.make_async_remote_copy(src_ref=src, dst_ref=dst, send_sem=send_sems.at[k], recv_sem=recv_sems.at[k],
                                        device_id=to, device_id_type=MESH)


def _comm_call(name, body, out_shapes, n_sems, args):
    return pl.pallas_call(
        body, name=name, out_shape=out_shapes, in_specs=[ANY] * len(args), out_specs=[ANY] * len(out_shapes),
        scratch_shapes=[pltpu.SemaphoreType.DMA((n_sems,)), pltpu.SemaphoreType.DMA((n_sems,))],
    )(*args)


def _finish(remote):
    for cp in remote:
        cp.wait_send()


def _set_slot(buf, slot, val):
    if val.ndim < buf.ndim:
        val = val[None]
    return lax.dynamic_update_slice(buf, val.astype(buf.dtype), (slot,) + (0,) * (buf.ndim - 1))


def gather_chips(name, arrays):
    n = len(arrays)
    halves = [a.shape[0] // 2 for a in arrays]

    def body(*refs):
        ins, outs = refs[:n], refs[n:2 * n]
        send_sems, recv_sems = refs[2 * n:]
        x, y, c, chips = _place()
        me, sib = _chip_id(x, y), (x, y, 1 - c)
        half = [pl.ds(c * h, h) for h in halves]
        other = [pl.ds((1 - c) * h, h) for h in halves]
        remote = []
        for a in range(n):
            for j, chip in enumerate(chips):
                remote.append(_rcopy(ins[a].at[half[a]], outs[a].at[me, half[a]], send_sems, recv_sems, 6 * a + j,
                                     (*chip, c)))
                remote[-1].start()
        for a in range(n):
            for j, chip in enumerate(chips):
                landed = outs[a].at[_chip_id(*chip), half[a]]
                _rcopy(ins[a].at[half[a]], landed, send_sems, recv_sems, 6 * a + j, sib).wait_recv()
                remote.append(_rcopy(landed, landed, send_sems, recv_sems, 6 * a + 3 + j, sib))
                remote[-1].start()
        for a in range(n):
            for j, chip in enumerate(chips):
                _rcopy(ins[a].at[half[a]], outs[a].at[_chip_id(*chip), other[a]], send_sems, recv_sems, 6 * a + 3 + j,
                       sib).wait_recv()
        _finish(remote)

    shapes = [jax.ShapeDtypeStruct((N_CHIPS,) + a.shape, a.dtype) for a in arrays]
    return _comm_call(name, body, shapes, 6 * n, arrays)


def swap_halves(name, arrays):
    n = len(arrays)

    def body(*refs):
        ins, outs = refs[:n], refs[n:2 * n]
        send_sems, recv_sems = refs[2 * n:]
        x, y, c, _ = _place()
        sib = (x, y, 1 - c)
        remote = []
        for a in range(n):
            for k in range(N_CHIPS):
                remote.append(_rcopy(ins[a].at[1 - c, k], outs[a].at[k], send_sems, recv_sems, N_CHIPS * a + k, sib))
                remote[-1].start()
        for a in range(n):
            for k in range(N_CHIPS):
                _rcopy(ins[a].at[1 - c, k], outs[a].at[k], send_sems, recv_sems, N_CHIPS * a + k, sib).wait_recv()
        _finish(remote)

    shapes = [jax.ShapeDtypeStruct(a.shape[1:], a.dtype) for a in arrays]
    return _comm_call(name, body, shapes, N_CHIPS * n, arrays)


def scatter_chips(name, arrays):
    n = len(arrays)

    def body(*refs):
        ins, outs = refs[:n], refs[n:2 * n]
        send_sems, recv_sems = refs[2 * n:]
        x, y, c, chips = _place()
        remote = []
        for a in range(n):
            for j, chip in enumerate(chips):
                remote.append(_rcopy(ins[a].at[_chip_id(*chip)], outs[a].at[j], send_sems, recv_sems, 3 * a + j,
                                     (*chip, c)))
                remote[-1].start()
        for a in range(n):
            for j, chip in enumerate(chips):
                _rcopy(ins[a].at[0], outs[a].at[j], send_sems, recv_sems, 3 * a + j, (*chip, c)).wait_recv()
        _finish(remote)

    shapes = [jax.ShapeDtypeStruct((3,) + a.shape[1:], a.dtype) for a in arrays]
    return _comm_call(name, body, shapes, 3 * n, arrays)


def join_halves(name, arrays):
    n = len(arrays)
    offs = [sum(a.shape[0] for a in arrays[:i]) for i in range(n)]

    def body(*refs):
        ins, outs = refs[:n], refs[n:2 * n]
        send_sems, recv_sems = refs[2 * n:]
        x, y, c, _ = _place()
        sib = (x, y, 1 - c)
        remote = []
        for a in range(n):
            h = arrays[a].shape[0]
            for k in range(h):
                remote.append(_rcopy(ins[a].at[k], outs[a].at[c * h + k], send_sems, recv_sems, offs[a] + k, sib))
                remote[-1].start()
        for a in range(n):
            h = arrays[a].shape[0]
            for k in range(h):
                _rcopy(ins[a].at[k], outs[a].at[(1 - c) * h + k], send_sems, recv_sems, offs[a] + k, sib).wait_recv()
        _finish(remote)

    shapes = [jax.ShapeDtypeStruct((2 * a.shape[0],) + a.shape[1:], a.dtype) for a in arrays]
    return _comm_call(name, body, shapes, offs[-1] + arrays[-1].shape[0], arrays)


def add_pair(name, mine2, other, c):
    a3, o2 = _as2d(mine2, 1), _as2d(other)
    r, cols = o2.shape
    rb = _row_block(r, cols, 6)

    def body(c_ref, a_ref, o_ref, out_ref):
        out_ref[...] = a_ref[0] + o_ref[...]

    out = pl.pallas_call(
        body, name=name, out_shape=jax.ShapeDtypeStruct((r, cols), F32),
        grid_spec=pltpu.PrefetchScalarGridSpec(
            num_scalar_prefetch=1, grid=(r // rb,),
            in_specs=[pl.BlockSpec((1, rb, cols), lambda i, s: (s[0], i, 0)),
                      pl.BlockSpec((rb, cols), lambda i, s: (i, 0))],
            out_specs=pl.BlockSpec((rb, cols), lambda i, s: (i, 0))),
        compiler_params=_cparams(1),
    )(jnp.reshape(c, (1,)).astype(jnp.int32), a3, o2)
    return out.reshape(other.shape)


def add_chips(name, part, recv, me):
    p3, r3 = _as2d(part, 1), _as2d(recv, 1)
    _, r, cols = p3.shape
    rb = _row_block(r, cols, 10)

    def body(me_ref, p_ref, r_ref, out_ref):
        own, acc = p_ref[0], None
        for k in range(N_CHIPS):
            rel = jnp.full(own.shape, me_ref[0] ^ k, jnp.int32)
            term = jnp.where(rel == 0, own, jnp.where(rel == 2, r_ref[0], jnp.where(rel == 1, r_ref[1], r_ref[2])))
            acc = term if acc is None else acc + term
        out_ref[...] = acc

    out = pl.pallas_call(
        body, name=name, out_shape=jax.ShapeDtypeStruct((r, cols), F32),
        grid_spec=pltpu.PrefetchScalarGridSpec(
            num_scalar_prefetch=1, grid=(r // rb,),
            in_specs=[pl.BlockSpec((1, rb, cols), lambda i, s: (s[0], i, 0)),
                      pl.BlockSpec((3, rb, cols), lambda i, s: (0, i, 0))],
            out_specs=pl.BlockSpec((rb, cols), lambda i, s: (i, 0))),
        compiler_params=_cparams(1),
    )(jnp.reshape(me, (1,)).astype(jnp.int32), p3, r3)
    return out.reshape(part.shape[1:])


def gather_all(name, slab):
    def body(x_ref, out_ref, send_sems, recv_sems):
        x, y, c, chips = _place()
        me, sib = (x, y, c), (x, y, 1 - c)

        def rows(px, py, pc):
            return out_ref.at[4 * px + 2 * py + pc]

        first = [_rcopy(x_ref, rows(*me), send_sems, recv_sems, 0, sib)]
        first += [_rcopy(x_ref, rows(*me), send_sems, recv_sems, 1 + j, (*chip, c)) for j, chip in enumerate(chips)]
        for cp in first:
            cp.start()
        passed = []
        for j, chip in enumerate(chips):
            landed = rows(*chip, c)
            _rcopy(x_ref, landed, send_sems, recv_sems, 1 + j, sib).wait_recv()
            passed.append(_rcopy(landed, landed, send_sems, recv_sems, 4 + j, sib))
            passed[-1].start()
        _rcopy(x_ref, rows(*sib), send_sems, recv_sems, 0, sib).wait_recv()
        for j, chip in enumerate(chips):
            _rcopy(x_ref, rows(*chip, 1 - c), send_sems, recv_sems, 4 + j, sib).wait_recv()
        _finish(first + passed)

    return _comm_call(name, body, [jax.ShapeDtypeStruct((8,) + slab.shape, slab.dtype)], 7, [slab])[0]


D, BW, NH_B, NG, NP_S5, CG = 1024, 512, 4, 32, 64, 16
P_S5 = NG * NP_S5
C_GATES, C_XA, C_GA, C_Q, C_K, C_V, C_Z, C_UC, C_BA, N_PROJ = 0, 3072, 3584, 4096, 4608, 5120, 5632, 6144, 6656, 7168
_IN_OFFS = (0, 512, 1024, 1536, 2048, 2560, 3072, 3076, 3080, 3592, 6664)


def _blockdiag2(w):
    z = jnp.zeros((w.shape[0] // 2, 2 * w.shape[1], 2 * w.shape[2]), w.dtype)
    return z.at[:, :64, :64].set(w[0::2]).at[:, 64:, 64:].set(w[1::2])


def _unblockdiag2(d):
    return jnp.stack([d[:, :64, :64], d[:, 64:, 64:]], axis=1).reshape(-1, 64, 64)


def _layer_params(w, l):
    p = {}
    wi = w["w_in"][l]
    xa, ga, q, k, v, z, beta, alpha, uc, gates = [wi[:, a:b] for a, b in zip(_IN_OFFS[:-1], _IN_OFFS[1:])]
    ba = jnp.zeros((D, NH_B, LANES), wi.dtype).at[:, :, 0].set(beta).at[:, :, 1].set(alpha).reshape(D, NH_B * LANES)
    p["w_in_pad"] = jnp.concatenate([gates, xa, ga, q, k, v, z, uc, ba], axis=1)
    row = lambda a: a.reshape(1, -1).astype(F32)
    acw, bcw, fcw = w["a_conv_w"][l], w["b_conv_w"][l], w["ffn_conv_w"][l]
    p["a_taps"] = [row(acw[i]) for i in range(4)]
    p["a_cb"], p["a_br"], p["a_bi"], p["a_lam"] = (row(w[n][l]) for n in ("a_conv_b", "a_b_r", "a_b_i", "a_lam"))
    p["a_wr"], p["a_wi"] = _blockdiag2(w["a_w_r"][l]), _blockdiag2(w["a_w_i"][l])
    p["b_taps"] = [row(bcw[i, j * BW:(j + 1) * BW]) for j in range(3) for i in range(4)]
    p["b_alog"] = row(jnp.repeat(w["b_a_log"][l], LANES))
    p["b_dtb"] = row(jnp.repeat(w["b_dt_bias"][l], LANES))
    p["b_nw"] = row(w["b_norm"][l])
    p["c_lr"], p["c_li"] = w["c_lam_re"][l][:, None, :], w["c_lam_im"][l][:, None, :]
    p["c_ldt"] = w["c_log_dt"][l][:, None, None]
    p["c_btr"], p["c_bti"] = w["c_b_re"][l].transpose(0, 2, 1), w["c_b_im"][l].transpose(0, 2, 1)
    eye = jnp.eye(NG, dtype=F32)
    p["eye"] = eye
    expand = lambda t: (t[:, :, None, :] * eye[:, None, :, None]).reshape(t.shape[0] * t.shape[1], -1)
    p["expand"] = expand
    p["c_cre"] = expand(w["c_c_re"][l].transpose(0, 2, 1))
    p["c_cimn"] = expand(-w["c_c_im"][l].transpose(0, 2, 1))
    p["c_d"], p["c_glu_b"] = row(w["c_d"][l]), row(w["c_glu_b"][l])
    p["c_glu_w"] = w["c_glu_w"][l].astype(F32)
    p["wb"] = [w["w_branch"][l, k].astype(F32) for k in range(3)]
    p["bg"] = [row(w["b_gate"][l, k * D:(k + 1) * D]) for k in range(3)]
    for n in ("w_out", "xa_w_q", "xa_w_kv", "xa_w_o", "ffn_w_up", "ffn_w_down"):
        p[n] = w[n][l]
    for n in ("mix_norm", "xa_norm", "mem_norm", "ffn_norm"):
        p[n] = row(w[n][l])
    dff = fcw.shape[1] // 2
    p["f_taps"] = [row(fcw[i, :dff]) for i in range(3)] + [row(fcw[i, dff:]) for i in range(3)]
    p["f_b"] = [row(w["ffn_conv_b"][l, :dff]), row(w["ffn_conv_b"][l, dff:])]
    return p


def _chan(arr, cb=LANES):
    return (arr, (1, cb), lambda j: (0, j))


def _a_params(p):
    blk = lambda a: (a, (1, LANES, LANES), lambda j: (j, 0, 0))
    return [_chan(t) for t in p["a_taps"]] + [_chan(p["a_cb"]), blk(p["a_wr"]), blk(p["a_wi"]),
                                              _chan(p["a_br"]), _chan(p["a_bi"]), _chan(p["a_lam"])]


def _f_params(p):
    return [_chan(t) for t in p["f_taps"]] + [_chan(b) for b in p["f_b"]]


def _layer_fwd(x, mem, p, nb, s, m):
    sv = {"x": x}
    h = rowop_fwd("norm_mix", _norm_f, [(x, 0, D)], [p["mix_norm"]], [(D, BF16)])[0]
    proj = mm("in_proj", h, p["w_in_pad"], "nn")
    a, b = seqop_fwd("rglru_pre", _rglru_pre_f, [(proj, C_XA)], _a_params(p), [F32, F32], nb=nb, s=s, nblk=BW // LANES)
    ha = rscan("rglru_scan", a, b, False, nb=nb, s=s)
    yb, states = deltanet_fwd(proj, (C_Q, C_K, C_V, C_Z, C_BA), p["b_taps"], p["b_alog"], p["b_dtb"], p["b_nw"],
                              nb=nb, s=s, nh=NH_B)
    ar, ai, bbr, bbi = s5_param_fwd(p["c_lr"], p["c_li"], p["c_ldt"], p["c_btr"], p["c_bti"])
    bd = jnp.concatenate([p["expand"](bbr), p["expand"](bbi)], axis=1)
    ar2, ai2 = ar.reshape(1, P_S5), ai.reshape(1, P_S5)
    bu = mm("s5_bu", proj, bd, "nn", a_cols=(C_UC, BW))
    hr, hi = cscan("s5_scan", bu, ar2, ai2, False, nb=nb, s=s)
    ys = mm("s5_y_im", hi, p["c_cimn"], "nn", res=mm("s5_y_re", hr, p["c_cre"], "nn"))
    yc = rowop_fwd("s5_post", _s5_post_f, [(ys, 0, BW), (proj, C_UC, BW)], [p["c_d"], p["c_glu_w"], p["c_glu_b"]],
                   [(BW, F32)])[0]
    merge_rows = [(ha, 0, BW), (proj, C_GA, BW), (yb, 0, BW), (yc, 0, BW),
                  (proj, 0, D), (proj, D, D), (proj, 2 * D, D)]
    merged = rowop_fwd("merge", _merge_f, merge_rows, p["wb"] + p["bg"], [(D, BF16)])[0]
    x1 = mm("out_proj", merged, p["w_out"], "nn", res=x)
    hq = rowop_fwd("norm_xa", _norm_f, [(x1, 0, D)], [p["xa_norm"]], [(D, BF16)])[0]
    q = mm("xa_q", hq, p["xa_w_q"], "nn", out_dtype=BF16)
    memn = rowop_fwd("norm_mem", _norm_f, [(mem, 0, D)], [p["mem_norm"]], [(D, BF16)])[0]
    kv = mm("xa_kv", memn, p["xa_w_kv"], "nn")
    o = attn_fwd(q, kv, nb=nb, s=s, m=m, nh=4)
    x2 = mm("xa_o", o, p["xa_w_o"], "nn", res=x1)
    hf = rowop_fwd("norm_ffn", _norm_f, [(x2, 0, D)], [p["ffn_norm"]], [(D, BF16)])[0]
    up = mm("ffn_up", hf, p["ffn_w_up"], "nn")
    dff = up.shape[1] // 2
    act = seqop_fwd("ffn_act", _ffn_act_f, [(up, 0), (up, dff)], _f_params(p), [BF16], nb=nb, s=s,
                    nblk=dff // LANES)[0]
    x3 = mm("ffn_down", act, p["ffn_w_down"], "nn", res=x2)
    sv.update(h=h, proj=proj, a=a, ha=ha, yb=yb, states=states, ar=ar2, ai=ai2, bd=bd, hr=hr, hi=hi, ys=ys, yc=yc,
              merged=merged, x1=x1, hq=hq, q=q, memn=memn, kv=kv, o=o, x2=x2, hf=hf, up=up, act=act)
    return x3, sv


def _layer_bwd(dx3, mem, sv, p, nb, s, m):
    g = {}
    up, dff = sv["up"], sv["up"].shape[1] // 2
    dact = mm("d_ffn_down_x", dx3, p["ffn_w_down"], "nt")
    g["ffn_w_down"] = mm("d_ffn_down_w", sv["act"], dx3, "tn")
    (dug, duv), dfp = seqop_bwd("ffn_act_bwd", _ffn_act_f, [(up, 0), (up, dff)], _f_params(p), [(dact, 0)],
                                lambda t0, e: (e[SUBLANES:],), nb=nb, s=s, nblk=dff // LANES)
    dhf = mm("d_ffn_up_x1", duv, p["ffn_w_up"], "nt", b_cols=(dff, dff),
             res=mm("d_ffn_up_x0", dug, p["ffn_w_up"], "nt", b_cols=(0, dff)))
    g["ffn_w_up"] = jnp.concatenate([mm("d_ffn_up_w0", sv["hf"], dug, "tn"), mm("d_ffn_up_w1", sv["hf"], duv, "tn")],
                                    axis=1)
    g["ffn_conv_w"] = jnp.concatenate([jnp.concatenate(dfp[0:3], axis=0), jnp.concatenate(dfp[3:6], axis=0)], axis=1)
    g["ffn_conv_b"] = jnp.concatenate([dfp[6], dfp[7]], axis=1)[0]
    (dx2,), (gn,) = rowop_bwd("norm_ffn_bwd", _norm_f, [(sv["x2"], 0, D)], [p["ffn_norm"]], [(dhf, 0, D)], [True],
                              add=(dx3, 0, D))
    g["ffn_norm"] = gn[0]
    do = mm("d_xa_o_x", dx2, p["xa_w_o"], "nt")
    g["xa_w_o"] = mm("d_xa_o_w", sv["o"], dx2, "tn")
    dq, dk, dv = attn_bwd(sv["q"], sv["kv"], do, nb=nb, s=s, m=m, nh=4)
    dkv = jnp.concatenate([dk, dv], axis=1)
    dhq = mm("d_xa_q_x", dq, p["xa_w_q"], "nt")
    g["xa_w_q"] = mm("d_xa_q_w", sv["hq"], dq, "tn")
    dmemn = mm("d_xa_kv_x", dkv, p["xa_w_kv"], "nt")
    g["xa_w_kv"] = mm("d_xa_kv_w", sv["memn"], dkv, "tn")
    (dx1,), (gn,) = rowop_bwd("norm_xa_bwd", _norm_f, [(sv["x1"], 0, D)], [p["xa_norm"]], [(dhq, 0, D)], [True],
                              add=(dx2, 0, D))
    g["xa_norm"] = gn[0]
    _, (gn,) = rowop_bwd("norm_mem_bwd", _norm_f, [(mem, 0, D)], [p["mem_norm"]], [(dmemn, 0, D)], [False])
    g["mem_norm"] = gn[0]
    proj = sv["proj"]
    dmerged = mm("d_out_proj_x", dx1, p["w_out"], "nt")
    g["w_out"] = mm("d_out_proj_w", sv["merged"], dx1, "tn")
    merge_rows = [(sv["ha"], 0, BW), (proj, C_GA, BW), (sv["yb"], 0, BW), (sv["yc"], 0, BW),
                  (proj, 0, D), (proj, D, D), (proj, 2 * D, D)]
    (dha, dga, dyb, dyc, dg0, dg1, dg2), dmp = rowop_bwd("merge_bwd", _merge_f, merge_rows, p["wb"] + p["bg"],
                                                          [(dmerged, 0, D)], [True] * 7)
    g["w_branch"] = jnp.stack(dmp[0:3])
    g["b_gate"] = jnp.concatenate(dmp[3:6], axis=1)[0]
    (dys, duc), (gd, ggw, ggb) = rowop_bwd("s5_post_bwd", _s5_post_f, [(sv["ys"], 0, BW), (proj, C_UC, BW)],
                                           [p["c_d"], p["c_glu_w"], p["c_glu_b"]], [(dyc, 0, BW)], [True, True])
    g["c_d"], g["c_glu_w"], g["c_glu_b"] = gd[0], ggw, ggb[0]
    cd = jnp.concatenate([p["c_cre"], p["c_cimn"]], axis=0)
    dhs = mm("d_s5_y_x", dys, cd, "nt")
    dcre = mm("d_s5_y_wre", sv["hr"], dys, "tn")
    dcimn = mm("d_s5_y_wim", sv["hi"], dys, "tn")
    gr, gi = cscan("s5_scan_bwd", dhs, sv["ar"], sv["ai"], True, nb=nb, s=s)
    dar, dai = s5_dlam(gr, gi, sv["hr"], sv["hi"], nb=nb, s=s)
    duc = mm("d_s5_bu_x1", gi, sv["bd"], "nt", b_cols=(P_S5, P_S5),
             res=mm("d_s5_bu_x0", gr, sv["bd"], "nt", b_cols=(0, P_S5), res=duc))
    dbd_re = mm("d_s5_bu_wre", proj, gr, "tn", a_cols=(C_UC, BW))
    dbd_im = mm("d_s5_bu_wim", proj, gi, "tn", a_cols=(C_UC, BW))
    eye = p["eye"]
    pick = lambda dmat, r, c: jnp.einsum("grhc,gh->grc", dmat.reshape(NG, r, NG, c), eye)
    glr, gli, gldt, gbtr, gbti = s5_param_bwd(
        p["c_lr"], p["c_li"], p["c_ldt"], p["c_btr"], p["c_bti"],
        (dar.reshape(NG, 1, NP_S5), dai.reshape(NG, 1, NP_S5), pick(dbd_re, CG, NP_S5), pick(dbd_im, CG, NP_S5)))
    g["c_lam_re"], g["c_lam_im"], g["c_log_dt"] = glr[:, 0, :], gli[:, 0, :], gldt[:, 0, 0]
    g["c_b_re"], g["c_b_im"] = gbtr.transpose(0, 2, 1), gbti.transpose(0, 2, 1)
    g["c_c_re"] = pick(dcre, NP_S5, CG).transpose(0, 2, 1)
    g["c_c_im"] = -pick(dcimn, NP_S5, CG).transpose(0, 2, 1)
    (dq_b, dk_b, dv_b, dz_b, dba), dbp = deltanet_bwd(proj, (C_Q, C_K, C_V, C_Z, C_BA), p["b_taps"], p["b_alog"],
                                                      p["b_dtb"], p["b_nw"], sv["states"], dyb, nb=nb, s=s, nh=NH_B)
    g["b_conv_w"] = jnp.concatenate([jnp.concatenate(dbp[4 * j:4 * j + 4], axis=0) for j in range(3)], axis=1)
    g["b_a_log"] = dbp[12].reshape(NH_B, LANES)[:, 0]
    g["b_dt_bias"] = dbp[13].reshape(NH_B, LANES)[:, 0]
    g["b_norm"] = dbp[14][0]
    gsc = rscan("rglru_scan_bwd", sv["a"], dha, True, nb=nb, s=s)
    (dxa,), dap = seqop_bwd("rglru_pre_bwd", _rglru_pre_f, [(proj, C_XA)], _a_params(p), [(gsc, 0), (sv["ha"], 0)],
                            lambda t0, ge, he: (ge[SUBLANES:] * taps(he, 2)[1], ge[SUBLANES:]),
                            nb=nb, s=s, nblk=BW // LANES)
    g["a_conv_w"] = jnp.concatenate(dap[0:4], axis=0)
    g["a_conv_b"], g["a_b_r"], g["a_b_i"], g["a_lam"] = dap[4][0], dap[7][0], dap[8][0], dap[9][0]
    g["a_w_r"], g["a_w_i"] = _unblockdiag2(dap[5]), _unblockdiag2(dap[6])
    dproj = jnp.concatenate([dg0, dg1, dg2, dxa, dga, dq_b, dk_b, dv_b, dz_b, duc, dba], axis=1)
    dh = mm("d_in_proj_x", dproj, p["w_in_pad"], "nt")
    dwp = mm("d_in_proj_w", sv["h"], dproj, "tn")
    dba_w = dwp[:, C_BA:].reshape(D, NH_B, LANES)
    pieces = [dwp[:, c0:c0 + BW] for c0 in (C_XA, C_GA, C_Q, C_K, C_V, C_Z)]
    g["w_in"] = jnp.concatenate(pieces + [dba_w[:, :, 0], dba_w[:, :, 1], dwp[:, C_UC:C_UC + BW], dwp[:, :3 * D]],
                                axis=1)
    (dx,), (gn,) = rowop_bwd("norm_mix_bwd", _norm_f, [(sv["x"], 0, D)], [p["mix_norm"]], [(dh, 0, D)], [True],
                             add=(dx1, 0, D))
    g["mix_norm"] = gn[0]
    return dx, g


def _local_step(x, mem, target, w):
    nb, s, _ = x.shape
    m = mem.shape[1]
    depth = w["mix_norm"].shape[0]
    x2d, mem2d = x.reshape(nb * s, D), mem.reshape(nb * m, D)
    params, saved = [], []
    for l in range(depth):
        params.append(_layer_params(w, l))
        x2d, sv = _layer_fwd(x2d, mem2d, params[l], nb, s, m)
        saved.append(sv)
    loss_row, dx, gfn = loss_head(x2d, target.reshape(nb * s, D), w["final_norm"].reshape(1, D))
    grads = [None] * depth
    for l in reversed(range(depth)):
        dx, grads[l] = _layer_bwd(dx, mem2d, saved[l], params[l], nb, s, m)
    out = {n: jnp.stack([grads[l][n] for l in range(depth)]) for n in grads[0]}
    out["final_norm"] = gfn[0]
    return loss_row, dx.reshape(x.shape), out


SLAB_UNIT = 2 * 1024 * LANES


def _slab(parts, dtype, unit):
    flat = jnp.concatenate([a.astype(dtype).reshape(-1) for a in parts])
    pad = (-flat.shape[0]) % unit
    return jnp.pad(flat, (0, pad)) if pad else flat


def _unslab(flat, like):
    out, off = [], 0
    for a in like:
        out.append(flat[off:off + a.size].reshape(a.shape))
        off += a.size
    return out


def _gather_weights(name, local, names_axes, dtype):
    mine = [local[n].astype(dtype) for n, _ in names_axes]
    me = _chip_id(lax.axis_index("x"), lax.axis_index("y"))
    got = [_set_slot(g, me, a) for g, a in zip(gather_chips(name, mine), mine)]
    return {n: jnp.concatenate([g[j] for j in range(N_CHIPS)], axis=ax) for (n, ax), g in zip(names_axes, got)}


def kernel(x, mem, mix_norm, w_in, b_gate, a_conv_w, a_conv_b, a_w_r, a_b_r, a_w_i, a_b_i, a_lam, b_conv_w, b_a_log, b_dt_bias, b_norm, c_lam_re, c_lam_im, c_log_dt, c_b_re, c_b_im, c_c_re, c_c_im, c_d, c_glu_w, c_glu_b, w_branch, w_out, xa_norm, mem_norm, xa_w_q, xa_w_kv, xa_w_o, ffn_norm, ffn_w_up, ffn_conv_w, ffn_conv_b, ffn_w_down, final_norm, loss_target, m_mix_norm, m_w_in, m_b_gate, m_a_conv_w, m_a_conv_b, m_a_w_r, m_a_b_r, m_a_w_i, m_a_b_i, m_a_lam, m_b_conv_w, m_b_a_log, m_b_dt_bias, m_b_norm, m_c_lam_re, m_c_lam_im, m_c_log_dt, m_c_b_re, m_c_b_im, m_c_c_re, m_c_c_im, m_c_d, m_c_glu_w, m_c_glu_b, m_w_branch, m_w_out, m_xa_norm, m_mem_norm, m_xa_w_q, m_xa_w_kv, m_xa_w_o, m_ffn_norm, m_ffn_w_up, m_ffn_conv_w, m_ffn_conv_b, m_ffn_w_down, m_final_norm, v_mix_norm, v_w_in, v_b_gate, v_a_conv_w, v_a_conv_b, v_a_w_r, v_a_b_r, v_a_w_i, v_a_b_i, v_a_lam, v_b_conv_w, v_b_a_log, v_b_dt_bias, v_b_norm, v_c_lam_re, v_c_lam_im, v_c_log_dt, v_c_b_re, v_c_b_im, v_c_c_re, v_c_c_im, v_c_d, v_c_glu_w, v_c_glu_b, v_w_branch, v_w_out, v_xa_norm, v_mem_norm, v_xa_w_q, v_xa_w_kv, v_xa_w_o, v_ffn_norm, v_ffn_w_up, v_ffn_conv_w, v_ffn_conv_b, v_ffn_w_down, v_final_norm):
    env = dict(locals())
    w = {n: env[n] for n in WEIGHTS}
    mom = {n: env["m_" + n] for n in WEIGHTS}
    var = {n: env["v_" + n] for n in WEIGHTS}

    full = dict(w)
    full.update(_gather_weights("gather_mats", w, SHARDED_MATS, BF16))
    full.update(_gather_weights("gather_convs", w, SHARDED_CONVS, F32))
    loss_row, grad_x, gfull = _local_step(x, mem, loss_target, full)

    depth = mix_norm.shape[0]
    assert depth % 2 == 0
    hl = depth // 2
    res = {}
    by_chip = []
    for n, ax in SHARDED:
        width = w[n].shape[ax]
        by_chip.append(jnp.stack([jnp.stack([
            lax.slice_in_dim(gfull[n][hf * hl:(hf + 1) * hl], j * width, (j + 1) * width, axis=ax)
            for j in range(N_CHIPS)]) for hf in range(2)]))
    core = lax.axis_index("c")
    chip = _chip_id(lax.axis_index("x"), lax.axis_index("y"))
    sibs = swap_halves("swap_halves", by_chip)
    part = [add_pair("add_cores_" + n, g2, sb, core) for (n, _), g2, sb in zip(SHARDED, by_chip, sibs)]
    recv = scatter_chips("scatter_chips", part)
    mine = [add_chips("add_chips_" + n, p, r, chip) for (n, _), p, r in zip(SHARDED, part, recv)]
    for (n, _), own, joined in zip(SHARDED, mine, join_halves("join_halves", mine)):
        g = _set_slot(joined, core * hl, own)
        res["grad_" + n] = g
        res["delta_" + n], res["new_m_" + n], res["new_v_" + n] = adamw("adamw_" + n, w[n], g, mom[n], var[n])

    unit = SUBLANES * LANES
    small = _slab([gfull[n] for n in REPLICATED] + [loss_row[:, :1]], F32, unit).reshape(-1, LANES)
    g_rp = sumk("add_devices", _set_slot(gather_all("gather_all", small), 2 * chip + core, small))
    slab1 = lambda d: _slab([d[n] for n in REPLICATED] + [jnp.zeros((1, 1), F32)], F32, unit).reshape(-1, LANES)
    d_rp, m_rp, v_rp = adamw("adamw_replicated", slab1(w), g_rp, slab1(mom), slab1(var))
    like_rp = [w[n] for n in REPLICATED] + [jnp.zeros((1, 1), F32)]
    for kind, rp in (("grad", g_rp), ("delta", d_rp), ("new_m", m_rp), ("new_v", v_rp)):
        for n, a in zip(list(REPLICATED) + ["loss"], _unslab(rp.reshape(-1), like_rp)):
            res[kind + "_" + n] = a
    loss = res["grad_loss"].reshape(())
    return (loss, grad_x, *[res["grad_" + n] for n in WEIGHTS], *[res["delta_" + n] for n in WEIGHTS],
            *[res["new_m_" + n] for n in WEIGHTS], *[res["new_v_" + n] for n in WEIGHTS])
```

```python
import functools
import math

import jax
import jax.numpy as jnp
from jax import lax
from jax.experimental import pallas as pl
from jax.experimental.pallas import tpu as pltpu

F32, BF16 = jnp.float32, jnp.bfloat16
MESH = pl.DeviceIdType.MESH
ANY = pl.BlockSpec(memory_space=pl.ANY)

VMEM_LIMIT_V7X = 56 * 1024 * 1024
MM_TILE = 1024
LANES, SUBLANES = 128, 8
EPS = 1e-6
RG_C = 8.0
DN_CHUNK = 64
N_CHIPS = 4
ADAM_LR, ADAM_B1, ADAM_B2, ADAM_EPS, ADAM_WD, ADAM_STEP = 0.001, 0.9, 0.999, 1e-08, 0.01, 10

SHARDED_MATS = (("w_in", 2), ("c_glu_w", 1), ("w_branch", 3), ("w_out", 1), ("xa_w_q", 1), ("xa_w_kv", 2),
                ("xa_w_o", 1), ("ffn_w_up", 2), ("ffn_w_down", 1))
SHARDED_CONVS = (("a_conv_w", 2), ("b_conv_w", 2), ("ffn_conv_w", 2))
SHARDED = SHARDED_MATS + SHARDED_CONVS
WEIGHTS = ("mix_norm", "w_in", "b_gate", "a_conv_w", "a_conv_b", "a_w_r", "a_b_r", "a_w_i", "a_b_i", "a_lam",
           "b_conv_w", "b_a_log", "b_dt_bias", "b_norm", "c_lam_re", "c_lam_im", "c_log_dt", "c_b_re", "c_b_im",
           "c_c_re", "c_c_im", "c_d", "c_glu_w", "c_glu_b", "w_branch", "w_out", "xa_norm", "mem_norm", "xa_w_q",
           "xa_w_kv", "xa_w_o", "ffn_norm", "ffn_w_up", "ffn_conv_w", "ffn_conv_b", "ffn_w_down", "final_norm")
REPLICATED = tuple(n for n in WEIGHTS if n not in dict(SHARDED))


def _cparams(n_axes):
    return pltpu.CompilerParams(dimension_semantics=("arbitrary",) * n_axes, vmem_limit_bytes=VMEM_LIMIT_V7X)


def _tile(n, pref, off=0):
    t = pref
    while t >= LANES:
        if n % t == 0 and off % t == 0:
            return t
        t //= 2
    assert off == 0, (n, pref, off)
    return n


def _full_spec(shape, n_grid):
    nd = len(shape)
    if n_grid == 1:
        return pl.BlockSpec(shape, lambda i: (0,) * nd)
    return pl.BlockSpec(shape, lambda i, j: (0,) * nd)


_NN, _NT, _TN = ((1,), (0,)), ((1,), (1,)), ((0,), (0,))


def _dot(a, b, dims, hp):
    if hp:
        return lax.dot_general(a, b, (dims, ((), ())), precision=lax.Precision.HIGH, preferred_element_type=F32)
    return lax.dot_general(a.astype(BF16), b.astype(BF16), (dims, ((), ())), preferred_element_type=F32)


@functools.partial(jax.custom_vjp, nondiff_argnums=(2,))
def mm_nn(a, b, hp=False):
    return _dot(a, b, _NN, hp)


@functools.partial(jax.custom_vjp, nondiff_argnums=(2,))
def mm_nt(a, b, hp=False):
    return _dot(a, b, _NT, hp)


@functools.partial(jax.custom_vjp, nondiff_argnums=(2,))
def mm_tn(a, b, hp=False):
    return _dot(a, b, _TN, hp)


mm_nn.defvjp(lambda a, b, hp: (_dot(a, b, _NN, hp), (a, b)),
             lambda hp, r, g: (mm_nt(g, r[1], hp), mm_tn(r[0], g, hp)))
mm_nt.defvjp(lambda a, b, hp: (_dot(a, b, _NT, hp), (a, b)),
             lambda hp, r, g: (mm_nn(g, r[1], hp), mm_tn(g, r[0], hp)))
mm_tn.defvjp(lambda a, b, hp: (_dot(a, b, _TN, hp), (a, b)),
             lambda hp, r, g: (mm_nt(r[1], g, hp), mm_nn(r[0], g, hp)))


@functools.partial(jax.custom_vjp, nondiff_argnums=(1,))
def taps(xext, k):
    return tuple((xext if s == 0 else pltpu.roll(xext, s, 0))[SUBLANES:] for s in range(k))


def _taps_fwd(xext, k):
    return taps(xext, k), None


def _taps_bwd(k, _, gs):
    tot = None
    for s, g in enumerate(gs):
        gp = jnp.concatenate([jnp.zeros((SUBLANES, g.shape[1]), g.dtype), g], axis=0)
        if s:
            gp = pltpu.roll(gp, gp.shape[0] - s, 0)
        tot = gp if tot is None else tot + gp
    return (tot,)


taps.defvjp(_taps_fwd, _taps_bwd)


def _gelu(x):
    return x * (0.5 * (1.0 + jnp.tanh(0.7978845608028654 * (x + 0.044715 * (x * x * x)))))


def _sigmoid(x):
    return jax.nn.sigmoid(x)


def _softplus(x):
    return jnp.maximum(x, 0.0) + jnp.log1p(jnp.exp(-jnp.abs(x)))


def _rmsnorm(x, g):
    return x * lax.rsqrt(jnp.mean(x * x, axis=-1, keepdims=True) + EPS) * g


def _lane_pick(x, lane):
    sel = lax.broadcasted_iota(jnp.int32, x.shape, 1) == lane
    return jnp.sum(jnp.where(sel, x, 0.0), axis=1, keepdims=True)


def _load_ext(ref, t0, n):
    main = ref[pl.ds(t0, n), :].astype(F32)
    hstart = pl.multiple_of(jnp.maximum(t0 - SUBLANES, 0), SUBLANES)
    halo = ref[pl.ds(hstart, SUBLANES), :].astype(F32)
    halo = jnp.where(t0 > 0, halo, 0.0)
    return jnp.concatenate([halo, main], axis=0)


def mm(name, a, b, mode, *, out_dtype=F32, res=None, a_cols=None, b_cols=None):
    a0, aw = a_cols if a_cols else (0, a.shape[1])
    b0, bw = b_cols if b_cols else (0, b.shape[1])
    big = MM_TILE
    if mode == "nn":
        m, k, n = a.shape[0], aw, bw
        assert b.shape[0] == k
        tm, tk, tn = _tile(m, big), _tile(k, big, a0), _tile(n, big, b0)
        a_spec = pl.BlockSpec((tm, tk), lambda i, j, kk: (i, kk + a0 // tk))
        b_spec = pl.BlockSpec((tk, tn), lambda i, j, kk: (kk, j + b0 // tn))
        dims = _NN
    elif mode == "nt":
        m, k, n = a.shape[0], aw, b.shape[0]
        assert bw == k
        tm, tn = _tile(m, big), _tile(n, big)
        tk = _tile(k, big, math.gcd(a0, b0) if (a0 or b0) else 0)
        assert a0 % tk == 0 and b0 % tk == 0
        a_spec = pl.BlockSpec((tm, tk), lambda i, j, kk: (i, kk + a0 // tk))
        b_spec = pl.BlockSpec((tn, tk), lambda i, j, kk: (j, kk + b0 // tk))
        dims = _NT
    else:
        k, m, n = a.shape[0], aw, bw
        assert b.shape[0] == k
        tk, tm, tn = _tile(k, big), _tile(m, big, a0), _tile(n, big, b0)
        a_spec = pl.BlockSpec((tk, tm), lambda i, j, kk: (kk, i + a0 // tm))
        b_spec = pl.BlockSpec((tk, tn), lambda i, j, kk: (kk, j + b0 // tn))
        dims = _TN
    nk = k // tk
    has_res = res is not None

    def tile_dot(a_ref, b_ref):
        return lax.dot_general(a_ref[...].astype(BF16), b_ref[...].astype(BF16), (dims, ((), ())),
                               preferred_element_type=F32)

    def finish(out, refs, o_ref):
        if has_res:
            out = out + refs[2][...].astype(F32)
        o_ref[...] = out.astype(o_ref.dtype)

    def body_single(*refs):
        finish(tile_dot(refs[0], refs[1]), refs, refs[-1])

    def body_acc(*refs):
        o_ref, acc_ref = refs[-2], refs[-1]
        kk = pl.program_id(2)

        @pl.when(kk == 0)
        def _():
            acc_ref[...] = tile_dot(refs[0], refs[1])

        @pl.when((kk > 0) & (kk < nk - 1))
        def _():
            acc_ref[...] += tile_dot(refs[0], refs[1])

        @pl.when(kk == nk - 1)
        def _():
            finish(acc_ref[...] + tile_dot(refs[0], refs[1]), refs, o_ref)

    o_spec = pl.BlockSpec((tm, tn), lambda i, j, kk: (i, j))
    return pl.pallas_call(
        body_single if nk == 1 else body_acc, name=name, grid=(m // tm, n // tn, nk),
        in_specs=[a_spec, b_spec] + ([o_spec] if has_res else []), out_specs=o_spec,
        out_shape=jax.ShapeDtypeStruct((m, n), out_dtype),
        scratch_shapes=[] if nk == 1 else [pltpu.VMEM((tm, tn), F32)], compiler_params=_cparams(3),
    )(*((a, b, res) if has_res else (a, b)))


def _row_specs(rows, tm):
    return [pl.BlockSpec((tm, cw), lambda i, c=c0 // cw: (i, c)) for (_, c0, cw) in rows]


def rowop_fwd(name, f, rows, params, outs, tm=256):
    t = rows[0][0].shape[0]
    tm = min(tm, t)
    n_r, n_p = len(rows), len(params)

    def body(*refs):
        vals = f(*[r[...].astype(F32) for r in refs[:n_r]], *[p[...] for p in refs[n_r:n_r + n_p]])
        for o, v in zip(refs[n_r + n_p:], vals):
            o[...] = v.astype(o.dtype)

    res = pl.pallas_call(
        body, name=name, grid=(t // tm,),
        in_specs=_row_specs(rows, tm) + [_full_spec(p.shape, 1) for p in params],
        out_specs=[pl.BlockSpec((tm, cw), lambda i: (i, 0)) for cw, _ in outs],
        out_shape=[jax.ShapeDtypeStruct((t, cw), dt) for cw, dt in outs],
        compiler_params=_cparams(1),
    )(*[r[0] for r in rows], *params)
    return list(res)


def rowop_bwd(name, f, rows, params, couts, need, tm=256, add=None):
    t = rows[0][0].shape[0]
    tm = min(tm, t)
    n_r, n_p, n_c = len(rows), len(params), len(couts)
    want = [k for k in range(n_r) if need[k]]
    adds = [add] if add is not None else []

    def body(*refs):
        xs = [r[...].astype(F32) for r in refs[:n_r]]
        ps = [p[...] for p in refs[n_r:n_r + n_p]]
        cs = tuple(c[...].astype(F32) for c in refs[n_r + n_p:n_r + n_p + n_c])
        outs = refs[n_r + n_p + n_c + len(adds):]
        _, vjp = jax.vjp(f, *xs, *ps)
        gs = vjp(cs)
        for o, k in zip(outs[:len(want)], want):
            o[...] = gs[k] + refs[n_r + n_p + n_c][...] if (adds and k == want[0]) else gs[k]

        @pl.when(pl.program_id(0) == 0)
        def _():
            for o in outs[len(want):]:
                o[...] = jnp.zeros_like(o)

        for o, g in zip(outs[len(want):], gs[n_r:]):
            o[...] += g

    res = pl.pallas_call(
        body, name=name, grid=(t // tm,),
        in_specs=_row_specs(rows, tm) + [_full_spec(p.shape, 1) for p in params] + _row_specs(couts, tm)
        + _row_specs(adds, tm),
        out_specs=[pl.BlockSpec((tm, rows[k][2]), lambda i: (i, 0)) for k in want]
        + [_full_spec(p.shape, 1) for p in params],
        out_shape=[jax.ShapeDtypeStruct((t, rows[k][2]), F32) for k in want]
        + [jax.ShapeDtypeStruct(p.shape, F32) for p in params],
        compiler_params=_cparams(1),
    )(*[r[0] for r in rows], *params, *[c[0] for c in couts], *[a[0] for a in adds])
    return list(res[:len(want)]), list(res[len(want):])


def _seq_specs(seqs, s, cb, order):
    if order == "bj":
        return [pl.BlockSpec((s, cb), lambda b, j, c=c0 // cb: (b, c + j)) for (_, c0) in seqs]
    return [pl.BlockSpec((s, cb), lambda j, b, c=c0 // cb: (b, c + j)) for (_, c0) in seqs]


def _param_specs(params, order):
    if order == "bj":
        return [pl.BlockSpec(bs, lambda b, j, fn=fn: fn(j)) for (_, bs, fn) in params]
    return [pl.BlockSpec(bs, lambda j, b, fn=fn: fn(j)) for (_, bs, fn) in params]


def seqop_fwd(name, f, seqs, params, out_dtypes, *, nb, s, nblk, cb=LANES, n=256):
    n = min(n, s)
    n_s, n_p = len(seqs), len(params)

    def body(*refs):
        seq_refs, par_refs, out_refs = refs[:n_s], refs[n_s:n_s + n_p], refs[n_s + n_p:]

        def step(i, carry):
            t0 = pl.multiple_of(i * n, n)
            vals = f(t0, *[_load_ext(r, t0, n) for r in seq_refs], *[p[...] for p in par_refs])
            for o, v in zip(out_refs, vals):
                o[pl.ds(t0, n), :] = v.astype(o.dtype)
            return carry

        lax.fori_loop(0, s // n, step, 0)

    res = pl.pallas_call(
        body, name=name, grid=(nb, nblk),
        in_specs=_seq_specs(seqs, s, cb, "bj") + _param_specs(params, "bj"),
        out_specs=[pl.BlockSpec((s, cb), lambda b, j: (b, j)) for _ in out_dtypes],
        out_shape=[jax.ShapeDtypeStruct((nb * s, nblk * cb), dt) for dt in out_dtypes],
        compiler_params=_cparams(2),
    )(*[q[0] for q in seqs], *[p[0] for p in params])
    return list(res)


def seqop_bwd(name, f, seqs, params, cot_seqs, cot_fn, *, nb, s, nblk, cb=LANES, n=256):
    n = min(n, s)
    n_s, n_p, n_c = len(seqs), len(params), len(cot_seqs)
    nchunk = s // n

    def body(*refs):
        seq_refs, par_refs = refs[:n_s], refs[n_s:n_s + n_p]
        cot_refs = refs[n_s + n_p:n_s + n_p + n_c]
        dseq_refs = refs[n_s + n_p + n_c:n_s + n_p + n_c + n_s]
        dpar_refs = refs[n_s + n_p + n_c + n_s:]

        @pl.when(pl.program_id(1) == 0)
        def _():
            for o in dpar_refs:
                o[...] = jnp.zeros_like(o)

        def step(ii, halos):
            t0 = pl.multiple_of((nchunk - 1 - ii) * n, n)
            exts = [_load_ext(r, t0, n) for r in seq_refs]
            ps = [p[...] for p in par_refs]
            cots = cot_fn(t0, *[_load_ext(r, t0, n) for r in cot_refs])
            _, vjp = jax.vjp(lambda *args: f(t0, *args), *exts, *ps)
            gs = vjp(tuple(cots))
            tail = pl.multiple_of(t0 + n - SUBLANES, SUBLANES)
            new_halos = []
            for o, g, h in zip(dseq_refs, gs[:n_s], halos):
                o[pl.ds(t0, n), :] = g[SUBLANES:]
                o[pl.ds(tail, SUBLANES), :] += h
                new_halos.append(g[:SUBLANES])
            for o, g in zip(dpar_refs, gs[n_s:]):
                o[...] += g
            return tuple(new_halos)

        lax.fori_loop(0, nchunk, step, tuple(jnp.zeros((SUBLANES, cb), F32) for _ in range(n_s)))

    res = pl.pallas_call(
        body, name=name, grid=(nblk, nb),
        in_specs=_seq_specs(seqs, s, cb, "jb") + _param_specs(params, "jb") + _seq_specs(cot_seqs, s, cb, "jb"),
        out_specs=[pl.BlockSpec((s, cb), lambda j, b: (b, j)) for _ in seqs] + _param_specs(params, "jb"),
        out_shape=[jax.ShapeDtypeStruct((nb * s, nblk * cb), F32) for _ in seqs]
        + [jax.ShapeDtypeStruct(p[0].shape, F32) for p in params],
        compiler_params=_cparams(2),
    )(*[q[0] for q in seqs], *[p[0] for p in params], *[q[0] for q in cot_seqs])
    return list(res[:n_s]), list(res[n_s:])


def rscan(name, a, b, reverse, *, nb, s, cb=256):
    c = a.shape[1]
    cb = min(cb, c)

    def body(a_ref, b_ref, h_ref):
        def step(ii, car):
            i = (s // SUBLANES - 1 - ii) if reverse else ii
            base = pl.multiple_of(i * SUBLANES, SUBLANES)
            for r in (reversed(range(SUBLANES)) if reverse else range(SUBLANES)):
                row = pl.ds(base + r, 1)
                if reverse:
                    g = b_ref[row, :] + car
                    h_ref[row, :] = g
                    car = a_ref[row, :] * g
                else:
                    car = a_ref[row, :] * car + b_ref[row, :]
                    h_ref[row, :] = car
            return car

        lax.fori_loop(0, s // SUBLANES, step, jnp.zeros((1, cb), F32))

    spec = pl.BlockSpec((s, cb), lambda bb, j: (bb, j))
    return pl.pallas_call(
        body, name=name, grid=(nb, c // cb), in_specs=[spec, spec], out_specs=spec,
        out_shape=jax.ShapeDtypeStruct(a.shape, F32), compiler_params=_cparams(2),
    )(a, b)


def cscan(name, bu, ar, ai, reverse, *, nb, s, cb=256):
    p = ar.shape[1]
    cb = min(cb, p)
    nj = p // cb

    def body(br_ref, bi_ref, ar_ref, ai_ref, hr_ref, hi_ref):
        lr = ar_ref[...]
        li = -ai_ref[...] if reverse else ai_ref[...]

        def step(ii, car):
            hr, hi = car
            i = (s // SUBLANES - 1 - ii) if reverse else ii
            base = pl.multiple_of(i * SUBLANES, SUBLANES)
            for r in (reversed(range(SUBLANES)) if reverse else range(SUBLANES)):
                row = pl.ds(base + r, 1)
                nr = lr * hr - li * hi + br_ref[row, :]
                ni = lr * hi + li * hr + bi_ref[row, :]
                hr, hi = nr, ni
                hr_ref[row, :] = hr
                hi_ref[row, :] = hi
            return hr, hi

        z = jnp.zeros((1, cb), F32)
        lax.fori_loop(0, s // SUBLANES, step, (z, z))

    o_spec = pl.BlockSpec((s, cb), lambda bb, j: (bb, j))
    l_spec = pl.BlockSpec((1, cb), lambda bb, j: (0, j))
    res = pl.pallas_call(
        body, name=name, grid=(nb, nj),
        in_specs=[o_spec, pl.BlockSpec((s, cb), lambda bb, j: (bb, j + nj)), l_spec, l_spec],
        out_specs=[o_spec, o_spec],
        out_shape=[jax.ShapeDtypeStruct((nb * s, p), F32)] * 2, compiler_params=_cparams(2),
    )(bu, bu, ar, ai)
    return res[0], res[1]


def _norm_f(x, g):
    return (_rmsnorm(x, g),)


def _rglru_pre_f(t0, xext, w0, w1, w2, w3, cb, wr, wi, br, bi, lam):
    x0, x1, x2, x3 = taps(xext, 4)
    xc = w3 * x0 + w2 * x1 + w1 * x2 + w0 * x3 + cb
    r = _sigmoid(mm_nn(xc, wr[0]) + br)
    ig = _sigmoid(mm_nn(xc, wi[0]) + bi)
    log_a = -RG_C * r * _softplus(-lam)
    a = jnp.exp(log_a)
    first = (lax.broadcasted_iota(jnp.int32, a.shape, 0) + t0) == 0
    mult = jnp.where(first, 1.0, jnp.sqrt(1.0 - jnp.exp(2.0 * log_a)))
    return a, mult * ig * xc


def _ffn_act_f(t0, gext, vext, g0, g1, g2, v0, v1, v2, bg, bv):
    ga, gb, gc = taps(gext, 3)
    va, vb, vc = taps(vext, 3)
    ug = g2 * ga + g1 * gb + g0 * gc + bg
    uv = v2 * va + v1 * vb + v0 * vc + bv
    return (_gelu(ug) * uv,)


def _s5_post_f(ys, uc, d, gw, gb):
    yc = _gelu(ys + d * uc)
    return (yc * _sigmoid(mm_nn(yc, gw) + gb),)


def _merge_f(ha, ga, yb, yc, g0, g1, g2, wb0, wb1, wb2, bg0, bg1, bg2):
    ya = ha * _gelu(ga)
    out = _sigmoid(g0 + bg0) * mm_nn(ya, wb0)
    out = out + _sigmoid(g1 + bg1) * mm_nn(yb, wb1)
    out = out + _sigmoid(g2 + bg2) * mm_nn(yc, wb2)
    return (out,)


def _s5_param_f(lr, li, ldt, btr, bti):
    dt = jnp.exp(ldt)
    mag = jnp.exp(lr * dt)
    ar, ai = mag * jnp.cos(li * dt), mag * jnp.sin(li * dt)
    den = lr * lr + li * li
    fr = ((ar - 1.0) * lr + ai * li) / den
    fi = (ai * lr - (ar - 1.0) * li) / den
    return ar, ai, fr * btr - fi * bti, fr * bti + fi * btr


def _each(fn, *lists):
    return [fn(*xs) for xs in zip(*lists)]


def _tri_inv(mats):
    c = mats[0].shape[0]
    ri = lax.broadcasted_iota(jnp.int32, (c, c), 0)
    ci = lax.broadcasted_iota(jnp.int32, (c, c), 1)
    eye = jnp.where(ri == ci, 1.0, 0.0).astype(F32)
    hp = lambda a, b: mm_nn(a, b, True)
    ad = [jnp.where((ri >> 4) == (ci >> 4), a, 0.0) for a in mats]
    ao = _each(lambda a, d: a - d, mats, ad)
    a2 = _each(hp, ad, ad)
    a4 = _each(hp, a2, a2)
    a8 = _each(hp, a4, a4)
    lo = _each(lambda d, s: hp(eye - d, eye + s), ad, a2)
    hi = _each(lambda s, e: hp(eye + s, eye + e), a4, a8)
    td = _each(hp, lo, hi)
    nn_ = _each(hp, td, ao)
    n2 = _each(hp, nn_, nn_)
    return _each(hp, _each(lambda n, s: hp(eye - n, eye + s), nn_, n2), td)


def _dn_chunk_f(nh, *args):
    per = [args[20 * h:20 * (h + 1)] for h in range(nh)]
    nw = args[20 * nh]
    state, qe, ke, ve, z, ba = ([p[j] for p in per] for j in range(6))
    c = DN_CHUNK
    dk = qe[0].shape[1]

    def conv_silu(xes, first):
        tp = [taps(x, 4) for x in xes]
        u = [p[first + 3] * t[0] + p[first + 2] * t[1] + p[first + 1] * t[2] + p[first] * t[3] for p, t in zip(per, tp)]
        return [x * _sigmoid(x) for x in u]

    def l2n(xs):
        return [x * lax.rsqrt(jnp.sum(x * x, axis=-1, keepdims=True) + EPS) for x in xs]

    q = [x * (dk ** -0.5) for x in l2n(conv_silu(qe, 6))]
    k = l2n(conv_silu(ke, 10))
    v = conv_silu(ve, 14)
    beta = [_sigmoid(_lane_pick(b, 0)) for b in ba]
    g = [-jnp.exp(_lane_pick(p[18], 0)) * _softplus(_lane_pick(b, 1) + _lane_pick(p[19], 0)) for p, b in zip(per, ba)]
    ri = lax.broadcasted_iota(jnp.int32, (c, c), 0)
    ci = lax.broadcasted_iota(jnp.int32, (c, c), 1)
    incl, strict = ri >= ci, ri > ci
    ltri = jnp.where(incl, 1.0, 0.0).astype(F32)
    mean_row = jnp.full((c, dk), 1.0 / dk, F32)
    gc_w = [mm_nn(ltri, jnp.broadcast_to(x, (c, dk)), True) for x in g]
    gc_c = [mm_nn(ltri, jnp.broadcast_to(x, (c, c)), True) for x in g]
    gc_r = [mm_nt(mean_row, x, True) for x in gc_w]
    gtot = [jnp.sum(x, axis=0, keepdims=True) for x in g]
    decay = _each(lambda a, b: jnp.exp(jnp.where(incl, a - b, -1e30)), gc_c, gc_r)
    e_gc = [jnp.exp(x) for x in gc_w]
    kb = _each(lambda a, b: a * b, k, beta)
    a_mat = _each(lambda a, b, d: jnp.where(strict, mm_nt(a, b) * d, 0.0), kb, k, decay)
    t_inv = _tri_inv(a_mat)
    u = _each(lambda t, a, b: mm_nn(t, a * b, True), t_inv, v, beta)
    w = _each(lambda t, a, e: mm_nn(t, a * e, True), t_inv, kb, e_gc)
    qk = _each(lambda a, b, d: jnp.where(incl, mm_nt(a, b) * d, 0.0), q, k, decay)
    v_new = _each(lambda a, b, s: a - mm_nn(b, s), u, w, state)
    o = _each(lambda a, e, s, b, n: mm_nn(a * e, s) + mm_nn(b, n), q, e_gc, state, qk, v_new)
    new_state = _each(lambda s, t, a, gw, n: s * jnp.exp(t) + mm_tn(a * jnp.exp(t - gw), n), state, gtot, k, gc_w, v_new)
    o = _each(lambda x, zz: _rmsnorm(x, nw) * (zz * _sigmoid(zz)), o, z)
    return tuple(o) + tuple(new_state)


def _attn_f(q, k, v):
    sc = mm_nt(q, k) * (q.shape[1] ** -0.5)
    sc = sc - jnp.max(sc, axis=-1, keepdims=True)
    e = jnp.exp(sc)
    p = e / jnp.sum(e, axis=-1, keepdims=True)
    return (mm_nn(p, v),)


DN_SEG = 512


def _dn_ext(ref, edge, t0, i, h):
    lanes = pl.ds(h * LANES, LANES)
    hstart = pl.multiple_of(jnp.maximum(t0 - SUBLANES, 0), SUBLANES)
    halo = jnp.where(i > 0, ref[pl.ds(hstart, SUBLANES), lanes], edge[:, h * LANES:(h + 1) * LANES])
    return jnp.concatenate([halo, ref[pl.ds(t0, DN_CHUNK), lanes]], axis=0)


def _dn_layout(cols, nb, s, nh, reverse):
    seg = min(DN_SEG, s)
    nseg, w = s // seg, nh * LANES
    hb = seg // SUBLANES

    def sg_of(g):
        return (nseg - 1 - g) if reverse else g

    main = [pl.BlockSpec((seg, w), lambda b, g, c=c0 // w: (b * nseg + sg_of(g), c)) for c0 in cols]
    edge = [pl.BlockSpec((SUBLANES, w), lambda b, g, c=c0 // w: (jnp.maximum((b * nseg + sg_of(g)) * hb - 1, 0), c))
            for c0 in cols[:3]]
    par = [pl.BlockSpec((1, w), lambda b, g: (0, 0))] * 14 + [pl.BlockSpec((1, LANES), lambda b, g: (0, 0))]
    seq = pl.BlockSpec((seg, w), lambda b, g: (b * nseg + sg_of(g), 0))
    st = pl.BlockSpec((1, nh, seg // DN_CHUNK, LANES, LANES), lambda b, g: (b, 0, sg_of(g), 0, 0))
    return seg, nseg, main, edge, par, seq, st


def deltanet_fwd(proj, cols, taps12, alog, dtb, nw, *, nb, s, nh):
    seg, nseg, main, edge, par, seq, st = _dn_layout(cols, nb, s, nh, False)
    ncs = seg // DN_CHUNK

    def body(*refs):
        q_ref, k_ref, v_ref, z_ref, ba_ref = refs[:5]
        edges = [jnp.where(pl.program_id(1) > 0, r[...], 0.0) for r in refs[5:8]]
        pars = [p[...] for p in refs[8:23]]
        o_ref, st_ref, state_scr = refs[23], refs[24], refs[25]

        @pl.when(pl.program_id(1) == 0)
        def _():
            state_scr[...] = jnp.zeros_like(state_scr)

        def step(i, carry):
            t0 = pl.multiple_of(i * DN_CHUNK, DN_CHUNK)
            rows = pl.ds(t0, DN_CHUNK)
            states = [state_scr[h] for h in range(nh)]
            args = []
            for h in range(nh):
                lanes = pl.ds(h * LANES, LANES)
                args += ([states[h]] + [_dn_ext(r, e, t0, i, h) for r, e in zip((q_ref, k_ref, v_ref), edges)]
                         + [z_ref[rows, lanes], ba_ref[rows, lanes]]
                         + [p[:, h * LANES:(h + 1) * LANES] for p in pars[:14]])
            outs = _dn_chunk_f(nh, *args, pars[14])
            for h in range(nh):
                st_ref[0, h, i] = states[h]
                o_ref[rows, pl.ds(h * LANES, LANES)] = outs[h]
                state_scr[h] = outs[nh + h]
            return carry

        lax.fori_loop(0, ncs, step, 0)

    res = pl.pallas_call(
        body, name="deltanet_fwd", grid=(nb, nseg),
        in_specs=main + edge + par, out_specs=[seq, st],
        out_shape=[jax.ShapeDtypeStruct((nb * s, nh * LANES), F32),
                   jax.ShapeDtypeStruct((nb, nh, s // DN_CHUNK, LANES, LANES), F32)],
        scratch_shapes=[pltpu.VMEM((nh, LANES, LANES), F32)], compiler_params=_cparams(2),
    )(proj, proj, proj, proj, proj, proj, proj, proj, *taps12, alog, dtb, nw)
    return res[0], res[1]


def deltanet_bwd(proj, cols, taps12, alog, dtb, nw, states, dyb, *, nb, s, nh):
    seg, nseg, main, edge, par, seq, st = _dn_layout(cols, nb, s, nh, True)
    ncs = seg // DN_CHUNK

    def body(*refs):
        q_ref, k_ref, v_ref, z_ref, ba_ref = refs[:5]
        first_in_time = pl.program_id(1) == nseg - 1
        edges = [jnp.where(first_in_time, 0.0, r[...]) for r in refs[5:8]]
        pars = [p[...] for p in refs[8:23]]
        st_ref, do_ref = refs[23], refs[24]
        dseq_refs = refs[25:28]
        dz_ref, dba_ref = refs[28], refs[29]
        dpar_refs = refs[30:45]
        dstate_scr, dhalo_scr = refs[45], refs[46]

        @pl.when((pl.program_id(0) == 0) & (pl.program_id(1) == 0))
        def _():
            for o in dpar_refs:
                o[...] = jnp.zeros_like(o)

        @pl.when(pl.program_id(1) == 0)
        def _():
            dstate_scr[...] = jnp.zeros_like(dstate_scr)
            dhalo_scr[...] = jnp.zeros_like(dhalo_scr)

        def step(ii, carry):
            i = ncs - 1 - ii
            t0 = pl.multiple_of(i * DN_CHUNK, DN_CHUNK)
            rows = pl.ds(t0, DN_CHUNK)
            d_states = [dstate_scr[h] for h in range(nh)]
            d_halos = [dhalo_scr[k] for k in range(3)]
            args = []
            for h in range(nh):
                lanes = pl.ds(h * LANES, LANES)
                args += ([st_ref[0, h, i]] + [_dn_ext(r, e, t0, i, h) for r, e in zip((q_ref, k_ref, v_ref), edges)]
                         + [z_ref[rows, lanes], ba_ref[rows, lanes]]
                         + [p[:, h * LANES:(h + 1) * LANES] for p in pars[:14]])
            _, vjp = jax.vjp(functools.partial(_dn_chunk_f, nh), *args, pars[14])
            grads = vjp(tuple(do_ref[rows, pl.ds(h * LANES, LANES)] for h in range(nh)) + tuple(d_states))
            dpar_refs[14][...] += grads[20 * nh]
            for h in range(nh):
                lanes = pl.ds(h * LANES, LANES)
                gs = grads[20 * h:20 * (h + 1)]
                dstate_scr[h] = gs[0]
                for k, (o, g) in enumerate(zip(dseq_refs, gs[1:4])):
                    o[rows, lanes] = jnp.concatenate(
                        [g[SUBLANES:DN_CHUNK], g[DN_CHUNK:] + d_halos[k][:, h * LANES:(h + 1) * LANES]], axis=0)
                    dhalo_scr[k, :, lanes] = g[:SUBLANES]
                dz_ref[rows, lanes] = gs[4]
                dba_ref[rows, lanes] = gs[5]
                for o, g in zip(dpar_refs[:14], gs[6:20]):
                    o[:, lanes] += g
            return carry

        lax.fori_loop(0, ncs, step, 0)

    w = nh * LANES
    res = pl.pallas_call(
        body, name="deltanet_bwd", grid=(nb, nseg),
        in_specs=main + edge + par + [st, seq], out_specs=[seq] * 5 + par,
        out_shape=[jax.ShapeDtypeStruct((nb * s, w), F32)] * 5
        + [jax.ShapeDtypeStruct((1, w), F32)] * 14 + [jax.ShapeDtypeStruct((1, LANES), F32)],
        scratch_shapes=[pltpu.VMEM((nh, LANES, LANES), F32), pltpu.VMEM((3, SUBLANES, w), F32)],
        compiler_params=_cparams(2),
    )(proj, proj, proj, proj, proj, proj, proj, proj, *taps12, alog, dtb, nw, states, dyb)
    return list(res[:5]), list(res[5:])


def attn_fwd(q, kv, *, nb, s, m, nh, tq=512):
    hd = q.shape[1] // nh
    tq = min(tq, s)
    nq = s // tq

    def body(q_ref, k_ref, v_ref, o_ref):
        o_ref[...] = _attn_f(q_ref[...].astype(F32), k_ref[...], v_ref[...])[0].astype(o_ref.dtype)

    return pl.pallas_call(
        body, name="attn_fwd", grid=(nb, nh, nq),
        in_specs=[pl.BlockSpec((tq, hd), lambda b, h, i: (b * nq + i, h)),
                  pl.BlockSpec((m, hd), lambda b, h, i: (b, h)),
                  pl.BlockSpec((m, hd), lambda b, h, i: (b, nh + h))],
        out_specs=pl.BlockSpec((tq, hd), lambda b, h, i: (b * nq + i, h)),
        out_shape=jax.ShapeDtypeStruct(q.shape, BF16), compiler_params=_cparams(3),
    )(q, kv, kv)


def attn_bwd(q, kv, do, *, nb, s, m, nh, tq=512):
    hd = q.shape[1] // nh
    tq = min(tq, s)
    nq = s // tq

    def body(q_ref, k_ref, v_ref, do_ref, dq_ref, dk_ref, dv_ref):
        _, vjp = jax.vjp(_attn_f, q_ref[...].astype(F32), k_ref[...], v_ref[...])
        dq, dk, dv = vjp((do_ref[...].astype(F32),))
        dq_ref[...] = dq

        @pl.when(pl.program_id(2) == 0)
        def _():
            dk_ref[...] = jnp.zeros_like(dk_ref)
            dv_ref[...] = jnp.zeros_like(dv_ref)

        dk_ref[...] += dk
        dv_ref[...] += dv

    q_spec = pl.BlockSpec((tq, hd), lambda b, h, i: (b * nq + i, h))
    m_spec = pl.BlockSpec((m, hd), lambda b, h, i: (b, h))
    res = pl.pallas_call(
        body, name="attn_bwd", grid=(nb, nh, nq),
        in_specs=[q_spec, m_spec, pl.BlockSpec((m, hd), lambda b, h, i: (b, nh + h)), q_spec],
        out_specs=[q_spec, m_spec, m_spec],
        out_shape=[jax.ShapeDtypeStruct(q.shape, F32), jax.ShapeDtypeStruct((nb * m, nh * hd), F32),
                   jax.ShapeDtypeStruct((nb * m, nh * hd), F32)],
        compiler_params=_cparams(3),
    )(q, kv, kv, do)
    return res[0], res[1], res[2]


def loss_head(x, target, gain, tm=256):
    t, d = x.shape
    tm = min(tm, t)

    def body(x_ref, t_ref, g_ref, l_ref, dx_ref, dg_ref):
        tgt = t_ref[...]

        def lf(xv, gv):
            e = _rmsnorm(xv, gv) - tgt
            return 0.5 * jnp.sum(jnp.mean(e * e, axis=-1, keepdims=True), axis=0, keepdims=True)

        lv, vjp = jax.vjp(lf, x_ref[...], g_ref[...])
        dx, dg = vjp(jnp.ones((1, 1), F32))
        dx_ref[...] = dx

        @pl.when(pl.program_id(0) == 0)
        def _():
            l_ref[...] = jnp.zeros_like(l_ref)
            dg_ref[...] = jnp.zeros_like(dg_ref)

        l_ref[...] += jnp.broadcast_to(lv, l_ref.shape)
        dg_ref[...] += dg

    row = pl.BlockSpec((tm, d), lambda i: (i, 0))
    res = pl.pallas_call(
        body, name="loss_head", grid=(t // tm,),
        in_specs=[row, row, _full_spec((1, d), 1)],
        out_specs=[_full_spec((1, LANES), 1), row, _full_spec((1, d), 1)],
        out_shape=[jax.ShapeDtypeStruct((1, LANES), F32), jax.ShapeDtypeStruct((t, d), F32),
                   jax.ShapeDtypeStruct((1, d), F32)],
        compiler_params=_cparams(1),
    )(x, target, gain)
    return res[0], res[1], res[2]


def s5_dlam(gr, gi, hr, hi, *, nb, s, cb=256, n=256):
    p = gr.shape[1]
    cb, n = min(cb, p), min(n, s)

    def body(gr_ref, gi_ref, hr_ref, hi_ref, dar_ref, dai_ref):
        @pl.when(pl.program_id(1) == 0)
        def _():
            dar_ref[...] = jnp.zeros_like(dar_ref)
            dai_ref[...] = jnp.zeros_like(dai_ref)

        def step(i, carry):
            t0 = pl.multiple_of(i * n, n)
            g_r, g_i = gr_ref[pl.ds(t0, n), :], gi_ref[pl.ds(t0, n), :]
            p_r = taps(_load_ext(hr_ref, t0, n), 2)[1]
            p_i = taps(_load_ext(hi_ref, t0, n), 2)[1]
            dar_ref[...] += jnp.sum(g_r * p_r + g_i * p_i, axis=0, keepdims=True)
            dai_ref[...] += jnp.sum(g_i * p_r - g_r * p_i, axis=0, keepdims=True)
            return carry

        lax.fori_loop(0, s // n, step, 0)

    blk = pl.BlockSpec((s, cb), lambda j, b: (b, j))
    acc = pl.BlockSpec((1, cb), lambda j, b: (0, j))
    res = pl.pallas_call(
        body, name="s5_dlam", grid=(p // cb, nb), in_specs=[blk] * 4, out_specs=[acc, acc],
        out_shape=[jax.ShapeDtypeStruct((1, p), F32)] * 2, compiler_params=_cparams(2),
    )(gr, gi, hr, hi)
    return res[0], res[1]


def s5_param_fwd(lr, li, ldt, btr, bti):
    def body(*refs):
        vals = _s5_param_f(*[r[...] for r in refs[:5]])
        for o, v in zip(refs[5:], vals):
            o[...] = v

    return pl.pallas_call(
        body, name="s5_param_fwd",
        out_shape=[jax.ShapeDtypeStruct(lr.shape, F32)] * 2 + [jax.ShapeDtypeStruct(btr.shape, F32)] * 2,
    )(lr, li, ldt, btr, bti)


def s5_param_bwd(lr, li, ldt, btr, bti, cots):
    def body(*refs):
        _, vjp = jax.vjp(_s5_param_f, *[r[...] for r in refs[:5]])
        gs = vjp(tuple(r[...] for r in refs[5:9]))
        for o, g in zip(refs[9:], gs):
            o[...] = g

    return pl.pallas_call(
        body, name="s5_param_bwd",
        out_shape=[jax.ShapeDtypeStruct(a.shape, F32) for a in (lr, li, ldt, btr, bti)],
    )(lr, li, ldt, btr, bti, *cots)


def _row_block(r, c, copies):
    lanes = -(-c // LANES) * LANES
    rb = 2048
    while rb > 2 * SUBLANES and (r % rb or copies * rb * lanes * 4 > 12 * 1024 * 1024):
        rb //= 2
    return rb if r % rb == 0 else r


def _as2d(a, lead=0):
    return a.reshape(a.shape[:lead] + (-1, a.shape[-1]))


def sumk(name, buf):
    b3 = _as2d(buf, 1)
    k, r, c = b3.shape
    rb = _row_block(r, c, 2 * (k + 1))

    def body(b_ref, o_ref):
        acc = b_ref[0]
        for i in range(1, k):
            acc = acc + b_ref[i]
        o_ref[...] = acc

    out = pl.pallas_call(
        body, name=name, grid=(r // rb,),
        in_specs=[pl.BlockSpec((k, rb, c), lambda i: (0, i, 0))],
        out_specs=pl.BlockSpec((rb, c), lambda i: (i, 0)),
        out_shape=jax.ShapeDtypeStruct((r, c), F32), compiler_params=_cparams(1),
    )(b3)
    return out.reshape(buf.shape[1:])


def adamw(name, w, g, m, v):
    shape = w.shape
    w, g, m, v = (_as2d(a) for a in (w, g, m, v))
    r, c = w.shape
    rb = _row_block(r, c, 14)

    def body(w_ref, g_ref, m_ref, v_ref, d_ref, nm_ref, nv_ref):
        gv = g_ref[...]
        nm = ADAM_B1 * m_ref[...] + (1.0 - ADAM_B1) * gv
        nv = ADAM_B2 * v_ref[...] + (1.0 - ADAM_B2) * (gv * gv)
        m_hat = nm / (1.0 - ADAM_B1 ** ADAM_STEP)
        v_hat = nv / (1.0 - ADAM_B2 ** ADAM_STEP)
        d_ref[...] = -ADAM_LR * (m_hat / (jnp.sqrt(v_hat) + ADAM_EPS) + ADAM_WD * w_ref[...])
        nm_ref[...] = nm
        nv_ref[...] = nv

    spec = pl.BlockSpec((rb, c), lambda i: (i, 0))
    res = pl.pallas_call(
        body, name=name, grid=(r // rb,), in_specs=[spec] * 4, out_specs=[spec] * 3,
        out_shape=[jax.ShapeDtypeStruct(w.shape, F32)] * 3, compiler_params=_cparams(1),
    )(w, g, m, v)
    return tuple(a.reshape(shape) for a in res)


def _place():
    x, y, c = lax.axis_index("x"), lax.axis_index("y"), lax.axis_index("c")
    chips = [(1 - x, y), (x, 1 - y), (1 - x, 1 - y)]
    return x, y, c, chips


def _chip_id(px, py):
    return 2 * px + py


def _rcopy(src, dst, send_sems, recv_sems, k, to):
    return pltpu.make_async_remote_copy(src_ref=src, dst_ref=dst, send_sem=send_sems.at[k], recv_sem=recv_sems.at[k],
                                        device_id=to, device_id_type=MESH)


def _comm_call(name, body, out_shapes, n_sems, args):
    return pl.pallas_call(
        body, name=name, out_shape=out_shapes, in_specs=[ANY] * len(args), out_specs=[ANY] * len(out_shapes),
        scratch_shapes=[pltpu.SemaphoreType.DMA((n_sems,)), pltpu.SemaphoreType.DMA((n_sems,))],
    )(*args)


def _finish(remote):
    for cp in remote:
        cp.wait_send()


def _set_slot(buf, slot, val):
    if val.ndim < buf.ndim:
        val = val[None]
    return lax.dynamic_update_slice(buf, val.astype(buf.dtype), (slot,) + (0,) * (buf.ndim - 1))


def gather_chips(name, arrays):
    n = len(arrays)
    halves = [a.shape[0] // 2 for a in arrays]

    def body(*refs):
        ins, outs = refs[:n], refs[n:2 * n]
        send_sems, recv_sems = refs[2 * n:]
        x, y, c, chips = _place()
        me, sib = _chip_id(x, y), (x, y, 1 - c)
        half = [pl.ds(c * h, h) for h in halves]
        other = [pl.ds((1 - c) * h, h) for h in halves]
        remote = []
        for a in range(n):
            for j, chip in enumerate(chips):
                remote.append(_rcopy(ins[a].at[half[a]], outs[a].at[me, half[a]], send_sems, recv_sems, 6 * a + j,
                                     (*chip, c)))
                remote[-1].start()
        for a in range(n):
            for j, chip in enumerate(chips):
                landed = outs[a].at[_chip_id(*chip), half[a]]
                _rcopy(ins[a].at[half[a]], landed, send_sems, recv_sems, 6 * a + j, sib).wait_recv()
                remote.append(_rcopy(landed, landed, send_sems, recv_sems, 6 * a + 3 + j, sib))
                remote[-1].start()
        for a in range(n):
            for j, chip in enumerate(chips):
                _rcopy(ins[a].at[half[a]], outs[a].at[_chip_id(*chip), other[a]], send_sems, recv_sems, 6 * a + 3 + j,
                       sib).wait_recv()
        _finish(remote)

    shapes = [jax.ShapeDtypeStruct((N_CHIPS,) + a.shape, a.dtype) for a in arrays]
    return _comm_call(name, body, shapes, 6 * n, arrays)


def swap_halves(name, arrays):
    n = len(arrays)

    def body(*refs):
        ins, outs = refs[:n], refs[n:2 * n]
        send_sems, recv_sems = refs[2 * n:]
        x, y, c, _ = _place()
        sib = (x, y, 1 - c)
        remote = []
        for a in range(n):
            for k in range(N_CHIPS):
                remote.append(_rcopy(ins[a].at[1 - c, k], outs[a].at[k], send_sems, recv_sems, N_CHIPS * a + k, sib))
                remote[-1].start()
        for a in range(n):
            for k in range(N_CHIPS):
                _rcopy(ins[a].at[1 - c, k], outs[a].at[k], send_sems, recv_sems, N_CHIPS * a + k, sib).wait_recv()
        _finish(remote)

    shapes = [jax.ShapeDtypeStruct(a.shape[1:], a.dtype) for a in arrays]
    return _comm_call(name, body, shapes, N_CHIPS * n, arrays)


def scatter_chips(name, arrays):
    n = len(arrays)

    def body(*refs):
        ins, outs = refs[:n], refs[n:2 * n]
        send_sems, recv_sems = refs[2 * n:]
        x, y, c, chips = _place()
        remote = []
        for a in range(n):
            for j, chip in enumerate(chips):
                remote.append(_rcopy(ins[a].at[_chip_id(*chip)], outs[a].at[j], send_sems, recv_sems, 3 * a + j,
                                     (*chip, c)))
                remote[-1].start()
        for a in range(n):
            for j, chip in enumerate(chips):
                _rcopy(ins[a].at[0], outs[a].at[j], send_sems, recv_sems, 3 * a + j, (*chip, c)).wait_recv()
        _finish(remote)

    shapes = [jax.ShapeDtypeStruct((3,) + a.shape[1:], a.dtype) for a in arrays]
    return _comm_call(name, body, shapes, 3 * n, arrays)


def join_halves(name, arrays):
    n = len(arrays)
    offs = [sum(a.shape[0] for a in arrays[:i]) for i in range(n)]

    def body(*refs):
        ins, outs = refs[:n], refs[n:2 * n]
        send_sems, recv_sems = refs[2 * n:]
        x, y, c, _ = _place()
        sib = (x, y, 1 - c)
        remote = []
        for a in range(n):
            h = arrays[a].shape[0]
            for k in range(h):
                remote.append(_rcopy(ins[a].at[k], outs[a].at[c * h + k], send_sems, recv_sems, offs[a] + k, sib))
                remote[-1].start()
        for a in range(n):
            h = arrays[a].shape[0]
            for k in range(h):
                _rcopy(ins[a].at[k], outs[a].at[(1 - c) * h + k], send_sems, recv_sems, offs[a] + k, sib).wait_recv()
        _finish(remote)

    shapes = [jax.ShapeDtypeStruct((2 * a.shape[0],) + a.shape[1:], a.dtype) for a in arrays]
    return _comm_call(name, body, shapes, offs[-1] + arrays[-1].shape[0], arrays)


def add_pair(name, mine2, other, c):
    a3, o2 = _as2d(mine2, 1), _as2d(other)
    r, cols = o2.shape
    rb = _row_block(r, cols, 8)

    def body(c_ref, a_ref, o_ref, out_ref, out16_ref):
        tot = a_ref[0] + o_ref[...]
        out_ref[...] = tot
        out16_ref[...] = tot.astype(BF16)

    spec = pl.BlockSpec((rb, cols), lambda i, s: (i, 0))
    out, out16 = pl.pallas_call(
        body, name=name, out_shape=[jax.ShapeDtypeStruct((r, cols), F32), jax.ShapeDtypeStruct((r, cols), BF16)],
        grid_spec=pltpu.PrefetchScalarGridSpec(
            num_scalar_prefetch=1, grid=(r // rb,),
            in_specs=[pl.BlockSpec((1, rb, cols), lambda i, s: (s[0], i, 0)), spec], out_specs=[spec, spec]),
        compiler_params=_cparams(1),
    )(jnp.reshape(c, (1,)).astype(jnp.int32), a3, o2)
    return out.reshape(other.shape), out16.reshape(other.shape)


def add_chips(name, part, recv, me):
    p3, r3 = _as2d(part, 1), _as2d(recv, 1)
    _, r, cols = p3.shape
    rb = _row_block(r, cols, 10)

    def body(me_ref, p_ref, r_ref, out_ref):
        own, acc = p_ref[0], None
        got = [r_ref[j].astype(F32) for j in range(3)]
        for k in range(N_CHIPS):
            rel = jnp.full(own.shape, me_ref[0] ^ k, jnp.int32)
            term = jnp.where(rel == 0, own, jnp.where(rel == 2, got[0], jnp.where(rel == 1, got[1], got[2])))
            acc = term if acc is None else acc + term
        out_ref[...] = acc

    out = pl.pallas_call(
        body, name=name, out_shape=jax.ShapeDtypeStruct((r, cols), F32),
        grid_spec=pltpu.PrefetchScalarGridSpec(
            num_scalar_prefetch=1, grid=(r // rb,),
            in_specs=[pl.BlockSpec((1, rb, cols), lambda i, s: (s[0], i, 0)),
                      pl.BlockSpec((3, rb, cols), lambda i, s: (0, i, 0))],
            out_specs=pl.BlockSpec((rb, cols), lambda i, s: (i, 0))),
        compiler_params=_cparams(1),
    )(jnp.reshape(me, (1,)).astype(jnp.int32), p3, r3)
    return out.reshape(part.shape[1:])


def gather_all(name, slab):
    def body(x_ref, out_ref, send_sems, recv_sems):
        x, y, c, chips = _place()
        me, sib = (x, y, c), (x, y, 1 - c)

        def rows(px, py, pc):
            return out_ref.at[4 * px + 2 * py + pc]

        first = [_rcopy(x_ref, rows(*me), send_sems, recv_sems, 0, sib)]
        first += [_rcopy(x_ref, rows(*me), send_sems, recv_sems, 1 + j, (*chip, c)) for j, chip in enumerate(chips)]
        for cp in first:
            cp.start()
        passed = []
        for j, chip in enumerate(chips):
            landed = rows(*chip, c)
            _rcopy(x_ref, landed, send_sems, recv_sems, 1 + j, sib).wait_recv()
            passed.append(_rcopy(landed, landed, send_sems, recv_sems, 4 + j, sib))
            passed[-1].start()
        _rcopy(x_ref, rows(*sib), send_sems, recv_sems, 0, sib).wait_recv()
        for j, chip in enumerate(chips):
            _rcopy(x_ref, rows(*chip, 1 - c), send_sems, recv_sems, 4 + j, sib).wait_recv()
        _finish(first + passed)

    return _comm_call(name, body, [jax.ShapeDtypeStruct((8,) + slab.shape, slab.dtype)], 7, [slab])[0]


D, BW, NH_B, NG, NP_S5, CG = 1024, 512, 4, 32, 64, 16
P_S5 = NG * NP_S5
C_GATES, C_XA, C_GA, C_Q, C_K, C_V, C_Z, C_UC, C_BA, N_PROJ = 0, 3072, 3584, 4096, 4608, 5120, 5632, 6144, 6656, 7168
_IN_OFFS = (0, 512, 1024, 1536, 2048, 2560, 3072, 3076, 3080, 3592, 6664)


def _blockdiag2(w):
    z = jnp.zeros((w.shape[0] // 2, 2 * w.shape[1], 2 * w.shape[2]), w.dtype)
    return z.at[:, :64, :64].set(w[0::2]).at[:, 64:, 64:].set(w[1::2])


def _unblockdiag2(d):
    return jnp.stack([d[:, :64, :64], d[:, 64:, 64:]], axis=1).reshape(-1, 64, 64)


def _layer_params(w, l):
    p = {}
    wi = w["w_in"][l]
    xa, ga, q, k, v, z, beta, alpha, uc, gates = [wi[:, a:b] for a, b in zip(_IN_OFFS[:-1], _IN_OFFS[1:])]
    ba = jnp.zeros((D, NH_B, LANES), wi.dtype).at[:, :, 0].set(beta).at[:, :, 1].set(alpha).reshape(D, NH_B * LANES)
    p["w_in_pad"] = jnp.concatenate([gates, xa, ga, q, k, v, z, uc, ba], axis=1)
    row = lambda a: a.reshape(1, -1).astype(F32)
    acw, bcw, fcw = w["a_conv_w"][l], w["b_conv_w"][l], w["ffn_conv_w"][l]
    p["a_taps"] = [row(acw[i]) for i in range(4)]
    p["a_cb"], p["a_br"], p["a_bi"], p["a_lam"] = (row(w[n][l]) for n in ("a_conv_b", "a_b_r", "a_b_i", "a_lam"))
    p["a_wr"], p["a_wi"] = _blockdiag2(w["a_w_r"][l]), _blockdiag2(w["a_w_i"][l])
    p["b_taps"] = [row(bcw[i, j * BW:(j + 1) * BW]) for j in range(3) for i in range(4)]
    p["b_alog"] = row(jnp.repeat(w["b_a_log"][l], LANES))
    p["b_dtb"] = row(jnp.repeat(w["b_dt_bias"][l], LANES))
    p["b_nw"] = row(w["b_norm"][l])
    p["c_lr"], p["c_li"] = w["c_lam_re"][l][:, None, :], w["c_lam_im"][l][:, None, :]
    p["c_ldt"] = w["c_log_dt"][l][:, None, None]
    p["c_btr"], p["c_bti"] = w["c_b_re"][l].transpose(0, 2, 1), w["c_b_im"][l].transpose(0, 2, 1)
    eye = jnp.eye(NG, dtype=F32)
    p["eye"] = eye
    expand = lambda t: (t[:, :, None, :] * eye[:, None, :, None]).reshape(t.shape[0] * t.shape[1], -1)
    p["expand"] = expand
    p["c_cre"] = expand(w["c_c_re"][l].transpose(0, 2, 1))
    p["c_cimn"] = expand(-w["c_c_im"][l].transpose(0, 2, 1))
    p["c_d"], p["c_glu_b"] = row(w["c_d"][l]), row(w["c_glu_b"][l])
    p["c_glu_w"] = w["c_glu_w"][l].astype(F32)
    p["wb"] = [w["w_branch"][l, k].astype(F32) for k in range(3)]
    p["bg"] = [row(w["b_gate"][l, k * D:(k + 1) * D]) for k in range(3)]
    for n in ("w_out", "xa_w_q", "xa_w_kv", "xa_w_o", "ffn_w_up", "ffn_w_down"):
        p[n] = w[n][l]
    for n in ("mix_norm", "xa_norm", "mem_norm", "ffn_norm"):
        p[n] = row(w[n][l])
    dff = fcw.shape[1] // 2
    p["f_taps"] = [row(fcw[i, :dff]) for i in range(3)] + [row(fcw[i, dff:]) for i in range(3)]
    p["f_b"] = [row(w["ffn_conv_b"][l, :dff]), row(w["ffn_conv_b"][l, dff:])]
    return p


def _chan(arr, cb=LANES):
    return (arr, (1, cb), lambda j: (0, j))


def _a_params(p):
    blk = lambda a: (a, (1, LANES, LANES), lambda j: (j, 0, 0))
    return [_chan(t) for t in p["a_taps"]] + [_chan(p["a_cb"]), blk(p["a_wr"]), blk(p["a_wi"]),
                                              _chan(p["a_br"]), _chan(p["a_bi"]), _chan(p["a_lam"])]


def _f_params(p):
    return [_chan(t) for t in p["f_taps"]] + [_chan(b) for b in p["f_b"]]


def _layer_fwd(x, mem, p, nb, s, m):
    sv = {"x": x}
    h = rowop_fwd("norm_mix", _norm_f, [(x, 0, D)], [p["mix_norm"]], [(D, BF16)])[0]
    proj = mm("in_proj", h, p["w_in_pad"], "nn")
    a, b = seqop_fwd("rglru_pre", _rglru_pre_f, [(proj, C_XA)], _a_params(p), [F32, F32], nb=nb, s=s, nblk=BW // LANES)
    ha = rscan("rglru_scan", a, b, False, nb=nb, s=s)
    yb, states = deltanet_fwd(proj, (C_Q, C_K, C_V, C_Z, C_BA), p["b_taps"], p["b_alog"], p["b_dtb"], p["b_nw"],
                              nb=nb, s=s, nh=NH_B)
    ar, ai, bbr, bbi = s5_param_fwd(p["c_lr"], p["c_li"], p["c_ldt"], p["c_btr"], p["c_bti"])
    bd = jnp.concatenate([p["expand"](bbr), p["expand"](bbi)], axis=1)
    ar2, ai2 = ar.reshape(1, P_S5), ai.reshape(1, P_S5)
    bu = mm("s5_bu", proj, bd, "nn", a_cols=(C_UC, BW))
    hr, hi = cscan("s5_scan", bu, ar2, ai2, False, nb=nb, s=s)
    ys = mm("s5_y_im", hi, p["c_cimn"], "nn", res=mm("s5_y_re", hr, p["c_cre"], "nn"))
    yc = rowop_fwd("s5_post", _s5_post_f, [(ys, 0, BW), (proj, C_UC, BW)], [p["c_d"], p["c_glu_w"], p["c_glu_b"]],
                   [(BW, F32)])[0]
    merge_rows = [(ha, 0, BW), (proj, C_GA, BW), (yb, 0, BW), (yc, 0, BW),
                  (proj, 0, D), (proj, D, D), (proj, 2 * D, D)]
    merged = rowop_fwd("merge", _merge_f, merge_rows, p["wb"] + p["bg"], [(D, BF16)])[0]
    x1 = mm("out_proj", merged, p["w_out"], "nn", res=x)
    hq = rowop_fwd("norm_xa", _norm_f, [(x1, 0, D)], [p["xa_norm"]], [(D, BF16)])[0]
    q = mm("xa_q", hq, p["xa_w_q"], "nn", out_dtype=BF16)
    memn = rowop_fwd("norm_mem", _norm_f, [(mem, 0, D)], [p["mem_norm"]], [(D, BF16)])[0]
    kv = mm("xa_kv", memn, p["xa_w_kv"], "nn")
    o = attn_fwd(q, kv, nb=nb, s=s, m=m, nh=4)
    x2 = mm("xa_o", o, p["xa_w_o"], "nn", res=x1)
    hf = rowop_fwd("norm_ffn", _norm_f, [(x2, 0, D)], [p["ffn_norm"]], [(D, BF16)])[0]
    up = mm("ffn_up", hf, p["ffn_w_up"], "nn")
    dff = up.shape[1] // 2
    act = seqop_fwd("ffn_act", _ffn_act_f, [(up, 0), (up, dff)], _f_params(p), [BF16], nb=nb, s=s,
                    nblk=dff // LANES)[0]
    x3 = mm("ffn_down", act, p["ffn_w_down"], "nn", res=x2)
    sv.update(h=h, proj=proj, a=a, ha=ha, yb=yb, states=states, ar=ar2, ai=ai2, bd=bd, hr=hr, hi=hi, ys=ys, yc=yc,
              merged=merged, x1=x1, hq=hq, q=q, memn=memn, kv=kv, o=o, x2=x2, hf=hf, up=up, act=act)
    return x3, sv


def _layer_bwd(dx3, mem, sv, p, nb, s, m):
    g = {}
    up, dff = sv["up"], sv["up"].shape[1] // 2
    dact = mm("d_ffn_down_x", dx3, p["ffn_w_down"], "nt")
    g["ffn_w_down"] = mm("d_ffn_down_w", sv["act"], dx3, "tn")
    (dug, duv), dfp = seqop_bwd("ffn_act_bwd", _ffn_act_f, [(up, 0), (up, dff)], _f_params(p), [(dact, 0)],
                                lambda t0, e: (e[SUBLANES:],), nb=nb, s=s, nblk=dff // LANES)
    dhf = mm("d_ffn_up_x1", duv, p["ffn_w_up"], "nt", b_cols=(dff, dff),
             res=mm("d_ffn_up_x0", dug, p["ffn_w_up"], "nt", b_cols=(0, dff)))
    g["ffn_w_up"] = jnp.concatenate([mm("d_ffn_up_w0", sv["hf"], dug, "tn"), mm("d_ffn_up_w1", sv["hf"], duv, "tn")],
                                    axis=1)
    g["ffn_conv_w"] = jnp.concatenate([jnp.concatenate(dfp[0:3], axis=0), jnp.concatenate(dfp[3:6], axis=0)], axis=1)
    g["ffn_conv_b"] = jnp.concatenate([dfp[6], dfp[7]], axis=1)[0]
    (dx2,), (gn,) = rowop_bwd("norm_ffn_bwd", _norm_f, [(sv["x2"], 0, D)], [p["ffn_norm"]], [(dhf, 0, D)], [True],
                              add=(dx3, 0, D))
    g["ffn_norm"] = gn[0]
    do = mm("d_xa_o_x", dx2, p["xa_w_o"], "nt")
    g["xa_w_o"] = mm("d_xa_o_w", sv["o"], dx2, "tn")
    dq, dk, dv = attn_bwd(sv["q"], sv["kv"], do, nb=nb, s=s, m=m, nh=4)
    dkv = jnp.concatenate([dk, dv], axis=1)
    dhq = mm("d_xa_q_x", dq, p["xa_w_q"], "nt")
    g["xa_w_q"] = mm("d_xa_q_w", sv["hq"], dq, "tn")
    dmemn = mm("d_xa_kv_x", dkv, p["xa_w_kv"], "nt")
    g["xa_w_kv"] = mm("d_xa_kv_w", sv["memn"], dkv, "tn")
    (dx1,), (gn,) = rowop_bwd("norm_xa_bwd", _norm_f, [(sv["x1"], 0, D)], [p["xa_norm"]], [(dhq, 0, D)], [True],
                              add=(dx2, 0, D))
    g["xa_norm"] = gn[0]
    _, (gn,) = rowop_bwd("norm_mem_bwd", _norm_f, [(mem, 0, D)], [p["mem_norm"]], [(dmemn, 0, D)], [False])
    g["mem_norm"] = gn[0]
    proj = sv["proj"]
    dmerged = mm("d_out_proj_x", dx1, p["w_out"], "nt")
    g["w_out"] = mm("d_out_proj_w", sv["merged"], dx1, "tn")
    merge_rows = [(sv["ha"], 0, BW), (proj, C_GA, BW), (sv["yb"], 0, BW), (sv["yc"], 0, BW),
                  (proj, 0, D), (proj, D, D), (proj, 2 * D, D)]
    (dha, dga, dyb, dyc, dg0, dg1, dg2), dmp = rowop_bwd("merge_bwd", _merge_f, merge_rows, p["wb"] + p["bg"],
                                                          [(dmerged, 0, D)], [True] * 7)
    g["w_branch"] = jnp.stack(dmp[0:3])
    g["b_gate"] = jnp.concatenate(dmp[3:6], axis=1)[0]
    (dys, duc), (gd, ggw, ggb) = rowop_bwd("s5_post_bwd", _s5_post_f, [(sv["ys"], 0, BW), (proj, C_UC, BW)],
                                           [p["c_d"], p["c_glu_w"], p["c_glu_b"]], [(dyc, 0, BW)], [True, True])
    g["c_d"], g["c_glu_w"], g["c_glu_b"] = gd[0], ggw, ggb[0]
    cd = jnp.concatenate([p["c_cre"], p["c_cimn"]], axis=0)
    dhs = mm("d_s5_y_x", dys, cd, "nt")
    dcre = mm("d_s5_y_wre", sv["hr"], dys, "tn")
    dcimn = mm("d_s5_y_wim", sv["hi"], dys, "tn")
    gr, gi = cscan("s5_scan_bwd", dhs, sv["ar"], sv["ai"], True, nb=nb, s=s)
    dar, dai = s5_dlam(gr, gi, sv["hr"], sv["hi"], nb=nb, s=s)
    duc = mm("d_s5_bu_x1", gi, sv["bd"], "nt", b_cols=(P_S5, P_S5),
             res=mm("d_s5_bu_x0", gr, sv["bd"], "nt", b_cols=(0, P_S5), res=duc))
    dbd_re = mm("d_s5_bu_wre", proj, gr, "tn", a_cols=(C_UC, BW))
    dbd_im = mm("d_s5_bu_wim", proj, gi, "tn", a_cols=(C_UC, BW))
    eye = p["eye"]
    pick = lambda dmat, r, c: jnp.einsum("grhc,gh->grc", dmat.reshape(NG, r, NG, c), eye)
    glr, gli, gldt, gbtr, gbti = s5_param_bwd(
        p["c_lr"], p["c_li"], p["c_ldt"], p["c_btr"], p["c_bti"],
        (dar.reshape(NG, 1, NP_S5), dai.reshape(NG, 1, NP_S5), pick(dbd_re, CG, NP_S5), pick(dbd_im, CG, NP_S5)))
    g["c_lam_re"], g["c_lam_im"], g["c_log_dt"] = glr[:, 0, :], gli[:, 0, :], gldt[:, 0, 0]
    g["c_b_re"], g["c_b_im"] = gbtr.transpose(0, 2, 1), gbti.transpose(0, 2, 1)
    g["c_c_re"] = pick(dcre, NP_S5, CG).transpose(0, 2, 1)
    g["c_c_im"] = -pick(dcimn, NP_S5, CG).transpose(0, 2, 1)
    (dq_b, dk_b, dv_b, dz_b, dba), dbp = deltanet_bwd(proj, (C_Q, C_K, C_V, C_Z, C_BA), p["b_taps"], p["b_alog"],
                                                      p["b_dtb"], p["b_nw"], sv["states"], dyb, nb=nb, s=s, nh=NH_B)
    g["b_conv_w"] = jnp.concatenate([jnp.concatenate(dbp[4 * j:4 * j + 4], axis=0) for j in range(3)], axis=1)
    g["b_a_log"] = dbp[12].reshape(NH_B, LANES)[:, 0]
    g["b_dt_bias"] = dbp[13].reshape(NH_B, LANES)[:, 0]
    g["b_norm"] = dbp[14][0]
    gsc = rscan("rglru_scan_bwd", sv["a"], dha, True, nb=nb, s=s)
    (dxa,), dap = seqop_bwd("rglru_pre_bwd", _rglru_pre_f, [(proj, C_XA)], _a_params(p), [(gsc, 0), (sv["ha"], 0)],
                            lambda t0, ge, he: (ge[SUBLANES:] * taps(he, 2)[1], ge[SUBLANES:]),
                            nb=nb, s=s, nblk=BW // LANES)
    g["a_conv_w"] = jnp.concatenate(dap[0:4], axis=0)
    g["a_conv_b"], g["a_b_r"], g["a_b_i"], g["a_lam"] = dap[4][0], dap[7][0], dap[8][0], dap[9][0]
    g["a_w_r"], g["a_w_i"] = _unblockdiag2(dap[5]), _unblockdiag2(dap[6])
    dproj = jnp.concatenate([dg0, dg1, dg2, dxa, dga, dq_b, dk_b, dv_b, dz_b, duc, dba], axis=1)
    dh = mm("d_in_proj_x", dproj, p["w_in_pad"], "nt")
    dwp = mm("d_in_proj_w", sv["h"], dproj, "tn")
    dba_w = dwp[:, C_BA:].reshape(D, NH_B, LANES)
    pieces = [dwp[:, c0:c0 + BW] for c0 in (C_XA, C_GA, C_Q, C_K, C_V, C_Z)]
    g["w_in"] = jnp.concatenate(pieces + [dba_w[:, :, 0], dba_w[:, :, 1], dwp[:, C_UC:C_UC + BW], dwp[:, :3 * D]],
                                axis=1)
    (dx,), (gn,) = rowop_bwd("norm_mix_bwd", _norm_f, [(sv["x"], 0, D)], [p["mix_norm"]], [(dh, 0, D)], [True],
                             add=(dx1, 0, D))
    g["mix_norm"] = gn[0]
    return dx, g


def _local_step(x, mem, target, w):
    nb, s, _ = x.shape
    m = mem.shape[1]
    depth = w["mix_norm"].shape[0]
    x2d, mem2d = x.reshape(nb * s, D), mem.reshape(nb * m, D)
    params, saved = [], []
    for l in range(depth):
        params.append(_layer_params(w, l))
        x2d, sv = _layer_fwd(x2d, mem2d, params[l], nb, s, m)
        saved.append(sv)
    loss_row, dx, gfn = loss_head(x2d, target.reshape(nb * s, D), w["final_norm"].reshape(1, D))
    grads = [None] * depth
    for l in reversed(range(depth)):
        dx, grads[l] = _layer_bwd(dx, mem2d, saved[l], params[l], nb, s, m)
    grads.append({"final_norm": gfn[0]})
    return loss_row, dx.reshape(x.shape), grads


def _stack_layers(grads, name):
    if name == "final_norm":
        return grads[-1][name]
    return jnp.stack([g[name] for g in grads[:-1]])


SLAB_UNIT = 2 * 1024 * LANES


def _slab(parts, dtype, unit):
    flat = jnp.concatenate([a.astype(dtype).reshape(-1) for a in parts])
    pad = (-flat.shape[0]) % unit
    return jnp.pad(flat, (0, pad)) if pad else flat


def _unslab(flat, like):
    out, off = [], 0
    for a in like:
        out.append(flat[off:off + a.size].reshape(a.shape))
        off += a.size
    return out


def _gather_weights(name, local, names_axes, dtype):
    mine = [local[n].astype(dtype) for n, _ in names_axes]
    me = _chip_id(lax.axis_index("x"), lax.axis_index("y"))
    got = [_set_slot(g, me, a) for g, a in zip(gather_chips(name, mine), mine)]
    return {n: jnp.concatenate([g[j] for j in range(N_CHIPS)], axis=ax) for (n, ax), g in zip(names_axes, got)}


def kernel(x, mem, mix_norm, w_in, b_gate, a_conv_w, a_conv_b, a_w_r, a_b_r, a_w_i, a_b_i, a_lam, b_conv_w, b_a_log, b_dt_bias, b_norm, c_lam_re, c_lam_im, c_log_dt, c_b_re, c_b_im, c_c_re, c_c_im, c_d, c_glu_w, c_glu_b, w_branch, w_out, xa_norm, mem_norm, xa_w_q, xa_w_kv, xa_w_o, ffn_norm, ffn_w_up, ffn_conv_w, ffn_conv_b, ffn_w_down, final_norm, loss_target, m_mix_norm, m_w_in, m_b_gate, m_a_conv_w, m_a_conv_b, m_a_w_r, m_a_b_r, m_a_w_i, m_a_b_i, m_a_lam, m_b_conv_w, m_b_a_log, m_b_dt_bias, m_b_norm, m_c_lam_re, m_c_lam_im, m_c_log_dt, m_c_b_re, m_c_b_im, m_c_c_re, m_c_c_im, m_c_d, m_c_glu_w, m_c_glu_b, m_w_branch, m_w_out, m_xa_norm, m_mem_norm, m_xa_w_q, m_xa_w_kv, m_xa_w_o, m_ffn_norm, m_ffn_w_up, m_ffn_conv_w, m_ffn_conv_b, m_ffn_w_down, m_final_norm, v_mix_norm, v_w_in, v_b_gate, v_a_conv_w, v_a_conv_b, v_a_w_r, v_a_b_r, v_a_w_i, v_a_b_i, v_a_lam, v_b_conv_w, v_b_a_log, v_b_dt_bias, v_b_norm, v_c_lam_re, v_c_lam_im, v_c_log_dt, v_c_b_re, v_c_b_im, v_c_c_re, v_c_c_im, v_c_d, v_c_glu_w, v_c_glu_b, v_w_branch, v_w_out, v_xa_norm, v_mem_norm, v_xa_w_q, v_xa_w_kv, v_xa_w_o, v_ffn_norm, v_ffn_w_up, v_ffn_conv_w, v_ffn_conv_b, v_ffn_w_down, v_final_norm):
    env = dict(locals())
    w = {n: env[n] for n in WEIGHTS}
    mom = {n: env["m_" + n] for n in WEIGHTS}
    var = {n: env["v_" + n] for n in WEIGHTS}

    full = dict(w)
    full.update(_gather_weights("gather_mats", w, SHARDED_MATS, BF16))
    full.update(_gather_weights("gather_convs", w, SHARDED_CONVS, F32))
    loss_row, grad_x, glayers = _local_step(x, mem, loss_target, full)

    depth = mix_norm.shape[0]
    assert depth % 2 == 0
    hl = depth // 2
    res = {}
    by_chip = []
    for n, ax in SHARDED:
        width = w[n].shape[ax]
        by_chip.append(jnp.stack([jnp.stack([jnp.stack([
            lax.slice_in_dim(glayers[l][n], j * width, (j + 1) * width, axis=ax - 1)
            for l in range(hf * hl, (hf + 1) * hl)]) for j in range(N_CHIPS)]) for hf in range(2)]))
    core = lax.axis_index("c")
    chip = _chip_id(lax.axis_index("x"), lax.axis_index("y"))
    sibs = swap_halves("swap_halves", by_chip)
    part = [add_pair("add_cores_" + n, g2, sb, core) for (n, _), g2, sb in zip(SHARDED, by_chip, sibs)]
    recv = scatter_chips("scatter_chips", [p16 for _, p16 in part])
    mine = [add_chips("add_chips_" + n, p32, r, chip) for (n, _), (p32, _), r in zip(SHARDED, part, recv)]
    for (n, _), own, joined in zip(SHARDED, mine, join_halves("join_halves", mine)):
        g = _set_slot(joined, core * hl, own)
        res["grad_" + n] = g
        res["delta_" + n], res["new_m_" + n], res["new_v_" + n] = adamw("adamw_" + n, w[n], g, mom[n], var[n])

    unit = SUBLANES * LANES
    small = _slab([_stack_layers(glayers, n) for n in REPLICATED] + [loss_row[:, :1]], F32, unit).reshape(-1, LANES)
    g_rp = sumk("add_devices", _set_slot(gather_all("gather_all", small), 2 * chip + core, small))
    slab1 = lambda d: _slab([d[n] for n in REPLICATED] + [jnp.zeros((1, 1), F32)], F32, unit).reshape(-1, LANES)
    d_rp, m_rp, v_rp = adamw("adamw_replicated", slab1(w), g_rp, slab1(mom), slab1(var))
    like_rp = [w[n] for n in REPLICATED] + [jnp.zeros((1, 1), F32)]
    for kind, rp in (("grad", g_rp), ("delta", d_rp), ("new_m", m_rp), ("new_v", v_rp)):
        for n, a in zip(list(REPLICATED) + ["loss"], _unslab(rp.reshape(-1), like_rp)):
            res[kind + "_" + n] = a
    loss = res["grad_loss"].reshape(())
    return (loss, grad_x, *[res["grad_" + n] for n in WEIGHTS], *[res["delta_" + n] for n in WEIGHTS],
            *[res["new_m_" + n] for n in WEIGHTS], *[res["new_v_" + n] for n in WEIGHTS])
```

```python
import functools
import math

import jax
import jax.numpy as jnp
from jax import lax
from jax.experimental import pallas as pl
from jax.experimental.pallas import tpu as pltpu

F32, BF16 = jnp.float32, jnp.bfloat16
MESH = pl.DeviceIdType.MESH
ANY = pl.BlockSpec(memory_space=pl.ANY)

VMEM_LIMIT_V7X = 56 * 1024 * 1024
MM_TILE = 1024
LANES, SUBLANES = 128, 8
EPS = 1e-6
RG_C = 8.0
DN_CHUNK = 64
N_CHIPS = 4
ADAM_LR, ADAM_B1, ADAM_B2, ADAM_EPS, ADAM_WD, ADAM_STEP = 0.001, 0.9, 0.999, 1e-08, 0.01, 10

SHARDED_MATS = (("w_in", 2), ("c_glu_w", 1), ("w_branch", 3), ("w_out", 1), ("xa_w_q", 1), ("xa_w_kv", 2),
                ("xa_w_o", 1), ("ffn_w_up", 2), ("ffn_w_down", 1))
SHARDED_CONVS = (("a_conv_w", 2), ("b_conv_w", 2), ("ffn_conv_w", 2))
SHARDED = SHARDED_MATS + SHARDED_CONVS
WEIGHTS = ("mix_norm", "w_in", "b_gate", "a_conv_w", "a_conv_b", "a_w_r", "a_b_r", "a_w_i", "a_b_i", "a_lam",
           "b_conv_w", "b_a_log", "b_dt_bias", "b_norm", "c_lam_re", "c_lam_im", "c_log_dt", "c_b_re", "c_b_im",
           "c_c_re", "c_c_im", "c_d", "c_glu_w", "c_glu_b", "w_branch", "w_out", "xa_norm", "mem_norm", "xa_w_q",
           "xa_w_kv", "xa_w_o", "ffn_norm", "ffn_w_up", "ffn_conv_w", "ffn_conv_b", "ffn_w_down", "final_norm")
REPLICATED = tuple(n for n in WEIGHTS if n not in dict(SHARDED))


def _cparams(n_axes):
    return pltpu.CompilerParams(dimension_semantics=("arbitrary",) * n_axes, vmem_limit_bytes=VMEM_LIMIT_V7X)


def _tile(n, pref, off=0):
    t = pref
    while t >= LANES:
        if n % t == 0 and off % t == 0:
            return t
        t //= 2
    assert off == 0, (n, pref, off)
    return n


def _full_spec(shape, n_grid):
    nd = len(shape)
    if n_grid == 1:
        return pl.BlockSpec(shape, lambda i: (0,) * nd)
    return pl.BlockSpec(shape, lambda i, j: (0,) * nd)


_NN, _NT, _TN = ((1,), (0,)), ((1,), (1,)), ((0,), (0,))


def _dot(a, b, dims, hp):
    if hp:
        return lax.dot_general(a, b, (dims, ((), ())), precision=lax.Precision.HIGH, preferred_element_type=F32)
    return lax.dot_general(a.astype(BF16), b.astype(BF16), (dims, ((), ())), preferred_element_type=F32)


@functools.partial(jax.custom_vjp, nondiff_argnums=(2,))
def mm_nn(a, b, hp=False):
    return _dot(a, b, _NN, hp)


@functools.partial(jax.custom_vjp, nondiff_argnums=(2,))
def mm_nt(a, b, hp=False):
    return _dot(a, b, _NT, hp)


@functools.partial(jax.custom_vjp, nondiff_argnums=(2,))
def mm_tn(a, b, hp=False):
    return _dot(a, b, _TN, hp)


mm_nn.defvjp(lambda a, b, hp: (_dot(a, b, _NN, hp), (a, b)),
             lambda hp, r, g: (mm_nt(g, r[1], hp), mm_tn(r[0], g, hp)))
mm_nt.defvjp(lambda a, b, hp: (_dot(a, b, _NT, hp), (a, b)),
             lambda hp, r, g: (mm_nn(g, r[1], hp), mm_tn(g, r[0], hp)))
mm_tn.defvjp(lambda a, b, hp: (_dot(a, b, _TN, hp), (a, b)),
             lambda hp, r, g: (mm_nt(r[1], g, hp), mm_nn(r[0], g, hp)))


@functools.partial(jax.custom_vjp, nondiff_argnums=(1,))
def taps(xext, k):
    return tuple((xext if s == 0 else pltpu.roll(xext, s, 0))[SUBLANES:] for s in range(k))


def _taps_fwd(xext, k):
    return taps(xext, k), None


def _taps_bwd(k, _, gs):
    tot = None
    for s, g in enumerate(gs):
        gp = jnp.concatenate([jnp.zeros((SUBLANES, g.shape[1]), g.dtype), g], axis=0)
        if s:
            gp = pltpu.roll(gp, gp.shape[0] - s, 0)
        tot = gp if tot is None else tot + gp
    return (tot,)


taps.defvjp(_taps_fwd, _taps_bwd)


def _gelu(x):
    return x * (0.5 * (1.0 + jnp.tanh(0.7978845608028654 * (x + 0.044715 * (x * x * x)))))


def _sigmoid(x):
    return jax.nn.sigmoid(x)


def _softplus(x):
    return jnp.maximum(x, 0.0) + jnp.log1p(jnp.exp(-jnp.abs(x)))


def _rmsnorm(x, g):
    return x * lax.rsqrt(jnp.mean(x * x, axis=-1, keepdims=True) + EPS) * g


def _lane_pick(x, lane):
    sel = lax.broadcasted_iota(jnp.int32, x.shape, 1) == lane
    return jnp.sum(jnp.where(sel, x, 0.0), axis=1, keepdims=True)


def _load_ext(ref, t0, n):
    main = ref[pl.ds(t0, n), :].astype(F32)
    hstart = pl.multiple_of(jnp.maximum(t0 - SUBLANES, 0), SUBLANES)
    halo = ref[pl.ds(hstart, SUBLANES), :].astype(F32)
    halo = jnp.where(t0 > 0, halo, 0.0)
    return jnp.concatenate([halo, main], axis=0)


def mm(name, a, b, mode, *, out_dtype=F32, res=None, a_cols=None, b_cols=None):
    a0, aw = a_cols if a_cols else (0, a.shape[1])
    b0, bw = b_cols if b_cols else (0, b.shape[1])
    big = MM_TILE
    if mode == "nn":
        m, k, n = a.shape[0], aw, bw
        assert b.shape[0] == k
        tm, tk, tn = _tile(m, big), _tile(k, big, a0), _tile(n, big, b0)
        a_spec = pl.BlockSpec((tm, tk), lambda i, j, kk: (i, kk + a0 // tk))
        b_spec = pl.BlockSpec((tk, tn), lambda i, j, kk: (kk, j + b0 // tn))
        dims = _NN
    elif mode == "nt":
        m, k, n = a.shape[0], aw, b.shape[0]
        assert bw == k
        tm, tn = _tile(m, big), _tile(n, big)
        tk = _tile(k, big, math.gcd(a0, b0) if (a0 or b0) else 0)
        assert a0 % tk == 0 and b0 % tk == 0
        a_spec = pl.BlockSpec((tm, tk), lambda i, j, kk: (i, kk + a0 // tk))
        b_spec = pl.BlockSpec((tn, tk), lambda i, j, kk: (j, kk + b0 // tk))
        dims = _NT
    else:
        k, m, n = a.shape[0], aw, bw
        assert b.shape[0] == k
        tk, tm, tn = _tile(k, big), _tile(m, big, a0), _tile(n, big, b0)
        a_spec = pl.BlockSpec((tk, tm), lambda i, j, kk: (kk, i + a0 // tm))
        b_spec = pl.BlockSpec((tk, tn), lambda i, j, kk: (kk, j + b0 // tn))
        dims = _TN
    nk = k // tk
    has_res = res is not None

    def tile_dot(a_ref, b_ref):
        return lax.dot_general(a_ref[...].astype(BF16), b_ref[...].astype(BF16), (dims, ((), ())),
                               preferred_element_type=F32)

    def finish(out, refs, o_ref):
        if has_res:
            out = out + refs[2][...].astype(F32)
        o_ref[...] = out.astype(o_ref.dtype)

    def body_single(*refs):
        finish(tile_dot(refs[0], refs[1]), refs, refs[-1])

    def body_acc(*refs):
        o_ref, acc_ref = refs[-2], refs[-1]
        kk = pl.program_id(2)

        @pl.when(kk == 0)
        def _():
            acc_ref[...] = tile_dot(refs[0], refs[1])

        @pl.when((kk > 0) & (kk < nk - 1))
        def _():
            acc_ref[...] += tile_dot(refs[0], refs[1])

        @pl.when(kk == nk - 1)
        def _():
            finish(acc_ref[...] + tile_dot(refs[0], refs[1]), refs, o_ref)

    o_spec = pl.BlockSpec((tm, tn), lambda i, j, kk: (i, j))
    return pl.pallas_call(
        body_single if nk == 1 else body_acc, name=name, grid=(m // tm, n // tn, nk),
        in_specs=[a_spec, b_spec] + ([o_spec] if has_res else []), out_specs=o_spec,
        out_shape=jax.ShapeDtypeStruct((m, n), out_dtype),
        scratch_shapes=[] if nk == 1 else [pltpu.VMEM((tm, tn), F32)], compiler_params=_cparams(3),
    )(*((a, b, res) if has_res else (a, b)))


def mm_bd(name, a, b, mode, *, nblk, bk, bn, res=None, a_col0=0, b_col0=0):
    m = a.shape[0]
    tm = _tile(m, MM_TILE)
    if mode == "tn":
        tk = tm
        nk = m // tk

        def body(a_ref, b_ref, o_ref, acc_ref):
            kk = pl.program_id(1)
            part = lax.dot_general(a_ref[...].astype(BF16), b_ref[...].astype(BF16), (_TN, ((), ())),
                                   preferred_element_type=F32)

            @pl.when(kk == 0)
            def _():
                acc_ref[...] = part

            @pl.when(kk > 0)
            def _():
                acc_ref[...] += part

            @pl.when(kk == nk - 1)
            def _():
                o_ref[...] = acc_ref[...]

        return pl.pallas_call(
            body, name=name, grid=(nblk, nk),
            in_specs=[pl.BlockSpec((tk, bk), lambda i, kk: (kk, i + a_col0 // bk)),
                      pl.BlockSpec((tk, bn), lambda i, kk: (kk, i + b_col0 // bn))],
            out_specs=pl.BlockSpec((bk, bn), lambda i, kk: (i, 0)),
            out_shape=jax.ShapeDtypeStruct((nblk * bk, bn), F32),
            scratch_shapes=[pltpu.VMEM((bk, bn), F32)], compiler_params=_cparams(2),
        )(a, b)

    dims = _NN if mode == "nn" else _NT
    if mode == "nn":
        n = b.shape[1]
        a_spec = pl.BlockSpec((tm, bk), lambda i, j: (i, a_col0 // bk + j % nblk))
        b_spec = pl.BlockSpec((bk, bn), lambda i, j: (j % nblk, j))
    else:
        n = b.shape[0]
        bk, bn = bn, bk
        a_spec = pl.BlockSpec((tm, bk), lambda i, j: (i, j % nblk))
        b_spec = pl.BlockSpec((bn, bk), lambda i, j: (j, b_col0 // bk + j % nblk))
    has_res = res is not None

    def body(*refs):
        out = lax.dot_general(refs[0][...].astype(BF16), refs[1][...].astype(BF16), (dims, ((), ())),
                              preferred_element_type=F32)
        if has_res:
            out = out + refs[2][...]
        refs[-1][...] = out

    o_spec = pl.BlockSpec((tm, bn), lambda i, j: (i, j))
    return pl.pallas_call(
        body, name=name, grid=(m // tm, n // bn), in_specs=[a_spec, b_spec] + ([o_spec] if has_res else []),
        out_specs=o_spec, out_shape=jax.ShapeDtypeStruct((m, n), F32), compiler_params=_cparams(2),
    )(*((a, b, res) if has_res else (a, b)))


def _row_specs(rows, tm):
    return [pl.BlockSpec((tm, cw), lambda i, c=c0 // cw: (i, c)) for (_, c0, cw) in rows]


def rowop_fwd(name, f, rows, params, outs, tm=256):
    t = rows[0][0].shape[0]
    tm = min(tm, t)
    n_r, n_p = len(rows), len(params)

    def body(*refs):
        vals = f(*[r[...].astype(F32) for r in refs[:n_r]], *[p[...] for p in refs[n_r:n_r + n_p]])
        for o, v in zip(refs[n_r + n_p:], vals):
            o[...] = v.astype(o.dtype)

    res = pl.pallas_call(
        body, name=name, grid=(t // tm,),
        in_specs=_row_specs(rows, tm) + [_full_spec(p.shape, 1) for p in params],
        out_specs=[pl.BlockSpec((tm, cw), lambda i: (i, 0)) for cw, _ in outs],
        out_shape=[jax.ShapeDtypeStruct((t, cw), dt) for cw, dt in outs],
        compiler_params=_cparams(1),
    )(*[r[0] for r in rows], *params)
    return list(res)


def rowop_bwd(name, f, rows, params, couts, need, tm=256, add=None):
    t = rows[0][0].shape[0]
    tm = min(tm, t)
    n_r, n_p, n_c = len(rows), len(params), len(couts)
    want = [k for k in range(n_r) if need[k]]
    adds = [add] if add is not None else []

    def body(*refs):
        xs = [r[...].astype(F32) for r in refs[:n_r]]
        ps = [p[...] for p in refs[n_r:n_r + n_p]]
        cs = tuple(c[...].astype(F32) for c in refs[n_r + n_p:n_r + n_p + n_c])
        outs = refs[n_r + n_p + n_c + len(adds):]
        _, vjp = jax.vjp(f, *xs, *ps)
        gs = vjp(cs)
        for o, k in zip(outs[:len(want)], want):
            o[...] = gs[k] + refs[n_r + n_p + n_c][...] if (adds and k == want[0]) else gs[k]

        @pl.when(pl.program_id(0) == 0)
        def _():
            for o in outs[len(want):]:
                o[...] = jnp.zeros_like(o)

        for o, g in zip(outs[len(want):], gs[n_r:]):
            o[...] += g

    res = pl.pallas_call(
        body, name=name, grid=(t // tm,),
        in_specs=_row_specs(rows, tm) + [_full_spec(p.shape, 1) for p in params] + _row_specs(couts, tm)
        + _row_specs(adds, tm),
        out_specs=[pl.BlockSpec((tm, rows[k][2]), lambda i: (i, 0)) for k in want]
        + [_full_spec(p.shape, 1) for p in params],
        out_shape=[jax.ShapeDtypeStruct((t, rows[k][2]), F32) for k in want]
        + [jax.ShapeDtypeStruct(p.shape, F32) for p in params],
        compiler_params=_cparams(1),
    )(*[r[0] for r in rows], *params, *[c[0] for c in couts], *[a[0] for a in adds])
    return list(res[:len(want)]), list(res[len(want):])


def _seq_specs(seqs, s, cb, order):
    if order == "bj":
        return [pl.BlockSpec((s, cb), lambda b, j, c=c0 // cb: (b, c + j)) for (_, c0) in seqs]
    return [pl.BlockSpec((s, cb), lambda j, b, c=c0 // cb: (b, c + j)) for (_, c0) in seqs]


def _param_specs(params, order):
    if order == "bj":
        return [pl.BlockSpec(bs, lambda b, j, fn=fn: fn(j)) for (_, bs, fn) in params]
    return [pl.BlockSpec(bs, lambda j, b, fn=fn: fn(j)) for (_, bs, fn) in params]


def seqop_fwd(name, f, seqs, params, out_dtypes, *, nb, s, nblk, cb=LANES, n=256):
    n = min(n, s)
    n_s, n_p = len(seqs), len(params)

    def body(*refs):
        seq_refs, par_refs, out_refs = refs[:n_s], refs[n_s:n_s + n_p], refs[n_s + n_p:]

        def step(i, carry):
            t0 = pl.multiple_of(i * n, n)
            vals = f(t0, *[_load_ext(r, t0, n) for r in seq_refs], *[p[...] for p in par_refs])
            for o, v in zip(out_refs, vals):
                o[pl.ds(t0, n), :] = v.astype(o.dtype)
            return carry

        lax.fori_loop(0, s // n, step, 0)

    res = pl.pallas_call(
        body, name=name, grid=(nb, nblk),
        in_specs=_seq_specs(seqs, s, cb, "bj") + _param_specs(params, "bj"),
        out_specs=[pl.BlockSpec((s, cb), lambda b, j: (b, j)) for _ in out_dtypes],
        out_shape=[jax.ShapeDtypeStruct((nb * s, nblk * cb), dt) for dt in out_dtypes],
        compiler_params=_cparams(2),
    )(*[q[0] for q in seqs], *[p[0] for p in params])
    return list(res)


def seqop_bwd(name, f, seqs, params, cot_seqs, cot_fn, *, nb, s, nblk, cb=LANES, n=256):
    n = min(n, s)
    n_s, n_p, n_c = len(seqs), len(params), len(cot_seqs)
    nchunk = s // n

    def body(*refs):
        seq_refs, par_refs = refs[:n_s], refs[n_s:n_s + n_p]
        cot_refs = refs[n_s + n_p:n_s + n_p + n_c]
        dseq_refs = refs[n_s + n_p + n_c:n_s + n_p + n_c + n_s]
        dpar_refs = refs[n_s + n_p + n_c + n_s:]

        @pl.when(pl.program_id(1) == 0)
        def _():
            for o in dpar_refs:
                o[...] = jnp.zeros_like(o)

        def step(ii, halos):
            t0 = pl.multiple_of((nchunk - 1 - ii) * n, n)
            exts = [_load_ext(r, t0, n) for r in seq_refs]
            ps = [p[...] for p in par_refs]
            cots = cot_fn(t0, *[_load_ext(r, t0, n) for r in cot_refs])
            _, vjp = jax.vjp(lambda *args: f(t0, *args), *exts, *ps)
            gs = vjp(tuple(cots))
            tail = pl.multiple_of(t0 + n - SUBLANES, SUBLANES)
            new_halos = []
            for o, g, h in zip(dseq_refs, gs[:n_s], halos):
                o[pl.ds(t0, n), :] = g[SUBLANES:]
                o[pl.ds(tail, SUBLANES), :] += h
                new_halos.append(g[:SUBLANES])
            for o, g in zip(dpar_refs, gs[n_s:]):
                o[...] += g
            return tuple(new_halos)

        lax.fori_loop(0, nchunk, step, tuple(jnp.zeros((SUBLANES, cb), F32) for _ in range(n_s)))

    res = pl.pallas_call(
        body, name=name, grid=(nblk, nb),
        in_specs=_seq_specs(seqs, s, cb, "jb") + _param_specs(params, "jb") + _seq_specs(cot_seqs, s, cb, "jb"),
        out_specs=[pl.BlockSpec((s, cb), lambda j, b: (b, j)) for _ in seqs] + _param_specs(params, "jb"),
        out_shape=[jax.ShapeDtypeStruct((nb * s, nblk * cb), F32) for _ in seqs]
        + [jax.ShapeDtypeStruct(p[0].shape, F32) for p in params],
        compiler_params=_cparams(2),
    )(*[q[0] for q in seqs], *[p[0] for p in params], *[q[0] for q in cot_seqs])
    return list(res[:n_s]), list(res[n_s:])


def rscan(name, a, b, reverse, *, nb, s, cb=256):
    c = a.shape[1]
    cb = min(cb, c)

    def body(a_ref, b_ref, h_ref):
        def step(ii, car):
            i = (s // SUBLANES - 1 - ii) if reverse else ii
            base = pl.multiple_of(i * SUBLANES, SUBLANES)
            for r in (reversed(range(SUBLANES)) if reverse else range(SUBLANES)):
                row = pl.ds(base + r, 1)
                if reverse:
                    g = b_ref[row, :] + car
                    h_ref[row, :] = g
                    car = a_ref[row, :] * g
                else:
                    car = a_ref[row, :] * car + b_ref[row, :]
                    h_ref[row, :] = car
            return car

        lax.fori_loop(0, s // SUBLANES, step, jnp.zeros((1, cb), F32))

    spec = pl.BlockSpec((s, cb), lambda bb, j: (bb, j))
    return pl.pallas_call(
        body, name=name, grid=(nb, c // cb), in_specs=[spec, spec], out_specs=spec,
        out_shape=jax.ShapeDtypeStruct(a.shape, F32), compiler_params=_cparams(2),
    )(a, b)


def cscan(name, bu, ar, ai, reverse, *, nb, s, cb=256):
    p = ar.shape[1]
    cb = min(cb, p)
    nj = p // cb

    def body(br_ref, bi_ref, ar_ref, ai_ref, hr_ref, hi_ref):
        lr = ar_ref[...]
        li = -ai_ref[...] if reverse else ai_ref[...]

        def step(ii, car):
            hr, hi = car
            i = (s // SUBLANES - 1 - ii) if reverse else ii
            base = pl.multiple_of(i * SUBLANES, SUBLANES)
            for r in (reversed(range(SUBLANES)) if reverse else range(SUBLANES)):
                row = pl.ds(base + r, 1)
                nr = lr * hr - li * hi + br_ref[row, :]
                ni = lr * hi + li * hr + bi_ref[row, :]
                hr, hi = nr, ni
                hr_ref[row, :] = hr
                hi_ref[row, :] = hi
            return hr, hi

        z = jnp.zeros((1, cb), F32)
        lax.fori_loop(0, s // SUBLANES, step, (z, z))

    o_spec = pl.BlockSpec((s, cb), lambda bb, j: (bb, j))
    l_spec = pl.BlockSpec((1, cb), lambda bb, j: (0, j))
    res = pl.pallas_call(
        body, name=name, grid=(nb, nj),
        in_specs=[o_spec, pl.BlockSpec((s, cb), lambda bb, j: (bb, j + nj)), l_spec, l_spec],
        out_specs=[o_spec, o_spec],
        out_shape=[jax.ShapeDtypeStruct((nb * s, p), F32)] * 2, compiler_params=_cparams(2),
    )(bu, bu, ar, ai)
    return res[0], res[1]


def _norm_f(x, g):
    return (_rmsnorm(x, g),)


def _rglru_pre_f(t0, xext, w0, w1, w2, w3, cb, wr, wi, br, bi, lam):
    x0, x1, x2, x3 = taps(xext, 4)
    xc = w3 * x0 + w2 * x1 + w1 * x2 + w0 * x3 + cb
    r = _sigmoid(mm_nn(xc, wr[0]) + br)
    ig = _sigmoid(mm_nn(xc, wi[0]) + bi)
    log_a = -RG_C * r * _softplus(-lam)
    a = jnp.exp(log_a)
    first = (lax.broadcasted_iota(jnp.int32, a.shape, 0) + t0) == 0
    mult = jnp.where(first, 1.0, jnp.sqrt(1.0 - jnp.exp(2.0 * log_a)))
    return a, mult * ig * xc


def _ffn_act_f(t0, gext, vext, g0, g1, g2, v0, v1, v2, bg, bv):
    ga, gb, gc = taps(gext, 3)
    va, vb, vc = taps(vext, 3)
    ug = g2 * ga + g1 * gb + g0 * gc + bg
    uv = v2 * va + v1 * vb + v0 * vc + bv
    return (_gelu(ug) * uv,)


def _s5_post_f(ys, uc, d, gw, gb):
    yc = _gelu(ys + d * uc)
    return (yc * _sigmoid(mm_nn(yc, gw) + gb),)


def _merge_f(ha, ga, yb, yc, g0, g1, g2, wb0, wb1, wb2, bg0, bg1, bg2):
    ya = ha * _gelu(ga)
    out = _sigmoid(g0 + bg0) * mm_nn(ya, wb0)
    out = out + _sigmoid(g1 + bg1) * mm_nn(yb, wb1)
    out = out + _sigmoid(g2 + bg2) * mm_nn(yc, wb2)
    return (out,)


def _s5_param_f(lr, li, ldt, btr, bti):
    dt = jnp.exp(ldt)
    mag = jnp.exp(lr * dt)
    ar, ai = mag * jnp.cos(li * dt), mag * jnp.sin(li * dt)
    den = lr * lr + li * li
    fr = ((ar - 1.0) * lr + ai * li) / den
    fi = (ai * lr - (ar - 1.0) * li) / den
    return ar, ai, fr * btr - fi * bti, fr * bti + fi * btr


def _each(fn, *lists):
    return [fn(*xs) for xs in zip(*lists)]


def _tri_inv(mats):
    c = mats[0].shape[0]
    ri = lax.broadcasted_iota(jnp.int32, (c, c), 0)
    ci = lax.broadcasted_iota(jnp.int32, (c, c), 1)
    eye = jnp.where(ri == ci, 1.0, 0.0).astype(F32)
    hp = lambda a, b: mm_nn(a, b, True)
    ad = [jnp.where((ri >> 4) == (ci >> 4), a, 0.0) for a in mats]
    ao = _each(lambda a, d: a - d, mats, ad)
    a2 = _each(hp, ad, ad)
    a4 = _each(hp, a2, a2)
    a8 = _each(hp, a4, a4)
    lo = _each(lambda d, s: hp(eye - d, eye + s), ad, a2)
    hi = _each(lambda s, e: hp(eye + s, eye + e), a4, a8)
    td = _each(hp, lo, hi)
    nn_ = _each(hp, td, ao)
    n2 = _each(hp, nn_, nn_)
    return _each(hp, _each(lambda n, s: hp(eye - n, eye + s), nn_, n2), td)


def _dn_chunk_f(nh, *args):
    per = [args[20 * h:20 * (h + 1)] for h in range(nh)]
    nw = args[20 * nh]
    state, qe, ke, ve, z, ba = ([p[j] for p in per] for j in range(6))
    c = DN_CHUNK
    dk = qe[0].shape[1]

    def conv_silu(xes, first):
        tp = [taps(x, 4) for x in xes]
        u = [p[first + 3] * t[0] + p[first + 2] * t[1] + p[first + 1] * t[2] + p[first] * t[3] for p, t in zip(per, tp)]
        return [x * _sigmoid(x) for x in u]

    def l2n(xs):
        return [x * lax.rsqrt(jnp.sum(x * x, axis=-1, keepdims=True) + EPS) for x in xs]

    q = [x * (dk ** -0.5) for x in l2n(conv_silu(qe, 6))]
    k = l2n(conv_silu(ke, 10))
    v = conv_silu(ve, 14)
    beta = [_sigmoid(_lane_pick(b, 0)) for b in ba]
    g = [-jnp.exp(_lane_pick(p[18], 0)) * _softplus(_lane_pick(b, 1) + _lane_pick(p[19], 0)) for p, b in zip(per, ba)]
    ri = lax.broadcasted_iota(jnp.int32, (c, c), 0)
    ci = lax.broadcasted_iota(jnp.int32, (c, c), 1)
    incl, strict = ri >= ci, ri > ci
    ltri = jnp.where(incl, 1.0, 0.0).astype(F32)
    mean_row = jnp.full((c, dk), 1.0 / dk, F32)
    gc_w = [mm_nn(ltri, jnp.broadcast_to(x, (c, dk)), True) for x in g]
    gc_c = [mm_nn(ltri, jnp.broadcast_to(x, (c, c)), True) for x in g]
    gc_r = [mm_nt(mean_row, x, True) for x in gc_w]
    gtot = [jnp.sum(x, axis=0, keepdims=True) for x in g]
    decay = _each(lambda a, b: jnp.exp(jnp.where(incl, a - b, -1e30)), gc_c, gc_r)
    e_gc = [jnp.exp(x) for x in gc_w]
    kb = _each(lambda a, b: a * b, k, beta)
    a_mat = _each(lambda a, b, d: jnp.where(strict, mm_nt(a, b) * d, 0.0), kb, k, decay)
    t_inv = _tri_inv(a_mat)
    u = _each(lambda t, a, b: mm_nn(t, a * b, True), t_inv, v, beta)
    w = _each(lambda t, a, e: mm_nn(t, a * e, True), t_inv, kb, e_gc)
    qk = _each(lambda a, b, d: jnp.where(incl, mm_nt(a, b) * d, 0.0), q, k, decay)
    v_new = _each(lambda a, b, s: a - mm_nn(b, s), u, w, state)
    o = _each(lambda a, e, s, b, n: mm_nn(a * e, s) + mm_nn(b, n), q, e_gc, state, qk, v_new)
    new_state = _each(lambda s, t, a, gw, n: s * jnp.exp(t) + mm_tn(a * jnp.exp(t - gw), n), state, gtot, k, gc_w, v_new)
    o = _each(lambda x, zz: _rmsnorm(x, nw) * (zz * _sigmoid(zz)), o, z)
    return tuple(o) + tuple(new_state)


def _attn_f(q, k, v):
    sc = mm_nt(q, k) * (q.shape[1] ** -0.5)
    sc = sc - jnp.max(sc, axis=-1, keepdims=True)
    e = jnp.exp(sc)
    p = e / jnp.sum(e, axis=-1, keepdims=True)
    return (mm_nn(p, v),)


DN_SEG = 512


def _dn_ext(ref, edge, t0, i, h):
    lanes = pl.ds(h * LANES, LANES)
    hstart = pl.multiple_of(jnp.maximum(t0 - SUBLANES, 0), SUBLANES)
    halo = jnp.where(i > 0, ref[pl.ds(hstart, SUBLANES), lanes], edge[:, h * LANES:(h + 1) * LANES])
    return jnp.concatenate([halo, ref[pl.ds(t0, DN_CHUNK), lanes]], axis=0)


def _dn_layout(cols, nb, s, nh, reverse):
    seg = min(DN_SEG, s)
    nseg, w = s // seg, nh * LANES
    hb = seg // SUBLANES

    def sg_of(g):
        return (nseg - 1 - g) if reverse else g

    main = [pl.BlockSpec((seg, w), lambda b, g, c=c0 // w: (b * nseg + sg_of(g), c)) for c0 in cols]
    edge = [pl.BlockSpec((SUBLANES, w), lambda b, g, c=c0 // w: (jnp.maximum((b * nseg + sg_of(g)) * hb - 1, 0), c))
            for c0 in cols[:3]]
    par = [pl.BlockSpec((1, w), lambda b, g: (0, 0))] * 14 + [pl.BlockSpec((1, LANES), lambda b, g: (0, 0))]
    seq = pl.BlockSpec((seg, w), lambda b, g: (b * nseg + sg_of(g), 0))
    st = pl.BlockSpec((1, nh, seg // DN_CHUNK, LANES, LANES), lambda b, g: (b, 0, sg_of(g), 0, 0))
    return seg, nseg, main, edge, par, seq, st


def deltanet_fwd(proj, cols, taps12, alog, dtb, nw, *, nb, s, nh):
    seg, nseg, main, edge, par, seq, st = _dn_layout(cols, nb, s, nh, False)
    ncs = seg // DN_CHUNK

    def body(*refs):
        q_ref, k_ref, v_ref, z_ref, ba_ref = refs[:5]
        edges = [jnp.where(pl.program_id(1) > 0, r[...], 0.0) for r in refs[5:8]]
        pars = [p[...] for p in refs[8:23]]
        o_ref, st_ref, state_scr = refs[23], refs[24], refs[25]

        @pl.when(pl.program_id(1) == 0)
        def _():
            state_scr[...] = jnp.zeros_like(state_scr)

        def step(i, carry):
            t0 = pl.multiple_of(i * DN_CHUNK, DN_CHUNK)
            rows = pl.ds(t0, DN_CHUNK)
            states = [state_scr[h] for h in range(nh)]
            args = []
            for h in range(nh):
                lanes = pl.ds(h * LANES, LANES)
                args += ([states[h]] + [_dn_ext(r, e, t0, i, h) for r, e in zip((q_ref, k_ref, v_ref), edges)]
                         + [z_ref[rows, lanes], ba_ref[rows, lanes]]
                         + [p[:, h * LANES:(h + 1) * LANES] for p in pars[:14]])
            outs = _dn_chunk_f(nh, *args, pars[14])
            for h in range(nh):
                st_ref[0, h, i] = states[h]
                o_ref[rows, pl.ds(h * LANES, LANES)] = outs[h]
                state_scr[h] = outs[nh + h]
            return carry

        lax.fori_loop(0, ncs, step, 0)

    res = pl.pallas_call(
        body, name="deltanet_fwd", grid=(nb, nseg),
        in_specs=main + edge + par, out_specs=[seq, st],
        out_shape=[jax.ShapeDtypeStruct((nb * s, nh * LANES), F32),
                   jax.ShapeDtypeStruct((nb, nh, s // DN_CHUNK, LANES, LANES), F32)],
        scratch_shapes=[pltpu.VMEM((nh, LANES, LANES), F32)], compiler_params=_cparams(2),
    )(proj, proj, proj, proj, proj, proj, proj, proj, *taps12, alog, dtb, nw)
    return res[0], res[1]


def deltanet_bwd(proj, cols, taps12, alog, dtb, nw, states, dyb, *, nb, s, nh):
    seg, nseg, main, edge, par, seq, st = _dn_layout(cols, nb, s, nh, True)
    ncs = seg // DN_CHUNK

    def body(*refs):
        q_ref, k_ref, v_ref, z_ref, ba_ref = refs[:5]
        first_in_time = pl.program_id(1) == nseg - 1
        edges = [jnp.where(first_in_time, 0.0, r[...]) for r in refs[5:8]]
        pars = [p[...] for p in refs[8:23]]
        st_ref, do_ref = refs[23], refs[24]
        dseq_refs = refs[25:28]
        dz_ref, dba_ref = refs[28], refs[29]
        dpar_refs = refs[30:45]
        dstate_scr, dhalo_scr = refs[45], refs[46]

        @pl.when((pl.program_id(0) == 0) & (pl.program_id(1) == 0))
        def _():
            for o in dpar_refs:
                o[...] = jnp.zeros_like(o)

        @pl.when(pl.program_id(1) == 0)
        def _():
            dstate_scr[...] = jnp.zeros_like(dstate_scr)
            dhalo_scr[...] = jnp.zeros_like(dhalo_scr)

        def step(ii, carry):
            i = ncs - 1 - ii
            t0 = pl.multiple_of(i * DN_CHUNK, DN_CHUNK)
            rows = pl.ds(t0, DN_CHUNK)
            d_states = [dstate_scr[h] for h in range(nh)]
            d_halos = [dhalo_scr[k] for k in range(3)]
            args = []
            for h in range(nh):
                lanes = pl.ds(h * LANES, LANES)
                args += ([st_ref[0, h, i]] + [_dn_ext(r, e, t0, i, h) for r, e in zip((q_ref, k_ref, v_ref), edges)]
                         + [z_ref[rows, lanes], ba_ref[rows, lanes]]
                         + [p[:, h * LANES:(h + 1) * LANES] for p in pars[:14]])
            _, vjp = jax.vjp(functools.partial(_dn_chunk_f, nh), *args, pars[14])
            grads = vjp(tuple(do_ref[rows, pl.ds(h * LANES, LANES)] for h in range(nh)) + tuple(d_states))
            dpar_refs[14][...] += grads[20 * nh]
            for h in range(nh):
                lanes = pl.ds(h * LANES, LANES)
                gs = grads[20 * h:20 * (h + 1)]
                dstate_scr[h] = gs[0]
                for k, (o, g) in enumerate(zip(dseq_refs, gs[1:4])):
                    o[rows, lanes] = jnp.concatenate(
                        [g[SUBLANES:DN_CHUNK], g[DN_CHUNK:] + d_halos[k][:, h * LANES:(h + 1) * LANES]], axis=0)
                    dhalo_scr[k, :, lanes] = g[:SUBLANES]
                dz_ref[rows, lanes] = gs[4]
                dba_ref[rows, lanes] = gs[5]
                for o, g in zip(dpar_refs[:14], gs[6:20]):
                    o[:, lanes] += g
            return carry

        lax.fori_loop(0, ncs, step, 0)

    w = nh * LANES
    res = pl.pallas_call(
        body, name="deltanet_bwd", grid=(nb, nseg),
        in_specs=main + edge + par + [st, seq], out_specs=[seq] * 5 + par,
        out_shape=[jax.ShapeDtypeStruct((nb * s, w), F32)] * 5
        + [jax.ShapeDtypeStruct((1, w), F32)] * 14 + [jax.ShapeDtypeStruct((1, LANES), F32)],
        scratch_shapes=[pltpu.VMEM((nh, LANES, LANES), F32), pltpu.VMEM((3, SUBLANES, w), F32)],
        compiler_params=_cparams(2),
    )(proj, proj, proj, proj, proj, proj, proj, proj, *taps12, alog, dtb, nw, states, dyb)
    return list(res[:5]), list(res[5:])


def attn_fwd(q, kv, *, nb, s, m, nh, tq=512):
    hd = q.shape[1] // nh
    tq = min(tq, s)
    nq = s // tq

    def body(q_ref, k_ref, v_ref, o_ref):
        o_ref[...] = _attn_f(q_ref[...].astype(F32), k_ref[...], v_ref[...])[0].astype(o_ref.dtype)

    return pl.pallas_call(
        body, name="attn_fwd", grid=(nb, nh, nq),
        in_specs=[pl.BlockSpec((tq, hd), lambda b, h, i: (b * nq + i, h)),
                  pl.BlockSpec((m, hd), lambda b, h, i: (b, h)),
                  pl.BlockSpec((m, hd), lambda b, h, i: (b, nh + h))],
        out_specs=pl.BlockSpec((tq, hd), lambda b, h, i: (b * nq + i, h)),
        out_shape=jax.ShapeDtypeStruct(q.shape, BF16), compiler_params=_cparams(3),
    )(q, kv, kv)


def attn_bwd(q, kv, do, *, nb, s, m, nh, tq=512):
    hd = q.shape[1] // nh
    tq = min(tq, s)
    nq = s // tq

    def body(q_ref, k_ref, v_ref, do_ref, dq_ref, dk_ref, dv_ref):
        _, vjp = jax.vjp(_attn_f, q_ref[...].astype(F32), k_ref[...], v_ref[...])
        dq, dk, dv = vjp((do_ref[...].astype(F32),))
        dq_ref[...] = dq

        @pl.when(pl.program_id(2) == 0)
        def _():
            dk_ref[...] = jnp.zeros_like(dk_ref)
            dv_ref[...] = jnp.zeros_like(dv_ref)

        dk_ref[...] += dk
        dv_ref[...] += dv

    q_spec = pl.BlockSpec((tq, hd), lambda b, h, i: (b * nq + i, h))
    m_spec = pl.BlockSpec((m, hd), lambda b, h, i: (b, h))
    res = pl.pallas_call(
        body, name="attn_bwd", grid=(nb, nh, nq),
        in_specs=[q_spec, m_spec, pl.BlockSpec((m, hd), lambda b, h, i: (b, nh + h)), q_spec],
        out_specs=[q_spec, m_spec, m_spec],
        out_shape=[jax.ShapeDtypeStruct(q.shape, F32), jax.ShapeDtypeStruct((nb * m, nh * hd), F32),
                   jax.ShapeDtypeStruct((nb * m, nh * hd), F32)],
        compiler_params=_cparams(3),
    )(q, kv, kv, do)
    return res[0], res[1], res[2]


def loss_head(x, target, gain, tm=256):
    t, d = x.shape
    tm = min(tm, t)

    def body(x_ref, t_ref, g_ref, l_ref, dx_ref, dg_ref):
        tgt = t_ref[...]

        def lf(xv, gv):
            e = _rmsnorm(xv, gv) - tgt
            return 0.5 * jnp.sum(jnp.mean(e * e, axis=-1, keepdims=True), axis=0, keepdims=True)

        lv, vjp = jax.vjp(lf, x_ref[...], g_ref[...])
        dx, dg = vjp(jnp.ones((1, 1), F32))
        dx_ref[...] = dx

        @pl.when(pl.program_id(0) == 0)
        def _():
            l_ref[...] = jnp.zeros_like(l_ref)
            dg_ref[...] = jnp.zeros_like(dg_ref)

        l_ref[...] += jnp.broadcast_to(lv, l_ref.shape)
        dg_ref[...] += dg

    row = pl.BlockSpec((tm, d), lambda i: (i, 0))
    res = pl.pallas_call(
        body, name="loss_head", grid=(t // tm,),
        in_specs=[row, row, _full_spec((1, d), 1)],
        out_specs=[_full_spec((1, LANES), 1), row, _full_spec((1, d), 1)],
        out_shape=[jax.ShapeDtypeStruct((1, LANES), F32), jax.ShapeDtypeStruct((t, d), F32),
                   jax.ShapeDtypeStruct((1, d), F32)],
        compiler_params=_cparams(1),
    )(x, target, gain)
    return res[0], res[1], res[2]


def s5_dlam(gr, gi, hr, hi, *, nb, s, cb=256, n=256):
    p = gr.shape[1]
    cb, n = min(cb, p), min(n, s)

    def body(gr_ref, gi_ref, hr_ref, hi_ref, dar_ref, dai_ref):
        @pl.when(pl.program_id(1) == 0)
        def _():
            dar_ref[...] = jnp.zeros_like(dar_ref)
            dai_ref[...] = jnp.zeros_like(dai_ref)

        def step(i, carry):
            t0 = pl.multiple_of(i * n, n)
            g_r, g_i = gr_ref[pl.ds(t0, n), :], gi_ref[pl.ds(t0, n), :]
            p_r = taps(_load_ext(hr_ref, t0, n), 2)[1]
            p_i = taps(_load_ext(hi_ref, t0, n), 2)[1]
            dar_ref[...] += jnp.sum(g_r * p_r + g_i * p_i, axis=0, keepdims=True)
            dai_ref[...] += jnp.sum(g_i * p_r - g_r * p_i, axis=0, keepdims=True)
            return carry

        lax.fori_loop(0, s // n, step, 0)

    blk = pl.BlockSpec((s, cb), lambda j, b: (b, j))
    acc = pl.BlockSpec((1, cb), lambda j, b: (0, j))
    res = pl.pallas_call(
        body, name="s5_dlam", grid=(p // cb, nb), in_specs=[blk] * 4, out_specs=[acc, acc],
        out_shape=[jax.ShapeDtypeStruct((1, p), F32)] * 2, compiler_params=_cparams(2),
    )(gr, gi, hr, hi)
    return res[0], res[1]


def s5_param_fwd(lr, li, ldt, btr, bti):
    def body(*refs):
        vals = _s5_param_f(*[r[...] for r in refs[:5]])
        for o, v in zip(refs[5:], vals):
            o[...] = v

    return pl.pallas_call(
        body, name="s5_param_fwd",
        out_shape=[jax.ShapeDtypeStruct(lr.shape, F32)] * 2 + [jax.ShapeDtypeStruct(btr.shape, F32)] * 2,
    )(lr, li, ldt, btr, bti)


def s5_param_bwd(lr, li, ldt, btr, bti, cots):
    def body(*refs):
        _, vjp = jax.vjp(_s5_param_f, *[r[...] for r in refs[:5]])
        gs = vjp(tuple(r[...] for r in refs[5:9]))
        for o, g in zip(refs[9:], gs):
            o[...] = g

    return pl.pallas_call(
        body, name="s5_param_bwd",
        out_shape=[jax.ShapeDtypeStruct(a.shape, F32) for a in (lr, li, ldt, btr, bti)],
    )(lr, li, ldt, btr, bti, *cots)


def _row_block(r, c, copies):
    lanes = -(-c // LANES) * LANES
    rb = 2048
    while rb > 2 * SUBLANES and (r % rb or copies * rb * lanes * 4 > 12 * 1024 * 1024):
        rb //= 2
    return rb if r % rb == 0 else r


def _as2d(a, lead=0):
    return a.reshape(a.shape[:lead] + (-1, a.shape[-1]))


def sumk(name, buf):
    b3 = _as2d(buf, 1)
    k, r, c = b3.shape
    rb = _row_block(r, c, 2 * (k + 1))

    def body(b_ref, o_ref):
        acc = b_ref[0]
        for i in range(1, k):
            acc = acc + b_ref[i]
        o_ref[...] = acc

    out = pl.pallas_call(
        body, name=name, grid=(r // rb,),
        in_specs=[pl.BlockSpec((k, rb, c), lambda i: (0, i, 0))],
        out_specs=pl.BlockSpec((rb, c), lambda i: (i, 0)),
        out_shape=jax.ShapeDtypeStruct((r, c), F32), compiler_params=_cparams(1),
    )(b3)
    return out.reshape(buf.shape[1:])


def adamw(name, w, g, m, v):
    shape = w.shape
    w, g, m, v = (_as2d(a) for a in (w, g, m, v))
    r, c = w.shape
    rb = _row_block(r, c, 14)

    def body(w_ref, g_ref, m_ref, v_ref, d_ref, nm_ref, nv_ref):
        gv = g_ref[...]
        nm = ADAM_B1 * m_ref[...] + (1.0 - ADAM_B1) * gv
        nv = ADAM_B2 * v_ref[...] + (1.0 - ADAM_B2) * (gv * gv)
        m_hat = nm / (1.0 - ADAM_B1 ** ADAM_STEP)
        v_hat = nv / (1.0 - ADAM_B2 ** ADAM_STEP)
        d_ref[...] = -ADAM_LR * (m_hat / (jnp.sqrt(v_hat) + ADAM_EPS) + ADAM_WD * w_ref[...])
        nm_ref[...] = nm
        nv_ref[...] = nv

    spec = pl.BlockSpec((rb, c), lambda i: (i, 0))
    res = pl.pallas_call(
        body, name=name, grid=(r // rb,), in_specs=[spec] * 4, out_specs=[spec] * 3,
        out_shape=[jax.ShapeDtypeStruct(w.shape, F32)] * 3, compiler_params=_cparams(1),
    )(w, g, m, v)
    return tuple(a.reshape(shape) for a in res)


def _place():
    x, y, c = lax.axis_index("x"), lax.axis_index("y"), lax.axis_index("c")
    chips = [(1 - x, y), (x, 1 - y), (1 - x, 1 - y)]
    return x, y, c, chips


def _chip_id(px, py):
    return 2 * px + py


def _rcopy(src, dst, send_sems, recv_sems, k, to):
    return pltpu.make_async_remote_copy(src_ref=src, dst_ref=dst, send_sem=send_sems.at[k], recv_sem=recv_sems.at[k],
                                        device_id=to, device_id_type=MESH)


def _comm_call(name, body, out_shapes, n_sems, args):
    return pl.pallas_call(
        body, name=name, out_shape=out_shapes, in_specs=[ANY] * len(args), out_specs=[ANY] * len(out_shapes),
        scratch_shapes=[pltpu.SemaphoreType.DMA((n_sems,)), pltpu.SemaphoreType.DMA((n_sems,))],
    )(*args)


def _finish(remote):
    for cp in remote:
        cp.wait_send()


def _set_slot(buf, slot, val):
    if val.ndim < buf.ndim:
        val = val[None]
    return lax.dynamic_update_slice(buf, val.astype(buf.dtype), (slot,) + (0,) * (buf.ndim - 1))


def gather_chips(name, arrays):
    n = len(arrays)
    halves = [a.shape[0] // 2 for a in arrays]

    def body(*refs):
        ins, outs = refs[:n], refs[n:2 * n]
        send_sems, recv_sems = refs[2 * n:]
        x, y, c, chips = _place()
        me, sib = _chip_id(x, y), (x, y, 1 - c)
        half = [pl.ds(c * h, h) for h in halves]
        other = [pl.ds((1 - c) * h, h) for h in halves]
        remote = []
        for a in range(n):
            for j, chip in enumerate(chips):
                remote.append(_rcopy(ins[a].at[half[a]], outs[a].at[me, half[a]], send_sems, recv_sems, 6 * a + j,
                                     (*chip, c)))
                remote[-1].start()
        for a in range(n):
            for j, chip in enumerate(chips):
                landed = outs[a].at[_chip_id(*chip), half[a]]
                _rcopy(ins[a].at[half[a]], landed, send_sems, recv_sems, 6 * a + j, sib).wait_recv()
                remote.append(_rcopy(landed, landed, send_sems, recv_sems, 6 * a + 3 + j, sib))
                remote[-1].start()
        for a in range(n):
            for j, chip in enumerate(chips):
                _rcopy(ins[a].at[half[a]], outs[a].at[_chip_id(*chip), other[a]], send_sems, recv_sems, 6 * a + 3 + j,
                       sib).wait_recv()
        _finish(remote)

    shapes = [jax.ShapeDtypeStruct((N_CHIPS,) + a.shape, a.dtype) for a in arrays]
    return _comm_call(name, body, shapes, 6 * n, arrays)


def swap_halves(name, arrays):
    n = len(arrays)

    def body(*refs):
        ins, outs = refs[:n], refs[n:2 * n]
        send_sems, recv_sems = refs[2 * n:]
        x, y, c, _ = _place()
        sib = (x, y, 1 - c)
        remote = []
        for a in range(n):
            for k in range(N_CHIPS):
                remote.append(_rcopy(ins[a].at[1 - c, k], outs[a].at[k], send_sems, recv_sems, N_CHIPS * a + k, sib))
                remote[-1].start()
        for a in range(n):
            for k in range(N_CHIPS):
                _rcopy(ins[a].at[1 - c, k], outs[a].at[k], send_sems, recv_sems, N_CHIPS * a + k, sib).wait_recv()
        _finish(remote)

    shapes = [jax.ShapeDtypeStruct(a.shape[1:], a.dtype) for a in arrays]
    return _comm_call(name, body, shapes, N_CHIPS * n, arrays)


def scatter_chips(name, arrays):
    n = len(arrays)

    def body(*refs):
        ins, outs = refs[:n], refs[n:2 * n]
        send_sems, recv_sems = refs[2 * n:]
        x, y, c, chips = _place()
        remote = []
        for a in range(n):
            for j, chip in enumerate(chips):
                remote.append(_rcopy(ins[a].at[_chip_id(*chip)], outs[a].at[j], send_sems, recv_sems, 3 * a + j,
                                     (*chip, c)))
                remote[-1].start()
        for a in range(n):
            for j, chip in enumerate(chips):
                _rcopy(ins[a].at[0], outs[a].at[j], send_sems, recv_sems, 3 * a + j, (*chip, c)).wait_recv()
        _finish(remote)

    shapes = [jax.ShapeDtypeStruct((3,) + a.shape[1:], a.dtype) for a in arrays]
    return _comm_call(name, body, shapes, 3 * n, arrays)


def join_halves(name, arrays):
    n = len(arrays)
    offs = [sum(a.shape[0] for a in arrays[:i]) for i in range(n)]

    def body(*refs):
        ins, outs = refs[:n], refs[n:2 * n]
        send_sems, recv_sems = refs[2 * n:]
        x, y, c, _ = _place()
        sib = (x, y, 1 - c)
        remote = []
        for a in range(n):
            h = arrays[a].shape[0]
            for k in range(h):
                remote.append(_rcopy(ins[a].at[k], outs[a].at[c * h + k], send_sems, recv_sems, offs[a] + k, sib))
                remote[-1].start()
        for a in range(n):
            h = arrays[a].shape[0]
            for k in range(h):
                _rcopy(ins[a].at[k], outs[a].at[(1 - c) * h + k], send_sems, recv_sems, offs[a] + k, sib).wait_recv()
        _finish(remote)

    shapes = [jax.ShapeDtypeStruct((2 * a.shape[0],) + a.shape[1:], a.dtype) for a in arrays]
    return _comm_call(name, body, shapes, offs[-1] + arrays[-1].shape[0], arrays)


def add_pair(name, mine2, other, c):
    a3, o2 = _as2d(mine2, 1), _as2d(other)
    r, cols = o2.shape
    rb = _row_block(r, cols, 8)

    def body(c_ref, a_ref, o_ref, out_ref, out16_ref):
        tot = a_ref[0] + o_ref[...]
        out_ref[...] = tot
        out16_ref[...] = tot.astype(BF16)

    spec = pl.BlockSpec((rb, cols), lambda i, s: (i, 0))
    out, out16 = pl.pallas_call(
        body, name=name, out_shape=[jax.ShapeDtypeStruct((r, cols), F32), jax.ShapeDtypeStruct((r, cols), BF16)],
        grid_spec=pltpu.PrefetchScalarGridSpec(
            num_scalar_prefetch=1, grid=(r // rb,),
            in_specs=[pl.BlockSpec((1, rb, cols), lambda i, s: (s[0], i, 0)), spec], out_specs=[spec, spec]),
        compiler_params=_cparams(1),
    )(jnp.reshape(c, (1,)).astype(jnp.int32), a3, o2)
    return out.reshape(other.shape), out16.reshape(other.shape)


def add_chips(name, part, recv, me):
    p3, r3 = _as2d(part, 1), _as2d(recv, 1)
    _, r, cols = p3.shape
    rb = _row_block(r, cols, 10)

    def body(me_ref, p_ref, r_ref, out_ref):
        own, acc = p_ref[0], None
        got = [r_ref[j].astype(F32) for j in range(3)]
        for k in range(N_CHIPS):
            rel = jnp.full(own.shape, me_ref[0] ^ k, jnp.int32)
            term = jnp.where(rel == 0, own, jnp.where(rel == 2, got[0], jnp.where(rel == 1, got[1], got[2])))
            acc = term if acc is None else acc + term
        out_ref[...] = acc

    out = pl.pallas_call(
        body, name=name, out_shape=jax.ShapeDtypeStruct((r, cols), F32),
        grid_spec=pltpu.PrefetchScalarGridSpec(
            num_scalar_prefetch=1, grid=(r // rb,),
            in_specs=[pl.BlockSpec((1, rb, cols), lambda i, s: (s[0], i, 0)),
                      pl.BlockSpec((3, rb, cols), lambda i, s: (0, i, 0))],
            out_specs=pl.BlockSpec((rb, cols), lambda i, s: (i, 0))),
        compiler_params=_cparams(1),
    )(jnp.reshape(me, (1,)).astype(jnp.int32), p3, r3)
    return out.reshape(part.shape[1:])


def gather_all(name, slab):
    def body(x_ref, out_ref, send_sems, recv_sems):
        x, y, c, chips = _place()
        me, sib = (x, y, c), (x, y, 1 - c)

        def rows(px, py, pc):
            return out_ref.at[4 * px + 2 * py + pc]

        first = [_rcopy(x_ref, rows(*me), send_sems, recv_sems, 0, sib)]
        first += [_rcopy(x_ref, rows(*me), send_sems, recv_sems, 1 + j, (*chip, c)) for j, chip in enumerate(chips)]
        for cp in first:
            cp.start()
        passed = []
        for j, chip in enumerate(chips):
            landed = rows(*chip, c)
            _rcopy(x_ref, landed, send_sems, recv_sems, 1 + j, sib).wait_recv()
            passed.append(_rcopy(landed, landed, send_sems, recv_sems, 4 + j, sib))
            passed[-1].start()
        _rcopy(x_ref, rows(*sib), send_sems, recv_sems, 0, sib).wait_recv()
        for j, chip in enumerate(chips):
            _rcopy(x_ref, rows(*chip, 1 - c), send_sems, recv_sems, 4 + j, sib).wait_recv()
        _finish(first + passed)

    return _comm_call(name, body, [jax.ShapeDtypeStruct((8,) + slab.shape, slab.dtype)], 7, [slab])[0]


D, BW, NH_B, NG, NP_S5, CG = 1024, 512, 4, 32, 64, 16
P_S5 = NG * NP_S5
C_GATES, C_XA, C_GA, C_Q, C_K, C_V, C_Z, C_UC, C_BA, N_PROJ = 0, 3072, 3584, 4096, 4608, 5120, 5632, 6144, 6656, 7168
_IN_OFFS = (0, 512, 1024, 1536, 2048, 2560, 3072, 3076, 3080, 3592, 6664)


def _blockdiag2(w):
    z = jnp.zeros((w.shape[0] // 2, 2 * w.shape[1], 2 * w.shape[2]), w.dtype)
    return z.at[:, :64, :64].set(w[0::2]).at[:, 64:, 64:].set(w[1::2])


def _unblockdiag2(d):
    return jnp.stack([d[:, :64, :64], d[:, 64:, 64:]], axis=1).reshape(-1, 64, 64)


def _expand(t):
    eye = jnp.eye(t.shape[0], dtype=t.dtype)
    return (t[:, :, None, :] * eye[:, None, :, None]).reshape(t.shape[0] * t.shape[1], -1)


def _layer_params(w):
    p = {}
    wi = w["w_in"]
    xa, ga, q, k, v, z, beta, alpha, uc, gates = [wi[:, a:b] for a, b in zip(_IN_OFFS[:-1], _IN_OFFS[1:])]
    ba = jnp.zeros((D, NH_B, LANES), wi.dtype).at[:, :, 0].set(beta).at[:, :, 1].set(alpha).reshape(D, NH_B * LANES)
    p["w_in_pad"] = jnp.concatenate([gates, xa, ga, q, k, v, z, uc, ba], axis=1)
    row = lambda a: a.reshape(1, -1).astype(F32)
    acw, bcw, fcw = w["a_conv_w"], w["b_conv_w"], w["ffn_conv_w"]
    p["a_taps"] = [row(acw[i]) for i in range(4)]
    p["a_cb"], p["a_br"], p["a_bi"], p["a_lam"] = (row(w[n]) for n in ("a_conv_b", "a_b_r", "a_b_i", "a_lam"))
    p["a_wr"], p["a_wi"] = _blockdiag2(w["a_w_r"]), _blockdiag2(w["a_w_i"])
    p["b_taps"] = [row(bcw[i, j * BW:(j + 1) * BW]) for j in range(3) for i in range(4)]
    p["b_alog"] = row(jnp.repeat(w["b_a_log"], LANES))
    p["b_dtb"] = row(jnp.repeat(w["b_dt_bias"], LANES))
    p["b_nw"] = row(w["b_norm"])
    p["c_lr"], p["c_li"] = w["c_lam_re"][:, None, :], w["c_lam_im"][:, None, :]
    p["c_ldt"] = w["c_log_dt"][:, None, None]
    p["c_btr"], p["c_bti"] = w["c_b_re"].transpose(0, 2, 1), w["c_b_im"].transpose(0, 2, 1)
    p["c_cre"] = _expand(w["c_c_re"].transpose(0, 2, 1))
    p["c_cimn"] = _expand(-w["c_c_im"].transpose(0, 2, 1))
    p["c_d"], p["c_glu_b"] = row(w["c_d"]), row(w["c_glu_b"])
    p["c_glu_w"] = w["c_glu_w"].astype(F32)
    p["wb"] = [w["w_branch"][k].astype(F32) for k in range(3)]
    p["bg"] = [row(w["b_gate"][k * D:(k + 1) * D]) for k in range(3)]
    for n in ("w_out", "xa_w_q", "xa_w_kv", "xa_w_o", "ffn_w_up", "ffn_w_down"):
        p[n] = w[n]
    for n in ("mix_norm", "xa_norm", "mem_norm", "ffn_norm"):
        p[n] = row(w[n])
    dff = fcw.shape[1] // 2
    p["f_taps"] = [row(fcw[i, :dff]) for i in range(3)] + [row(fcw[i, dff:]) for i in range(3)]
    p["f_b"] = [row(w["ffn_conv_b"][:dff]), row(w["ffn_conv_b"][dff:])]
    return p


def _chan(arr, cb=LANES):
    return (arr, (1, cb), lambda j: (0, j))


def _a_params(p):
    blk = lambda a: (a, (1, LANES, LANES), lambda j: (j, 0, 0))
    return [_chan(t) for t in p["a_taps"]] + [_chan(p["a_cb"]), blk(p["a_wr"]), blk(p["a_wi"]),
                                              _chan(p["a_br"]), _chan(p["a_bi"]), _chan(p["a_lam"])]


def _f_params(p):
    return [_chan(t) for t in p["f_taps"]] + [_chan(b) for b in p["f_b"]]


def _layer_fwd(x, mem, p, nb, s, m):
    sv = {"x": x}
    h = rowop_fwd("norm_mix", _norm_f, [(x, 0, D)], [p["mix_norm"]], [(D, BF16)])[0]
    proj = mm("in_proj", h, p["w_in_pad"], "nn")
    a, b = seqop_fwd("rglru_pre", _rglru_pre_f, [(proj, C_XA)], _a_params(p), [F32, F32], nb=nb, s=s, nblk=BW // LANES)
    ha = rscan("rglru_scan", a, b, False, nb=nb, s=s)
    yb, states = deltanet_fwd(proj, (C_Q, C_K, C_V, C_Z, C_BA), p["b_taps"], p["b_alog"], p["b_dtb"], p["b_nw"],
                              nb=nb, s=s, nh=NH_B)
    ar, ai, bbr, bbi = s5_param_fwd(p["c_lr"], p["c_li"], p["c_ldt"], p["c_btr"], p["c_bti"])
    bd = jnp.concatenate([_expand(bbr), _expand(bbi)], axis=1)
    ar2, ai2 = ar.reshape(1, P_S5), ai.reshape(1, P_S5)
    s5b = dict(nblk=BW // LANES, bk=LANES, bn=P_S5 * LANES // BW)
    s5c = dict(nblk=BW // LANES, bk=P_S5 * LANES // BW, bn=LANES)
    bu = mm_bd("s5_bu", proj, bd, "nn", a_col0=C_UC, **s5b)
    hr, hi = cscan("s5_scan", bu, ar2, ai2, False, nb=nb, s=s)
    ys = mm_bd("s5_y_im", hi, p["c_cimn"], "nn", res=mm_bd("s5_y_re", hr, p["c_cre"], "nn", **s5c), **s5c)
    yc = rowop_fwd("s5_post", _s5_post_f, [(ys, 0, BW), (proj, C_UC, BW)], [p["c_d"], p["c_glu_w"], p["c_glu_b"]],
                   [(BW, F32)])[0]
    merge_rows = [(ha, 0, BW), (proj, C_GA, BW), (yb, 0, BW), (yc, 0, BW),
                  (proj, 0, D), (proj, D, D), (proj, 2 * D, D)]
    merged = rowop_fwd("merge", _merge_f, merge_rows, p["wb"] + p["bg"], [(D, BF16)])[0]
    x1 = mm("out_proj", merged, p["w_out"], "nn", res=x)
    hq = rowop_fwd("norm_xa", _norm_f, [(x1, 0, D)], [p["xa_norm"]], [(D, BF16)])[0]
    q = mm("xa_q", hq, p["xa_w_q"], "nn", out_dtype=BF16)
    memn = rowop_fwd("norm_mem", _norm_f, [(mem, 0, D)], [p["mem_norm"]], [(D, BF16)])[0]
    kv = mm("xa_kv", memn, p["xa_w_kv"], "nn")
    o = attn_fwd(q, kv, nb=nb, s=s, m=m, nh=4)
    x2 = mm("xa_o", o, p["xa_w_o"], "nn", res=x1)
    hf = rowop_fwd("norm_ffn", _norm_f, [(x2, 0, D)], [p["ffn_norm"]], [(D, BF16)])[0]
    up = mm("ffn_up", hf, p["ffn_w_up"], "nn")
    dff = up.shape[1] // 2
    act = seqop_fwd("ffn_act", _ffn_act_f, [(up, 0), (up, dff)], _f_params(p), [BF16], nb=nb, s=s,
                    nblk=dff // LANES)[0]
    x3 = mm("ffn_down", act, p["ffn_w_down"], "nn", res=x2)
    sv.update(h=h, proj=proj, a=a, ha=ha, yb=yb, states=states, ar=ar2, ai=ai2, bd=bd, hr=hr, hi=hi, ys=ys, yc=yc,
              merged=merged, x1=x1, hq=hq, q=q, memn=memn, kv=kv, o=o, x2=x2, hf=hf, up=up, act=act)
    return x3, sv


def _layer_bwd(dx3, mem, sv, p, nb, s, m):
    g = {}
    up, dff = sv["up"], sv["up"].shape[1] // 2
    dact = mm("d_ffn_down_x", dx3, p["ffn_w_down"], "nt")
    g["ffn_w_down"] = mm("d_ffn_down_w", sv["act"], dx3, "tn")
    (dug, duv), dfp = seqop_bwd("ffn_act_bwd", _ffn_act_f, [(up, 0), (up, dff)], _f_params(p), [(dact, 0)],
                                lambda t0, e: (e[SUBLANES:],), nb=nb, s=s, nblk=dff // LANES)
    dhf = mm("d_ffn_up_x1", duv, p["ffn_w_up"], "nt", b_cols=(dff, dff),
             res=mm("d_ffn_up_x0", dug, p["ffn_w_up"], "nt", b_cols=(0, dff)))
    g["ffn_w_up"] = jnp.concatenate([mm("d_ffn_up_w0", sv["hf"], dug, "tn"), mm("d_ffn_up_w1", sv["hf"], duv, "tn")],
                                    axis=1)
    g["ffn_conv_w"] = jnp.concatenate([jnp.concatenate(dfp[0:3], axis=0), jnp.concatenate(dfp[3:6], axis=0)], axis=1)
    g["ffn_conv_b"] = jnp.concatenate([dfp[6], dfp[7]], axis=1)[0]
    (dx2,), (gn,) = rowop_bwd("norm_ffn_bwd", _norm_f, [(sv["x2"], 0, D)], [p["ffn_norm"]], [(dhf, 0, D)], [True],
                              add=(dx3, 0, D))
    g["ffn_norm"] = gn[0]
    do = mm("d_xa_o_x", dx2, p["xa_w_o"], "nt")
    g["xa_w_o"] = mm("d_xa_o_w", sv["o"], dx2, "tn")
    dq, dk, dv = attn_bwd(sv["q"], sv["kv"], do, nb=nb, s=s, m=m, nh=4)
    dkv = jnp.concatenate([dk, dv], axis=1)
    dhq = mm("d_xa_q_x", dq, p["xa_w_q"], "nt")
    g["xa_w_q"] = mm("d_xa_q_w", sv["hq"], dq, "tn")
    dmemn = mm("d_xa_kv_x", dkv, p["xa_w_kv"], "nt")
    g["xa_w_kv"] = mm("d_xa_kv_w", sv["memn"], dkv, "tn")
    (dx1,), (gn,) = rowop_bwd("norm_xa_bwd", _norm_f, [(sv["x1"], 0, D)], [p["xa_norm"]], [(dhq, 0, D)], [True],
                              add=(dx2, 0, D))
    g["xa_norm"] = gn[0]
    _, (gn,) = rowop_bwd("norm_mem_bwd", _norm_f, [(mem, 0, D)], [p["mem_norm"]], [(dmemn, 0, D)], [False])
    g["mem_norm"] = gn[0]
    proj = sv["proj"]
    dmerged = mm("d_out_proj_x", dx1, p["w_out"], "nt")
    g["w_out"] = mm("d_out_proj_w", sv["merged"], dx1, "tn")
    merge_rows = [(sv["ha"], 0, BW), (proj, C_GA, BW), (sv["yb"], 0, BW), (sv["yc"], 0, BW),
                  (proj, 0, D), (proj, D, D), (proj, 2 * D, D)]
    (dha, dga, dyb, dyc, dg0, dg1, dg2), dmp = rowop_bwd("merge_bwd", _merge_f, merge_rows, p["wb"] + p["bg"],
                                                          [(dmerged, 0, D)], [True] * 7)
    g["w_branch"] = jnp.stack(dmp[0:3])
    g["b_gate"] = jnp.concatenate(dmp[3:6], axis=1)[0]
    (dys, duc), (gd, ggw, ggb) = rowop_bwd("s5_post_bwd", _s5_post_f, [(sv["ys"], 0, BW), (proj, C_UC, BW)],
                                           [p["c_d"], p["c_glu_w"], p["c_glu_b"]], [(dyc, 0, BW)], [True, True])
    g["c_d"], g["c_glu_w"], g["c_glu_b"] = gd[0], ggw, ggb[0]
    cd = jnp.concatenate([p["c_cre"], p["c_cimn"]], axis=0)
    s5b = dict(nblk=BW // LANES, bk=LANES, bn=P_S5 * LANES // BW)
    s5c = dict(nblk=BW // LANES, bk=P_S5 * LANES // BW, bn=LANES)
    dhs = mm_bd("d_s5_y_x", dys, cd, "nt", **s5c)
    dcre = mm_bd("d_s5_y_wre", sv["hr"], dys, "tn", **s5c)
    dcimn = mm_bd("d_s5_y_wim", sv["hi"], dys, "tn", **s5c)
    gr, gi = cscan("s5_scan_bwd", dhs, sv["ar"], sv["ai"], True, nb=nb, s=s)
    dar, dai = s5_dlam(gr, gi, sv["hr"], sv["hi"], nb=nb, s=s)
    duc = mm_bd("d_s5_bu_x1", gi, sv["bd"], "nt", b_col0=P_S5,
                res=mm_bd("d_s5_bu_x0", gr, sv["bd"], "nt", res=duc, **s5b), **s5b)
    dbd_re = mm_bd("d_s5_bu_wre", proj, gr, "tn", a_col0=C_UC, **s5b)
    dbd_im = mm_bd("d_s5_bu_wim", proj, gi, "tn", a_col0=C_UC, **s5b)
    gpb = NG * LANES // BW
    eye = jnp.eye(gpb, dtype=F32)
    pick = lambda dmat, r, c: jnp.einsum("jgrhc,gh->jgrc", dmat.reshape(BW // LANES, gpb, r, gpb, c),
                                         eye).reshape(NG, r, c)
    glr, gli, gldt, gbtr, gbti = s5_param_bwd(
        p["c_lr"], p["c_li"], p["c_ldt"], p["c_btr"], p["c_bti"],
        (dar.reshape(NG, 1, NP_S5), dai.reshape(NG, 1, NP_S5), pick(dbd_re, CG, NP_S5), pick(dbd_im, CG, NP_S5)))
    g["c_lam_re"], g["c_lam_im"], g["c_log_dt"] = glr[:, 0, :], gli[:, 0, :], gldt[:, 0, 0]
    g["c_b_re"], g["c_b_im"] = gbtr.transpose(0, 2, 1), gbti.transpose(0, 2, 1)
    g["c_c_re"] = pick(dcre, NP_S5, CG).transpose(0, 2, 1)
    g["c_c_im"] = -pick(dcimn, NP_S5, CG).transpose(0, 2, 1)
    (dq_b, dk_b, dv_b, dz_b, dba), dbp = deltanet_bwd(proj, (C_Q, C_K, C_V, C_Z, C_BA), p["b_taps"], p["b_alog"],
                                                      p["b_dtb"], p["b_nw"], sv["states"], dyb, nb=nb, s=s, nh=NH_B)
    g["b_conv_w"] = jnp.concatenate([jnp.concatenate(dbp[4 * j:4 * j + 4], axis=0) for j in range(3)], axis=1)
    g["b_a_log"] = dbp[12].reshape(NH_B, LANES)[:, 0]
    g["b_dt_bias"] = dbp[13].reshape(NH_B, LANES)[:, 0]
    g["b_norm"] = dbp[14][0]
    gsc = rscan("rglru_scan_bwd", sv["a"], dha, True, nb=nb, s=s)
    (dxa,), dap = seqop_bwd("rglru_pre_bwd", _rglru_pre_f, [(proj, C_XA)], _a_params(p), [(gsc, 0), (sv["ha"], 0)],
                            lambda t0, ge, he: (ge[SUBLANES:] * taps(he, 2)[1], ge[SUBLANES:]),
                            nb=nb, s=s, nblk=BW // LANES)
    g["a_conv_w"] = jnp.concatenate(dap[0:4], axis=0)
    g["a_conv_b"], g["a_b_r"], g["a_b_i"], g["a_lam"] = dap[4][0], dap[7][0], dap[8][0], dap[9][0]
    g["a_w_r"], g["a_w_i"] = _unblockdiag2(dap[5]), _unblockdiag2(dap[6])
    dproj = jnp.concatenate([dg0, dg1, dg2, dxa, dga, dq_b, dk_b, dv_b, dz_b, duc, dba], axis=1).astype(BF16)
    dh = mm("d_in_proj_x", dproj, p["w_in_pad"], "nt")
    dwp = mm("d_in_proj_w", sv["h"], dproj, "tn")
    dba_w = dwp[:, C_BA:].reshape(D, NH_B, LANES)
    pieces = [dwp[:, c0:c0 + BW] for c0 in (C_XA, C_GA, C_Q, C_K, C_V, C_Z)]
    g["w_in"] = jnp.concatenate(pieces + [dba_w[:, :, 0], dba_w[:, :, 1], dwp[:, C_UC:C_UC + BW], dwp[:, :3 * D]],
                                axis=1)
    (dx,), (gn,) = rowop_bwd("norm_mix_bwd", _norm_f, [(sv["x"], 0, D)], [p["mix_norm"]], [(dh, 0, D)], [True],
                             add=(dx1, 0, D))
    g["mix_norm"] = gn[0]
    return dx, g


def _local_step(x, mem, target, w):
    nb, s, _ = x.shape
    m = mem.shape[1]
    depth = w["mix_norm"].shape[0]
    x2d, mem2d = x.reshape(nb * s, D), mem.reshape(nb * m, D)
    stacked = jax.vmap(_layer_params)({n: a for n, a in w.items() if n != "final_norm"})
    params, saved = [], []
    for l in range(depth):
        params.append(jax.tree.map(lambda a: a[l], stacked))
        x2d, sv = _layer_fwd(x2d, mem2d, params[l], nb, s, m)
        saved.append(sv)
    loss_row, dx, gfn = loss_head(x2d, target.reshape(nb * s, D), w["final_norm"].reshape(1, D))
    grads = [None] * depth
    for l in reversed(range(depth)):
        dx, grads[l] = _layer_bwd(dx, mem2d, saved[l], params[l], nb, s, m)
    grads.append({"final_norm": gfn[0]})
    return loss_row, dx.reshape(x.shape), grads


def _stack_layers(grads, name):
    if name == "final_norm":
        return grads[-1][name]
    return jnp.stack([g[name] for g in grads[:-1]])


SLAB_UNIT = 2 * 1024 * LANES


def _slab(parts, dtype, unit):
    flat = jnp.concatenate([a.astype(dtype).reshape(-1) for a in parts])
    pad = (-flat.shape[0]) % unit
    return jnp.pad(flat, (0, pad)) if pad else flat


def _unslab(flat, like):
    out, off = [], 0
    for a in like:
        out.append(flat[off:off + a.size].reshape(a.shape))
        off += a.size
    return out


def _gather_weights(name, local, names_axes, dtype):
    mine = [local[n].astype(dtype) for n, _ in names_axes]
    me = _chip_id(lax.axis_index("x"), lax.axis_index("y"))
    got = [_set_slot(g, me, a) for g, a in zip(gather_chips(name, mine), mine)]
    return {n: jnp.concatenate([g[j] for j in range(N_CHIPS)], axis=ax) for (n, ax), g in zip(names_axes, got)}


def kernel(x, mem, mix_norm, w_in, b_gate, a_conv_w, a_conv_b, a_w_r, a_b_r, a_w_i, a_b_i, a_lam, b_conv_w, b_a_log, b_dt_bias, b_norm, c_lam_re, c_lam_im, c_log_dt, c_b_re, c_b_im, c_c_re, c_c_im, c_d, c_glu_w, c_glu_b, w_branch, w_out, xa_norm, mem_norm, xa_w_q, xa_w_kv, xa_w_o, ffn_norm, ffn_w_up, ffn_conv_w, ffn_conv_b, ffn_w_down, final_norm, loss_target, m_mix_norm, m_w_in, m_b_gate, m_a_conv_w, m_a_conv_b, m_a_w_r, m_a_b_r, m_a_w_i, m_a_b_i, m_a_lam, m_b_conv_w, m_b_a_log, m_b_dt_bias, m_b_norm, m_c_lam_re, m_c_lam_im, m_c_log_dt, m_c_b_re, m_c_b_im, m_c_c_re, m_c_c_im, m_c_d, m_c_glu_w, m_c_glu_b, m_w_branch, m_w_out, m_xa_norm, m_mem_norm, m_xa_w_q, m_xa_w_kv, m_xa_w_o, m_ffn_norm, m_ffn_w_up, m_ffn_conv_w, m_ffn_conv_b, m_ffn_w_down, m_final_norm, v_mix_norm, v_w_in, v_b_gate, v_a_conv_w, v_a_conv_b, v_a_w_r, v_a_b_r, v_a_w_i, v_a_b_i, v_a_lam, v_b_conv_w, v_b_a_log, v_b_dt_bias, v_b_norm, v_c_lam_re, v_c_lam_im, v_c_log_dt, v_c_b_re, v_c_b_im, v_c_c_re, v_c_c_im, v_c_d, v_c_glu_w, v_c_glu_b, v_w_branch, v_w_out, v_xa_norm, v_mem_norm, v_xa_w_q, v_xa_w_kv, v_xa_w_o, v_ffn_norm, v_ffn_w_up, v_ffn_conv_w, v_ffn_conv_b, v_ffn_w_down, v_final_norm):
    env = dict(locals())
    w = {n: env[n] for n in WEIGHTS}
    mom = {n: env["m_" + n] for n in WEIGHTS}
    var = {n: env["v_" + n] for n in WEIGHTS}

    full = dict(w)
    full.update(_gather_weights("gather_mats", w, SHARDED_MATS, BF16))
    full.update(_gather_weights("gather_convs", w, SHARDED_CONVS, F32))
    loss_row, grad_x, glayers = _local_step(x, mem, loss_target, full)

    depth = mix_norm.shape[0]
    assert depth % 2 == 0
    hl = depth // 2
    res = {}
    by_chip = []
    for n, ax in SHARDED:
        width = w[n].shape[ax]
        by_chip.append(jnp.stack([jnp.stack([jnp.stack([
            lax.slice_in_dim(glayers[l][n], j * width, (j + 1) * width, axis=ax - 1)
            for l in range(hf * hl, (hf + 1) * hl)]) for j in range(N_CHIPS)]) for hf in range(2)]))
    core = lax.axis_index("c")
    chip = _chip_id(lax.axis_index("x"), lax.axis_index("y"))
    sibs = swap_halves("swap_halves", by_chip)
    part = [add_pair("add_cores_" + n, g2, sb, core) for (n, _), g2, sb in zip(SHARDED, by_chip, sibs)]
    recv = scatter_chips("scatter_chips", [p16 for _, p16 in part])
    mine = [add_chips("add_chips_" + n, p32, r, chip) for (n, _), (p32, _), r in zip(SHARDED, part, recv)]
    for (n, _), own, joined in zip(SHARDED, mine, join_halves("join_halves", mine)):
        g = _set_slot(joined, core * hl, own)
        res["grad_" + n] = g
        res["delta_" + n], res["new_m_" + n], res["new_v_" + n] = adamw("adamw_" + n, w[n], g, mom[n], var[n])

    unit = SUBLANES * LANES
    small = _slab([_stack_layers(glayers, n) for n in REPLICATED] + [loss_row[:, :1]], F32, unit).reshape(-1, LANES)
    g_rp = sumk("add_devices", _set_slot(gather_all("gather_all", small), 2 * chip + core, small))
    slab1 = lambda d: _slab([d[n] for n in REPLICATED] + [jnp.zeros((1, 1), F32)], F32, unit).reshape(-1, LANES)
    d_rp, m_rp, v_rp = adamw("adamw_replicated", slab1(w), g_rp, slab1(mom), slab1(var))
    like_rp = [w[n] for n in REPLICATED] + [jnp.zeros((1, 1), F32)]
    for kind, rp in (("grad", g_rp), ("delta", d_rp), ("new_m", m_rp), ("new_v", v_rp)):
        for n, a in zip(list(REPLICATED) + ["loss"], _unslab(rp.reshape(-1), like_rp)):
            res[kind + "_" + n] = a
    loss = res["grad_loss"].reshape(())
    return (loss, grad_x, *[res["grad_" + n] for n in WEIGHTS], *[res["delta_" + n] for n in WEIGHTS],
            *[res["new_m_" + n] for n in WEIGHTS], *[res["new_v_" + n] for n in WEIGHTS])
```

```python
import functools
import math

import jax
import jax.numpy as jnp
from jax import lax
from jax.experimental import pallas as pl
from jax.experimental.pallas import tpu as pltpu

F32, BF16 = jnp.float32, jnp.bfloat16
MESH = pl.DeviceIdType.MESH
ANY = pl.BlockSpec(memory_space=pl.ANY)

VMEM_LIMIT_V7X = 56 * 1024 * 1024
MM_TILE = 1024
LANES, SUBLANES = 128, 8
EPS = 1e-6
RG_C = 8.0
DN_CHUNK = 64
N_CHIPS = 4
ADAM_LR, ADAM_B1, ADAM_B2, ADAM_EPS, ADAM_WD, ADAM_STEP = 0.001, 0.9, 0.999, 1e-08, 0.01, 10

SHARDED_MATS = (("w_in", 2), ("c_glu_w", 1), ("w_branch", 3), ("w_out", 1), ("xa_w_q", 1), ("xa_w_kv", 2),
                ("xa_w_o", 1), ("ffn_w_up", 2), ("ffn_w_down", 1))
SHARDED_CONVS = (("a_conv_w", 2), ("b_conv_w", 2), ("ffn_conv_w", 2))
SHARDED = SHARDED_MATS + SHARDED_CONVS
WEIGHTS = ("mix_norm", "w_in", "b_gate", "a_conv_w", "a_conv_b", "a_w_r", "a_b_r", "a_w_i", "a_b_i", "a_lam",
           "b_conv_w", "b_a_log", "b_dt_bias", "b_norm", "c_lam_re", "c_lam_im", "c_log_dt", "c_b_re", "c_b_im",
           "c_c_re", "c_c_im", "c_d", "c_glu_w", "c_glu_b", "w_branch", "w_out", "xa_norm", "mem_norm", "xa_w_q",
           "xa_w_kv", "xa_w_o", "ffn_norm", "ffn_w_up", "ffn_conv_w", "ffn_conv_b", "ffn_w_down", "final_norm")
REPLICATED = tuple(n for n in WEIGHTS if n not in dict(SHARDED))


def _cparams(n_axes):
    return pltpu.CompilerParams(dimension_semantics=("arbitrary",) * n_axes, vmem_limit_bytes=VMEM_LIMIT_V7X)


def _tile(n, pref, off=0):
    t = pref
    while t >= LANES:
        if n % t == 0 and off % t == 0:
            return t
        t //= 2
    assert off == 0, (n, pref, off)
    return n


def _full_spec(shape, n_grid):
    nd = len(shape)
    if n_grid == 1:
        return pl.BlockSpec(shape, lambda i: (0,) * nd)
    return pl.BlockSpec(shape, lambda i, j: (0,) * nd)


_NN, _NT, _TN = ((1,), (0,)), ((1,), (1,)), ((0,), (0,))


def _dot(a, b, dims, hp):
    if hp:
        return lax.dot_general(a, b, (dims, ((), ())), precision=lax.Precision.HIGH, preferred_element_type=F32)
    return lax.dot_general(a.astype(BF16), b.astype(BF16), (dims, ((), ())), preferred_element_type=F32)


@functools.partial(jax.custom_vjp, nondiff_argnums=(2,))
def mm_nn(a, b, hp=False):
    return _dot(a, b, _NN, hp)


@functools.partial(jax.custom_vjp, nondiff_argnums=(2,))
def mm_nt(a, b, hp=False):
    return _dot(a, b, _NT, hp)


@functools.partial(jax.custom_vjp, nondiff_argnums=(2,))
def mm_tn(a, b, hp=False):
    return _dot(a, b, _TN, hp)


mm_nn.defvjp(lambda a, b, hp: (_dot(a, b, _NN, hp), (a, b)),
             lambda hp, r, g: (mm_nt(g, r[1], hp), mm_tn(r[0], g, hp)))
mm_nt.defvjp(lambda a, b, hp: (_dot(a, b, _NT, hp), (a, b)),
             lambda hp, r, g: (mm_nn(g, r[1], hp), mm_tn(g, r[0], hp)))
mm_tn.defvjp(lambda a, b, hp: (_dot(a, b, _TN, hp), (a, b)),
             lambda hp, r, g: (mm_nt(r[1], g, hp), mm_nn(r[0], g, hp)))


@functools.partial(jax.custom_vjp, nondiff_argnums=(1,))
def taps(xext, k):
    return tuple((xext if s == 0 else pltpu.roll(xext, s, 0))[SUBLANES:] for s in range(k))


def _taps_fwd(xext, k):
    return taps(xext, k), None


def _taps_bwd(k, _, gs):
    tot = None
    for s, g in enumerate(gs):
        gp = jnp.concatenate([jnp.zeros((SUBLANES, g.shape[1]), g.dtype), g], axis=0)
        if s:
            gp = pltpu.roll(gp, gp.shape[0] - s, 0)
        tot = gp if tot is None else tot + gp
    return (tot,)


taps.defvjp(_taps_fwd, _taps_bwd)


def _gelu(x):
    return x * (0.5 * (1.0 + jnp.tanh(0.7978845608028654 * (x + 0.044715 * (x * x * x)))))


def _sigmoid(x):
    return jax.nn.sigmoid(x)


def _softplus(x):
    return jnp.maximum(x, 0.0) + jnp.log1p(jnp.exp(-jnp.abs(x)))


def _rmsnorm(x, g):
    return x * lax.rsqrt(jnp.mean(x * x, axis=-1, keepdims=True) + EPS) * g


def _lane_pick(x, lane):
    sel = lax.broadcasted_iota(jnp.int32, x.shape, 1) == lane
    return jnp.sum(jnp.where(sel, x, 0.0), axis=1, keepdims=True)


def _load_ext(ref, t0, n):
    main = ref[pl.ds(t0, n), :].astype(F32)
    hstart = pl.multiple_of(jnp.maximum(t0 - SUBLANES, 0), SUBLANES)
    halo = ref[pl.ds(hstart, SUBLANES), :].astype(F32)
    halo = jnp.where(t0 > 0, halo, 0.0)
    return jnp.concatenate([halo, main], axis=0)


def mm(name, a, b, mode, *, out_dtype=F32, res=None, a_cols=None, b_cols=None):
    a0, aw = a_cols if a_cols else (0, a.shape[1])
    b0, bw = b_cols if b_cols else (0, b.shape[1])
    big = MM_TILE
    if mode == "nn":
        m, k, n = a.shape[0], aw, bw
        assert b.shape[0] == k
        tm, tk, tn = _tile(m, big), _tile(k, big, a0), _tile(n, big, b0)
        a_spec = pl.BlockSpec((tm, tk), lambda i, j, kk: (i, kk + a0 // tk))
        b_spec = pl.BlockSpec((tk, tn), lambda i, j, kk: (kk, j + b0 // tn))
        dims = _NN
    elif mode == "nt":
        m, k, n = a.shape[0], aw, b.shape[0]
        assert bw == k
        tm, tn = _tile(m, big), _tile(n, big)
        tk = _tile(k, big, math.gcd(a0, b0) if (a0 or b0) else 0)
        assert a0 % tk == 0 and b0 % tk == 0
        a_spec = pl.BlockSpec((tm, tk), lambda i, j, kk: (i, kk + a0 // tk))
        b_spec = pl.BlockSpec((tn, tk), lambda i, j, kk: (j, kk + b0 // tk))
        dims = _NT
    else:
        k, m, n = a.shape[0], aw, bw
        assert b.shape[0] == k
        tk, tm, tn = _tile(k, big), _tile(m, big, a0), _tile(n, big, b0)
        a_spec = pl.BlockSpec((tk, tm), lambda i, j, kk: (kk, i + a0 // tm))
        b_spec = pl.BlockSpec((tk, tn), lambda i, j, kk: (kk, j + b0 // tn))
        dims = _TN
    nk = k // tk
    has_res = res is not None

    def tile_dot(a_ref, b_ref):
        return lax.dot_general(a_ref[...].astype(BF16), b_ref[...].astype(BF16), (dims, ((), ())),
                               preferred_element_type=F32)

    def finish(out, refs, o_ref):
        if has_res:
            out = out + refs[2][...].astype(F32)
        o_ref[...] = out.astype(o_ref.dtype)

    def body_single(*refs):
        finish(tile_dot(refs[0], refs[1]), refs, refs[-1])

    def body_acc(*refs):
        o_ref, acc_ref = refs[-2], refs[-1]
        kk = pl.program_id(2)

        @pl.when(kk == 0)
        def _():
            acc_ref[...] = tile_dot(refs[0], refs[1])

        @pl.when((kk > 0) & (kk < nk - 1))
        def _():
            acc_ref[...] += tile_dot(refs[0], refs[1])

        @pl.when(kk == nk - 1)
        def _():
            finish(acc_ref[...] + tile_dot(refs[0], refs[1]), refs, o_ref)

    o_spec = pl.BlockSpec((tm, tn), lambda i, j, kk: (i, j))
    return pl.pallas_call(
        body_single if nk == 1 else body_acc, name=name, grid=(m // tm, n // tn, nk),
        in_specs=[a_spec, b_spec] + ([o_spec] if has_res else []), out_specs=o_spec,
        out_shape=jax.ShapeDtypeStruct((m, n), out_dtype),
        scratch_shapes=[] if nk == 1 else [pltpu.VMEM((tm, tn), F32)], compiler_params=_cparams(3),
    )(*((a, b, res) if has_res else (a, b)))


def mm_bd(name, a, b, mode, *, nblk, bk, bn, res=None, a_col0=0, b_col0=0):
    m = a.shape[0]
    tm = _tile(m, MM_TILE)
    if mode == "tn":
        tk = tm
        nk = m // tk

        def body(a_ref, b_ref, o_ref, acc_ref):
            kk = pl.program_id(1)
            part = lax.dot_general(a_ref[...].astype(BF16), b_ref[...].astype(BF16), (_TN, ((), ())),
                                   preferred_element_type=F32)

            @pl.when(kk == 0)
            def _():
                acc_ref[...] = part

            @pl.when(kk > 0)
            def _():
                acc_ref[...] += part

            @pl.when(kk == nk - 1)
            def _():
                o_ref[...] = acc_ref[...]

        return pl.pallas_call(
            body, name=name, grid=(nblk, nk),
            in_specs=[pl.BlockSpec((tk, bk), lambda i, kk: (kk, i + a_col0 // bk)),
                      pl.BlockSpec((tk, bn), lambda i, kk: (kk, i + b_col0 // bn))],
            out_specs=pl.BlockSpec((bk, bn), lambda i, kk: (i, 0)),
            out_shape=jax.ShapeDtypeStruct((nblk * bk, bn), F32),
            scratch_shapes=[pltpu.VMEM((bk, bn), F32)], compiler_params=_cparams(2),
        )(a, b)

    dims = _NN if mode == "nn" else _NT
    if mode == "nn":
        n = b.shape[1]
        a_spec = pl.BlockSpec((tm, bk), lambda i, j: (i, a_col0 // bk + j % nblk))
        b_spec = pl.BlockSpec((bk, bn), lambda i, j: (j % nblk, j))
    else:
        n = b.shape[0]
        bk, bn = bn, bk
        a_spec = pl.BlockSpec((tm, bk), lambda i, j: (i, j % nblk))
        b_spec = pl.BlockSpec((bn, bk), lambda i, j: (j, b_col0 // bk + j % nblk))
    has_res = res is not None

    def body(*refs):
        out = lax.dot_general(refs[0][...].astype(BF16), refs[1][...].astype(BF16), (dims, ((), ())),
                              preferred_element_type=F32)
        if has_res:
            out = out + refs[2][...]
        refs[-1][...] = out

    o_spec = pl.BlockSpec((tm, bn), lambda i, j: (i, j))
    return pl.pallas_call(
        body, name=name, grid=(m // tm, n // bn), in_specs=[a_spec, b_spec] + ([o_spec] if has_res else []),
        out_specs=o_spec, out_shape=jax.ShapeDtypeStruct((m, n), F32), compiler_params=_cparams(2),
    )(*((a, b, res) if has_res else (a, b)))


def _row_specs(rows, tm):
    return [pl.BlockSpec((tm, cw), lambda i, c=c0 // cw: (i, c)) for (_, c0, cw) in rows]


def rowop_fwd(name, f, rows, params, outs, tm=256):
    t = rows[0][0].shape[0]
    tm = min(tm, t)
    n_r, n_p = len(rows), len(params)

    def body(*refs):
        vals = f(*[r[...].astype(F32) for r in refs[:n_r]], *[p[...] for p in refs[n_r:n_r + n_p]])
        for o, v in zip(refs[n_r + n_p:], vals):
            o[...] = v.astype(o.dtype)

    res = pl.pallas_call(
        body, name=name, grid=(t // tm,),
        in_specs=_row_specs(rows, tm) + [_full_spec(p.shape, 1) for p in params],
        out_specs=[pl.BlockSpec((tm, cw), lambda i: (i, 0)) for cw, _ in outs],
        out_shape=[jax.ShapeDtypeStruct((t, cw), dt) for cw, dt in outs],
        compiler_params=_cparams(1),
    )(*[r[0] for r in rows], *params)
    return list(res)


def rowop_bwd(name, f, rows, params, couts, need, tm=256, add=None):
    t = rows[0][0].shape[0]
    tm = min(tm, t)
    n_r, n_p, n_c = len(rows), len(params), len(couts)
    want = [k for k in range(n_r) if need[k]]
    adds = [add] if add is not None else []

    def body(*refs):
        xs = [r[...].astype(F32) for r in refs[:n_r]]
        ps = [p[...] for p in refs[n_r:n_r + n_p]]
        cs = tuple(c[...].astype(F32) for c in refs[n_r + n_p:n_r + n_p + n_c])
        outs = refs[n_r + n_p + n_c + len(adds):]
        _, vjp = jax.vjp(f, *xs, *ps)
        gs = vjp(cs)
        for o, k in zip(outs[:len(want)], want):
            o[...] = gs[k] + refs[n_r + n_p + n_c][...] if (adds and k == want[0]) else gs[k]

        @pl.when(pl.program_id(0) == 0)
        def _():
            for o in outs[len(want):]:
                o[...] = jnp.zeros_like(o)

        for o, g in zip(outs[len(want):], gs[n_r:]):
            o[...] += g

    res = pl.pallas_call(
        body, name=name, grid=(t // tm,),
        in_specs=_row_specs(rows, tm) + [_full_spec(p.shape, 1) for p in params] + _row_specs(couts, tm)
        + _row_specs(adds, tm),
        out_specs=[pl.BlockSpec((tm, rows[k][2]), lambda i: (i, 0)) for k in want]
        + [_full_spec(p.shape, 1) for p in params],
        out_shape=[jax.ShapeDtypeStruct((t, rows[k][2]), F32) for k in want]
        + [jax.ShapeDtypeStruct(p.shape, F32) for p in params],
        compiler_params=_cparams(1),
    )(*[r[0] for r in rows], *params, *[c[0] for c in couts], *[a[0] for a in adds])
    return list(res[:len(want)]), list(res[len(want):])


def _seq_specs(seqs, s, cb, order):
    if order == "bj":
        return [pl.BlockSpec((s, cb), lambda b, j, c=c0 // cb: (b, c + j)) for (_, c0) in seqs]
    return [pl.BlockSpec((s, cb), lambda j, b, c=c0 // cb: (b, c + j)) for (_, c0) in seqs]


def _param_specs(params, order):
    if order == "bj":
        return [pl.BlockSpec(bs, lambda b, j, fn=fn: fn(j)) for (_, bs, fn) in params]
    return [pl.BlockSpec(bs, lambda j, b, fn=fn: fn(j)) for (_, bs, fn) in params]


def seqop_fwd(name, f, seqs, params, out_dtypes, *, nb, s, nblk, cb=LANES, n=256):
    n = min(n, s)
    n_s, n_p = len(seqs), len(params)

    def body(*refs):
        seq_refs, par_refs, out_refs = refs[:n_s], refs[n_s:n_s + n_p], refs[n_s + n_p:]

        def step(i, carry):
            t0 = pl.multiple_of(i * n, n)
            vals = f(t0, *[_load_ext(r, t0, n) for r in seq_refs], *[p[...] for p in par_refs])
            for o, v in zip(out_refs, vals):
                o[pl.ds(t0, n), :] = v.astype(o.dtype)
            return carry

        lax.fori_loop(0, s // n, step, 0)

    res = pl.pallas_call(
        body, name=name, grid=(nb, nblk),
        in_specs=_seq_specs(seqs, s, cb, "bj") + _param_specs(params, "bj"),
        out_specs=[pl.BlockSpec((s, cb), lambda b, j: (b, j)) for _ in out_dtypes],
        out_shape=[jax.ShapeDtypeStruct((nb * s, nblk * cb), dt) for dt in out_dtypes],
        compiler_params=_cparams(2),
    )(*[q[0] for q in seqs], *[p[0] for p in params])
    return list(res)


def seqop_bwd(name, f, seqs, params, cot_seqs, cot_fn, *, nb, s, nblk, cb=LANES, n=256):
    n = min(n, s)
    n_s, n_p, n_c = len(seqs), len(params), len(cot_seqs)
    nchunk = s // n

    def body(*refs):
        seq_refs, par_refs = refs[:n_s], refs[n_s:n_s + n_p]
        cot_refs = refs[n_s + n_p:n_s + n_p + n_c]
        dseq_refs = refs[n_s + n_p + n_c:n_s + n_p + n_c + n_s]
        dpar_refs = refs[n_s + n_p + n_c + n_s:]

        @pl.when(pl.program_id(1) == 0)
        def _():
            for o in dpar_refs:
                o[...] = jnp.zeros_like(o)

        def step(ii, halos):
            t0 = pl.multiple_of((nchunk - 1 - ii) * n, n)
            exts = [_load_ext(r, t0, n) for r in seq_refs]
            ps = [p[...] for p in par_refs]
            cots = cot_fn(t0, *[_load_ext(r, t0, n) for r in cot_refs])
            _, vjp = jax.vjp(lambda *args: f(t0, *args), *exts, *ps)
            gs = vjp(tuple(cots))
            tail = pl.multiple_of(t0 + n - SUBLANES, SUBLANES)
            new_halos = []
            for o, g, h in zip(dseq_refs, gs[:n_s], halos):
                o[pl.ds(t0, n), :] = g[SUBLANES:]
                o[pl.ds(tail, SUBLANES), :] += h
                new_halos.append(g[:SUBLANES])
            for o, g in zip(dpar_refs, gs[n_s:]):
                o[...] += g
            return tuple(new_halos)

        lax.fori_loop(0, nchunk, step, tuple(jnp.zeros((SUBLANES, cb), F32) for _ in range(n_s)))

    res = pl.pallas_call(
        body, name=name, grid=(nblk, nb),
        in_specs=_seq_specs(seqs, s, cb, "jb") + _param_specs(params, "jb") + _seq_specs(cot_seqs, s, cb, "jb"),
        out_specs=[pl.BlockSpec((s, cb), lambda j, b: (b, j)) for _ in seqs] + _param_specs(params, "jb"),
        out_shape=[jax.ShapeDtypeStruct((nb * s, nblk * cb), F32) for _ in seqs]
        + [jax.ShapeDtypeStruct(p[0].shape, F32) for p in params],
        compiler_params=_cparams(2),
    )(*[q[0] for q in seqs], *[p[0] for p in params], *[q[0] for q in cot_seqs])
    return list(res[:n_s]), list(res[n_s:])


def rscan(name, a, b, reverse, *, nb, s, cb=256):
    c = a.shape[1]
    cb = min(cb, c)

    def body(a_ref, b_ref, h_ref):
        def step(ii, car):
            i = (s // SUBLANES - 1 - ii) if reverse else ii
            base = pl.multiple_of(i * SUBLANES, SUBLANES)
            for r in (reversed(range(SUBLANES)) if reverse else range(SUBLANES)):
                row = pl.ds(base + r, 1)
                if reverse:
                    g = b_ref[row, :] + car
                    h_ref[row, :] = g
                    car = a_ref[row, :] * g
                else:
                    car = a_ref[row, :] * car + b_ref[row, :]
                    h_ref[row, :] = car
            return car

        lax.fori_loop(0, s // SUBLANES, step, jnp.zeros((1, cb), F32))

    spec = pl.BlockSpec((s, cb), lambda bb, j: (bb, j))
    return pl.pallas_call(
        body, name=name, grid=(nb, c // cb), in_specs=[spec, spec], out_specs=spec,
        out_shape=jax.ShapeDtypeStruct(a.shape, F32), compiler_params=_cparams(2),
    )(a, b)


def cscan(name, bu, ar, ai, reverse, *, nb, s, cb=512, seg=2048):
    p = ar.shape[1]
    cb = min(cb, p)
    nj = p // cb
    seg = min(seg, s)
    nseg = s // seg

    def sg_of(g):
        return (nseg - 1 - g) if reverse else g

    def body(br_ref, bi_ref, ar_ref, ai_ref, hr_ref, hi_ref, cr_scr, ci_scr):
        lr = ar_ref[...]
        li = -ai_ref[...] if reverse else ai_ref[...]

        @pl.when(pl.program_id(2) == 0)
        def _():
            cr_scr[...] = jnp.zeros_like(cr_scr)
            ci_scr[...] = jnp.zeros_like(ci_scr)

        def step(ii, car):
            hr, hi = car
            i = (seg // SUBLANES - 1 - ii) if reverse else ii
            base = pl.multiple_of(i * SUBLANES, SUBLANES)
            for r in (reversed(range(SUBLANES)) if reverse else range(SUBLANES)):
                row = pl.ds(base + r, 1)
                nr = lr * hr - li * hi + br_ref[row, :]
                ni = lr * hi + li * hr + bi_ref[row, :]
                hr, hi = nr, ni
                hr_ref[row, :] = hr
                hi_ref[row, :] = hi
            return hr, hi

        hr, hi = lax.fori_loop(0, seg // SUBLANES, step, (cr_scr[...], ci_scr[...]))
        cr_scr[...] = hr
        ci_scr[...] = hi

    o_spec = pl.BlockSpec((seg, cb), lambda bb, j, g: (bb * nseg + sg_of(g), j))
    l_spec = pl.BlockSpec((1, cb), lambda bb, j, g: (0, j))
    res = pl.pallas_call(
        body, name=name, grid=(nb, nj, nseg),
        in_specs=[o_spec, pl.BlockSpec((seg, cb), lambda bb, j, g: (bb * nseg + sg_of(g), j + nj)), l_spec, l_spec],
        out_specs=[o_spec, o_spec],
        out_shape=[jax.ShapeDtypeStruct((nb * s, p), F32)] * 2,
        scratch_shapes=[pltpu.VMEM((1, cb), F32), pltpu.VMEM((1, cb), F32)], compiler_params=_cparams(3),
    )(bu, bu, ar, ai)
    return res[0], res[1]


def _norm_f(x, g):
    return (_rmsnorm(x, g),)


def _rglru_pre_f(t0, xext, w0, w1, w2, w3, cb, wr, wi, br, bi, lam):
    x0, x1, x2, x3 = taps(xext, 4)
    xc = w3 * x0 + w2 * x1 + w1 * x2 + w0 * x3 + cb
    r = _sigmoid(mm_nn(xc, wr[0]) + br)
    ig = _sigmoid(mm_nn(xc, wi[0]) + bi)
    log_a = -RG_C * r * _softplus(-lam)
    a = jnp.exp(log_a)
    first = (lax.broadcasted_iota(jnp.int32, a.shape, 0) + t0) == 0
    mult = jnp.where(first, 1.0, jnp.sqrt(1.0 - jnp.exp(2.0 * log_a)))
    return a, mult * ig * xc


def _ffn_act_f(t0, gext, vext, g0, g1, g2, v0, v1, v2, bg, bv):
    ga, gb, gc = taps(gext, 3)
    va, vb, vc = taps(vext, 3)
    ug = g2 * ga + g1 * gb + g0 * gc + bg
    uv = v2 * va + v1 * vb + v0 * vc + bv
    return (_gelu(ug) * uv,)


def _s5_post_f(ys, uc, d, gw, gb):
    yc = _gelu(ys + d * uc)
    return (yc * _sigmoid(mm_nn(yc, gw) + gb),)


def _merge_f(ha, ga, yb, yc, g0, g1, g2, wb0, wb1, wb2, bg0, bg1, bg2):
    ya = ha * _gelu(ga)
    out = _sigmoid(g0 + bg0) * mm_nn(ya, wb0)
    out = out + _sigmoid(g1 + bg1) * mm_nn(yb, wb1)
    out = out + _sigmoid(g2 + bg2) * mm_nn(yc, wb2)
    return (out,)


def _s5_param_f(lr, li, ldt, btr, bti):
    dt = jnp.exp(ldt)
    mag = jnp.exp(lr * dt)
    ar, ai = mag * jnp.cos(li * dt), mag * jnp.sin(li * dt)
    den = lr * lr + li * li
    fr = ((ar - 1.0) * lr + ai * li) / den
    fi = (ai * lr - (ar - 1.0) * li) / den
    return ar, ai, fr * btr - fi * bti, fr * bti + fi * btr


def _each(fn, *lists):
    return [fn(*xs) for xs in zip(*lists)]


def _tri_inv(mats):
    c = mats[0].shape[0]
    ri = lax.broadcasted_iota(jnp.int32, (c, c), 0)
    ci = lax.broadcasted_iota(jnp.int32, (c, c), 1)
    eye = jnp.where(ri == ci, 1.0, 0.0).astype(F32)
    hp = lambda a, b: mm_nn(a, b, True)
    ad = [jnp.where((ri >> 4) == (ci >> 4), a, 0.0) for a in mats]
    ao = _each(lambda a, d: a - d, mats, ad)
    a2 = _each(hp, ad, ad)
    a4 = _each(hp, a2, a2)
    a8 = _each(hp, a4, a4)
    lo = _each(lambda d, s: hp(eye - d, eye + s), ad, a2)
    hi = _each(lambda s, e: hp(eye + s, eye + e), a4, a8)
    td = _each(hp, lo, hi)
    nn_ = _each(hp, td, ao)
    n2 = _each(hp, nn_, nn_)
    return _each(hp, _each(lambda n, s: hp(eye - n, eye + s), nn_, n2), td)


def _dn_chunk_f(nh, *args):
    per = [args[20 * h:20 * (h + 1)] for h in range(nh)]
    nw = args[20 * nh]
    state, qe, ke, ve, z, ba = ([p[j] for p in per] for j in range(6))
    c = DN_CHUNK
    dk = qe[0].shape[1]

    def conv_silu(xes, first):
        tp = [taps(x, 4) for x in xes]
        u = [p[first + 3] * t[0] + p[first + 2] * t[1] + p[first + 1] * t[2] + p[first] * t[3] for p, t in zip(per, tp)]
        return [x * _sigmoid(x) for x in u]

    def l2n(xs):
        return [x * lax.rsqrt(jnp.sum(x * x, axis=-1, keepdims=True) + EPS) for x in xs]

    q = [x * (dk ** -0.5) for x in l2n(conv_silu(qe, 6))]
    k = l2n(conv_silu(ke, 10))
    v = conv_silu(ve, 14)
    beta = [_sigmoid(_lane_pick(b, 0)) for b in ba]
    g = [-jnp.exp(_lane_pick(p[18], 0)) * _softplus(_lane_pick(b, 1) + _lane_pick(p[19], 0)) for p, b in zip(per, ba)]
    ri = lax.broadcasted_iota(jnp.int32, (c, c), 0)
    ci = lax.broadcasted_iota(jnp.int32, (c, c), 1)
    incl, strict = ri >= ci, ri > ci
    ltri = jnp.where(incl, 1.0, 0.0).astype(F32)
    mean_row = jnp.full((c, dk), 1.0 / dk, F32)
    gc_w = [mm_nn(ltri, jnp.broadcast_to(x, (c, dk)), True) for x in g]
    gc_c = [mm_nn(ltri, jnp.broadcast_to(x, (c, c)), True) for x in g]
    gc_r = [mm_nt(mean_row, x, True) for x in gc_w]
    gtot = [jnp.sum(x, axis=0, keepdims=True) for x in g]
    decay = _each(lambda a, b: jnp.exp(jnp.where(incl, a - b, -1e30)), gc_c, gc_r)
    e_gc = [jnp.exp(x) for x in gc_w]
    kb = _each(lambda a, b: a * b, k, beta)
    a_mat = _each(lambda a, b, d: jnp.where(strict, mm_nt(a, b) * d, 0.0), kb, k, decay)
    t_inv = _tri_inv(a_mat)
    u = _each(lambda t, a, b: mm_nn(t, a * b, True), t_inv, v, beta)
    w = _each(lambda t, a, e: mm_nn(t, a * e, True), t_inv, kb, e_gc)
    qk = _each(lambda a, b, d: jnp.where(incl, mm_nt(a, b) * d, 0.0), q, k, decay)
    v_new = _each(lambda a, b, s: a - mm_nn(b, s), u, w, state)
    o = _each(lambda a, e, s, b, n: mm_nn(a * e, s) + mm_nn(b, n), q, e_gc, state, qk, v_new)
    new_state = _each(lambda s, t, a, gw, n: s * jnp.exp(t) + mm_tn(a * jnp.exp(t - gw), n), state, gtot, k, gc_w, v_new)
    o = _each(lambda x, zz: _rmsnorm(x, nw) * (zz * _sigmoid(zz)), o, z)
    return tuple(o) + tuple(new_state)


def _attn_f(q, k, v):
    sc = mm_nt(q, k) * (q.shape[1] ** -0.5)
    sc = sc - jnp.max(sc, axis=-1, keepdims=True)
    e = jnp.exp(sc)
    p = e / jnp.sum(e, axis=-1, keepdims=True)
    return (mm_nn(p, v),)


DN_SEG = 512


def _dn_ext(ref, edge, t0, i, h):
    lanes = pl.ds(h * LANES, LANES)
    hstart = pl.multiple_of(jnp.maximum(t0 - SUBLANES, 0), SUBLANES)
    halo = jnp.where(i > 0, ref[pl.ds(hstart, SUBLANES), lanes], edge[:, h * LANES:(h + 1) * LANES])
    return jnp.concatenate([halo, ref[pl.ds(t0, DN_CHUNK), lanes]], axis=0)


def _dn_layout(cols, nb, s, nh, reverse):
    seg = min(DN_SEG, s)
    nseg, w = s // seg, nh * LANES
    hb = seg // SUBLANES

    def sg_of(g):
        return (nseg - 1 - g) if reverse else g

    main = [pl.BlockSpec((seg, w), lambda b, g, c=c0 // w: (b * nseg + sg_of(g), c)) for c0 in cols]
    edge = [pl.BlockSpec((SUBLANES, w), lambda b, g, c=c0 // w: (jnp.maximum((b * nseg + sg_of(g)) * hb - 1, 0), c))
            for c0 in cols[:3]]
    par = [pl.BlockSpec((1, w), lambda b, g: (0, 0))] * 14 + [pl.BlockSpec((1, LANES), lambda b, g: (0, 0))]
    seq = pl.BlockSpec((seg, w), lambda b, g: (b * nseg + sg_of(g), 0))
    st = pl.BlockSpec((1, nh, seg // DN_CHUNK, LANES, LANES), lambda b, g: (b, 0, sg_of(g), 0, 0))
    return seg, nseg, main, edge, par, seq, st


def deltanet_fwd(proj, cols, taps12, alog, dtb, nw, *, nb, s, nh):
    seg, nseg, main, edge, par, seq, st = _dn_layout(cols, nb, s, nh, False)
    ncs = seg // DN_CHUNK

    def body(*refs):
        q_ref, k_ref, v_ref, z_ref, ba_ref = refs[:5]
        edges = [jnp.where(pl.program_id(1) > 0, r[...], 0.0) for r in refs[5:8]]
        pars = [p[...] for p in refs[8:23]]
        o_ref, st_ref, state_scr = refs[23], refs[24], refs[25]

        @pl.when(pl.program_id(1) == 0)
        def _():
            state_scr[...] = jnp.zeros_like(state_scr)

        def step(i, carry):
            t0 = pl.multiple_of(i * DN_CHUNK, DN_CHUNK)
            rows = pl.ds(t0, DN_CHUNK)
            states = [state_scr[h] for h in range(nh)]
            args = []
            for h in range(nh):
                lanes = pl.ds(h * LANES, LANES)
                args += ([states[h]] + [_dn_ext(r, e, t0, i, h) for r, e in zip((q_ref, k_ref, v_ref), edges)]
                         + [z_ref[rows, lanes], ba_ref[rows, lanes]]
                         + [p[:, h * LANES:(h + 1) * LANES] for p in pars[:14]])
            outs = _dn_chunk_f(nh, *args, pars[14])
            for h in range(nh):
                st_ref[0, h, i] = states[h]
                o_ref[rows, pl.ds(h * LANES, LANES)] = outs[h]
                state_scr[h] = outs[nh + h]
            return carry

        lax.fori_loop(0, ncs, step, 0)

    res = pl.pallas_call(
        body, name="deltanet_fwd", grid=(nb, nseg),
        in_specs=main + edge + par, out_specs=[seq, st],
        out_shape=[jax.ShapeDtypeStruct((nb * s, nh * LANES), F32),
                   jax.ShapeDtypeStruct((nb, nh, s // DN_CHUNK, LANES, LANES), F32)],
        scratch_shapes=[pltpu.VMEM((nh, LANES, LANES), F32)], compiler_params=_cparams(2),
    )(proj, proj, proj, proj, proj, proj, proj, proj, *taps12, alog, dtb, nw)
    return res[0], res[1]


def deltanet_bwd(proj, cols, taps12, alog, dtb, nw, states, dyb, *, nb, s, nh):
    seg, nseg, main, edge, par, seq, st = _dn_layout(cols, nb, s, nh, True)
    ncs = seg // DN_CHUNK

    def body(*refs):
        q_ref, k_ref, v_ref, z_ref, ba_ref = refs[:5]
        first_in_time = pl.program_id(1) == nseg - 1
        edges = [jnp.where(first_in_time, 0.0, r[...]) for r in refs[5:8]]
        pars = [p[...] for p in refs[8:23]]
        st_ref, do_ref = refs[23], refs[24]
        dseq_refs = refs[25:28]
        dz_ref, dba_ref = refs[28], refs[29]
        dpar_refs = refs[30:45]
        dstate_scr, dhalo_scr = refs[45], refs[46]

        @pl.when((pl.program_id(0) == 0) & (pl.program_id(1) == 0))
        def _():
            for o in dpar_refs:
                o[...] = jnp.zeros_like(o)

        @pl.when(pl.program_id(1) == 0)
        def _():
            dstate_scr[...] = jnp.zeros_like(dstate_scr)
            dhalo_scr[...] = jnp.zeros_like(dhalo_scr)

        def step(ii, carry):
            i = ncs - 1 - ii
            t0 = pl.multiple_of(i * DN_CHUNK, DN_CHUNK)
            rows = pl.ds(t0, DN_CHUNK)
            d_states = [dstate_scr[h] for h in range(nh)]
            d_halos = [dhalo_scr[k] for k in range(3)]
            args = []
            for h in range(nh):
                lanes = pl.ds(h * LANES, LANES)
                args += ([st_ref[0, h, i]] + [_dn_ext(r, e, t0, i, h) for r, e in zip((q_ref, k_ref, v_ref), edges)]
                         + [z_ref[rows, lanes], ba_ref[rows, lanes]]
                         + [p[:, h * LANES:(h + 1) * LANES] for p in pars[:14]])
            _, vjp = jax.vjp(functools.partial(_dn_chunk_f, nh), *args, pars[14])
            grads = vjp(tuple(do_ref[rows, pl.ds(h * LANES, LANES)] for h in range(nh)) + tuple(d_states))
            dpar_refs[14][...] += grads[20 * nh]
            for h in range(nh):
                lanes = pl.ds(h * LANES, LANES)
                gs = grads[20 * h:20 * (h + 1)]
                dstate_scr[h] = gs[0]
                for k, (o, g) in enumerate(zip(dseq_refs, gs[1:4])):
                    o[rows, lanes] = jnp.concatenate(
                        [g[SUBLANES:DN_CHUNK], g[DN_CHUNK:] + d_halos[k][:, h * LANES:(h + 1) * LANES]], axis=0)
                    dhalo_scr[k, :, lanes] = g[:SUBLANES]
                dz_ref[rows, lanes] = gs[4]
                dba_ref[rows, lanes] = gs[5]
                for o, g in zip(dpar_refs[:14], gs[6:20]):
                    o[:, lanes] += g
            return carry

        lax.fori_loop(0, ncs, step, 0)

    w = nh * LANES
    res = pl.pallas_call(
        body, name="deltanet_bwd", grid=(nb, nseg),
        in_specs=main + edge + par + [st, seq], out_specs=[seq] * 5 + par,
        out_shape=[jax.ShapeDtypeStruct((nb * s, w), F32)] * 5
        + [jax.ShapeDtypeStruct((1, w), F32)] * 14 + [jax.ShapeDtypeStruct((1, LANES), F32)],
        scratch_shapes=[pltpu.VMEM((nh, LANES, LANES), F32), pltpu.VMEM((3, SUBLANES, w), F32)],
        compiler_params=_cparams(2),
    )(proj, proj, proj, proj, proj, proj, proj, proj, *taps12, alog, dtb, nw, states, dyb)
    return list(res[:5]), list(res[5:])


def attn_fwd(q, kv, *, nb, s, m, nh, tq=512):
    hd = q.shape[1] // nh
    tq = min(tq, s)
    nq = s // tq

    def body(q_ref, k_ref, v_ref, o_ref):
        o_ref[...] = _attn_f(q_ref[...].astype(F32), k_ref[...], v_ref[...])[0].astype(o_ref.dtype)

    return pl.pallas_call(
        body, name="attn_fwd", grid=(nb, nh, nq),
        in_specs=[pl.BlockSpec((tq, hd), lambda b, h, i: (b * nq + i, h)),
                  pl.BlockSpec((m, hd), lambda b, h, i: (b, h)),
                  pl.BlockSpec((m, hd), lambda b, h, i: (b, nh + h))],
        out_specs=pl.BlockSpec((tq, hd), lambda b, h, i: (b * nq + i, h)),
        out_shape=jax.ShapeDtypeStruct(q.shape, BF16), compiler_params=_cparams(3),
    )(q, kv, kv)


def attn_bwd(q, kv, do, *, nb, s, m, nh, tq=512):
    hd = q.shape[1] // nh
    tq = min(tq, s)
    nq = s // tq

    def body(q_ref, k_ref, v_ref, do_ref, dq_ref, dk_ref, dv_ref):
        _, vjp = jax.vjp(_attn_f, q_ref[...].astype(F32), k_ref[...], v_ref[...])
        dq, dk, dv = vjp((do_ref[...].astype(F32),))
        dq_ref[...] = dq

        @pl.when(pl.program_id(2) == 0)
        def _():
            dk_ref[...] = jnp.zeros_like(dk_ref)
            dv_ref[...] = jnp.zeros_like(dv_ref)

        dk_ref[...] += dk
        dv_ref[...] += dv

    q_spec = pl.BlockSpec((tq, hd), lambda b, h, i: (b * nq + i, h))
    m_spec = pl.BlockSpec((m, hd), lambda b, h, i: (b, h))
    res = pl.pallas_call(
        body, name="attn_bwd", grid=(nb, nh, nq),
        in_specs=[q_spec, m_spec, pl.BlockSpec((m, hd), lambda b, h, i: (b, nh + h)), q_spec],
        out_specs=[q_spec, m_spec, m_spec],
        out_shape=[jax.ShapeDtypeStruct(q.shape, F32), jax.ShapeDtypeStruct((nb * m, nh * hd), F32),
                   jax.ShapeDtypeStruct((nb * m, nh * hd), F32)],
        compiler_params=_cparams(3),
    )(q, kv, kv, do)
    return res[0], res[1], res[2]


def loss_head(x, target, gain, tm=256):
    t, d = x.shape
    tm = min(tm, t)

    def body(x_ref, t_ref, g_ref, l_ref, dx_ref, dg_ref):
        tgt = t_ref[...]

        def lf(xv, gv):
            e = _rmsnorm(xv, gv) - tgt
            return 0.5 * jnp.sum(jnp.mean(e * e, axis=-1, keepdims=True), axis=0, keepdims=True)

        lv, vjp = jax.vjp(lf, x_ref[...], g_ref[...])
        dx, dg = vjp(jnp.ones((1, 1), F32))
        dx_ref[...] = dx

        @pl.when(pl.program_id(0) == 0)
        def _():
            l_ref[...] = jnp.zeros_like(l_ref)
            dg_ref[...] = jnp.zeros_like(dg_ref)

        l_ref[...] += jnp.broadcast_to(lv, l_ref.shape)
        dg_ref[...] += dg

    row = pl.BlockSpec((tm, d), lambda i: (i, 0))
    res = pl.pallas_call(
        body, name="loss_head", grid=(t // tm,),
        in_specs=[row, row, _full_spec((1, d), 1)],
        out_specs=[_full_spec((1, LANES), 1), row, _full_spec((1, d), 1)],
        out_shape=[jax.ShapeDtypeStruct((1, LANES), F32), jax.ShapeDtypeStruct((t, d), F32),
                   jax.ShapeDtypeStruct((1, d), F32)],
        compiler_params=_cparams(1),
    )(x, target, gain)
    return res[0], res[1], res[2]


def s5_dlam(gr, gi, hr, hi, *, nb, s, cb=256, n=256):
    p = gr.shape[1]
    cb, n = min(cb, p), min(n, s)

    def body(gr_ref, gi_ref, hr_ref, hi_ref, dar_ref, dai_ref):
        @pl.when(pl.program_id(1) == 0)
        def _():
            dar_ref[...] = jnp.zeros_like(dar_ref)
            dai_ref[...] = jnp.zeros_like(dai_ref)

        def step(i, carry):
            t0 = pl.multiple_of(i * n, n)
            g_r, g_i = gr_ref[pl.ds(t0, n), :], gi_ref[pl.ds(t0, n), :]
            p_r = taps(_load_ext(hr_ref, t0, n), 2)[1]
            p_i = taps(_load_ext(hi_ref, t0, n), 2)[1]
            dar_ref[...] += jnp.sum(g_r * p_r + g_i * p_i, axis=0, keepdims=True)
            dai_ref[...] += jnp.sum(g_i * p_r - g_r * p_i, axis=0, keepdims=True)
            return carry

        lax.fori_loop(0, s // n, step, 0)

    blk = pl.BlockSpec((s, cb), lambda j, b: (b, j))
    acc = pl.BlockSpec((1, cb), lambda j, b: (0, j))
    res = pl.pallas_call(
        body, name="s5_dlam", grid=(p // cb, nb), in_specs=[blk] * 4, out_specs=[acc, acc],
        out_shape=[jax.ShapeDtypeStruct((1, p), F32)] * 2, compiler_params=_cparams(2),
    )(gr, gi, hr, hi)
    return res[0], res[1]


def s5_param_fwd(lr, li, ldt, btr, bti):
    def body(*refs):
        vals = _s5_param_f(*[r[...] for r in refs[:5]])
        for o, v in zip(refs[5:], vals):
            o[...] = v

    return pl.pallas_call(
        body, name="s5_param_fwd",
        out_shape=[jax.ShapeDtypeStruct(lr.shape, F32)] * 2 + [jax.ShapeDtypeStruct(btr.shape, F32)] * 2,
    )(lr, li, ldt, btr, bti)


def s5_param_bwd(lr, li, ldt, btr, bti, cots):
    def body(*refs):
        _, vjp = jax.vjp(_s5_param_f, *[r[...] for r in refs[:5]])
        gs = vjp(tuple(r[...] for r in refs[5:9]))
        for o, g in zip(refs[9:], gs):
            o[...] = g

    return pl.pallas_call(
        body, name="s5_param_bwd",
        out_shape=[jax.ShapeDtypeStruct(a.shape, F32) for a in (lr, li, ldt, btr, bti)],
    )(lr, li, ldt, btr, bti, *cots)


def _row_block(r, c, copies):
    lanes = -(-c // LANES) * LANES
    rb = 2048
    while rb > 2 * SUBLANES and (r % rb or copies * rb * lanes * 4 > 12 * 1024 * 1024):
        rb //= 2
    return rb if r % rb == 0 else r


def _as2d(a, lead=0):
    return a.reshape(a.shape[:lead] + (-1, a.shape[-1]))


def sumk(name, buf):
    b3 = _as2d(buf, 1)
    k, r, c = b3.shape
    rb = _row_block(r, c, 2 * (k + 1))

    def body(b_ref, o_ref):
        acc = b_ref[0]
        for i in range(1, k):
            acc = acc + b_ref[i]
        o_ref[...] = acc

    out = pl.pallas_call(
        body, name=name, grid=(r // rb,),
        in_specs=[pl.BlockSpec((k, rb, c), lambda i: (0, i, 0))],
        out_specs=pl.BlockSpec((rb, c), lambda i: (i, 0)),
        out_shape=jax.ShapeDtypeStruct((r, c), F32), compiler_params=_cparams(1),
    )(b3)
    return out.reshape(buf.shape[1:])


def adamw(name, w, g, m, v):
    shape = w.shape
    w, g, m, v = (_as2d(a) for a in (w, g, m, v))
    r, c = w.shape
    rb = _row_block(r, c, 14)

    def body(w_ref, g_ref, m_ref, v_ref, d_ref, nm_ref, nv_ref):
        gv = g_ref[...]
        nm = ADAM_B1 * m_ref[...] + (1.0 - ADAM_B1) * gv
        nv = ADAM_B2 * v_ref[...] + (1.0 - ADAM_B2) * (gv * gv)
        m_hat = nm / (1.0 - ADAM_B1 ** ADAM_STEP)
        v_hat = nv / (1.0 - ADAM_B2 ** ADAM_STEP)
        d_ref[...] = -ADAM_LR * (m_hat / (jnp.sqrt(v_hat) + ADAM_EPS) + ADAM_WD * w_ref[...])
        nm_ref[...] = nm
        nv_ref[...] = nv

    spec = pl.BlockSpec((rb, c), lambda i: (i, 0))
    res = pl.pallas_call(
        body, name=name, grid=(r // rb,), in_specs=[spec] * 4, out_specs=[spec] * 3,
        out_shape=[jax.ShapeDtypeStruct(w.shape, F32)] * 3, compiler_params=_cparams(1),
    )(w, g, m, v)
    return tuple(a.reshape(shape) for a in res)


def _place():
    x, y, c = lax.axis_index("x"), lax.axis_index("y"), lax.axis_index("c")
    chips = [(1 - x, y), (x, 1 - y), (1 - x, 1 - y)]
    return x, y, c, chips


def _chip_id(px, py):
    return 2 * px + py


def _rcopy(src, dst, send_sems, recv_sems, k, to):
    return pltpu.make_async_remote_copy(src_ref=src, dst_ref=dst, send_sem=send_sems.at[k], recv_sem=recv_sems.at[k],
                                        device_id=to, device_id_type=MESH)


def _comm_call(name, body, out_shapes, n_sems, args):
    return pl.pallas_call(
        body, name=name, out_shape=out_shapes, in_specs=[ANY] * len(args), out_specs=[ANY] * len(out_shapes),
        scratch_shapes=[pltpu.SemaphoreType.DMA((n_sems,)), pltpu.SemaphoreType.DMA((n_sems,))],
    )(*args)


def _finish(remote):
    for cp in remote:
        cp.wait_send()


def _set_slot(buf, slot, val):
    if val.ndim < buf.ndim:
        val = val[None]
    return lax.dynamic_update_slice(buf, val.astype(buf.dtype), (slot,) + (0,) * (buf.ndim - 1))


def gather_chips(name, arrays):
    n = len(arrays)
    halves = [a.shape[0] // 2 for a in arrays]

    def body(*refs):
        ins, outs = refs[:n], refs[n:2 * n]
        send_sems, recv_sems = refs[2 * n:]
        x, y, c, chips = _place()
        me, sib = _chip_id(x, y), (x, y, 1 - c)
        half = [pl.ds(c * h, h) for h in halves]
        other = [pl.ds((1 - c) * h, h) for h in halves]
        remote = []
        for a in range(n):
            for j, chip in enumerate(chips):
                remote.append(_rcopy(ins[a].at[half[a]], outs[a].at[me, half[a]], send_sems, recv_sems, 6 * a + j,
                                     (*chip, c)))
                remote[-1].start()
        for a in range(n):
            for j, chip in enumerate(chips):
                landed = outs[a].at[_chip_id(*chip), half[a]]
                _rcopy(ins[a].at[half[a]], landed, send_sems, recv_sems, 6 * a + j, sib).wait_recv()
                remote.append(_rcopy(landed, landed, send_sems, recv_sems, 6 * a + 3 + j, sib))
                remote[-1].start()
        for a in range(n):
            for j, chip in enumerate(chips):
                _rcopy(ins[a].at[half[a]], outs[a].at[_chip_id(*chip), other[a]], send_sems, recv_sems, 6 * a + 3 + j,
                       sib).wait_recv()
        _finish(remote)

    shapes = [jax.ShapeDtypeStruct((N_CHIPS,) + a.shape, a.dtype) for a in arrays]
    return _comm_call(name, body, shapes, 6 * n, arrays)


def swap_halves(name, arrays):
    n = len(arrays)

    def body(*refs):
        ins, outs = refs[:n], refs[n:2 * n]
        send_sems, recv_sems = refs[2 * n:]
        x, y, c, _ = _place()
        sib = (x, y, 1 - c)
        remote = []
        for a in range(n):
            for k in range(N_CHIPS):
                remote.append(_rcopy(ins[a].at[1 - c, k], outs[a].at[k], send_sems, recv_sems, N_CHIPS * a + k, sib))
                remote[-1].start()
        for a in range(n):
            for k in range(N_CHIPS):
                _rcopy(ins[a].at[1 - c, k], outs[a].at[k], send_sems, recv_sems, N_CHIPS * a + k, sib).wait_recv()
        _finish(remote)

    shapes = [jax.ShapeDtypeStruct(a.shape[1:], a.dtype) for a in arrays]
    return _comm_call(name, body, shapes, N_CHIPS * n, arrays)


def scatter_chips(name, arrays):
    n = len(arrays)

    def body(*refs):
        ins, outs = refs[:n], refs[n:2 * n]
        send_sems, recv_sems = refs[2 * n:]
        x, y, c, chips = _place()
        remote = []
        for a in range(n):
            for j, chip in enumerate(chips):
                remote.append(_rcopy(ins[a].at[_chip_id(*chip)], outs[a].at[j], send_sems, recv_sems, 3 * a + j,
                                     (*chip, c)))
                remote[-1].start()
        for a in range(n):
            for j, chip in enumerate(chips):
                _rcopy(ins[a].at[0], outs[a].at[j], send_sems, recv_sems, 3 * a + j, (*chip, c)).wait_recv()
        _finish(remote)

    shapes = [jax.ShapeDtypeStruct((3,) + a.shape[1:], a.dtype) for a in arrays]
    return _comm_call(name, body, shapes, 3 * n, arrays)


def join_halves(name, arrays):
    n = len(arrays)
    offs = [sum(a.shape[0] for a in arrays[:i]) for i in range(n)]

    def body(*refs):
        ins, outs = refs[:n], refs[n:2 * n]
        send_sems, recv_sems = refs[2 * n:]
        x, y, c, _ = _place()
        sib = (x, y, 1 - c)
        remote = []
        for a in range(n):
            h = arrays[a].shape[0]
            for k in range(h):
                remote.append(_rcopy(ins[a].at[k], outs[a].at[c * h + k], send_sems, recv_sems, offs[a] + k, sib))
                remote[-1].start()
        for a in range(n):
            h = arrays[a].shape[0]
            for k in range(h):
                _rcopy(ins[a].at[k], outs[a].at[(1 - c) * h + k], send_sems, recv_sems, offs[a] + k, sib).wait_recv()
        _finish(remote)

    shapes = [jax.ShapeDtypeStruct((2 * a.shape[0],) + a.shape[1:], a.dtype) for a in arrays]
    return _comm_call(name, body, shapes, offs[-1] + arrays[-1].shape[0], arrays)


def add_pair(name, mine2, other, c):
    a3, o2 = _as2d(mine2, 1), _as2d(other)
    r, cols = o2.shape
    rb = _row_block(r, cols, 8)

    def body(c_ref, a_ref, o_ref, out_ref, out16_ref):
        tot = a_ref[0] + o_ref[...]
        out_ref[...] = tot
        out16_ref[...] = tot.astype(BF16)

    spec = pl.BlockSpec((rb, cols), lambda i, s: (i, 0))
    out, out16 = pl.pallas_call(
        body, name=name, out_shape=[jax.ShapeDtypeStruct((r, cols), F32), jax.ShapeDtypeStruct((r, cols), BF16)],
        grid_spec=pltpu.PrefetchScalarGridSpec(
            num_scalar_prefetch=1, grid=(r // rb,),
            in_specs=[pl.BlockSpec((1, rb, cols), lambda i, s: (s[0], i, 0)), spec], out_specs=[spec, spec]),
        compiler_params=_cparams(1),
    )(jnp.reshape(c, (1,)).astype(jnp.int32), a3, o2)
    return out.reshape(other.shape), out16.reshape(other.shape)


def add_chips(name, part, recv, me):
    p3, r3 = _as2d(part, 1), _as2d(recv, 1)
    _, r, cols = p3.shape
    rb = _row_block(r, cols, 10)

    def body(me_ref, p_ref, r_ref, out_ref):
        own, acc = p_ref[0], None
        got = [r_ref[j].astype(F32) for j in range(3)]
        for k in range(N_CHIPS):
            rel = jnp.full(own.shape, me_ref[0] ^ k, jnp.int32)
            term = jnp.where(rel == 0, own, jnp.where(rel == 2, got[0], jnp.where(rel == 1, got[1], got[2])))
            acc = term if acc is None else acc + term
        out_ref[...] = acc

    out = pl.pallas_call(
        body, name=name, out_shape=jax.ShapeDtypeStruct((r, cols), F32),
        grid_spec=pltpu.PrefetchScalarGridSpec(
            num_scalar_prefetch=1, grid=(r // rb,),
            in_specs=[pl.BlockSpec((1, rb, cols), lambda i, s: (s[0], i, 0)),
                      pl.BlockSpec((3, rb, cols), lambda i, s: (0, i, 0))],
            out_specs=pl.BlockSpec((rb, cols), lambda i, s: (i, 0))),
        compiler_params=_cparams(1),
    )(jnp.reshape(me, (1,)).astype(jnp.int32), p3, r3)
    return out.reshape(part.shape[1:])


def gather_all(name, slab):
    def body(x_ref, out_ref, send_sems, recv_sems):
        x, y, c, chips = _place()
        me, sib = (x, y, c), (x, y, 1 - c)

        def rows(px, py, pc):
            return out_ref.at[4 * px + 2 * py + pc]

        first = [_rcopy(x_ref, rows(*me), send_sems, recv_sems, 0, sib)]
        first += [_rcopy(x_ref, rows(*me), send_sems, recv_sems, 1 + j, (*chip, c)) for j, chip in enumerate(chips)]
        for cp in first:
            cp.start()
        passed = []
        for j, chip in enumerate(chips):
            landed = rows(*chip, c)
            _rcopy(x_ref, landed, send_sems, recv_sems, 1 + j, sib).wait_recv()
            passed.append(_rcopy(landed, landed, send_sems, recv_sems, 4 + j, sib))
            passed[-1].start()
        _rcopy(x_ref, rows(*sib), send_sems, recv_sems, 0, sib).wait_recv()
        for j, chip in enumerate(chips):
            _rcopy(x_ref, rows(*chip, 1 - c), send_sems, recv_sems, 4 + j, sib).wait_recv()
        _finish(first + passed)

    return _comm_call(name, body, [jax.ShapeDtypeStruct((8,) + slab.shape, slab.dtype)], 7, [slab])[0]


D, BW, NH_B, NG, NP_S5, CG = 1024, 512, 4, 32, 64, 16
P_S5 = NG * NP_S5
C_GATES, C_XA, C_GA, C_Q, C_K, C_V, C_Z, C_UC, C_BA, N_PROJ = 0, 3072, 3584, 4096, 4608, 5120, 5632, 6144, 6656, 7168
_IN_OFFS = (0, 512, 1024, 1536, 2048, 2560, 3072, 3076, 3080, 3592, 6664)


def _blockdiag2(w):
    z = jnp.zeros((w.shape[0] // 2, 2 * w.shape[1], 2 * w.shape[2]), w.dtype)
    return z.at[:, :64, :64].set(w[0::2]).at[:, 64:, 64:].set(w[1::2])


def _unblockdiag2(d):
    return jnp.stack([d[:, :64, :64], d[:, 64:, 64:]], axis=1).reshape(-1, 64, 64)


def _expand(t):
    eye = jnp.eye(t.shape[0], dtype=t.dtype)
    return (t[:, :, None, :] * eye[:, None, :, None]).reshape(t.shape[0] * t.shape[1], -1)


def _layer_params(w):
    p = {}
    wi = w["w_in"]
    xa, ga, q, k, v, z, beta, alpha, uc, gates = [wi[:, a:b] for a, b in zip(_IN_OFFS[:-1], _IN_OFFS[1:])]
    ba = jnp.zeros((D, NH_B, LANES), wi.dtype).at[:, :, 0].set(beta).at[:, :, 1].set(alpha).reshape(D, NH_B * LANES)
    p["w_in_pad"] = jnp.concatenate([gates, xa, ga, q, k, v, z, uc, ba], axis=1)
    row = lambda a: a.reshape(1, -1).astype(F32)
    acw, bcw, fcw = w["a_conv_w"], w["b_conv_w"], w["ffn_conv_w"]
    p["a_taps"] = [row(acw[i]) for i in range(4)]
    p["a_cb"], p["a_br"], p["a_bi"], p["a_lam"] = (row(w[n]) for n in ("a_conv_b", "a_b_r", "a_b_i", "a_lam"))
    p["a_wr"], p["a_wi"] = _blockdiag2(w["a_w_r"]), _blockdiag2(w["a_w_i"])
    p["b_taps"] = [row(bcw[i, j * BW:(j + 1) * BW]) for j in range(3) for i in range(4)]
    p["b_alog"] = row(jnp.repeat(w["b_a_log"], LANES))
    p["b_dtb"] = row(jnp.repeat(w["b_dt_bias"], LANES))
    p["b_nw"] = row(w["b_norm"])
    p["c_lr"], p["c_li"] = w["c_lam_re"][:, None, :], w["c_lam_im"][:, None, :]
    p["c_ldt"] = w["c_log_dt"][:, None, None]
    p["c_btr"], p["c_bti"] = w["c_b_re"].transpose(0, 2, 1), w["c_b_im"].transpose(0, 2, 1)
    p["c_cre"] = _expand(w["c_c_re"].transpose(0, 2, 1))
    p["c_cimn"] = _expand(-w["c_c_im"].transpose(0, 2, 1))
    p["c_d"], p["c_glu_b"] = row(w["c_d"]), row(w["c_glu_b"])
    p["c_glu_w"] = w["c_glu_w"].astype(F32)
    p["wb"] = [w["w_branch"][k].astype(F32) for k in range(3)]
    p["bg"] = [row(w["b_gate"][k * D:(k + 1) * D]) for k in range(3)]
    for n in ("w_out", "xa_w_q", "xa_w_kv", "xa_w_o", "ffn_w_up", "ffn_w_down"):
        p[n] = w[n]
    for n in ("mix_norm", "xa_norm", "mem_norm", "ffn_norm"):
        p[n] = row(w[n])
    dff = fcw.shape[1] // 2
    p["f_taps"] = [row(fcw[i, :dff]) for i in range(3)] + [row(fcw[i, dff:]) for i in range(3)]
    p["f_b"] = [row(w["ffn_conv_b"][:dff]), row(w["ffn_conv_b"][dff:])]
    return p


def _chan(arr, cb=LANES):
    return (arr, (1, cb), lambda j: (0, j))


def _a_params(p):
    blk = lambda a: (a, (1, LANES, LANES), lambda j: (j, 0, 0))
    return [_chan(t) for t in p["a_taps"]] + [_chan(p["a_cb"]), blk(p["a_wr"]), blk(p["a_wi"]),
                                              _chan(p["a_br"]), _chan(p["a_bi"]), _chan(p["a_lam"])]


def _f_params(p):
    return [_chan(t) for t in p["f_taps"]] + [_chan(b) for b in p["f_b"]]


def _layer_fwd(x, mem, p, nb, s, m):
    sv = {"x": x}
    h = rowop_fwd("norm_mix", _norm_f, [(x, 0, D)], [p["mix_norm"]], [(D, BF16)])[0]
    proj = mm("in_proj", h, p["w_in_pad"], "nn")
    a, b = seqop_fwd("rglru_pre", _rglru_pre_f, [(proj, C_XA)], _a_params(p), [F32, F32], nb=nb, s=s, nblk=BW // LANES)
    ha = rscan("rglru_scan", a, b, False, nb=nb, s=s)
    yb, states = deltanet_fwd(proj, (C_Q, C_K, C_V, C_Z, C_BA), p["b_taps"], p["b_alog"], p["b_dtb"], p["b_nw"],
                              nb=nb, s=s, nh=NH_B)
    ar, ai, bbr, bbi = s5_param_fwd(p["c_lr"], p["c_li"], p["c_ldt"], p["c_btr"], p["c_bti"])
    bd = jnp.concatenate([_expand(bbr), _expand(bbi)], axis=1)
    ar2, ai2 = ar.reshape(1, P_S5), ai.reshape(1, P_S5)
    s5b = dict(nblk=BW // LANES, bk=LANES, bn=P_S5 * LANES // BW)
    s5c = dict(nblk=BW // LANES, bk=P_S5 * LANES // BW, bn=LANES)
    bu = mm_bd("s5_bu", proj, bd, "nn", a_col0=C_UC, **s5b)
    hr, hi = cscan("s5_scan", bu, ar2, ai2, False, nb=nb, s=s)
    ys = mm_bd("s5_y_im", hi, p["c_cimn"], "nn", res=mm_bd("s5_y_re", hr, p["c_cre"], "nn", **s5c), **s5c)
    yc = rowop_fwd("s5_post", _s5_post_f, [(ys, 0, BW), (proj, C_UC, BW)], [p["c_d"], p["c_glu_w"], p["c_glu_b"]],
                   [(BW, F32)])[0]
    merge_rows = [(ha, 0, BW), (proj, C_GA, BW), (yb, 0, BW), (yc, 0, BW),
                  (proj, 0, D), (proj, D, D), (proj, 2 * D, D)]
    merged = rowop_fwd("merge", _merge_f, merge_rows, p["wb"] + p["bg"], [(D, BF16)])[0]
    x1 = mm("out_proj", merged, p["w_out"], "nn", res=x)
    hq = rowop_fwd("norm_xa", _norm_f, [(x1, 0, D)], [p["xa_norm"]], [(D, BF16)])[0]
    q = mm("xa_q", hq, p["xa_w_q"], "nn", out_dtype=BF16)
    memn = rowop_fwd("norm_mem", _norm_f, [(mem, 0, D)], [p["mem_norm"]], [(D, BF16)])[0]
    kv = mm("xa_kv", memn, p["xa_w_kv"], "nn")
    o = attn_fwd(q, kv, nb=nb, s=s, m=m, nh=4)
    x2 = mm("xa_o", o, p["xa_w_o"], "nn", res=x1)
    hf = rowop_fwd("norm_ffn", _norm_f, [(x2, 0, D)], [p["ffn_norm"]], [(D, BF16)])[0]
    up = mm("ffn_up", hf, p["ffn_w_up"], "nn")
    dff = up.shape[1] // 2
    act = seqop_fwd("ffn_act", _ffn_act_f, [(up, 0), (up, dff)], _f_params(p), [BF16], nb=nb, s=s,
                    nblk=dff // LANES)[0]
    x3 = mm("ffn_down", act, p["ffn_w_down"], "nn", res=x2)
    sv.update(h=h, proj=proj, a=a, ha=ha, yb=yb, states=states, ar=ar2, ai=ai2, bd=bd, hr=hr, hi=hi, ys=ys, yc=yc,
              merged=merged, x1=x1, hq=hq, q=q, memn=memn, kv=kv, o=o, x2=x2, hf=hf, up=up, act=act)
    return x3, sv


def _layer_bwd(dx3, mem, sv, p, nb, s, m):
    g = {}
    up, dff = sv["up"], sv["up"].shape[1] // 2
    dact = mm("d_ffn_down_x", dx3, p["ffn_w_down"], "nt")
    g["ffn_w_down"] = mm("d_ffn_down_w", sv["act"], dx3, "tn")
    (dug, duv), dfp = seqop_bwd("ffn_act_bwd", _ffn_act_f, [(up, 0), (up, dff)], _f_params(p), [(dact, 0)],
                                lambda t0, e: (e[SUBLANES:],), nb=nb, s=s, nblk=dff // LANES, n=128)
    dhf = mm("d_ffn_up_x1", duv, p["ffn_w_up"], "nt", b_cols=(dff, dff),
             res=mm("d_ffn_up_x0", dug, p["ffn_w_up"], "nt", b_cols=(0, dff)))
    g["ffn_w_up"] = jnp.concatenate([mm("d_ffn_up_w0", sv["hf"], dug, "tn"), mm("d_ffn_up_w1", sv["hf"], duv, "tn")],
                                    axis=1)
    g["ffn_conv_w"] = jnp.concatenate([jnp.concatenate(dfp[0:3], axis=0), jnp.concatenate(dfp[3:6], axis=0)], axis=1)
    g["ffn_conv_b"] = jnp.concatenate([dfp[6], dfp[7]], axis=1)[0]
    (dx2,), (gn,) = rowop_bwd("norm_ffn_bwd", _norm_f, [(sv["x2"], 0, D)], [p["ffn_norm"]], [(dhf, 0, D)], [True],
                              add=(dx3, 0, D))
    g["ffn_norm"] = gn[0]
    do = mm("d_xa_o_x", dx2, p["xa_w_o"], "nt")
    g["xa_w_o"] = mm("d_xa_o_w", sv["o"], dx2, "tn")
    dq, dk, dv = attn_bwd(sv["q"], sv["kv"], do, nb=nb, s=s, m=m, nh=4)
    dkv = jnp.concatenate([dk, dv], axis=1)
    dhq = mm("d_xa_q_x", dq, p["xa_w_q"], "nt")
    g["xa_w_q"] = mm("d_xa_q_w", sv["hq"], dq, "tn")
    dmemn = mm("d_xa_kv_x", dkv, p["xa_w_kv"], "nt")
    g["xa_w_kv"] = mm("d_xa_kv_w", sv["memn"], dkv, "tn")
    (dx1,), (gn,) = rowop_bwd("norm_xa_bwd", _norm_f, [(sv["x1"], 0, D)], [p["xa_norm"]], [(dhq, 0, D)], [True],
                              add=(dx2, 0, D))
    g["xa_norm"] = gn[0]
    _, (gn,) = rowop_bwd("norm_mem_bwd", _norm_f, [(mem, 0, D)], [p["mem_norm"]], [(dmemn, 0, D)], [False])
    g["mem_norm"] = gn[0]
    proj = sv["proj"]
    dmerged = mm("d_out_proj_x", dx1, p["w_out"], "nt")
    g["w_out"] = mm("d_out_proj_w", sv["merged"], dx1, "tn")
    merge_rows = [(sv["ha"], 0, BW), (proj, C_GA, BW), (sv["yb"], 0, BW), (sv["yc"], 0, BW),
                  (proj, 0, D), (proj, D, D), (proj, 2 * D, D)]
    (dha, dga, dyb, dyc, dg0, dg1, dg2), dmp = rowop_bwd("merge_bwd", _merge_f, merge_rows, p["wb"] + p["bg"],
                                                          [(dmerged, 0, D)], [True] * 7)
    g["w_branch"] = jnp.stack(dmp[0:3])
    g["b_gate"] = jnp.concatenate(dmp[3:6], axis=1)[0]
    (dys, duc), (gd, ggw, ggb) = rowop_bwd("s5_post_bwd", _s5_post_f, [(sv["ys"], 0, BW), (proj, C_UC, BW)],
                                           [p["c_d"], p["c_glu_w"], p["c_glu_b"]], [(dyc, 0, BW)], [True, True])
    g["c_d"], g["c_glu_w"], g["c_glu_b"] = gd[0], ggw, ggb[0]
    cd = jnp.concatenate([p["c_cre"], p["c_cimn"]], axis=0)
    s5b = dict(nblk=BW // LANES, bk=LANES, bn=P_S5 * LANES // BW)
    s5c = dict(nblk=BW // LANES, bk=P_S5 * LANES // BW, bn=LANES)
    dhs = mm_bd("d_s5_y_x", dys, cd, "nt", **s5c)
    dcre = mm_bd("d_s5_y_wre", sv["hr"], dys, "tn", **s5c)
    dcimn = mm_bd("d_s5_y_wim", sv["hi"], dys, "tn", **s5c)
    gr, gi = cscan("s5_scan_bwd", dhs, sv["ar"], sv["ai"], True, nb=nb, s=s)
    dar, dai = s5_dlam(gr, gi, sv["hr"], sv["hi"], nb=nb, s=s)
    duc = mm_bd("d_s5_bu_x1", gi, sv["bd"], "nt", b_col0=P_S5,
                res=mm_bd("d_s5_bu_x0", gr, sv["bd"], "nt", res=duc, **s5b), **s5b)
    dbd_re = mm_bd("d_s5_bu_wre", proj, gr, "tn", a_col0=C_UC, **s5b)
    dbd_im = mm_bd("d_s5_bu_wim", proj, gi, "tn", a_col0=C_UC, **s5b)
    gpb = NG * LANES // BW
    eye = jnp.eye(gpb, dtype=F32)
    pick = lambda dmat, r, c: jnp.einsum("jgrhc,gh->jgrc", dmat.reshape(BW // LANES, gpb, r, gpb, c),
                                         eye).reshape(NG, r, c)
    glr, gli, gldt, gbtr, gbti = s5_param_bwd(
        p["c_lr"], p["c_li"], p["c_ldt"], p["c_btr"], p["c_bti"],
        (dar.reshape(NG, 1, NP_S5), dai.reshape(NG, 1, NP_S5), pick(dbd_re, CG, NP_S5), pick(dbd_im, CG, NP_S5)))
    g["c_lam_re"], g["c_lam_im"], g["c_log_dt"] = glr[:, 0, :], gli[:, 0, :], gldt[:, 0, 0]
    g["c_b_re"], g["c_b_im"] = gbtr.transpose(0, 2, 1), gbti.transpose(0, 2, 1)
    g["c_c_re"] = pick(dcre, NP_S5, CG).transpose(0, 2, 1)
    g["c_c_im"] = -pick(dcimn, NP_S5, CG).transpose(0, 2, 1)
    (dq_b, dk_b, dv_b, dz_b, dba), dbp = deltanet_bwd(proj, (C_Q, C_K, C_V, C_Z, C_BA), p["b_taps"], p["b_alog"],
                                                      p["b_dtb"], p["b_nw"], sv["states"], dyb, nb=nb, s=s, nh=NH_B)
    g["b_conv_w"] = jnp.concatenate([jnp.concatenate(dbp[4 * j:4 * j + 4], axis=0) for j in range(3)], axis=1)
    g["b_a_log"] = dbp[12].reshape(NH_B, LANES)[:, 0]
    g["b_dt_bias"] = dbp[13].reshape(NH_B, LANES)[:, 0]
    g["b_norm"] = dbp[14][0]
    gsc = rscan("rglru_scan_bwd", sv["a"], dha, True, nb=nb, s=s)
    (dxa,), dap = seqop_bwd("rglru_pre_bwd", _rglru_pre_f, [(proj, C_XA)], _a_params(p), [(gsc, 0), (sv["ha"], 0)],
                            lambda t0, ge, he: (ge[SUBLANES:] * taps(he, 2)[1], ge[SUBLANES:]),
                            nb=nb, s=s, nblk=BW // LANES)
    g["a_conv_w"] = jnp.concatenate(dap[0:4], axis=0)
    g["a_conv_b"], g["a_b_r"], g["a_b_i"], g["a_lam"] = dap[4][0], dap[7][0], dap[8][0], dap[9][0]
    g["a_w_r"], g["a_w_i"] = _unblockdiag2(dap[5]), _unblockdiag2(dap[6])
    dproj = jnp.concatenate([dg0, dg1, dg2, dxa, dga, dq_b, dk_b, dv_b, dz_b, duc, dba], axis=1).astype(BF16)
    dh = mm("d_in_proj_x", dproj, p["w_in_pad"], "nt")
    dwp = mm("d_in_proj_w", sv["h"], dproj, "tn")
    dba_w = dwp[:, C_BA:].reshape(D, NH_B, LANES)
    pieces = [dwp[:, c0:c0 + BW] for c0 in (C_XA, C_GA, C_Q, C_K, C_V, C_Z)]
    g["w_in"] = jnp.concatenate(pieces + [dba_w[:, :, 0], dba_w[:, :, 1], dwp[:, C_UC:C_UC + BW], dwp[:, :3 * D]],
                                axis=1)
    (dx,), (gn,) = rowop_bwd("norm_mix_bwd", _norm_f, [(sv["x"], 0, D)], [p["mix_norm"]], [(dh, 0, D)], [True],
                             add=(dx1, 0, D))
    g["mix_norm"] = gn[0]
    return dx, g


def _local_step(x, mem, target, w):
    nb, s, _ = x.shape
    m = mem.shape[1]
    depth = w["mix_norm"].shape[0]
    x2d, mem2d = x.reshape(nb * s, D), mem.reshape(nb * m, D)
    stacked = jax.vmap(_layer_params)({n: a for n, a in w.items() if n != "final_norm"})
    params, saved = [], []
    for l in range(depth):
        params.append(jax.tree.map(lambda a: a[l], stacked))
        x2d, sv = _layer_fwd(x2d, mem2d, params[l], nb, s, m)
        saved.append(sv)
    loss_row, dx, gfn = loss_head(x2d, target.reshape(nb * s, D), w["final_norm"].reshape(1, D))
    grads = [None] * depth
    for l in reversed(range(depth)):
        dx, grads[l] = _layer_bwd(dx, mem2d, saved[l], params[l], nb, s, m)
    grads.append({"final_norm": gfn[0]})
    return loss_row, dx.reshape(x.shape), grads


def _stack_layers(grads, name):
    if name == "final_norm":
        return grads[-1][name]
    return jnp.stack([g[name] for g in grads[:-1]])


SLAB_UNIT = 2 * 1024 * LANES


def _slab(parts, dtype, unit):
    flat = jnp.concatenate([a.astype(dtype).reshape(-1) for a in parts])
    pad = (-flat.shape[0]) % unit
    return jnp.pad(flat, (0, pad)) if pad else flat


def _unslab(flat, like):
    out, off = [], 0
    for a in like:
        out.append(flat[off:off + a.size].reshape(a.shape))
        off += a.size
    return out


def _gather_weights(name, local, names_axes, dtype):
    mine = [local[n].astype(dtype) for n, _ in names_axes]
    me = _chip_id(lax.axis_index("x"), lax.axis_index("y"))
    got = [_set_slot(g, me, a) for g, a in zip(gather_chips(name, mine), mine)]
    return {n: jnp.concatenate([g[j] for j in range(N_CHIPS)], axis=ax) for (n, ax), g in zip(names_axes, got)}


def kernel(x, mem, mix_norm, w_in, b_gate, a_conv_w, a_conv_b, a_w_r, a_b_r, a_w_i, a_b_i, a_lam, b_conv_w, b_a_log, b_dt_bias, b_norm, c_lam_re, c_lam_im, c_log_dt, c_b_re, c_b_im, c_c_re, c_c_im, c_d, c_glu_w, c_glu_b, w_branch, w_out, xa_norm, mem_norm, xa_w_q, xa_w_kv, xa_w_o, ffn_norm, ffn_w_up, ffn_conv_w, ffn_conv_b, ffn_w_down, final_norm, loss_target, m_mix_norm, m_w_in, m_b_gate, m_a_conv_w, m_a_conv_b, m_a_w_r, m_a_b_r, m_a_w_i, m_a_b_i, m_a_lam, m_b_conv_w, m_b_a_log, m_b_dt_bias, m_b_norm, m_c_lam_re, m_c_lam_im, m_c_log_dt, m_c_b_re, m_c_b_im, m_c_c_re, m_c_c_im, m_c_d, m_c_glu_w, m_c_glu_b, m_w_branch, m_w_out, m_xa_norm, m_mem_norm, m_xa_w_q, m_xa_w_kv, m_xa_w_o, m_ffn_norm, m_ffn_w_up, m_ffn_conv_w, m_ffn_conv_b, m_ffn_w_down, m_final_norm, v_mix_norm, v_w_in, v_b_gate, v_a_conv_w, v_a_conv_b, v_a_w_r, v_a_b_r, v_a_w_i, v_a_b_i, v_a_lam, v_b_conv_w, v_b_a_log, v_b_dt_bias, v_b_norm, v_c_lam_re, v_c_lam_im, v_c_log_dt, v_c_b_re, v_c_b_im, v_c_c_re, v_c_c_im, v_c_d, v_c_glu_w, v_c_glu_b, v_w_branch, v_w_out, v_xa_norm, v_mem_norm, v_xa_w_q, v_xa_w_kv, v_xa_w_o, v_ffn_norm, v_ffn_w_up, v_ffn_conv_w, v_ffn_conv_b, v_ffn_w_down, v_final_norm):
    env = dict(locals())
    w = {n: env[n] for n in WEIGHTS}
    mom = {n: env["m_" + n] for n in WEIGHTS}
    var = {n: env["v_" + n] for n in WEIGHTS}

    full = dict(w)
    full.update(_gather_weights("gather_mats", w, SHARDED_MATS, BF16))
    full.update(_gather_weights("gather_convs", w, SHARDED_CONVS, F32))
    loss_row, grad_x, glayers = _local_step(x, mem, loss_target, full)

    depth = mix_norm.shape[0]
    assert depth % 2 == 0
    hl = depth // 2
    res = {}
    by_chip = []
    for n, ax in SHARDED:
        width = w[n].shape[ax]
        by_chip.append(jnp.stack([jnp.stack([jnp.stack([
            lax.slice_in_dim(glayers[l][n], j * width, (j + 1) * width, axis=ax - 1)
            for l in range(hf * hl, (hf + 1) * hl)]) for j in range(N_CHIPS)]) for hf in range(2)]))
    core = lax.axis_index("c")
    chip = _chip_id(lax.axis_index("x"), lax.axis_index("y"))
    sibs = swap_halves("swap_halves", by_chip)
    part = [add_pair("add_cores_" + n, g2, sb, core) for (n, _), g2, sb in zip(SHARDED, by_chip, sibs)]
    recv = scatter_chips("scatter_chips", [p16 for _, p16 in part])
    mine = [add_chips("add_chips_" + n, p32, r, chip) for (n, _), (p32, _), r in zip(SHARDED, part, recv)]
    for (n, _), own, joined in zip(SHARDED, mine, join_halves("join_halves", mine)):
        g = _set_slot(joined, core * hl, own)
        res["grad_" + n] = g
        res["delta_" + n], res["new_m_" + n], res["new_v_" + n] = adamw("adamw_" + n, w[n], g, mom[n], var[n])

    unit = 1024 * LANES
    small = _slab([_stack_layers(glayers, n) for n in REPLICATED] + [loss_row[:, :1]], F32, unit).reshape(-1, LANES)
    g_rp = sumk("add_devices", _set_slot(gather_all("gather_all", small), 2 * chip + core, small))
    slab1 = lambda d: _slab([d[n] for n in REPLICATED] + [jnp.zeros((1, 1), F32)], F32, unit).reshape(-1, LANES)
    d_rp, m_rp, v_rp = adamw("adamw_replicated", slab1(w), g_rp, slab1(mom), slab1(var))
    like_rp = [w[n] for n in REPLICATED] + [jnp.zeros((1, 1), F32)]
    for kind, rp in (("grad", g_rp), ("delta", d_rp), ("new_m", m_rp), ("new_v", v_rp)):
        for n, a in zip(list(REPLICATED) + ["loss"], _unslab(rp.reshape(-1), like_rp)):
            res[kind + "_" + n] = a
    loss = res["grad_loss"].reshape(())
    return (loss, grad_x, *[res["grad_" + n] for n in WEIGHTS], *[res["delta_" + n] for n in WEIGHTS],
            *[res["new_m_" + n] for n in WEIGHTS], *[res["new_v_" + n] for n in WEIGHTS])
```

```python
import functools
import math

import jax
import jax.numpy as jnp
from jax import lax
from jax.experimental import pallas as pl
from jax.experimental.pallas import tpu as pltpu

F32, BF16 = jnp.float32, jnp.bfloat16
MESH = pl.DeviceIdType.MESH
ANY = pl.BlockSpec(memory_space=pl.ANY)

VMEM_LIMIT_V7X = 56 * 1024 * 1024
MM_TILE = 1024
LANES, SUBLANES = 128, 8
EPS = 1e-6
RG_C = 8.0
DN_CHUNK = 64
N_CHIPS = 4
ADAM_LR, ADAM_B1, ADAM_B2, ADAM_EPS, ADAM_WD, ADAM_STEP = 0.001, 0.9, 0.999, 1e-08, 0.01, 10

SHARDED_MATS = (("w_in", 2), ("c_glu_w", 1), ("w_branch", 3), ("w_out", 1), ("xa_w_q", 1), ("xa_w_kv", 2),
                ("xa_w_o", 1), ("ffn_w_up", 2), ("ffn_w_down", 1))
SHARDED_CONVS = (("a_conv_w", 2), ("b_conv_w", 2), ("ffn_conv_w", 2))
SHARDED = SHARDED_MATS + SHARDED_CONVS
WEIGHTS = ("mix_norm", "w_in", "b_gate", "a_conv_w", "a_conv_b", "a_w_r", "a_b_r", "a_w_i", "a_b_i", "a_lam",
           "b_conv_w", "b_a_log", "b_dt_bias", "b_norm", "c_lam_re", "c_lam_im", "c_log_dt", "c_b_re", "c_b_im",
           "c_c_re", "c_c_im", "c_d", "c_glu_w", "c_glu_b", "w_branch", "w_out", "xa_norm", "mem_norm", "xa_w_q",
           "xa_w_kv", "xa_w_o", "ffn_norm", "ffn_w_up", "ffn_conv_w", "ffn_conv_b", "ffn_w_down", "final_norm")
REPLICATED = tuple(n for n in WEIGHTS if n not in dict(SHARDED))


def _cparams(n_axes):
    return pltpu.CompilerParams(dimension_semantics=("arbitrary",) * n_axes, vmem_limit_bytes=VMEM_LIMIT_V7X)


def _tile(n, pref, off=0):
    t = pref
    while t >= LANES:
        if n % t == 0 and off % t == 0:
            return t
        t //= 2
    assert off == 0, (n, pref, off)
    return n


def _full_spec(shape, n_grid):
    nd = len(shape)
    if n_grid == 1:
        return pl.BlockSpec(shape, lambda i: (0,) * nd)
    return pl.BlockSpec(shape, lambda i, j: (0,) * nd)


_NN, _NT, _TN = ((1,), (0,)), ((1,), (1,)), ((0,), (0,))


def _dot(a, b, dims, hp):
    if hp:
        return lax.dot_general(a, b, (dims, ((), ())), precision=lax.Precision.HIGH, preferred_element_type=F32)
    return lax.dot_general(a.astype(BF16), b.astype(BF16), (dims, ((), ())), preferred_element_type=F32)


@functools.partial(jax.custom_vjp, nondiff_argnums=(2,))
def mm_nn(a, b, hp=False):
    return _dot(a, b, _NN, hp)


@functools.partial(jax.custom_vjp, nondiff_argnums=(2,))
def mm_nt(a, b, hp=False):
    return _dot(a, b, _NT, hp)


@functools.partial(jax.custom_vjp, nondiff_argnums=(2,))
def mm_tn(a, b, hp=False):
    return _dot(a, b, _TN, hp)


mm_nn.defvjp(lambda a, b, hp: (_dot(a, b, _NN, hp), (a, b)),
             lambda hp, r, g: (mm_nt(g, r[1], hp), mm_tn(r[0], g, hp)))
mm_nt.defvjp(lambda a, b, hp: (_dot(a, b, _NT, hp), (a, b)),
             lambda hp, r, g: (mm_nn(g, r[1], hp), mm_tn(g, r[0], hp)))
mm_tn.defvjp(lambda a, b, hp: (_dot(a, b, _TN, hp), (a, b)),
             lambda hp, r, g: (mm_nt(r[1], g, hp), mm_nn(r[0], g, hp)))


@functools.partial(jax.custom_vjp, nondiff_argnums=(1,))
def taps(xext, k):
    return tuple((xext if s == 0 else pltpu.roll(xext, s, 0))[SUBLANES:] for s in range(k))


def _taps_fwd(xext, k):
    return taps(xext, k), None


def _taps_bwd(k, _, gs):
    tot = None
    for s, g in enumerate(gs):
        gp = jnp.concatenate([jnp.zeros((SUBLANES, g.shape[1]), g.dtype), g], axis=0)
        if s:
            gp = pltpu.roll(gp, gp.shape[0] - s, 0)
        tot = gp if tot is None else tot + gp
    return (tot,)


taps.defvjp(_taps_fwd, _taps_bwd)


def _gelu(x):
    return x * (0.5 * (1.0 + jnp.tanh(0.7978845608028654 * (x + 0.044715 * (x * x * x)))))


def _sigmoid(x):
    return jax.nn.sigmoid(x)


def _softplus(x):
    return jnp.maximum(x, 0.0) + jnp.log1p(jnp.exp(-jnp.abs(x)))


def _rmsnorm(x, g):
    return x * lax.rsqrt(jnp.mean(x * x, axis=-1, keepdims=True) + EPS) * g


def _lane_pick(x, lane):
    sel = lax.broadcasted_iota(jnp.int32, x.shape, 1) == lane
    return jnp.sum(jnp.where(sel, x, 0.0), axis=1, keepdims=True)


def _load_ext(ref, t0, n):
    main = ref[pl.ds(t0, n), :].astype(F32)
    hstart = pl.multiple_of(jnp.maximum(t0 - SUBLANES, 0), SUBLANES)
    halo = ref[pl.ds(hstart, SUBLANES), :].astype(F32)
    halo = jnp.where(t0 > 0, halo, 0.0)
    return jnp.concatenate([halo, main], axis=0)


def mm(name, a, b, mode, *, out_dtype=F32, res=None, a_cols=None, b_cols=None):
    a0, aw = a_cols if a_cols else (0, a.shape[1])
    b0, bw = b_cols if b_cols else (0, b.shape[1])
    big = MM_TILE
    if mode == "nn":
        m, k, n = a.shape[0], aw, bw
        assert b.shape[0] == k
        tm, tk, tn = _tile(m, big), _tile(k, big, a0), _tile(n, big, b0)
        a_spec = pl.BlockSpec((tm, tk), lambda i, j, kk: (i, kk + a0 // tk))
        b_spec = pl.BlockSpec((tk, tn), lambda i, j, kk: (kk, j + b0 // tn))
        dims = _NN
    elif mode == "nt":
        m, k, n = a.shape[0], aw, b.shape[0]
        assert bw == k
        tm, tn = _tile(m, big), _tile(n, big)
        tk = _tile(k, big, math.gcd(a0, b0) if (a0 or b0) else 0)
        assert a0 % tk == 0 and b0 % tk == 0
        a_spec = pl.BlockSpec((tm, tk), lambda i, j, kk: (i, kk + a0 // tk))
        b_spec = pl.BlockSpec((tn, tk), lambda i, j, kk: (j, kk + b0 // tk))
        dims = _NT
    else:
        k, m, n = a.shape[0], aw, bw
        assert b.shape[0] == k
        tk, tm, tn = _tile(k, big), _tile(m, big, a0), _tile(n, big, b0)
        a_spec = pl.BlockSpec((tk, tm), lambda i, j, kk: (kk, i + a0 // tm))
        b_spec = pl.BlockSpec((tk, tn), lambda i, j, kk: (kk, j + b0 // tn))
        dims = _TN
    nk = k // tk
    has_res = res is not None

    def tile_dot(a_ref, b_ref):
        return lax.dot_general(a_ref[...].astype(BF16), b_ref[...].astype(BF16), (dims, ((), ())),
                               preferred_element_type=F32)

    def finish(out, refs, o_ref):
        if has_res:
            out = out + refs[2][...].astype(F32)
        o_ref[...] = out.astype(o_ref.dtype)

    def body_single(*refs):
        finish(tile_dot(refs[0], refs[1]), refs, refs[-1])

    def body_acc(*refs):
        o_ref, acc_ref = refs[-2], refs[-1]
        kk = pl.program_id(2)

        @pl.when(kk == 0)
        def _():
            acc_ref[...] = tile_dot(refs[0], refs[1])

        @pl.when((kk > 0) & (kk < nk - 1))
        def _():
            acc_ref[...] += tile_dot(refs[0], refs[1])

        @pl.when(kk == nk - 1)
        def _():
            finish(acc_ref[...] + tile_dot(refs[0], refs[1]), refs, o_ref)

    o_spec = pl.BlockSpec((tm, tn), lambda i, j, kk: (i, j))
    return pl.pallas_call(
        body_single if nk == 1 else body_acc, name=name, grid=(m // tm, n // tn, nk),
        in_specs=[a_spec, b_spec] + ([o_spec] if has_res else []), out_specs=o_spec,
        out_shape=jax.ShapeDtypeStruct((m, n), out_dtype),
        scratch_shapes=[] if nk == 1 else [pltpu.VMEM((tm, tn), F32)], compiler_params=_cparams(3),
    )(*((a, b, res) if has_res else (a, b)))


def mm_bd(name, a, b, mode, *, nblk, bk, bn, res=None, a_col0=0, b_col0=0):
    m = a.shape[0]
    tm = _tile(m, MM_TILE)
    if mode == "tn":
        tk = tm
        nk = m // tk

        def body(a_ref, b_ref, o_ref, acc_ref):
            kk = pl.program_id(1)
            part = lax.dot_general(a_ref[...].astype(BF16), b_ref[...].astype(BF16), (_TN, ((), ())),
                                   preferred_element_type=F32)

            @pl.when(kk == 0)
            def _():
                acc_ref[...] = part

            @pl.when(kk > 0)
            def _():
                acc_ref[...] += part

            @pl.when(kk == nk - 1)
            def _():
                o_ref[...] = acc_ref[...]

        return pl.pallas_call(
            body, name=name, grid=(nblk, nk),
            in_specs=[pl.BlockSpec((tk, bk), lambda i, kk: (kk, i + a_col0 // bk)),
                      pl.BlockSpec((tk, bn), lambda i, kk: (kk, i + b_col0 // bn))],
            out_specs=pl.BlockSpec((bk, bn), lambda i, kk: (i, 0)),
            out_shape=jax.ShapeDtypeStruct((nblk * bk, bn), F32),
            scratch_shapes=[pltpu.VMEM((bk, bn), F32)], compiler_params=_cparams(2),
        )(a, b)

    dims = _NN if mode == "nn" else _NT
    if mode == "nn":
        n = b.shape[1]
        a_spec = pl.BlockSpec((tm, bk), lambda i, j: (i, a_col0 // bk + j % nblk))
        b_spec = pl.BlockSpec((bk, bn), lambda i, j: (j % nblk, j))
    else:
        n = b.shape[0]
        bk, bn = bn, bk
        a_spec = pl.BlockSpec((tm, bk), lambda i, j: (i, j % nblk))
        b_spec = pl.BlockSpec((bn, bk), lambda i, j: (j, b_col0 // bk + j % nblk))
    has_res = res is not None

    def body(*refs):
        out = lax.dot_general(refs[0][...].astype(BF16), refs[1][...].astype(BF16), (dims, ((), ())),
                              preferred_element_type=F32)
        if has_res:
            out = out + refs[2][...]
        refs[-1][...] = out

    o_spec = pl.BlockSpec((tm, bn), lambda i, j: (i, j))
    return pl.pallas_call(
        body, name=name, grid=(m // tm, n // bn), in_specs=[a_spec, b_spec] + ([o_spec] if has_res else []),
        out_specs=o_spec, out_shape=jax.ShapeDtypeStruct((m, n), F32), compiler_params=_cparams(2),
    )(*((a, b, res) if has_res else (a, b)))


def _row_specs(rows, tm):
    return [pl.BlockSpec((tm, cw), lambda i, c=c0 // cw: (i, c)) for (_, c0, cw) in rows]


def rowop_fwd(name, f, rows, params, outs, tm=256):
    t = rows[0][0].shape[0]
    tm = min(tm, t)
    n_r, n_p = len(rows), len(params)

    def body(*refs):
        vals = f(*[r[...].astype(F32) for r in refs[:n_r]], *[p[...] for p in refs[n_r:n_r + n_p]])
        for o, v in zip(refs[n_r + n_p:], vals):
            o[...] = v.astype(o.dtype)

    res = pl.pallas_call(
        body, name=name, grid=(t // tm,),
        in_specs=_row_specs(rows, tm) + [_full_spec(p.shape, 1) for p in params],
        out_specs=[pl.BlockSpec((tm, cw), lambda i: (i, 0)) for cw, _ in outs],
        out_shape=[jax.ShapeDtypeStruct((t, cw), dt) for cw, dt in outs],
        compiler_params=_cparams(1),
    )(*[r[0] for r in rows], *params)
    return list(res)


def rowop_bwd(name, f, rows, params, couts, need, tm=256, add=None):
    t = rows[0][0].shape[0]
    tm = min(tm, t)
    n_r, n_p, n_c = len(rows), len(params), len(couts)
    want = [k for k in range(n_r) if need[k]]
    adds = [add] if add is not None else []

    def body(*refs):
        xs = [r[...].astype(F32) for r in refs[:n_r]]
        ps = [p[...] for p in refs[n_r:n_r + n_p]]
        cs = tuple(c[...].astype(F32) for c in refs[n_r + n_p:n_r + n_p + n_c])
        outs = refs[n_r + n_p + n_c + len(adds):]
        _, vjp = jax.vjp(f, *xs, *ps)
        gs = vjp(cs)
        for o, k in zip(outs[:len(want)], want):
            o[...] = gs[k] + refs[n_r + n_p + n_c][...] if (adds and k == want[0]) else gs[k]

        @pl.when(pl.program_id(0) == 0)
        def _():
            for o in outs[len(want):]:
                o[...] = jnp.zeros_like(o)

        for o, g in zip(outs[len(want):], gs[n_r:]):
            o[...] += g

    res = pl.pallas_call(
        body, name=name, grid=(t // tm,),
        in_specs=_row_specs(rows, tm) + [_full_spec(p.shape, 1) for p in params] + _row_specs(couts, tm)
        + _row_specs(adds, tm),
        out_specs=[pl.BlockSpec((tm, rows[k][2]), lambda i: (i, 0)) for k in want]
        + [_full_spec(p.shape, 1) for p in params],
        out_shape=[jax.ShapeDtypeStruct((t, rows[k][2]), F32) for k in want]
        + [jax.ShapeDtypeStruct(p.shape, F32) for p in params],
        compiler_params=_cparams(1),
    )(*[r[0] for r in rows], *params, *[c[0] for c in couts], *[a[0] for a in adds])
    return list(res[:len(want)]), list(res[len(want):])


def _seq_specs(seqs, s, cb, order):
    if order == "bj":
        return [pl.BlockSpec((s, cb), lambda b, j, c=c0 // cb: (b, c + j)) for (_, c0) in seqs]
    return [pl.BlockSpec((s, cb), lambda j, b, c=c0 // cb: (b, c + j)) for (_, c0) in seqs]


def _param_specs(params, order):
    if order == "bj":
        return [pl.BlockSpec(bs, lambda b, j, fn=fn: fn(j)) for (_, bs, fn) in params]
    return [pl.BlockSpec(bs, lambda j, b, fn=fn: fn(j)) for (_, bs, fn) in params]


def seqop_fwd(name, f, seqs, params, out_dtypes, *, nb, s, nblk, cb=LANES, n=256):
    n = min(n, s)
    n_s, n_p = len(seqs), len(params)

    def body(*refs):
        seq_refs, par_refs, out_refs = refs[:n_s], refs[n_s:n_s + n_p], refs[n_s + n_p:]

        def step(i, carry):
            t0 = pl.multiple_of(i * n, n)
            vals = f(t0, *[_load_ext(r, t0, n) for r in seq_refs], *[p[...] for p in par_refs])
            for o, v in zip(out_refs, vals):
                o[pl.ds(t0, n), :] = v.astype(o.dtype)
            return carry

        lax.fori_loop(0, s // n, step, 0)

    res = pl.pallas_call(
        body, name=name, grid=(nb, nblk),
        in_specs=_seq_specs(seqs, s, cb, "bj") + _param_specs(params, "bj"),
        out_specs=[pl.BlockSpec((s, cb), lambda b, j: (b, j)) for _ in out_dtypes],
        out_shape=[jax.ShapeDtypeStruct((nb * s, nblk * cb), dt) for dt in out_dtypes],
        compiler_params=_cparams(2),
    )(*[q[0] for q in seqs], *[p[0] for p in params])
    return list(res)


def seqop_bwd(name, f, seqs, params, cot_seqs, cot_fn, *, nb, s, nblk, cb=LANES, n=256):
    n = min(n, s)
    n_s, n_p, n_c = len(seqs), len(params), len(cot_seqs)
    nchunk = s // n

    def body(*refs):
        seq_refs, par_refs = refs[:n_s], refs[n_s:n_s + n_p]
        cot_refs = refs[n_s + n_p:n_s + n_p + n_c]
        dseq_refs = refs[n_s + n_p + n_c:n_s + n_p + n_c + n_s]
        dpar_refs = refs[n_s + n_p + n_c + n_s:]

        @pl.when(pl.program_id(1) == 0)
        def _():
            for o in dpar_refs:
                o[...] = jnp.zeros_like(o)

        def step(ii, halos):
            t0 = pl.multiple_of((nchunk - 1 - ii) * n, n)
            exts = [_load_ext(r, t0, n) for r in seq_refs]
            ps = [p[...] for p in par_refs]
            cots = cot_fn(t0, *[_load_ext(r, t0, n) for r in cot_refs])
            _, vjp = jax.vjp(lambda *args: f(t0, *args), *exts, *ps)
            gs = vjp(tuple(cots))
            tail = pl.multiple_of(t0 + n - SUBLANES, SUBLANES)
            new_halos = []
            for o, g, h in zip(dseq_refs, gs[:n_s], halos):
                o[pl.ds(t0, n), :] = g[SUBLANES:]
                o[pl.ds(tail, SUBLANES), :] += h
                new_halos.append(g[:SUBLANES])
            for o, g in zip(dpar_refs, gs[n_s:]):
                o[...] += g
            return tuple(new_halos)

        lax.fori_loop(0, nchunk, step, tuple(jnp.zeros((SUBLANES, cb), F32) for _ in range(n_s)))

    res = pl.pallas_call(
        body, name=name, grid=(nblk, nb),
        in_specs=_seq_specs(seqs, s, cb, "jb") + _param_specs(params, "jb") + _seq_specs(cot_seqs, s, cb, "jb"),
        out_specs=[pl.BlockSpec((s, cb), lambda j, b: (b, j)) for _ in seqs] + _param_specs(params, "jb"),
        out_shape=[jax.ShapeDtypeStruct((nb * s, nblk * cb), F32) for _ in seqs]
        + [jax.ShapeDtypeStruct(p[0].shape, F32) for p in params],
        compiler_params=_cparams(2),
    )(*[q[0] for q in seqs], *[p[0] for p in params], *[q[0] for q in cot_seqs])
    return list(res[:n_s]), list(res[n_s:])


def rscan(name, a, b, reverse, *, nb, s, cb=256):
    c = a.shape[1]
    cb = min(cb, c)

    def body(a_ref, b_ref, h_ref):
        def step(ii, car):
            i = (s // SUBLANES - 1 - ii) if reverse else ii
            base = pl.multiple_of(i * SUBLANES, SUBLANES)
            for r in (reversed(range(SUBLANES)) if reverse else range(SUBLANES)):
                row = pl.ds(base + r, 1)
                if reverse:
                    g = b_ref[row, :] + car
                    h_ref[row, :] = g
                    car = a_ref[row, :] * g
                else:
                    car = a_ref[row, :] * car + b_ref[row, :]
                    h_ref[row, :] = car
            return car

        lax.fori_loop(0, s // SUBLANES, step, jnp.zeros((1, cb), F32))

    spec = pl.BlockSpec((s, cb), lambda bb, j: (bb, j))
    return pl.pallas_call(
        body, name=name, grid=(nb, c // cb), in_specs=[spec, spec], out_specs=spec,
        out_shape=jax.ShapeDtypeStruct(a.shape, F32), compiler_params=_cparams(2),
    )(a, b)


def cscan(name, bu, ar, ai, reverse, *, nb, s, cb=512, seg=2048):
    p = ar.shape[1]
    cb = min(cb, p)
    nj = p // cb
    seg = min(seg, s)
    nseg = s // seg

    def sg_of(g):
        return (nseg - 1 - g) if reverse else g

    def body(br_ref, bi_ref, ar_ref, ai_ref, hr_ref, hi_ref, cr_scr, ci_scr):
        lr = ar_ref[...]
        li = -ai_ref[...] if reverse else ai_ref[...]

        @pl.when(pl.program_id(2) == 0)
        def _():
            cr_scr[...] = jnp.zeros_like(cr_scr)
            ci_scr[...] = jnp.zeros_like(ci_scr)

        def step(ii, car):
            hr, hi = car
            i = (seg // SUBLANES - 1 - ii) if reverse else ii
            base = pl.multiple_of(i * SUBLANES, SUBLANES)
            for r in (reversed(range(SUBLANES)) if reverse else range(SUBLANES)):
                row = pl.ds(base + r, 1)
                nr = lr * hr - li * hi + br_ref[row, :]
                ni = lr * hi + li * hr + bi_ref[row, :]
                hr, hi = nr, ni
                hr_ref[row, :] = hr
                hi_ref[row, :] = hi
            return hr, hi

        hr, hi = lax.fori_loop(0, seg // SUBLANES, step, (cr_scr[...], ci_scr[...]))
        cr_scr[...] = hr
        ci_scr[...] = hi

    o_spec = pl.BlockSpec((seg, cb), lambda bb, j, g: (bb * nseg + sg_of(g), j))
    l_spec = pl.BlockSpec((1, cb), lambda bb, j, g: (0, j))
    res = pl.pallas_call(
        body, name=name, grid=(nb, nj, nseg),
        in_specs=[o_spec, pl.BlockSpec((seg, cb), lambda bb, j, g: (bb * nseg + sg_of(g), j + nj)), l_spec, l_spec],
        out_specs=[o_spec, o_spec],
        out_shape=[jax.ShapeDtypeStruct((nb * s, p), F32)] * 2,
        scratch_shapes=[pltpu.VMEM((1, cb), F32), pltpu.VMEM((1, cb), F32)], compiler_params=_cparams(3),
    )(bu, bu, ar, ai)
    return res[0], res[1]


def _norm_f(x, g):
    return (_rmsnorm(x, g),)


def _rglru_pre_f(t0, xext, w0, w1, w2, w3, cb, wr, wi, br, bi, lam):
    x0, x1, x2, x3 = taps(xext, 4)
    xc = w3 * x0 + w2 * x1 + w1 * x2 + w0 * x3 + cb
    r = _sigmoid(mm_nn(xc, wr[0]) + br)
    ig = _sigmoid(mm_nn(xc, wi[0]) + bi)
    log_a = -RG_C * r * _softplus(-lam)
    a = jnp.exp(log_a)
    first = (lax.broadcasted_iota(jnp.int32, a.shape, 0) + t0) == 0
    mult = jnp.where(first, 1.0, jnp.sqrt(1.0 - jnp.exp(2.0 * log_a)))
    return a, mult * ig * xc


def _ffn_act_f(t0, gext, vext, g0, g1, g2, v0, v1, v2, bg, bv):
    ga, gb, gc = taps(gext, 3)
    va, vb, vc = taps(vext, 3)
    ug = g2 * ga + g1 * gb + g0 * gc + bg
    uv = v2 * va + v1 * vb + v0 * vc + bv
    return (_gelu(ug) * uv,)


def _s5_post_f(ys, uc, d, gw, gb):
    yc = _gelu(ys + d * uc)
    return (yc * _sigmoid(mm_nn(yc, gw) + gb),)


def _merge_f(ha, ga, yb, yc, g0, g1, g2, wb0, wb1, wb2, bg0, bg1, bg2):
    ya = ha * _gelu(ga)
    out = _sigmoid(g0 + bg0) * mm_nn(ya, wb0)
    out = out + _sigmoid(g1 + bg1) * mm_nn(yb, wb1)
    out = out + _sigmoid(g2 + bg2) * mm_nn(yc, wb2)
    return (out,)


def _s5_param_f(lr, li, ldt, btr, bti):
    dt = jnp.exp(ldt)
    mag = jnp.exp(lr * dt)
    ar, ai = mag * jnp.cos(li * dt), mag * jnp.sin(li * dt)
    den = lr * lr + li * li
    fr = ((ar - 1.0) * lr + ai * li) / den
    fi = (ai * lr - (ar - 1.0) * li) / den
    return ar, ai, fr * btr - fi * bti, fr * bti + fi * btr


def _each(fn, *lists):
    return [fn(*xs) for xs in zip(*lists)]


@jax.custom_vjp
def _tri_inv(*mats):
    return tuple(_tri_inv_products(list(mats)))


def _tri_inv_fwd(*mats):
    ts = tuple(_tri_inv_products(list(mats)))
    return ts, ts


def _tri_inv_bwd(ts, gs):
    inner = [mm_nt(g, t, True) for g, t in zip(gs, ts)]
    return tuple(-mm_tn(t, x, True) for t, x in zip(ts, inner))


_tri_inv.defvjp(_tri_inv_fwd, _tri_inv_bwd)


def _tri_inv_products(mats):
    c = mats[0].shape[0]
    ri = lax.broadcasted_iota(jnp.int32, (c, c), 0)
    ci = lax.broadcasted_iota(jnp.int32, (c, c), 1)
    eye = jnp.where(ri == ci, 1.0, 0.0).astype(F32)
    hp = lambda a, b: mm_nn(a, b, True)
    ad = [jnp.where((ri >> 4) == (ci >> 4), a, 0.0) for a in mats]
    ao = _each(lambda a, d: a - d, mats, ad)
    a2 = _each(hp, ad, ad)
    a4 = _each(hp, a2, a2)
    a8 = _each(hp, a4, a4)
    lo = _each(lambda d, s: hp(eye - d, eye + s), ad, a2)
    hi = _each(lambda s, e: hp(eye + s, eye + e), a4, a8)
    td = _each(hp, lo, hi)
    nn_ = _each(hp, td, ao)
    n2 = _each(hp, nn_, nn_)
    return _each(hp, _each(lambda n, s: hp(eye - n, eye + s), nn_, n2), td)


def _dn_chunk_f(nh, *args):
    per = [args[20 * h:20 * (h + 1)] for h in range(nh)]
    nw = args[20 * nh]
    state, qe, ke, ve, z, ba = ([p[j] for p in per] for j in range(6))
    c = DN_CHUNK
    dk = qe[0].shape[1]

    def conv_silu(xes, first):
        tp = [taps(x, 4) for x in xes]
        u = [p[first + 3] * t[0] + p[first + 2] * t[1] + p[first + 1] * t[2] + p[first] * t[3] for p, t in zip(per, tp)]
        return [x * _sigmoid(x) for x in u]

    def l2n(xs):
        return [x * lax.rsqrt(jnp.sum(x * x, axis=-1, keepdims=True) + EPS) for x in xs]

    q = [x * (dk ** -0.5) for x in l2n(conv_silu(qe, 6))]
    k = l2n(conv_silu(ke, 10))
    v = conv_silu(ve, 14)
    beta = [_sigmoid(_lane_pick(b, 0)) for b in ba]
    g = [-jnp.exp(_lane_pick(p[18], 0)) * _softplus(_lane_pick(b, 1) + _lane_pick(p[19], 0)) for p, b in zip(per, ba)]
    ri = lax.broadcasted_iota(jnp.int32, (c, c), 0)
    ci = lax.broadcasted_iota(jnp.int32, (c, c), 1)
    incl, strict = ri >= ci, ri > ci
    ltri = jnp.where(incl, 1.0, 0.0).astype(F32)
    mean_row = jnp.full((c, dk), 1.0 / dk, F32)
    gc_w = [mm_nn(ltri, jnp.broadcast_to(x, (c, dk)), True) for x in g]
    gc_c = [mm_nn(ltri, jnp.broadcast_to(x, (c, c)), True) for x in g]
    gc_r = [mm_nt(mean_row, x, True) for x in gc_w]
    gtot = [jnp.sum(x, axis=0, keepdims=True) for x in g]
    decay = _each(lambda a, b: jnp.exp(jnp.where(incl, a - b, -1e30)), gc_c, gc_r)
    e_gc = [jnp.exp(x) for x in gc_w]
    kb = _each(lambda a, b: a * b, k, beta)
    a_mat = _each(lambda a, b, d: jnp.where(strict, mm_nt(a, b) * d, 0.0), kb, k, decay)
    t_inv = _tri_inv(*a_mat)
    u = _each(lambda t, a, b: mm_nn(t, a * b, True), t_inv, v, beta)
    w = _each(lambda t, a, e: mm_nn(t, a * e, True), t_inv, kb, e_gc)
    qk = _each(lambda a, b, d: jnp.where(incl, mm_nt(a, b) * d, 0.0), q, k, decay)
    v_new = _each(lambda a, b, s: a - mm_nn(b, s), u, w, state)
    o = _each(lambda a, e, s, b, n: mm_nn(a * e, s) + mm_nn(b, n), q, e_gc, state, qk, v_new)
    new_state = _each(lambda s, t, a, gw, n: s * jnp.exp(t) + mm_tn(a * jnp.exp(t - gw), n), state, gtot, k, gc_w, v_new)
    o = _each(lambda x, zz: _rmsnorm(x, nw) * (zz * _sigmoid(zz)), o, z)
    return tuple(o) + tuple(new_state)


def _attn_f(q, k, v):
    sc = mm_nt(q, k) * (q.shape[1] ** -0.5)
    sc = sc - jnp.max(sc, axis=-1, keepdims=True)
    e = jnp.exp(sc)
    p = e / jnp.sum(e, axis=-1, keepdims=True)
    return (mm_nn(p, v),)


DN_SEG = 512


def _dn_ext(ref, edge, t0, i, h):
    lanes = pl.ds(h * LANES, LANES)
    hstart = pl.multiple_of(jnp.maximum(t0 - SUBLANES, 0), SUBLANES)
    halo = jnp.where(i > 0, ref[pl.ds(hstart, SUBLANES), lanes], edge[:, h * LANES:(h + 1) * LANES])
    return jnp.concatenate([halo, ref[pl.ds(t0, DN_CHUNK), lanes]], axis=0)


def _dn_layout(cols, nb, s, nh, reverse):
    seg = min(DN_SEG, s)
    nseg, w = s // seg, nh * LANES
    hb = seg // SUBLANES

    def sg_of(g):
        return (nseg - 1 - g) if reverse else g

    main = [pl.BlockSpec((seg, w), lambda b, g, c=c0 // w: (b * nseg + sg_of(g), c)) for c0 in cols]
    edge = [pl.BlockSpec((SUBLANES, w), lambda b, g, c=c0 // w: (jnp.maximum((b * nseg + sg_of(g)) * hb - 1, 0), c))
            for c0 in cols[:3]]
    par = [pl.BlockSpec((1, w), lambda b, g: (0, 0))] * 14 + [pl.BlockSpec((1, LANES), lambda b, g: (0, 0))]
    seq = pl.BlockSpec((seg, w), lambda b, g: (b * nseg + sg_of(g), 0))
    st = pl.BlockSpec((1, nh, seg // DN_CHUNK, LANES, LANES), lambda b, g: (b, 0, sg_of(g), 0, 0))
    return seg, nseg, main, edge, par, seq, st


def deltanet_fwd(proj, cols, taps12, alog, dtb, nw, *, nb, s, nh):
    seg, nseg, main, edge, par, seq, st = _dn_layout(cols, nb, s, nh, False)
    ncs = seg // DN_CHUNK

    def body(*refs):
        q_ref, k_ref, v_ref, z_ref, ba_ref = refs[:5]
        edges = [jnp.where(pl.program_id(1) > 0, r[...], 0.0) for r in refs[5:8]]
        pars = [p[...] for p in refs[8:23]]
        o_ref, st_ref, state_scr = refs[23], refs[24], refs[25]

        @pl.when(pl.program_id(1) == 0)
        def _():
            state_scr[...] = jnp.zeros_like(state_scr)

        def step(i, carry):
            t0 = pl.multiple_of(i * DN_CHUNK, DN_CHUNK)
            rows = pl.ds(t0, DN_CHUNK)
            states = [state_scr[h] for h in range(nh)]
            args = []
            for h in range(nh):
                lanes = pl.ds(h * LANES, LANES)
                args += ([states[h]] + [_dn_ext(r, e, t0, i, h) for r, e in zip((q_ref, k_ref, v_ref), edges)]
                         + [z_ref[rows, lanes], ba_ref[rows, lanes]]
                         + [p[:, h * LANES:(h + 1) * LANES] for p in pars[:14]])
            outs = _dn_chunk_f(nh, *args, pars[14])
            for h in range(nh):
                st_ref[0, h, i] = states[h]
                o_ref[rows, pl.ds(h * LANES, LANES)] = outs[h]
                state_scr[h] = outs[nh + h]
            return carry

        lax.fori_loop(0, ncs, step, 0)

    res = pl.pallas_call(
        body, name="deltanet_fwd", grid=(nb, nseg),
        in_specs=main + edge + par, out_specs=[seq, st],
        out_shape=[jax.ShapeDtypeStruct((nb * s, nh * LANES), F32),
                   jax.ShapeDtypeStruct((nb, nh, s // DN_CHUNK, LANES, LANES), F32)],
        scratch_shapes=[pltpu.VMEM((nh, LANES, LANES), F32)], compiler_params=_cparams(2),
    )(proj, proj, proj, proj, proj, proj, proj, proj, *taps12, alog, dtb, nw)
    return res[0], res[1]


def deltanet_bwd(proj, cols, taps12, alog, dtb, nw, states, dyb, *, nb, s, nh):
    seg, nseg, main, edge, par, seq, st = _dn_layout(cols, nb, s, nh, True)
    ncs = seg // DN_CHUNK

    def body(*refs):
        q_ref, k_ref, v_ref, z_ref, ba_ref = refs[:5]
        first_in_time = pl.program_id(1) == nseg - 1
        edges = [jnp.where(first_in_time, 0.0, r[...]) for r in refs[5:8]]
        pars = [p[...] for p in refs[8:23]]
        st_ref, do_ref = refs[23], refs[24]
        dseq_refs = refs[25:28]
        dz_ref, dba_ref = refs[28], refs[29]
        dpar_refs = refs[30:45]
        dstate_scr, dhalo_scr = refs[45], refs[46]

        @pl.when((pl.program_id(0) == 0) & (pl.program_id(1) == 0))
        def _():
            for o in dpar_refs:
                o[...] = jnp.zeros_like(o)

        @pl.when(pl.program_id(1) == 0)
        def _():
            dstate_scr[...] = jnp.zeros_like(dstate_scr)
            dhalo_scr[...] = jnp.zeros_like(dhalo_scr)

        def step(ii, carry):
            i = ncs - 1 - ii
            t0 = pl.multiple_of(i * DN_CHUNK, DN_CHUNK)
            rows = pl.ds(t0, DN_CHUNK)
            d_states = [dstate_scr[h] for h in range(nh)]
            d_halos = [dhalo_scr[k] for k in range(3)]
            args = []
            for h in range(nh):
                lanes = pl.ds(h * LANES, LANES)
                args += ([st_ref[0, h, i]] + [_dn_ext(r, e, t0, i, h) for r, e in zip((q_ref, k_ref, v_ref), edges)]
                         + [z_ref[rows, lanes], ba_ref[rows, lanes]]
                         + [p[:, h * LANES:(h + 1) * LANES] for p in pars[:14]])
            _, vjp = jax.vjp(functools.partial(_dn_chunk_f, nh), *args, pars[14])
            grads = vjp(tuple(do_ref[rows, pl.ds(h * LANES, LANES)] for h in range(nh)) + tuple(d_states))
            dpar_refs[14][...] += grads[20 * nh]
            for h in range(nh):
                lanes = pl.ds(h * LANES, LANES)
                gs = grads[20 * h:20 * (h + 1)]
                dstate_scr[h] = gs[0]
                for k, (o, g) in enumerate(zip(dseq_refs, gs[1:4])):
                    o[rows, lanes] = jnp.concatenate(
                        [g[SUBLANES:DN_CHUNK], g[DN_CHUNK:] + d_halos[k][:, h * LANES:(h + 1) * LANES]], axis=0)
                    dhalo_scr[k, :, lanes] = g[:SUBLANES]
                dz_ref[rows, lanes] = gs[4]
                dba_ref[rows, lanes] = gs[5]
                for o, g in zip(dpar_refs[:14], gs[6:20]):
                    o[:, lanes] += g
            return carry

        lax.fori_loop(0, ncs, step, 0)

    w = nh * LANES
    res = pl.pallas_call(
        body, name="deltanet_bwd", grid=(nb, nseg),
        in_specs=main + edge + par + [st, seq], out_specs=[seq] * 5 + par,
        out_shape=[jax.ShapeDtypeStruct((nb * s, w), F32)] * 5
        + [jax.ShapeDtypeStruct((1, w), F32)] * 14 + [jax.ShapeDtypeStruct((1, LANES), F32)],
        scratch_shapes=[pltpu.VMEM((nh, LANES, LANES), F32), pltpu.VMEM((3, SUBLANES, w), F32)],
        compiler_params=_cparams(2),
    )(proj, proj, proj, proj, proj, proj, proj, proj, *taps12, alog, dtb, nw, states, dyb)
    return list(res[:5]), list(res[5:])


def attn_fwd(q, kv, *, nb, s, m, nh, tq=512):
    hd = q.shape[1] // nh
    tq = min(tq, s)
    nq = s // tq

    def body(q_ref, k_ref, v_ref, o_ref):
        o_ref[...] = _attn_f(q_ref[...].astype(F32), k_ref[...], v_ref[...])[0].astype(o_ref.dtype)

    return pl.pallas_call(
        body, name="attn_fwd", grid=(nb, nh, nq),
        in_specs=[pl.BlockSpec((tq, hd), lambda b, h, i: (b * nq + i, h)),
                  pl.BlockSpec((m, hd), lambda b, h, i: (b, h)),
                  pl.BlockSpec((m, hd), lambda b, h, i: (b, nh + h))],
        out_specs=pl.BlockSpec((tq, hd), lambda b, h, i: (b * nq + i, h)),
        out_shape=jax.ShapeDtypeStruct(q.shape, BF16), compiler_params=_cparams(3),
    )(q, kv, kv)


def attn_bwd(q, kv, do, *, nb, s, m, nh, tq=512):
    hd = q.shape[1] // nh
    tq = min(tq, s)
    nq = s // tq

    def body(q_ref, k_ref, v_ref, do_ref, dq_ref, dk_ref, dv_ref):
        _, vjp = jax.vjp(_attn_f, q_ref[...].astype(F32), k_ref[...], v_ref[...])
        dq, dk, dv = vjp((do_ref[...].astype(F32),))
        dq_ref[...] = dq

        @pl.when(pl.program_id(2) == 0)
        def _():
            dk_ref[...] = jnp.zeros_like(dk_ref)
            dv_ref[...] = jnp.zeros_like(dv_ref)

        dk_ref[...] += dk
        dv_ref[...] += dv

    q_spec = pl.BlockSpec((tq, hd), lambda b, h, i: (b * nq + i, h))
    m_spec = pl.BlockSpec((m, hd), lambda b, h, i: (b, h))
    res = pl.pallas_call(
        body, name="attn_bwd", grid=(nb, nh, nq),
        in_specs=[q_spec, m_spec, pl.BlockSpec((m, hd), lambda b, h, i: (b, nh + h)), q_spec],
        out_specs=[q_spec, m_spec, m_spec],
        out_shape=[jax.ShapeDtypeStruct(q.shape, F32), jax.ShapeDtypeStruct((nb * m, nh * hd), F32),
                   jax.ShapeDtypeStruct((nb * m, nh * hd), F32)],
        compiler_params=_cparams(3),
    )(q, kv, kv, do)
    return res[0], res[1], res[2]


def loss_head(x, target, gain, tm=256):
    t, d = x.shape
    tm = min(tm, t)

    def body(x_ref, t_ref, g_ref, l_ref, dx_ref, dg_ref):
        tgt = t_ref[...]

        def lf(xv, gv):
            e = _rmsnorm(xv, gv) - tgt
            return 0.5 * jnp.sum(jnp.mean(e * e, axis=-1, keepdims=True), axis=0, keepdims=True)

        lv, vjp = jax.vjp(lf, x_ref[...], g_ref[...])
        dx, dg = vjp(jnp.ones((1, 1), F32))
        dx_ref[...] = dx

        @pl.when(pl.program_id(0) == 0)
        def _():
            l_ref[...] = jnp.zeros_like(l_ref)
            dg_ref[...] = jnp.zeros_like(dg_ref)

        l_ref[...] += jnp.broadcast_to(lv, l_ref.shape)
        dg_ref[...] += dg

    row = pl.BlockSpec((tm, d), lambda i: (i, 0))
    res = pl.pallas_call(
        body, name="loss_head", grid=(t // tm,),
        in_specs=[row, row, _full_spec((1, d), 1)],
        out_specs=[_full_spec((1, LANES), 1), row, _full_spec((1, d), 1)],
        out_shape=[jax.ShapeDtypeStruct((1, LANES), F32), jax.ShapeDtypeStruct((t, d), F32),
                   jax.ShapeDtypeStruct((1, d), F32)],
        compiler_params=_cparams(1),
    )(x, target, gain)
    return res[0], res[1], res[2]


def s5_dlam(gr, gi, hr, hi, *, nb, s, cb=256, n=256):
    p = gr.shape[1]
    cb, n = min(cb, p), min(n, s)

    def body(gr_ref, gi_ref, hr_ref, hi_ref, dar_ref, dai_ref):
        @pl.when(pl.program_id(1) == 0)
        def _():
            dar_ref[...] = jnp.zeros_like(dar_ref)
            dai_ref[...] = jnp.zeros_like(dai_ref)

        def step(i, carry):
            t0 = pl.multiple_of(i * n, n)
            g_r, g_i = gr_ref[pl.ds(t0, n), :], gi_ref[pl.ds(t0, n), :]
            p_r = taps(_load_ext(hr_ref, t0, n), 2)[1]
            p_i = taps(_load_ext(hi_ref, t0, n), 2)[1]
            dar_ref[...] += jnp.sum(g_r * p_r + g_i * p_i, axis=0, keepdims=True)
            dai_ref[...] += jnp.sum(g_i * p_r - g_r * p_i, axis=0, keepdims=True)
            return carry

        lax.fori_loop(0, s // n, step, 0)

    blk = pl.BlockSpec((s, cb), lambda j, b: (b, j))
    acc = pl.BlockSpec((1, cb), lambda j, b: (0, j))
    res = pl.pallas_call(
        body, name="s5_dlam", grid=(p // cb, nb), in_specs=[blk] * 4, out_specs=[acc, acc],
        out_shape=[jax.ShapeDtypeStruct((1, p), F32)] * 2, compiler_params=_cparams(2),
    )(gr, gi, hr, hi)
    return res[0], res[1]


def s5_param_fwd(lr, li, ldt, btr, bti):
    def body(*refs):
        vals = _s5_param_f(*[r[...] for r in refs[:5]])
        for o, v in zip(refs[5:], vals):
            o[...] = v

    return pl.pallas_call(
        body, name="s5_param_fwd",
        out_shape=[jax.ShapeDtypeStruct(lr.shape, F32)] * 2 + [jax.ShapeDtypeStruct(btr.shape, F32)] * 2,
    )(lr, li, ldt, btr, bti)


def s5_param_bwd(lr, li, ldt, btr, bti, cots):
    def body(*refs):
        _, vjp = jax.vjp(_s5_param_f, *[r[...] for r in refs[:5]])
        gs = vjp(tuple(r[...] for r in refs[5:9]))
        for o, g in zip(refs[9:], gs):
            o[...] = g

    return pl.pallas_call(
        body, name="s5_param_bwd",
        out_shape=[jax.ShapeDtypeStruct(a.shape, F32) for a in (lr, li, ldt, btr, bti)],
    )(lr, li, ldt, btr, bti, *cots)


def _row_block(r, c, copies):
    lanes = -(-c // LANES) * LANES
    rb = 2048
    while rb > 2 * SUBLANES and (r % rb or copies * rb * lanes * 4 > 12 * 1024 * 1024):
        rb //= 2
    return rb if r % rb == 0 else r


def _as2d(a, lead=0):
    return a.reshape(a.shape[:lead] + (-1, a.shape[-1]))


def sumk(name, buf):
    b3 = _as2d(buf, 1)
    k, r, c = b3.shape
    rb = _row_block(r, c, 2 * (k + 1))

    def body(b_ref, o_ref):
        acc = b_ref[0]
        for i in range(1, k):
            acc = acc + b_ref[i]
        o_ref[...] = acc

    out = pl.pallas_call(
        body, name=name, grid=(r // rb,),
        in_specs=[pl.BlockSpec((k, rb, c), lambda i: (0, i, 0))],
        out_specs=pl.BlockSpec((rb, c), lambda i: (i, 0)),
        out_shape=jax.ShapeDtypeStruct((r, c), F32), compiler_params=_cparams(1),
    )(b3)
    return out.reshape(buf.shape[1:])


def adamw(name, w, g, m, v):
    shape = w.shape
    w, g, m, v = (_as2d(a) for a in (w, g, m, v))
    r, c = w.shape
    rb = _row_block(r, c, 14)

    def body(w_ref, g_ref, m_ref, v_ref, d_ref, nm_ref, nv_ref):
        gv = g_ref[...]
        nm = ADAM_B1 * m_ref[...] + (1.0 - ADAM_B1) * gv
        nv = ADAM_B2 * v_ref[...] + (1.0 - ADAM_B2) * (gv * gv)
        m_hat = nm / (1.0 - ADAM_B1 ** ADAM_STEP)
        v_hat = nv / (1.0 - ADAM_B2 ** ADAM_STEP)
        d_ref[...] = -ADAM_LR * (m_hat / (jnp.sqrt(v_hat) + ADAM_EPS) + ADAM_WD * w_ref[...])
        nm_ref[...] = nm
        nv_ref[...] = nv

    spec = pl.BlockSpec((rb, c), lambda i: (i, 0))
    res = pl.pallas_call(
        body, name=name, grid=(r // rb,), in_specs=[spec] * 4, out_specs=[spec] * 3,
        out_shape=[jax.ShapeDtypeStruct(w.shape, F32)] * 3, compiler_params=_cparams(1),
    )(w, g, m, v)
    return tuple(a.reshape(shape) for a in res)


def _place():
    x, y, c = lax.axis_index("x"), lax.axis_index("y"), lax.axis_index("c")
    chips = [(1 - x, y), (x, 1 - y), (1 - x, 1 - y)]
    return x, y, c, chips


def _chip_id(px, py):
    return 2 * px + py


def _rcopy(src, dst, send_sems, recv_sems, k, to):
    return pltpu.make_async_remote_copy(src_ref=src, dst_ref=dst, send_sem=send_sems.at[k], recv_sem=recv_sems.at[k],
                                        device_id=to, device_id_type=MESH)


def _comm_call(name, body, out_shapes, n_sems, args):
    return pl.pallas_call(
        body, name=name, out_shape=out_shapes, in_specs=[ANY] * len(args), out_specs=[ANY] * len(out_shapes),
        scratch_shapes=[pltpu.SemaphoreType.DMA((n_sems,)), pltpu.SemaphoreType.DMA((n_sems,))],
    )(*args)


def _finish(remote):
    for cp in remote:
        cp.wait_send()


def _set_slot(buf, slot, val):
    if val.ndim < buf.ndim:
        val = val[None]
    return lax.dynamic_update_slice(buf, val.astype(buf.dtype), (slot,) + (0,) * (buf.ndim - 1))


def gather_chips(name, arrays):
    n = len(arrays)
    halves = [a.shape[0] // 2 for a in arrays]

    def body(*refs):
        ins, outs = refs[:n], refs[n:2 * n]
        send_sems, recv_sems = refs[2 * n:]
        x, y, c, chips = _place()
        me, sib = _chip_id(x, y), (x, y, 1 - c)
        half = [pl.ds(c * h, h) for h in halves]
        other = [pl.ds((1 - c) * h, h) for h in halves]
        remote = []
        for a in range(n):
            for j, chip in enumerate(chips):
                remote.append(_rcopy(ins[a].at[half[a]], outs[a].at[me, half[a]], send_sems, recv_sems, 6 * a + j,
                                     (*chip, c)))
                remote[-1].start()
        for a in range(n):
            for j, chip in enumerate(chips):
                landed = outs[a].at[_chip_id(*chip), half[a]]
                _rcopy(ins[a].at[half[a]], landed, send_sems, recv_sems, 6 * a + j, sib).wait_recv()
                remote.append(_rcopy(landed, landed, send_sems, recv_sems, 6 * a + 3 + j, sib))
                remote[-1].start()
        for a in range(n):
            for j, chip in enumerate(chips):
                _rcopy(ins[a].at[half[a]], outs[a].at[_chip_id(*chip), other[a]], send_sems, recv_sems, 6 * a + 3 + j,
                       sib).wait_recv()
        _finish(remote)

    shapes = [jax.ShapeDtypeStruct((N_CHIPS,) + a.shape, a.dtype) for a in arrays]
    return _comm_call(name, body, shapes, 6 * n, arrays)


def swap_halves(name, arrays):
    n = len(arrays)

    def body(*refs):
        ins, outs = refs[:n], refs[n:2 * n]
        send_sems, recv_sems = refs[2 * n:]
        x, y, c, _ = _place()
        sib = (x, y, 1 - c)
        remote = []
        for a in range(n):
            for k in range(N_CHIPS):
                remote.append(_rcopy(ins[a].at[1 - c, k], outs[a].at[k], send_sems, recv_sems, N_CHIPS * a + k, sib))
                remote[-1].start()
        for a in range(n):
            for k in range(N_CHIPS):
                _rcopy(ins[a].at[1 - c, k], outs[a].at[k], send_sems, recv_sems, N_CHIPS * a + k, sib).wait_recv()
        _finish(remote)

    shapes = [jax.ShapeDtypeStruct(a.shape[1:], a.dtype) for a in arrays]
    return _comm_call(name, body, shapes, N_CHIPS * n, arrays)


def scatter_chips(name, arrays):
    n = len(arrays)

    def body(*refs):
        ins, outs = refs[:n], refs[n:2 * n]
        send_sems, recv_sems = refs[2 * n:]
        x, y, c, chips = _place()
        remote = []
        for a in range(n):
            for j, chip in enumerate(chips):
                remote.append(_rcopy(ins[a].at[_chip_id(*chip)], outs[a].at[j], send_sems, recv_sems, 3 * a + j,
                                     (*chip, c)))
                remote[-1].start()
        for a in range(n):
            for j, chip in enumerate(chips):
                _rcopy(ins[a].at[0], outs[a].at[j], send_sems, recv_sems, 3 * a + j, (*chip, c)).wait_recv()
        _finish(remote)

    shapes = [jax.ShapeDtypeStruct((3,) + a.shape[1:], a.dtype) for a in arrays]
    return _comm_call(name, body, shapes, 3 * n, arrays)


def join_halves(name, arrays):
    n = len(arrays)
    offs = [sum(a.shape[0] for a in arrays[:i]) for i in range(n)]

    def body(*refs):
        ins, outs = refs[:n], refs[n:2 * n]
        send_sems, recv_sems = refs[2 * n:]
        x, y, c, _ = _place()
        sib = (x, y, 1 - c)
        remote = []
        for a in range(n):
            h = arrays[a].shape[0]
            for k in range(h):
                remote.append(_rcopy(ins[a].at[k], outs[a].at[c * h + k], send_sems, recv_sems, offs[a] + k, sib))
                remote[-1].start()
        for a in range(n):
            h = arrays[a].shape[0]
            for k in range(h):
                _rcopy(ins[a].at[k], outs[a].at[(1 - c) * h + k], send_sems, recv_sems, offs[a] + k, sib).wait_recv()
        _finish(remote)

    shapes = [jax.ShapeDtypeStruct((2 * a.shape[0],) + a.shape[1:], a.dtype) for a in arrays]
    return _comm_call(name, body, shapes, offs[-1] + arrays[-1].shape[0], arrays)


def add_pair(name, mine2, other, c):
    a3, o2 = _as2d(mine2, 1), _as2d(other)
    r, cols = o2.shape
    rb = _row_block(r, cols, 8)

    def body(c_ref, a_ref, o_ref, out_ref, out16_ref):
        tot = a_ref[0] + o_ref[...]
        out_ref[...] = tot
        out16_ref[...] = tot.astype(BF16)

    spec = pl.BlockSpec((rb, cols), lambda i, s: (i, 0))
    out, out16 = pl.pallas_call(
        body, name=name, out_shape=[jax.ShapeDtypeStruct((r, cols), F32), jax.ShapeDtypeStruct((r, cols), BF16)],
        grid_spec=pltpu.PrefetchScalarGridSpec(
            num_scalar_prefetch=1, grid=(r // rb,),
            in_specs=[pl.BlockSpec((1, rb, cols), lambda i, s: (s[0], i, 0)), spec], out_specs=[spec, spec]),
        compiler_params=_cparams(1),
    )(jnp.reshape(c, (1,)).astype(jnp.int32), a3, o2)
    return out.reshape(other.shape), out16.reshape(other.shape)


def add_chips(name, part, recv, me):
    p3, r3 = _as2d(part, 1), _as2d(recv, 1)
    _, r, cols = p3.shape
    rb = _row_block(r, cols, 10)

    def body(me_ref, p_ref, r_ref, out_ref):
        own, acc = p_ref[0], None
        got = [r_ref[j].astype(F32) for j in range(3)]
        for k in range(N_CHIPS):
            rel = jnp.full(own.shape, me_ref[0] ^ k, jnp.int32)
            term = jnp.where(rel == 0, own, jnp.where(rel == 2, got[0], jnp.where(rel == 1, got[1], got[2])))
            acc = term if acc is None else acc + term
        out_ref[...] = acc

    out = pl.pallas_call(
        body, name=name, out_shape=jax.ShapeDtypeStruct((r, cols), F32),
        grid_spec=pltpu.PrefetchScalarGridSpec(
            num_scalar_prefetch=1, grid=(r // rb,),
            in_specs=[pl.BlockSpec((1, rb, cols), lambda i, s: (s[0], i, 0)),
                      pl.BlockSpec((3, rb, cols), lambda i, s: (0, i, 0))],
            out_specs=pl.BlockSpec((rb, cols), lambda i, s: (i, 0))),
        compiler_params=_cparams(1),
    )(jnp.reshape(me, (1,)).astype(jnp.int32), p3, r3)
    return out.reshape(part.shape[1:])


def gather_all(name, slab):
    def body(x_ref, out_ref, send_sems, recv_sems):
        x, y, c, chips = _place()
        me, sib = (x, y, c), (x, y, 1 - c)

        def rows(px, py, pc):
            return out_ref.at[4 * px + 2 * py + pc]

        first = [_rcopy(x_ref, rows(*me), send_sems, recv_sems, 0, sib)]
        first += [_rcopy(x_ref, rows(*me), send_sems, recv_sems, 1 + j, (*chip, c)) for j, chip in enumerate(chips)]
        for cp in first:
            cp.start()
        passed = []
        for j, chip in enumerate(chips):
            landed = rows(*chip, c)
            _rcopy(x_ref, landed, send_sems, recv_sems, 1 + j, sib).wait_recv()
            passed.append(_rcopy(landed, landed, send_sems, recv_sems, 4 + j, sib))
            passed[-1].start()
        _rcopy(x_ref, rows(*sib), send_sems, recv_sems, 0, sib).wait_recv()
        for j, chip in enumerate(chips):
            _rcopy(x_ref, rows(*chip, 1 - c), send_sems, recv_sems, 4 + j, sib).wait_recv()
        _finish(first + passed)

    return _comm_call(name, body, [jax.ShapeDtypeStruct((8,) + slab.shape, slab.dtype)], 7, [slab])[0]


D, BW, NH_B, NG, NP_S5, CG = 1024, 512, 4, 32, 64, 16
P_S5 = NG * NP_S5
C_GATES, C_XA, C_GA, C_Q, C_K, C_V, C_Z, C_UC, C_BA, N_PROJ = 0, 3072, 3584, 4096, 4608, 5120, 5632, 6144, 6656, 7168
_IN_OFFS = (0, 512, 1024, 1536, 2048, 2560, 3072, 3076, 3080, 3592, 6664)


def _blockdiag2(w):
    z = jnp.zeros((w.shape[0] // 2, 2 * w.shape[1], 2 * w.shape[2]), w.dtype)
    return z.at[:, :64, :64].set(w[0::2]).at[:, 64:, 64:].set(w[1::2])


def _unblockdiag2(d):
    return jnp.stack([d[:, :64, :64], d[:, 64:, 64:]], axis=1).reshape(-1, 64, 64)


def _expand(t):
    eye = jnp.eye(t.shape[0], dtype=t.dtype)
    return (t[:, :, None, :] * eye[:, None, :, None]).reshape(t.shape[0] * t.shape[1], -1)


def _layer_params(w):
    p = {}
    wi = w["w_in"]
    xa, ga, q, k, v, z, beta, alpha, uc, gates = [wi[:, a:b] for a, b in zip(_IN_OFFS[:-1], _IN_OFFS[1:])]
    ba = jnp.zeros((D, NH_B, LANES), wi.dtype).at[:, :, 0].set(beta).at[:, :, 1].set(alpha).reshape(D, NH_B * LANES)
    p["w_in_pad"] = jnp.concatenate([gates, xa, ga, q, k, v, z, uc, ba], axis=1)
    row = lambda a: a.reshape(1, -1).astype(F32)
    acw, bcw, fcw = w["a_conv_w"], w["b_conv_w"], w["ffn_conv_w"]
    p["a_taps"] = [row(acw[i]) for i in range(4)]
    p["a_cb"], p["a_br"], p["a_bi"], p["a_lam"] = (row(w[n]) for n in ("a_conv_b", "a_b_r", "a_b_i", "a_lam"))
    p["a_wr"], p["a_wi"] = _blockdiag2(w["a_w_r"]), _blockdiag2(w["a_w_i"])
    p["b_taps"] = [row(bcw[i, j * BW:(j + 1) * BW]) for j in range(3) for i in range(4)]
    p["b_alog"] = row(jnp.repeat(w["b_a_log"], LANES))
    p["b_dtb"] = row(jnp.repeat(w["b_dt_bias"], LANES))
    p["b_nw"] = row(w["b_norm"])
    p["c_lr"], p["c_li"] = w["c_lam_re"][:, None, :], w["c_lam_im"][:, None, :]
    p["c_ldt"] = w["c_log_dt"][:, None, None]
    p["c_btr"], p["c_bti"] = w["c_b_re"].transpose(0, 2, 1), w["c_b_im"].transpose(0, 2, 1)
    p["c_cre"] = _expand(w["c_c_re"].transpose(0, 2, 1))
    p["c_cimn"] = _expand(-w["c_c_im"].transpose(0, 2, 1))
    p["c_d"], p["c_glu_b"] = row(w["c_d"]), row(w["c_glu_b"])
    p["c_glu_w"] = w["c_glu_w"].astype(F32)
    p["wb"] = [w["w_branch"][k].astype(F32) for k in range(3)]
    p["bg"] = [row(w["b_gate"][k * D:(k + 1) * D]) for k in range(3)]
    for n in ("w_out", "xa_w_q", "xa_w_kv", "xa_w_o", "ffn_w_up", "ffn_w_down"):
        p[n] = w[n]
    for n in ("mix_norm", "xa_norm", "mem_norm", "ffn_norm"):
        p[n] = row(w[n])
    dff = fcw.shape[1] // 2
    p["f_taps"] = [row(fcw[i, :dff]) for i in range(3)] + [row(fcw[i, dff:]) for i in range(3)]
    p["f_b"] = [row(w["ffn_conv_b"][:dff]), row(w["ffn_conv_b"][dff:])]
    return p


def _chan(arr, cb=LANES):
    return (arr, (1, cb), lambda j: (0, j))


def _a_params(p):
    blk = lambda a: (a, (1, LANES, LANES), lambda j: (j, 0, 0))
    return [_chan(t) for t in p["a_taps"]] + [_chan(p["a_cb"]), blk(p["a_wr"]), blk(p["a_wi"]),
                                              _chan(p["a_br"]), _chan(p["a_bi"]), _chan(p["a_lam"])]


def _f_params(p):
    return [_chan(t) for t in p["f_taps"]] + [_chan(b) for b in p["f_b"]]


def _layer_fwd(x, mem, p, nb, s, m):
    sv = {"x": x}
    h = rowop_fwd("norm_mix", _norm_f, [(x, 0, D)], [p["mix_norm"]], [(D, BF16)])[0]
    proj = mm("in_proj", h, p["w_in_pad"], "nn")
    a, b = seqop_fwd("rglru_pre", _rglru_pre_f, [(proj, C_XA)], _a_params(p), [F32, F32], nb=nb, s=s, nblk=BW // LANES)
    ha = rscan("rglru_scan", a, b, False, nb=nb, s=s)
    yb, states = deltanet_fwd(proj, (C_Q, C_K, C_V, C_Z, C_BA), p["b_taps"], p["b_alog"], p["b_dtb"], p["b_nw"],
                              nb=nb, s=s, nh=NH_B)
    ar, ai, bbr, bbi = s5_param_fwd(p["c_lr"], p["c_li"], p["c_ldt"], p["c_btr"], p["c_bti"])
    bd = jnp.concatenate([_expand(bbr), _expand(bbi)], axis=1)
    ar2, ai2 = ar.reshape(1, P_S5), ai.reshape(1, P_S5)
    s5b = dict(nblk=BW // LANES, bk=LANES, bn=P_S5 * LANES // BW)
    s5c = dict(nblk=BW // LANES, bk=P_S5 * LANES // BW, bn=LANES)
    bu = mm_bd("s5_bu", proj, bd, "nn", a_col0=C_UC, **s5b)
    hr, hi = cscan("s5_scan", bu, ar2, ai2, False, nb=nb, s=s)
    ys = mm_bd("s5_y_im", hi, p["c_cimn"], "nn", res=mm_bd("s5_y_re", hr, p["c_cre"], "nn", **s5c), **s5c)
    yc = rowop_fwd("s5_post", _s5_post_f, [(ys, 0, BW), (proj, C_UC, BW)], [p["c_d"], p["c_glu_w"], p["c_glu_b"]],
                   [(BW, F32)])[0]
    merge_rows = [(ha, 0, BW), (proj, C_GA, BW), (yb, 0, BW), (yc, 0, BW),
                  (proj, 0, D), (proj, D, D), (proj, 2 * D, D)]
    merged = rowop_fwd("merge", _merge_f, merge_rows, p["wb"] + p["bg"], [(D, BF16)])[0]
    x1 = mm("out_proj", merged, p["w_out"], "nn", res=x)
    hq = rowop_fwd("norm_xa", _norm_f, [(x1, 0, D)], [p["xa_norm"]], [(D, BF16)])[0]
    q = mm("xa_q", hq, p["xa_w_q"], "nn", out_dtype=BF16)
    memn = rowop_fwd("norm_mem", _norm_f, [(mem, 0, D)], [p["mem_norm"]], [(D, BF16)])[0]
    kv = mm("xa_kv", memn, p["xa_w_kv"], "nn")
    o = attn_fwd(q, kv, nb=nb, s=s, m=m, nh=4)
    x2 = mm("xa_o", o, p["xa_w_o"], "nn", res=x1)
    hf = rowop_fwd("norm_ffn", _norm_f, [(x2, 0, D)], [p["ffn_norm"]], [(D, BF16)])[0]
    up = mm("ffn_up", hf, p["ffn_w_up"], "nn")
    dff = up.shape[1] // 2
    act = seqop_fwd("ffn_act", _ffn_act_f, [(up, 0), (up, dff)], _f_params(p), [BF16], nb=nb, s=s,
                    nblk=dff // LANES)[0]
    x3 = mm("ffn_down", act, p["ffn_w_down"], "nn", res=x2)
    sv.update(h=h, proj=proj, a=a, ha=ha, yb=yb, states=states, ar=ar2, ai=ai2, bd=bd, hr=hr, hi=hi, ys=ys, yc=yc,
              merged=merged, x1=x1, hq=hq, q=q, memn=memn, kv=kv, o=o, x2=x2, hf=hf, up=up, act=act)
    return x3, sv


def _layer_bwd(dx3, mem, sv, p, nb, s, m):
    g = {}
    up, dff = sv["up"], sv["up"].shape[1] // 2
    dact = mm("d_ffn_down_x", dx3, p["ffn_w_down"], "nt")
    g["ffn_w_down"] = mm("d_ffn_down_w", sv["act"], dx3, "tn")
    (dug, duv), dfp = seqop_bwd("ffn_act_bwd", _ffn_act_f, [(up, 0), (up, dff)], _f_params(p), [(dact, 0)],
                                lambda t0, e: (e[SUBLANES:],), nb=nb, s=s, nblk=dff // LANES, n=128)
    dhf = mm("d_ffn_up_x1", duv, p["ffn_w_up"], "nt", b_cols=(dff, dff),
             res=mm("d_ffn_up_x0", dug, p["ffn_w_up"], "nt", b_cols=(0, dff)))
    g["ffn_w_up"] = jnp.concatenate([mm("d_ffn_up_w0", sv["hf"], dug, "tn"), mm("d_ffn_up_w1", sv["hf"], duv, "tn")],
                                    axis=1)
    g["ffn_conv_w"] = jnp.concatenate([jnp.concatenate(dfp[0:3], axis=0), jnp.concatenate(dfp[3:6], axis=0)], axis=1)
    g["ffn_conv_b"] = jnp.concatenate([dfp[6], dfp[7]], axis=1)[0]
    (dx2,), (gn,) = rowop_bwd("norm_ffn_bwd", _norm_f, [(sv["x2"], 0, D)], [p["ffn_norm"]], [(dhf, 0, D)], [True],
                              add=(dx3, 0, D))
    g["ffn_norm"] = gn[0]
    do = mm("d_xa_o_x", dx2, p["xa_w_o"], "nt")
    g["xa_w_o"] = mm("d_xa_o_w", sv["o"], dx2, "tn")
    dq, dk, dv = attn_bwd(sv["q"], sv["kv"], do, nb=nb, s=s, m=m, nh=4)
    dkv = jnp.concatenate([dk, dv], axis=1)
    dhq = mm("d_xa_q_x", dq, p["xa_w_q"], "nt")
    g["xa_w_q"] = mm("d_xa_q_w", sv["hq"], dq, "tn")
    dmemn = mm("d_xa_kv_x", dkv, p["xa_w_kv"], "nt")
    g["xa_w_kv"] = mm("d_xa_kv_w", sv["memn"], dkv, "tn")
    (dx1,), (gn,) = rowop_bwd("norm_xa_bwd", _norm_f, [(sv["x1"], 0, D)], [p["xa_norm"]], [(dhq, 0, D)], [True],
                              add=(dx2, 0, D))
    g["xa_norm"] = gn[0]
    _, (gn,) = rowop_bwd("norm_mem_bwd", _norm_f, [(mem, 0, D)], [p["mem_norm"]], [(dmemn, 0, D)], [False])
    g["mem_norm"] = gn[0]
    proj = sv["proj"]
    dmerged = mm("d_out_proj_x", dx1, p["w_out"], "nt")
    g["w_out"] = mm("d_out_proj_w", sv["merged"], dx1, "tn")
    merge_rows = [(sv["ha"], 0, BW), (proj, C_GA, BW), (sv["yb"], 0, BW), (sv["yc"], 0, BW),
                  (proj, 0, D), (proj, D, D), (proj, 2 * D, D)]
    (dha, dga, dyb, dyc, dg0, dg1, dg2), dmp = rowop_bwd("merge_bwd", _merge_f, merge_rows, p["wb"] + p["bg"],
                                                          [(dmerged, 0, D)], [True] * 7)
    g["w_branch"] = jnp.stack(dmp[0:3])
    g["b_gate"] = jnp.concatenate(dmp[3:6], axis=1)[0]
    (dys, duc), (gd, ggw, ggb) = rowop_bwd("s5_post_bwd", _s5_post_f, [(sv["ys"], 0, BW), (proj, C_UC, BW)],
                                           [p["c_d"], p["c_glu_w"], p["c_glu_b"]], [(dyc, 0, BW)], [True, True])
    g["c_d"], g["c_glu_w"], g["c_glu_b"] = gd[0], ggw, ggb[0]
    cd = jnp.concatenate([p["c_cre"], p["c_cimn"]], axis=0)
    s5b = dict(nblk=BW // LANES, bk=LANES, bn=P_S5 * LANES // BW)
    s5c = dict(nblk=BW // LANES, bk=P_S5 * LANES // BW, bn=LANES)
    dhs = mm_bd("d_s5_y_x", dys, cd, "nt", **s5c)
    dcre = mm_bd("d_s5_y_wre", sv["hr"], dys, "tn", **s5c)
    dcimn = mm_bd("d_s5_y_wim", sv["hi"], dys, "tn", **s5c)
    gr, gi = cscan("s5_scan_bwd", dhs, sv["ar"], sv["ai"], True, nb=nb, s=s)
    dar, dai = s5_dlam(gr, gi, sv["hr"], sv["hi"], nb=nb, s=s)
    duc = mm_bd("d_s5_bu_x1", gi, sv["bd"], "nt", b_col0=P_S5,
                res=mm_bd("d_s5_bu_x0", gr, sv["bd"], "nt", res=duc, **s5b), **s5b)
    dbd_re = mm_bd("d_s5_bu_wre", proj, gr, "tn", a_col0=C_UC, **s5b)
    dbd_im = mm_bd("d_s5_bu_wim", proj, gi, "tn", a_col0=C_UC, **s5b)
    gpb = NG * LANES // BW
    eye = jnp.eye(gpb, dtype=F32)
    pick = lambda dmat, r, c: jnp.einsum("jgrhc,gh->jgrc", dmat.reshape(BW // LANES, gpb, r, gpb, c),
                                         eye).reshape(NG, r, c)
    glr, gli, gldt, gbtr, gbti = s5_param_bwd(
        p["c_lr"], p["c_li"], p["c_ldt"], p["c_btr"], p["c_bti"],
        (dar.reshape(NG, 1, NP_S5), dai.reshape(NG, 1, NP_S5), pick(dbd_re, CG, NP_S5), pick(dbd_im, CG, NP_S5)))
    g["c_lam_re"], g["c_lam_im"], g["c_log_dt"] = glr[:, 0, :], gli[:, 0, :], gldt[:, 0, 0]
    g["c_b_re"], g["c_b_im"] = gbtr.transpose(0, 2, 1), gbti.transpose(0, 2, 1)
    g["c_c_re"] = pick(dcre, NP_S5, CG).transpose(0, 2, 1)
    g["c_c_im"] = -pick(dcimn, NP_S5, CG).transpose(0, 2, 1)
    (dq_b, dk_b, dv_b, dz_b, dba), dbp = deltanet_bwd(proj, (C_Q, C_K, C_V, C_Z, C_BA), p["b_taps"], p["b_alog"],
                                                      p["b_dtb"], p["b_nw"], sv["states"], dyb, nb=nb, s=s, nh=NH_B)
    g["b_conv_w"] = jnp.concatenate([jnp.concatenate(dbp[4 * j:4 * j + 4], axis=0) for j in range(3)], axis=1)
    g["b_a_log"] = dbp[12].reshape(NH_B, LANES)[:, 0]
    g["b_dt_bias"] = dbp[13].reshape(NH_B, LANES)[:, 0]
    g["b_norm"] = dbp[14][0]
    gsc = rscan("rglru_scan_bwd", sv["a"], dha, True, nb=nb, s=s)
    (dxa,), dap = seqop_bwd("rglru_pre_bwd", _rglru_pre_f, [(proj, C_XA)], _a_params(p), [(gsc, 0), (sv["ha"], 0)],
                            lambda t0, ge, he: (ge[SUBLANES:] * taps(he, 2)[1], ge[SUBLANES:]),
                            nb=nb, s=s, nblk=BW // LANES, n=128)
    g["a_conv_w"] = jnp.concatenate(dap[0:4], axis=0)
    g["a_conv_b"], g["a_b_r"], g["a_b_i"], g["a_lam"] = dap[4][0], dap[7][0], dap[8][0], dap[9][0]
    g["a_w_r"], g["a_w_i"] = _unblockdiag2(dap[5]), _unblockdiag2(dap[6])
    dproj = jnp.concatenate([dg0, dg1, dg2, dxa, dga, dq_b, dk_b, dv_b, dz_b, duc, dba], axis=1).astype(BF16)
    dh = mm("d_in_proj_x", dproj, p["w_in_pad"], "nt")
    dwp = mm("d_in_proj_w", sv["h"], dproj, "tn")
    dba_w = dwp[:, C_BA:].reshape(D, NH_B, LANES)
    pieces = [dwp[:, c0:c0 + BW] for c0 in (C_XA, C_GA, C_Q, C_K, C_V, C_Z)]
    g["w_in"] = jnp.concatenate(pieces + [dba_w[:, :, 0], dba_w[:, :, 1], dwp[:, C_UC:C_UC + BW], dwp[:, :3 * D]],
                                axis=1)
    (dx,), (gn,) = rowop_bwd("norm_mix_bwd", _norm_f, [(sv["x"], 0, D)], [p["mix_norm"]], [(dh, 0, D)], [True],
                             add=(dx1, 0, D))
    g["mix_norm"] = gn[0]
    return dx, g


def _local_step(x, mem, target, w):
    nb, s, _ = x.shape
    m = mem.shape[1]
    depth = w["mix_norm"].shape[0]
    x2d, mem2d = x.reshape(nb * s, D), mem.reshape(nb * m, D)
    stacked = jax.vmap(_layer_params)({n: a for n, a in w.items() if n != "final_norm"})
    params, saved = [], []
    for l in range(depth):
        params.append(jax.tree.map(lambda a: a[l], stacked))
        x2d, sv = _layer_fwd(x2d, mem2d, params[l], nb, s, m)
        saved.append(sv)
    loss_row, dx, gfn = loss_head(x2d, target.reshape(nb * s, D), w["final_norm"].reshape(1, D))
    grads = [None] * depth
    for l in reversed(range(depth)):
        dx, grads[l] = _layer_bwd(dx, mem2d, saved[l], params[l], nb, s, m)
    grads.append({"final_norm": gfn[0]})
    return loss_row, dx.reshape(x.shape), grads


def _stack_layers(grads, name):
    if name == "final_norm":
        return grads[-1][name]
    return jnp.stack([g[name] for g in grads[:-1]])


SLAB_UNIT = 2 * 1024 * LANES


def _slab(parts, dtype, unit):
    flat = jnp.concatenate([a.astype(dtype).reshape(-1) for a in parts])
    pad = (-flat.shape[0]) % unit
    return jnp.pad(flat, (0, pad)) if pad else flat


def _unslab(flat, like):
    out, off = [], 0
    for a in like:
        out.append(flat[off:off + a.size].reshape(a.shape))
        off += a.size
    return out


def _gather_weights(name, local, names_axes, dtype):
    mine = [local[n].astype(dtype) for n, _ in names_axes]
    me = _chip_id(lax.axis_index("x"), lax.axis_index("y"))
    got = [_set_slot(g, me, a) for g, a in zip(gather_chips(name, mine), mine)]
    return {n: jnp.concatenate([g[j] for j in range(N_CHIPS)], axis=ax) for (n, ax), g in zip(names_axes, got)}


def kernel(x, mem, mix_norm, w_in, b_gate, a_conv_w, a_conv_b, a_w_r, a_b_r, a_w_i, a_b_i, a_lam, b_conv_w, b_a_log, b_dt_bias, b_norm, c_lam_re, c_lam_im, c_log_dt, c_b_re, c_b_im, c_c_re, c_c_im, c_d, c_glu_w, c_glu_b, w_branch, w_out, xa_norm, mem_norm, xa_w_q, xa_w_kv, xa_w_o, ffn_norm, ffn_w_up, ffn_conv_w, ffn_conv_b, ffn_w_down, final_norm, loss_target, m_mix_norm, m_w_in, m_b_gate, m_a_conv_w, m_a_conv_b, m_a_w_r, m_a_b_r, m_a_w_i, m_a_b_i, m_a_lam, m_b_conv_w, m_b_a_log, m_b_dt_bias, m_b_norm, m_c_lam_re, m_c_lam_im, m_c_log_dt, m_c_b_re, m_c_b_im, m_c_c_re, m_c_c_im, m_c_d, m_c_glu_w, m_c_glu_b, m_w_branch, m_w_out, m_xa_norm, m_mem_norm, m_xa_w_q, m_xa_w_kv, m_xa_w_o, m_ffn_norm, m_ffn_w_up, m_ffn_conv_w, m_ffn_conv_b, m_ffn_w_down, m_final_norm, v_mix_norm, v_w_in, v_b_gate, v_a_conv_w, v_a_conv_b, v_a_w_r, v_a_b_r, v_a_w_i, v_a_b_i, v_a_lam, v_b_conv_w, v_b_a_log, v_b_dt_bias, v_b_norm, v_c_lam_re, v_c_lam_im, v_c_log_dt, v_c_b_re, v_c_b_im, v_c_c_re, v_c_c_im, v_c_d, v_c_glu_w, v_c_glu_b, v_w_branch, v_w_out, v_xa_norm, v_mem_norm, v_xa_w_q, v_xa_w_kv, v_xa_w_o, v_ffn_norm, v_ffn_w_up, v_ffn_conv_w, v_ffn_conv_b, v_ffn_w_down, v_final_norm):
    env = dict(locals())
    w = {n: env[n] for n in WEIGHTS}
    mom = {n: env["m_" + n] for n in WEIGHTS}
    var = {n: env["v_" + n] for n in WEIGHTS}

    full = dict(w)
    full.update(_gather_weights("gather_mats", w, SHARDED_MATS, BF16))
    full.update(_gather_weights("gather_convs", w, SHARDED_CONVS, F32))
    loss_row, grad_x, glayers = _local_step(x, mem, loss_target, full)

    depth = mix_norm.shape[0]
    assert depth % 2 == 0
    hl = depth // 2
    res = {}
    by_chip = []
    for n, ax in SHARDED:
        width = w[n].shape[ax]
        by_chip.append(jnp.stack([jnp.stack([jnp.stack([
            lax.slice_in_dim(glayers[l][n], j * width, (j + 1) * width, axis=ax - 1)
            for l in range(hf * hl, (hf + 1) * hl)]) for j in range(N_CHIPS)]) for hf in range(2)]))
    core = lax.axis_index("c")
    chip = _chip_id(lax.axis_index("x"), lax.axis_index("y"))
    sibs = swap_halves("swap_halves", by_chip)
    part = [add_pair("add_cores_" + n, g2, sb, core) for (n, _), g2, sb in zip(SHARDED, by_chip, sibs)]
    recv = scatter_chips("scatter_chips", [p16 for _, p16 in part])
    mine = [add_chips("add_chips_" + n, p32, r, chip) for (n, _), (p32, _), r in zip(SHARDED, part, recv)]
    for (n, _), own, joined in zip(SHARDED, mine, join_halves("join_halves", mine)):
        g = _set_slot(joined, core * hl, own)
        res["grad_" + n] = g
        res["delta_" + n], res["new_m_" + n], res["new_v_" + n] = adamw("adamw_" + n, w[n], g, mom[n], var[n])

    unit = 1024 * LANES
    small = _slab([_stack_layers(glayers, n) for n in REPLICATED] + [loss_row[:, :1]], F32, unit).reshape(-1, LANES)
    g_rp = sumk("add_devices", _set_slot(gather_all("gather_all", small), 2 * chip + core, small))
    slab1 = lambda d: _slab([d[n] for n in REPLICATED] + [jnp.zeros((1, 1), F32)], F32, unit).reshape(-1, LANES)
    d_rp, m_rp, v_rp = adamw("adamw_replicated", slab1(w), g_rp, slab1(mom), slab1(var))
    like_rp = [w[n] for n in REPLICATED] + [jnp.zeros((1, 1), F32)]
    for kind, rp in (("grad", g_rp), ("delta", d_rp), ("new_m", m_rp), ("new_v", v_rp)):
        for n, a in zip(list(REPLICATED) + ["loss"], _unslab(rp.reshape(-1), like_rp)):
            res[kind + "_" + n] = a
    loss = res["grad_loss"].reshape(())
    return (loss, grad_x, *[res["grad_" + n] for n in WEIGHTS], *[res["delta_" + n] for n in WEIGHTS],
            *[res["new_m_" + n] for n in WEIGHTS], *[res["new_v_" + n] for n in WEIGHTS])
```

```python
import functools
import math

import jax
import jax.numpy as jnp
from jax import lax
from jax.experimental import pallas as pl
from jax.experimental.pallas import tpu as pltpu

F32, BF16 = jnp.float32, jnp.bfloat16
MESH = pl.DeviceIdType.MESH
ANY = pl.BlockSpec(memory_space=pl.ANY)

VMEM_LIMIT_V7X = 56 * 1024 * 1024
MM_TILE = 1024
LANES, SUBLANES = 128, 8
EPS = 1e-6
RG_C = 8.0
DN_CHUNK = 64
N_CHIPS = 4
ADAM_LR, ADAM_B1, ADAM_B2, ADAM_EPS, ADAM_WD, ADAM_STEP = 0.001, 0.9, 0.999, 1e-08, 0.01, 10

SHARDED_MATS = (("w_in", 2), ("c_glu_w", 1), ("w_branch", 3), ("w_out", 1), ("xa_w_q", 1), ("xa_w_kv", 2),
                ("xa_w_o", 1), ("ffn_w_up", 2), ("ffn_w_down", 1))
SHARDED_CONVS = (("a_conv_w", 2), ("b_conv_w", 2), ("ffn_conv_w", 2))
SHARDED = SHARDED_MATS + SHARDED_CONVS
WEIGHTS = ("mix_norm", "w_in", "b_gate", "a_conv_w", "a_conv_b", "a_w_r", "a_b_r", "a_w_i", "a_b_i", "a_lam",
           "b_conv_w", "b_a_log", "b_dt_bias", "b_norm", "c_lam_re", "c_lam_im", "c_log_dt", "c_b_re", "c_b_im",
           "c_c_re", "c_c_im", "c_d", "c_glu_w", "c_glu_b", "w_branch", "w_out", "xa_norm", "mem_norm", "xa_w_q",
           "xa_w_kv", "xa_w_o", "ffn_norm", "ffn_w_up", "ffn_conv_w", "ffn_conv_b", "ffn_w_down", "final_norm")
REPLICATED = tuple(n for n in WEIGHTS if n not in dict(SHARDED))


def _cparams(n_axes):
    return pltpu.CompilerParams(dimension_semantics=("arbitrary",) * n_axes, vmem_limit_bytes=VMEM_LIMIT_V7X)


def _tile(n, pref, off=0):
    t = pref
    while t >= LANES:
        if n % t == 0 and off % t == 0:
            return t
        t //= 2
    assert off == 0, (n, pref, off)
    return n


def _full_spec(shape, n_grid):
    nd = len(shape)
    if n_grid == 1:
        return pl.BlockSpec(shape, lambda i: (0,) * nd)
    return pl.BlockSpec(shape, lambda i, j: (0,) * nd)


_NN, _NT, _TN = ((1,), (0,)), ((1,), (1,)), ((0,), (0,))


def _dot(a, b, dims, hp):
    if hp:
        return lax.dot_general(a, b, (dims, ((), ())), precision=lax.Precision.HIGH, preferred_element_type=F32)
    return lax.dot_general(a.astype(BF16), b.astype(BF16), (dims, ((), ())), preferred_element_type=F32)


@functools.partial(jax.custom_vjp, nondiff_argnums=(2,))
def mm_nn(a, b, hp=False):
    return _dot(a, b, _NN, hp)


@functools.partial(jax.custom_vjp, nondiff_argnums=(2,))
def mm_nt(a, b, hp=False):
    return _dot(a, b, _NT, hp)


@functools.partial(jax.custom_vjp, nondiff_argnums=(2,))
def mm_tn(a, b, hp=False):
    return _dot(a, b, _TN, hp)


mm_nn.defvjp(lambda a, b, hp: (_dot(a, b, _NN, hp), (a, b)),
             lambda hp, r, g: (mm_nt(g, r[1], hp), mm_tn(r[0], g, hp)))
mm_nt.defvjp(lambda a, b, hp: (_dot(a, b, _NT, hp), (a, b)),
             lambda hp, r, g: (mm_nn(g, r[1], hp), mm_tn(g, r[0], hp)))
mm_tn.defvjp(lambda a, b, hp: (_dot(a, b, _TN, hp), (a, b)),
             lambda hp, r, g: (mm_nt(r[1], g, hp), mm_nn(r[0], g, hp)))


@functools.partial(jax.custom_vjp, nondiff_argnums=(1,))
def taps(xext, k):
    return tuple((xext if s == 0 else pltpu.roll(xext, s, 0))[SUBLANES:] for s in range(k))


def _taps_fwd(xext, k):
    return taps(xext, k), None


def _taps_bwd(k, _, gs):
    tot = None
    for s, g in enumerate(gs):
        gp = jnp.concatenate([jnp.zeros((SUBLANES, g.shape[1]), g.dtype), g], axis=0)
        if s:
            gp = pltpu.roll(gp, gp.shape[0] - s, 0)
        tot = gp if tot is None else tot + gp
    return (tot,)


taps.defvjp(_taps_fwd, _taps_bwd)


def _gelu(x):
    return x * (0.5 * (1.0 + jnp.tanh(0.7978845608028654 * (x + 0.044715 * (x * x * x)))))


def _sigmoid(x):
    return jax.nn.sigmoid(x)


def _softplus(x):
    return jnp.maximum(x, 0.0) + jnp.log1p(jnp.exp(-jnp.abs(x)))


def _rmsnorm(x, g):
    return x * lax.rsqrt(jnp.mean(x * x, axis=-1, keepdims=True) + EPS) * g


def _lane_pick(x, lane):
    sel = lax.broadcasted_iota(jnp.int32, x.shape, 1) == lane
    return jnp.sum(jnp.where(sel, x, 0.0), axis=1, keepdims=True)


def _load_ext(ref, t0, n):
    main = ref[pl.ds(t0, n), :].astype(F32)
    hstart = pl.multiple_of(jnp.maximum(t0 - SUBLANES, 0), SUBLANES)
    halo = ref[pl.ds(hstart, SUBLANES), :].astype(F32)
    halo = jnp.where(t0 > 0, halo, 0.0)
    return jnp.concatenate([halo, main], axis=0)


def mm(name, a, b, mode, *, out_dtype=F32, res=None, a_cols=None, b_cols=None):
    a0, aw = a_cols if a_cols else (0, a.shape[1])
    b0, bw = b_cols if b_cols else (0, b.shape[1])
    big = MM_TILE
    if mode == "nn":
        m, k, n = a.shape[0], aw, bw
        assert b.shape[0] == k
        tm, tk, tn = _tile(m, big), _tile(k, big, a0), _tile(n, big, b0)
        a_spec = pl.BlockSpec((tm, tk), lambda i, j, kk: (i, kk + a0 // tk))
        b_spec = pl.BlockSpec((tk, tn), lambda i, j, kk: (kk, j + b0 // tn))
        dims = _NN
    elif mode == "nt":
        m, k, n = a.shape[0], aw, b.shape[0]
        assert bw == k
        tm, tn = _tile(m, big), _tile(n, big)
        tk = _tile(k, big, math.gcd(a0, b0) if (a0 or b0) else 0)
        assert a0 % tk == 0 and b0 % tk == 0
        a_spec = pl.BlockSpec((tm, tk), lambda i, j, kk: (i, kk + a0 // tk))
        b_spec = pl.BlockSpec((tn, tk), lambda i, j, kk: (j, kk + b0 // tk))
        dims = _NT
    else:
        k, m, n = a.shape[0], aw, bw
        assert b.shape[0] == k
        tk, tm, tn = _tile(k, big), _tile(m, big, a0), _tile(n, big, b0)
        a_spec = pl.BlockSpec((tk, tm), lambda i, j, kk: (kk, i + a0 // tm))
        b_spec = pl.BlockSpec((tk, tn), lambda i, j, kk: (kk, j + b0 // tn))
        dims = _TN
    nk = k // tk
    has_res = res is not None

    def tile_dot(a_ref, b_ref):
        return lax.dot_general(a_ref[...].astype(BF16), b_ref[...].astype(BF16), (dims, ((), ())),
                               preferred_element_type=F32)

    def finish(out, refs, o_ref):
        if has_res:
            out = out + refs[2][...].astype(F32)
        o_ref[...] = out.astype(o_ref.dtype)

    def body_single(*refs):
        finish(tile_dot(refs[0], refs[1]), refs, refs[-1])

    def body_acc(*refs):
        o_ref, acc_ref = refs[-2], refs[-1]
        kk = pl.program_id(2)

        @pl.when(kk == 0)
        def _():
            acc_ref[...] = tile_dot(refs[0], refs[1])

        @pl.when((kk > 0) & (kk < nk - 1))
        def _():
            acc_ref[...] += tile_dot(refs[0], refs[1])

        @pl.when(kk == nk - 1)
        def _():
            finish(acc_ref[...] + tile_dot(refs[0], refs[1]), refs, o_ref)

    o_spec = pl.BlockSpec((tm, tn), lambda i, j, kk: (i, j))
    return pl.pallas_call(
        body_single if nk == 1 else body_acc, name=name, grid=(m // tm, n // tn, nk),
        in_specs=[a_spec, b_spec] + ([o_spec] if has_res else []), out_specs=o_spec,
        out_shape=jax.ShapeDtypeStruct((m, n), out_dtype),
        scratch_shapes=[] if nk == 1 else [pltpu.VMEM((tm, tn), F32)], compiler_params=_cparams(3),
    )(*((a, b, res) if has_res else (a, b)))


def mm_bd(name, a, b, mode, *, nblk, bk, bn, res=None, a_col0=0, b_col0=0):
    m = a.shape[0]
    tm = _tile(m, MM_TILE)
    if mode == "tn":
        tk = tm
        nk = m // tk

        def body(a_ref, b_ref, o_ref, acc_ref):
            kk = pl.program_id(1)
            part = lax.dot_general(a_ref[...].astype(BF16), b_ref[...].astype(BF16), (_TN, ((), ())),
                                   preferred_element_type=F32)

            @pl.when(kk == 0)
            def _():
                acc_ref[...] = part

            @pl.when(kk > 0)
            def _():
                acc_ref[...] += part

            @pl.when(kk == nk - 1)
            def _():
                o_ref[...] = acc_ref[...]

        return pl.pallas_call(
            body, name=name, grid=(nblk, nk),
            in_specs=[pl.BlockSpec((tk, bk), lambda i, kk: (kk, i + a_col0 // bk)),
                      pl.BlockSpec((tk, bn), lambda i, kk: (kk, i + b_col0 // bn))],
            out_specs=pl.BlockSpec((bk, bn), lambda i, kk: (i, 0)),
            out_shape=jax.ShapeDtypeStruct((nblk * bk, bn), F32),
            scratch_shapes=[pltpu.VMEM((bk, bn), F32)], compiler_params=_cparams(2),
        )(a, b)

    dims = _NN if mode == "nn" else _NT
    if mode == "nn":
        n = b.shape[1]
        a_spec = pl.BlockSpec((tm, bk), lambda i, j: (i, a_col0 // bk + j % nblk))
        b_spec = pl.BlockSpec((bk, bn), lambda i, j: (j % nblk, j))
    else:
        n = b.shape[0]
        bk, bn = bn, bk
        a_spec = pl.BlockSpec((tm, bk), lambda i, j: (i, j % nblk))
        b_spec = pl.BlockSpec((bn, bk), lambda i, j: (j, b_col0 // bk + j % nblk))
    has_res = res is not None

    def body(*refs):
        out = lax.dot_general(refs[0][...].astype(BF16), refs[1][...].astype(BF16), (dims, ((), ())),
                              preferred_element_type=F32)
        if has_res:
            out = out + refs[2][...]
        refs[-1][...] = out

    o_spec = pl.BlockSpec((tm, bn), lambda i, j: (i, j))
    return pl.pallas_call(
        body, name=name, grid=(m // tm, n // bn), in_specs=[a_spec, b_spec] + ([o_spec] if has_res else []),
        out_specs=o_spec, out_shape=jax.ShapeDtypeStruct((m, n), F32), compiler_params=_cparams(2),
    )(*((a, b, res) if has_res else (a, b)))


def _row_specs(rows, tm):
    return [pl.BlockSpec((tm, cw), lambda i, c=c0 // cw: (i, c)) for (_, c0, cw) in rows]


def rowop_fwd(name, f, rows, params, outs, tm=256):
    t = rows[0][0].shape[0]
    tm = min(tm, t)
    n_r, n_p = len(rows), len(params)

    def body(*refs):
        vals = f(*[r[...].astype(F32) for r in refs[:n_r]], *[p[...] for p in refs[n_r:n_r + n_p]])
        for o, v in zip(refs[n_r + n_p:], vals):
            o[...] = v.astype(o.dtype)

    res = pl.pallas_call(
        body, name=name, grid=(t // tm,),
        in_specs=_row_specs(rows, tm) + [_full_spec(p.shape, 1) for p in params],
        out_specs=[pl.BlockSpec((tm, cw), lambda i: (i, 0)) for cw, _ in outs],
        out_shape=[jax.ShapeDtypeStruct((t, cw), dt) for cw, dt in outs],
        compiler_params=_cparams(1),
    )(*[r[0] for r in rows], *params)
    return list(res)


def rowop_bwd(name, f, rows, params, couts, need, tm=256, add=None):
    t = rows[0][0].shape[0]
    tm = min(tm, t)
    n_r, n_p, n_c = len(rows), len(params), len(couts)
    want = [k for k in range(n_r) if need[k]]
    adds = [add] if add is not None else []

    def body(*refs):
        xs = [r[...].astype(F32) for r in refs[:n_r]]
        ps = [p[...] for p in refs[n_r:n_r + n_p]]
        cs = tuple(c[...].astype(F32) for c in refs[n_r + n_p:n_r + n_p + n_c])
        outs = refs[n_r + n_p + n_c + len(adds):]
        _, vjp = jax.vjp(f, *xs, *ps)
        gs = vjp(cs)
        for o, k in zip(outs[:len(want)], want):
            o[...] = gs[k] + refs[n_r + n_p + n_c][...] if (adds and k == want[0]) else gs[k]

        @pl.when(pl.program_id(0) == 0)
        def _():
            for o in outs[len(want):]:
                o[...] = jnp.zeros_like(o)

        for o, g in zip(outs[len(want):], gs[n_r:]):
            o[...] += g

    res = pl.pallas_call(
        body, name=name, grid=(t // tm,),
        in_specs=_row_specs(rows, tm) + [_full_spec(p.shape, 1) for p in params] + _row_specs(couts, tm)
        + _row_specs(adds, tm),
        out_specs=[pl.BlockSpec((tm, rows[k][2]), lambda i: (i, 0)) for k in want]
        + [_full_spec(p.shape, 1) for p in params],
        out_shape=[jax.ShapeDtypeStruct((t, rows[k][2]), F32) for k in want]
        + [jax.ShapeDtypeStruct(p.shape, F32) for p in params],
        compiler_params=_cparams(1),
    )(*[r[0] for r in rows], *params, *[c[0] for c in couts], *[a[0] for a in adds])
    return list(res[:len(want)]), list(res[len(want):])


def _seq_specs(seqs, s, cb, order):
    if order == "bj":
        return [pl.BlockSpec((s, cb), lambda b, j, c=c0 // cb: (b, c + j)) for (_, c0) in seqs]
    return [pl.BlockSpec((s, cb), lambda j, b, c=c0 // cb: (b, c + j)) for (_, c0) in seqs]


def _param_specs(params, order):
    if order == "bj":
        return [pl.BlockSpec(bs, lambda b, j, fn=fn: fn(j)) for (_, bs, fn) in params]
    return [pl.BlockSpec(bs, lambda j, b, fn=fn: fn(j)) for (_, bs, fn) in params]


def seqop_fwd(name, f, seqs, params, out_dtypes, *, nb, s, nblk, cb=LANES, n=256):
    n = min(n, s)
    n_s, n_p = len(seqs), len(params)

    def body(*refs):
        seq_refs, par_refs, out_refs = refs[:n_s], refs[n_s:n_s + n_p], refs[n_s + n_p:]

        def step(i, carry):
            t0 = pl.multiple_of(i * n, n)
            vals = f(t0, *[_load_ext(r, t0, n) for r in seq_refs], *[p[...] for p in par_refs])
            for o, v in zip(out_refs, vals):
                o[pl.ds(t0, n), :] = v.astype(o.dtype)
            return carry

        lax.fori_loop(0, s // n, step, 0)

    res = pl.pallas_call(
        body, name=name, grid=(nb, nblk),
        in_specs=_seq_specs(seqs, s, cb, "bj") + _param_specs(params, "bj"),
        out_specs=[pl.BlockSpec((s, cb), lambda b, j: (b, j)) for _ in out_dtypes],
        out_shape=[jax.ShapeDtypeStruct((nb * s, nblk * cb), dt) for dt in out_dtypes],
        compiler_params=_cparams(2),
    )(*[q[0] for q in seqs], *[p[0] for p in params])
    return list(res)


def seqop_bwd(name, f, seqs, params, cot_seqs, cot_fn, *, nb, s, nblk, cb=LANES, n=256):
    n = min(n, s)
    n_s, n_p, n_c = len(seqs), len(params), len(cot_seqs)
    nchunk = s // n

    def body(*refs):
        seq_refs, par_refs = refs[:n_s], refs[n_s:n_s + n_p]
        cot_refs = refs[n_s + n_p:n_s + n_p + n_c]
        dseq_refs = refs[n_s + n_p + n_c:n_s + n_p + n_c + n_s]
        dpar_refs = refs[n_s + n_p + n_c + n_s:]

        @pl.when(pl.program_id(1) == 0)
        def _():
            for o in dpar_refs:
                o[...] = jnp.zeros_like(o)

        def step(ii, halos):
            t0 = pl.multiple_of((nchunk - 1 - ii) * n, n)
            exts = [_load_ext(r, t0, n) for r in seq_refs]
            ps = [p[...] for p in par_refs]
            cots = cot_fn(t0, *[_load_ext(r, t0, n) for r in cot_refs])
            _, vjp = jax.vjp(lambda *args: f(t0, *args), *exts, *ps)
            gs = vjp(tuple(cots))
            tail = pl.multiple_of(t0 + n - SUBLANES, SUBLANES)
            new_halos = []
            for o, g, h in zip(dseq_refs, gs[:n_s], halos):
                o[pl.ds(t0, n), :] = g[SUBLANES:]
                o[pl.ds(tail, SUBLANES), :] += h
                new_halos.append(g[:SUBLANES])
            for o, g in zip(dpar_refs, gs[n_s:]):
                o[...] += g
            return tuple(new_halos)

        lax.fori_loop(0, nchunk, step, tuple(jnp.zeros((SUBLANES, cb), F32) for _ in range(n_s)))

    res = pl.pallas_call(
        body, name=name, grid=(nblk, nb),
        in_specs=_seq_specs(seqs, s, cb, "jb") + _param_specs(params, "jb") + _seq_specs(cot_seqs, s, cb, "jb"),
        out_specs=[pl.BlockSpec((s, cb), lambda j, b: (b, j)) for _ in seqs] + _param_specs(params, "jb"),
        out_shape=[jax.ShapeDtypeStruct((nb * s, nblk * cb), F32) for _ in seqs]
        + [jax.ShapeDtypeStruct(p[0].shape, F32) for p in params],
        compiler_params=_cparams(2),
    )(*[q[0] for q in seqs], *[p[0] for p in params], *[q[0] for q in cot_seqs])
    return list(res[:n_s]), list(res[n_s:])


def rscan(name, a, b, reverse, *, nb, s, cb=512, seg=2048):
    c = a.shape[1]
    cb = min(cb, c)
    seg = min(seg, s)
    nseg = s // seg

    def sg_of(g):
        return (nseg - 1 - g) if reverse else g

    def body(a_ref, b_ref, h_ref, car_scr):
        @pl.when(pl.program_id(2) == 0)
        def _():
            car_scr[...] = jnp.zeros_like(car_scr)

        def step(ii, car):
            i = (seg // SUBLANES - 1 - ii) if reverse else ii
            base = pl.multiple_of(i * SUBLANES, SUBLANES)
            for r in (reversed(range(SUBLANES)) if reverse else range(SUBLANES)):
                row = pl.ds(base + r, 1)
                if reverse:
                    g = b_ref[row, :] + car
                    h_ref[row, :] = g
                    car = a_ref[row, :] * g
                else:
                    car = a_ref[row, :] * car + b_ref[row, :]
                    h_ref[row, :] = car
            return car

        car_scr[...] = lax.fori_loop(0, seg // SUBLANES, step, car_scr[...])

    spec = pl.BlockSpec((seg, cb), lambda bb, j, g: (bb * nseg + sg_of(g), j))
    return pl.pallas_call(
        body, name=name, grid=(nb, c // cb, nseg), in_specs=[spec, spec], out_specs=spec,
        out_shape=jax.ShapeDtypeStruct(a.shape, F32),
        scratch_shapes=[pltpu.VMEM((1, cb), F32)], compiler_params=_cparams(3),
    )(a, b)


def cscan(name, bu, ar, ai, reverse, *, nb, s, cb=512, seg=2048):
    p = ar.shape[1]
    cb = min(cb, p)
    nj = p // cb
    seg = min(seg, s)
    nseg = s // seg

    def sg_of(g):
        return (nseg - 1 - g) if reverse else g

    def body(br_ref, bi_ref, ar_ref, ai_ref, hr_ref, hi_ref, cr_scr, ci_scr):
        lr = ar_ref[...]
        li = -ai_ref[...] if reverse else ai_ref[...]

        @pl.when(pl.program_id(2) == 0)
        def _():
            cr_scr[...] = jnp.zeros_like(cr_scr)
            ci_scr[...] = jnp.zeros_like(ci_scr)

        def step(ii, car):
            hr, hi = car
            i = (seg // SUBLANES - 1 - ii) if reverse else ii
            base = pl.multiple_of(i * SUBLANES, SUBLANES)
            for r in (reversed(range(SUBLANES)) if reverse else range(SUBLANES)):
                row = pl.ds(base + r, 1)
                nr = lr * hr - li * hi + br_ref[row, :]
                ni = lr * hi + li * hr + bi_ref[row, :]
                hr, hi = nr, ni
                hr_ref[row, :] = hr
                hi_ref[row, :] = hi
            return hr, hi

        hr, hi = lax.fori_loop(0, seg // SUBLANES, step, (cr_scr[...], ci_scr[...]))
        cr_scr[...] = hr
        ci_scr[...] = hi

    o_spec = pl.BlockSpec((seg, cb), lambda bb, j, g: (bb * nseg + sg_of(g), j))
    l_spec = pl.BlockSpec((1, cb), lambda bb, j, g: (0, j))
    res = pl.pallas_call(
        body, name=name, grid=(nb, nj, nseg),
        in_specs=[o_spec, pl.BlockSpec((seg, cb), lambda bb, j, g: (bb * nseg + sg_of(g), j + nj)), l_spec, l_spec],
        out_specs=[o_spec, o_spec],
        out_shape=[jax.ShapeDtypeStruct((nb * s, p), F32)] * 2,
        scratch_shapes=[pltpu.VMEM((1, cb), F32), pltpu.VMEM((1, cb), F32)], compiler_params=_cparams(3),
    )(bu, bu, ar, ai)
    return res[0], res[1]


def _norm_f(x, g):
    return (_rmsnorm(x, g),)


def _rglru_pre_f(t0, xext, w0, w1, w2, w3, cb, wr, wi, br, bi, lam):
    x0, x1, x2, x3 = taps(xext, 4)
    xc = w3 * x0 + w2 * x1 + w1 * x2 + w0 * x3 + cb
    r = _sigmoid(mm_nn(xc, wr[0]) + br)
    ig = _sigmoid(mm_nn(xc, wi[0]) + bi)
    log_a = -RG_C * r * _softplus(-lam)
    a = jnp.exp(log_a)
    first = (lax.broadcasted_iota(jnp.int32, a.shape, 0) + t0) == 0
    mult = jnp.where(first, 1.0, jnp.sqrt(1.0 - jnp.exp(2.0 * log_a)))
    return a, mult * ig * xc


def _ffn_act_f(t0, gext, vext, g0, g1, g2, v0, v1, v2, bg, bv):
    ga, gb, gc = taps(gext, 3)
    va, vb, vc = taps(vext, 3)
    ug = g2 * ga + g1 * gb + g0 * gc + bg
    uv = v2 * va + v1 * vb + v0 * vc + bv
    return (_gelu(ug) * uv,)


def _s5_post_f(ys, uc, d, gw, gb):
    yc = _gelu(ys + d * uc)
    return (yc * _sigmoid(mm_nn(yc, gw) + gb),)


def _merge_f(ha, ga, yb, yc, g0, g1, g2, wb0, wb1, wb2, bg0, bg1, bg2):
    ya = ha * _gelu(ga)
    out = _sigmoid(g0 + bg0) * mm_nn(ya, wb0)
    out = out + _sigmoid(g1 + bg1) * mm_nn(yb, wb1)
    out = out + _sigmoid(g2 + bg2) * mm_nn(yc, wb2)
    return (out,)


def _s5_param_f(lr, li, ldt, btr, bti):
    dt = jnp.exp(ldt)
    mag = jnp.exp(lr * dt)
    ar, ai = mag * jnp.cos(li * dt), mag * jnp.sin(li * dt)
    den = lr * lr + li * li
    fr = ((ar - 1.0) * lr + ai * li) / den
    fi = (ai * lr - (ar - 1.0) * li) / den
    return ar, ai, fr * btr - fi * bti, fr * bti + fi * btr


def _each(fn, *lists):
    return [fn(*xs) for xs in zip(*lists)]


@jax.custom_vjp
def _tri_inv(*mats):
    return tuple(_tri_inv_products(list(mats)))


def _tri_inv_fwd(*mats):
    ts = tuple(_tri_inv_products(list(mats)))
    return ts, ts


def _tri_inv_bwd(ts, gs):
    inner = [mm_nt(g, t, True) for g, t in zip(gs, ts)]
    return tuple(-mm_tn(t, x, True) for t, x in zip(ts, inner))


_tri_inv.defvjp(_tri_inv_fwd, _tri_inv_bwd)


def _tri_inv_products(mats):
    c = mats[0].shape[0]
    ri = lax.broadcasted_iota(jnp.int32, (c, c), 0)
    ci = lax.broadcasted_iota(jnp.int32, (c, c), 1)
    eye = jnp.where(ri == ci, 1.0, 0.0).astype(F32)
    hp = lambda a, b: mm_nn(a, b, True)
    ad = [jnp.where((ri >> 4) == (ci >> 4), a, 0.0) for a in mats]
    ao = _each(lambda a, d: a - d, mats, ad)
    a2 = _each(hp, ad, ad)
    a4 = _each(hp, a2, a2)
    a8 = _each(hp, a4, a4)
    lo = _each(lambda d, s: hp(eye - d, eye + s), ad, a2)
    hi = _each(lambda s, e: hp(eye + s, eye + e), a4, a8)
    td = _each(hp, lo, hi)
    nn_ = _each(hp, td, ao)
    n2 = _each(hp, nn_, nn_)
    return _each(hp, _each(lambda n, s: hp(eye - n, eye + s), nn_, n2), td)


def _dn_chunk_f(nh, *args):
    per = [args[20 * h:20 * (h + 1)] for h in range(nh)]
    nw = args[20 * nh]
    state, qe, ke, ve, z, ba = ([p[j] for p in per] for j in range(6))
    c = DN_CHUNK
    dk = qe[0].shape[1]

    def conv_silu(xes, first):
        tp = [taps(x, 4) for x in xes]
        u = [p[first + 3] * t[0] + p[first + 2] * t[1] + p[first + 1] * t[2] + p[first] * t[3] for p, t in zip(per, tp)]
        return [x * _sigmoid(x) for x in u]

    def l2n(xs):
        return [x * lax.rsqrt(jnp.sum(x * x, axis=-1, keepdims=True) + EPS) for x in xs]

    q = [x * (dk ** -0.5) for x in l2n(conv_silu(qe, 6))]
    k = l2n(conv_silu(ke, 10))
    v = conv_silu(ve, 14)
    beta = [_sigmoid(_lane_pick(b, 0)) for b in ba]
    g = [-jnp.exp(_lane_pick(p[18], 0)) * _softplus(_lane_pick(b, 1) + _lane_pick(p[19], 0)) for p, b in zip(per, ba)]
    ri = lax.broadcasted_iota(jnp.int32, (c, c), 0)
    ci = lax.broadcasted_iota(jnp.int32, (c, c), 1)
    incl, strict = ri >= ci, ri > ci
    ltri = jnp.where(incl, 1.0, 0.0).astype(F32)
    mean_row = jnp.full((c, dk), 1.0 / dk, F32)
    gc_w = [mm_nn(ltri, jnp.broadcast_to(x, (c, dk)), True) for x in g]
    gc_c = [mm_nn(ltri, jnp.broadcast_to(x, (c, c)), True) for x in g]
    gc_r = [mm_nt(mean_row, x, True) for x in gc_w]
    gtot = [jnp.sum(x, axis=0, keepdims=True) for x in g]
    decay = _each(lambda a, b: jnp.exp(jnp.where(incl, a - b, -1e30)), gc_c, gc_r)
    e_gc = [jnp.exp(x) for x in gc_w]
    kb = _each(lambda a, b: a * b, k, beta)
    a_mat = _each(lambda a, b, d: jnp.where(strict, mm_nt(a, b) * d, 0.0), kb, k, decay)
    t_inv = _tri_inv(*a_mat)
    u = _each(lambda t, a, b: mm_nn(t, a * b, True), t_inv, v, beta)
    w = _each(lambda t, a, e: mm_nn(t, a * e, True), t_inv, kb, e_gc)
    qk = _each(lambda a, b, d: jnp.where(incl, mm_nt(a, b) * d, 0.0), q, k, decay)
    v_new = _each(lambda a, b, s: a - mm_nn(b, s), u, w, state)
    o = _each(lambda a, e, s, b, n: mm_nn(a * e, s) + mm_nn(b, n), q, e_gc, state, qk, v_new)
    new_state = _each(lambda s, t, a, gw, n: s * jnp.exp(t) + mm_tn(a * jnp.exp(t - gw), n), state, gtot, k, gc_w, v_new)
    o = _each(lambda x, zz: _rmsnorm(x, nw) * (zz * _sigmoid(zz)), o, z)
    return tuple(o) + tuple(new_state)


def _attn_f(q, k, v):
    sc = mm_nt(q, k) * (q.shape[1] ** -0.5)
    sc = sc - jnp.max(sc, axis=-1, keepdims=True)
    e = jnp.exp(sc)
    p = e / jnp.sum(e, axis=-1, keepdims=True)
    return (mm_nn(p, v),)


DN_SEG = 512


def _dn_ext(ref, edge, t0, i, h):
    lanes = pl.ds(h * LANES, LANES)
    hstart = pl.multiple_of(jnp.maximum(t0 - SUBLANES, 0), SUBLANES)
    halo = jnp.where(i > 0, ref[pl.ds(hstart, SUBLANES), lanes], edge[:, h * LANES:(h + 1) * LANES])
    return jnp.concatenate([halo, ref[pl.ds(t0, DN_CHUNK), lanes]], axis=0)


def _dn_layout(cols, nb, s, nh, reverse):
    seg = min(DN_SEG, s)
    nseg, w = s // seg, nh * LANES
    hb = seg // SUBLANES

    def sg_of(g):
        return (nseg - 1 - g) if reverse else g

    main = [pl.BlockSpec((seg, w), lambda b, g, c=c0 // w: (b * nseg + sg_of(g), c)) for c0 in cols]
    edge = [pl.BlockSpec((SUBLANES, w), lambda b, g, c=c0 // w: (jnp.maximum((b * nseg + sg_of(g)) * hb - 1, 0), c))
            for c0 in cols[:3]]
    par = [pl.BlockSpec((1, w), lambda b, g: (0, 0))] * 14 + [pl.BlockSpec((1, LANES), lambda b, g: (0, 0))]
    seq = pl.BlockSpec((seg, w), lambda b, g: (b * nseg + sg_of(g), 0))
    st = pl.BlockSpec((1, nh, seg // DN_CHUNK, LANES, LANES), lambda b, g: (b, 0, sg_of(g), 0, 0))
    return seg, nseg, main, edge, par, seq, st


def deltanet_fwd(proj, cols, taps12, alog, dtb, nw, *, nb, s, nh):
    seg, nseg, main, edge, par, seq, st = _dn_layout(cols, nb, s, nh, False)
    ncs = seg // DN_CHUNK

    def body(*refs):
        q_ref, k_ref, v_ref, z_ref, ba_ref = refs[:5]
        edges = [jnp.where(pl.program_id(1) > 0, r[...], 0.0) for r in refs[5:8]]
        pars = [p[...] for p in refs[8:23]]
        o_ref, st_ref, state_scr = refs[23], refs[24], refs[25]

        @pl.when(pl.program_id(1) == 0)
        def _():
            state_scr[...] = jnp.zeros_like(state_scr)

        def step(i, carry):
            t0 = pl.multiple_of(i * DN_CHUNK, DN_CHUNK)
            rows = pl.ds(t0, DN_CHUNK)
            states = [state_scr[h] for h in range(nh)]
            args = []
            for h in range(nh):
                lanes = pl.ds(h * LANES, LANES)
                args += ([states[h]] + [_dn_ext(r, e, t0, i, h) for r, e in zip((q_ref, k_ref, v_ref), edges)]
                         + [z_ref[rows, lanes], ba_ref[rows, lanes]]
                         + [p[:, h * LANES:(h + 1) * LANES] for p in pars[:14]])
            outs = _dn_chunk_f(nh, *args, pars[14])
            for h in range(nh):
                st_ref[0, h, i] = states[h]
                o_ref[rows, pl.ds(h * LANES, LANES)] = outs[h]
                state_scr[h] = outs[nh + h]
            return carry

        lax.fori_loop(0, ncs, step, 0)

    res = pl.pallas_call(
        body, name="deltanet_fwd", grid=(nb, nseg),
        in_specs=main + edge + par, out_specs=[seq, st],
        out_shape=[jax.ShapeDtypeStruct((nb * s, nh * LANES), F32),
                   jax.ShapeDtypeStruct((nb, nh, s // DN_CHUNK, LANES, LANES), F32)],
        scratch_shapes=[pltpu.VMEM((nh, LANES, LANES), F32)], compiler_params=_cparams(2),
    )(proj, proj, proj, proj, proj, proj, proj, proj, *taps12, alog, dtb, nw)
    return res[0], res[1]


def deltanet_bwd(proj, cols, taps12, alog, dtb, nw, states, dyb, *, nb, s, nh):
    seg, nseg, main, edge, par, seq, st = _dn_layout(cols, nb, s, nh, True)
    ncs = seg // DN_CHUNK

    def body(*refs):
        q_ref, k_ref, v_ref, z_ref, ba_ref = refs[:5]
        first_in_time = pl.program_id(1) == nseg - 1
        edges = [jnp.where(first_in_time, 0.0, r[...]) for r in refs[5:8]]
        pars = [p[...] for p in refs[8:23]]
        st_ref, do_ref = refs[23], refs[24]
        dseq_refs = refs[25:28]
        dz_ref, dba_ref = refs[28], refs[29]
        dpar_refs = refs[30:45]
        dstate_scr, dhalo_scr = refs[45], refs[46]

        @pl.when((pl.program_id(0) == 0) & (pl.program_id(1) == 0))
        def _():
            for o in dpar_refs:
                o[...] = jnp.zeros_like(o)

        @pl.when(pl.program_id(1) == 0)
        def _():
            dstate_scr[...] = jnp.zeros_like(dstate_scr)
            dhalo_scr[...] = jnp.zeros_like(dhalo_scr)

        def step(ii, carry):
            i = ncs - 1 - ii
            t0 = pl.multiple_of(i * DN_CHUNK, DN_CHUNK)
            rows = pl.ds(t0, DN_CHUNK)
            d_states = [dstate_scr[h] for h in range(nh)]
            d_halos = [dhalo_scr[k] for k in range(3)]
            args = []
            for h in range(nh):
                lanes = pl.ds(h * LANES, LANES)
                args += ([st_ref[0, h, i]] + [_dn_ext(r, e, t0, i, h) for r, e in zip((q_ref, k_ref, v_ref), edges)]
                         + [z_ref[rows, lanes], ba_ref[rows, lanes]]
                         + [p[:, h * LANES:(h + 1) * LANES] for p in pars[:14]])
            _, vjp = jax.vjp(functools.partial(_dn_chunk_f, nh), *args, pars[14])
            grads = vjp(tuple(do_ref[rows, pl.ds(h * LANES, LANES)] for h in range(nh)) + tuple(d_states))
            dpar_refs[14][...] += grads[20 * nh]
            for h in range(nh):
                lanes = pl.ds(h * LANES, LANES)
                gs = grads[20 * h:20 * (h + 1)]
                dstate_scr[h] = gs[0]
                for k, (o, g) in enumerate(zip(dseq_refs, gs[1:4])):
                    o[rows, lanes] = jnp.concatenate(
                        [g[SUBLANES:DN_CHUNK], g[DN_CHUNK:] + d_halos[k][:, h * LANES:(h + 1) * LANES]], axis=0)
                    dhalo_scr[k, :, lanes] = g[:SUBLANES]
                dz_ref[rows, lanes] = gs[4]
                dba_ref[rows, lanes] = gs[5]
                for o, g in zip(dpar_refs[:14], gs[6:20]):
                    o[:, lanes] += g
            return carry

        lax.fori_loop(0, ncs, step, 0)

    w = nh * LANES
    res = pl.pallas_call(
        body, name="deltanet_bwd", grid=(nb, nseg),
        in_specs=main + edge + par + [st, seq], out_specs=[seq] * 5 + par,
        out_shape=[jax.ShapeDtypeStruct((nb * s, w), F32)] * 5
        + [jax.ShapeDtypeStruct((1, w), F32)] * 14 + [jax.ShapeDtypeStruct((1, LANES), F32)],
        scratch_shapes=[pltpu.VMEM((nh, LANES, LANES), F32), pltpu.VMEM((3, SUBLANES, w), F32)],
        compiler_params=_cparams(2),
    )(proj, proj, proj, proj, proj, proj, proj, proj, *taps12, alog, dtb, nw, states, dyb)
    return list(res[:5]), list(res[5:])


def attn_fwd(q, kv, *, nb, s, m, nh, tq=512):
    hd = q.shape[1] // nh
    tq = min(tq, s)
    nq = s // tq

    def body(q_ref, k_ref, v_ref, o_ref):
        o_ref[...] = _attn_f(q_ref[...].astype(F32), k_ref[...], v_ref[...])[0].astype(o_ref.dtype)

    return pl.pallas_call(
        body, name="attn_fwd", grid=(nb, nh, nq),
        in_specs=[pl.BlockSpec((tq, hd), lambda b, h, i: (b * nq + i, h)),
                  pl.BlockSpec((m, hd), lambda b, h, i: (b, h)),
                  pl.BlockSpec((m, hd), lambda b, h, i: (b, nh + h))],
        out_specs=pl.BlockSpec((tq, hd), lambda b, h, i: (b * nq + i, h)),
        out_shape=jax.ShapeDtypeStruct(q.shape, BF16), compiler_params=_cparams(3),
    )(q, kv, kv)


def attn_bwd(q, kv, do, *, nb, s, m, nh, tq=512):
    hd = q.shape[1] // nh
    tq = min(tq, s)
    nq = s // tq

    def body(q_ref, k_ref, v_ref, do_ref, dq_ref, dk_ref, dv_ref):
        _, vjp = jax.vjp(_attn_f, q_ref[...].astype(F32), k_ref[...], v_ref[...])
        dq, dk, dv = vjp((do_ref[...].astype(F32),))
        dq_ref[...] = dq

        @pl.when(pl.program_id(2) == 0)
        def _():
            dk_ref[...] = jnp.zeros_like(dk_ref)
            dv_ref[...] = jnp.zeros_like(dv_ref)

        dk_ref[...] += dk
        dv_ref[...] += dv

    q_spec = pl.BlockSpec((tq, hd), lambda b, h, i: (b * nq + i, h))
    m_spec = pl.BlockSpec((m, hd), lambda b, h, i: (b, h))
    res = pl.pallas_call(
        body, name="attn_bwd", grid=(nb, nh, nq),
        in_specs=[q_spec, m_spec, pl.BlockSpec((m, hd), lambda b, h, i: (b, nh + h)), q_spec],
        out_specs=[q_spec, m_spec, m_spec],
        out_shape=[jax.ShapeDtypeStruct(q.shape, F32), jax.ShapeDtypeStruct((nb * m, nh * hd), F32),
                   jax.ShapeDtypeStruct((nb * m, nh * hd), F32)],
        compiler_params=_cparams(3),
    )(q, kv, kv, do)
    return res[0], res[1], res[2]


def loss_head(x, target, gain, tm=256):
    t, d = x.shape
    tm = min(tm, t)

    def body(x_ref, t_ref, g_ref, l_ref, dx_ref, dg_ref):
        tgt = t_ref[...]

        def lf(xv, gv):
            e = _rmsnorm(xv, gv) - tgt
            return 0.5 * jnp.sum(jnp.mean(e * e, axis=-1, keepdims=True), axis=0, keepdims=True)

        lv, vjp = jax.vjp(lf, x_ref[...], g_ref[...])
        dx, dg = vjp(jnp.ones((1, 1), F32))
        dx_ref[...] = dx

        @pl.when(pl.program_id(0) == 0)
        def _():
            l_ref[...] = jnp.zeros_like(l_ref)
            dg_ref[...] = jnp.zeros_like(dg_ref)

        l_ref[...] += jnp.broadcast_to(lv, l_ref.shape)
        dg_ref[...] += dg

    row = pl.BlockSpec((tm, d), lambda i: (i, 0))
    res = pl.pallas_call(
        body, name="loss_head", grid=(t // tm,),
        in_specs=[row, row, _full_spec((1, d), 1)],
        out_specs=[_full_spec((1, LANES), 1), row, _full_spec((1, d), 1)],
        out_shape=[jax.ShapeDtypeStruct((1, LANES), F32), jax.ShapeDtypeStruct((t, d), F32),
                   jax.ShapeDtypeStruct((1, d), F32)],
        compiler_params=_cparams(1),
    )(x, target, gain)
    return res[0], res[1], res[2]


def s5_dlam(gr, gi, hr, hi, *, nb, s, cb=256, n=256):
    p = gr.shape[1]
    cb, n = min(cb, p), min(n, s)

    def body(gr_ref, gi_ref, hr_ref, hi_ref, dar_ref, dai_ref):
        @pl.when(pl.program_id(1) == 0)
        def _():
            dar_ref[...] = jnp.zeros_like(dar_ref)
            dai_ref[...] = jnp.zeros_like(dai_ref)

        def step(i, carry):
            t0 = pl.multiple_of(i * n, n)
            g_r, g_i = gr_ref[pl.ds(t0, n), :], gi_ref[pl.ds(t0, n), :]
            p_r = taps(_load_ext(hr_ref, t0, n), 2)[1]
            p_i = taps(_load_ext(hi_ref, t0, n), 2)[1]
            dar_ref[...] += jnp.sum(g_r * p_r + g_i * p_i, axis=0, keepdims=True)
            dai_ref[...] += jnp.sum(g_i * p_r - g_r * p_i, axis=0, keepdims=True)
            return carry

        lax.fori_loop(0, s // n, step, 0)

    blk = pl.BlockSpec((s, cb), lambda j, b: (b, j))
    acc = pl.BlockSpec((1, cb), lambda j, b: (0, j))
    res = pl.pallas_call(
        body, name="s5_dlam", grid=(p // cb, nb), in_specs=[blk] * 4, out_specs=[acc, acc],
        out_shape=[jax.ShapeDtypeStruct((1, p), F32)] * 2, compiler_params=_cparams(2),
    )(gr, gi, hr, hi)
    return res[0], res[1]


def s5_param_fwd(lr, li, ldt, btr, bti):
    def body(*refs):
        vals = _s5_param_f(*[r[...] for r in refs[:5]])
        for o, v in zip(refs[5:], vals):
            o[...] = v

    return pl.pallas_call(
        body, name="s5_param_fwd",
        out_shape=[jax.ShapeDtypeStruct(lr.shape, F32)] * 2 + [jax.ShapeDtypeStruct(btr.shape, F32)] * 2,
    )(lr, li, ldt, btr, bti)


def s5_param_bwd(lr, li, ldt, btr, bti, cots):
    def body(*refs):
        _, vjp = jax.vjp(_s5_param_f, *[r[...] for r in refs[:5]])
        gs = vjp(tuple(r[...] for r in refs[5:9]))
        for o, g in zip(refs[9:], gs):
            o[...] = g

    return pl.pallas_call(
        body, name="s5_param_bwd",
        out_shape=[jax.ShapeDtypeStruct(a.shape, F32) for a in (lr, li, ldt, btr, bti)],
    )(lr, li, ldt, btr, bti, *cots)


def _row_block(r, c, copies):
    lanes = -(-c // LANES) * LANES
    rb = 2048
    while rb > 2 * SUBLANES and (r % rb or copies * rb * lanes * 4 > 12 * 1024 * 1024):
        rb //= 2
    return rb if r % rb == 0 else r


def _as2d(a, lead=0):
    return a.reshape(a.shape[:lead] + (-1, a.shape[-1]))


def sumk(name, buf):
    b3 = _as2d(buf, 1)
    k, r, c = b3.shape
    rb = _row_block(r, c, 2 * (k + 1))

    def body(b_ref, o_ref):
        acc = b_ref[0]
        for i in range(1, k):
            acc = acc + b_ref[i]
        o_ref[...] = acc

    out = pl.pallas_call(
        body, name=name, grid=(r // rb,),
        in_specs=[pl.BlockSpec((k, rb, c), lambda i: (0, i, 0))],
        out_specs=pl.BlockSpec((rb, c), lambda i: (i, 0)),
        out_shape=jax.ShapeDtypeStruct((r, c), F32), compiler_params=_cparams(1),
    )(b3)
    return out.reshape(buf.shape[1:])


def adamw(name, w, g, m, v):
    shape = w.shape
    w, g, m, v = (_as2d(a) for a in (w, g, m, v))
    r, c = w.shape
    rb = _row_block(r, c, 14)

    def body(w_ref, g_ref, m_ref, v_ref, d_ref, nm_ref, nv_ref):
        gv = g_ref[...]
        nm = ADAM_B1 * m_ref[...] + (1.0 - ADAM_B1) * gv
        nv = ADAM_B2 * v_ref[...] + (1.0 - ADAM_B2) * (gv * gv)
        m_hat = nm / (1.0 - ADAM_B1 ** ADAM_STEP)
        v_hat = nv / (1.0 - ADAM_B2 ** ADAM_STEP)
        d_ref[...] = -ADAM_LR * (m_hat / (jnp.sqrt(v_hat) + ADAM_EPS) + ADAM_WD * w_ref[...])
        nm_ref[...] = nm
        nv_ref[...] = nv

    spec = pl.BlockSpec((rb, c), lambda i: (i, 0))
    res = pl.pallas_call(
        body, name=name, grid=(r // rb,), in_specs=[spec] * 4, out_specs=[spec] * 3,
        out_shape=[jax.ShapeDtypeStruct(w.shape, F32)] * 3, compiler_params=_cparams(1),
    )(w, g, m, v)
    return tuple(a.reshape(shape) for a in res)


def _place():
    x, y, c = lax.axis_index("x"), lax.axis_index("y"), lax.axis_index("c")
    chips = [(1 - x, y), (x, 1 - y), (1 - x, 1 - y)]
    return x, y, c, chips


def _chip_id(px, py):
    return 2 * px + py


def _rcopy(src, dst, send_sems, recv_sems, k, to):
    return pltpu.make_async_remote_copy(src_ref=src, dst_ref=dst, send_sem=send_sems.at[k], recv_sem=recv_sems.at[k],
                                        device_id=to, device_id_type=MESH)


def _comm_call(name, body, out_shapes, n_sems, args):
    return pl.pallas_call(
        body, name=name, out_shape=out_shapes, in_specs=[ANY] * len(args), out_specs=[ANY] * len(out_shapes),
        scratch_shapes=[pltpu.SemaphoreType.DMA((n_sems,)), pltpu.SemaphoreType.DMA((n_sems,))],
    )(*args)


def _finish(remote):
    for cp in remote:
        cp.wait_send()


def _set_slot(buf, slot, val):
    if val.ndim < buf.ndim:
        val = val[None]
    return lax.dynamic_update_slice(buf, val.astype(buf.dtype), (slot,) + (0,) * (buf.ndim - 1))


def gather_chips(name, arrays):
    n = len(arrays)
    halves = [a.shape[0] // 2 for a in arrays]

    def body(*refs):
        ins, outs = refs[:n], refs[n:2 * n]
        send_sems, recv_sems = refs[2 * n:]
        x, y, c, chips = _place()
        me, sib = _chip_id(x, y), (x, y, 1 - c)
        half = [pl.ds(c * h, h) for h in halves]
        other = [pl.ds((1 - c) * h, h) for h in halves]
        remote = []
        for a in range(n):
            for j, chip in enumerate(chips):
                remote.append(_rcopy(ins[a].at[half[a]], outs[a].at[me, half[a]], send_sems, recv_sems, 6 * a + j,
                                     (*chip, c)))
                remote[-1].start()
        for a in range(n):
            for j, chip in enumerate(chips):
                landed = outs[a].at[_chip_id(*chip), half[a]]
                _rcopy(ins[a].at[half[a]], landed, send_sems, recv_sems, 6 * a + j, sib).wait_recv()
                remote.append(_rcopy(landed, landed, send_sems, recv_sems, 6 * a + 3 + j, sib))
                remote[-1].start()
        for a in range(n):
            for j, chip in enumerate(chips):
                _rcopy(ins[a].at[half[a]], outs[a].at[_chip_id(*chip), other[a]], send_sems, recv_sems, 6 * a + 3 + j,
                       sib).wait_recv()
        _finish(remote)

    shapes = [jax.ShapeDtypeStruct((N_CHIPS,) + a.shape, a.dtype) for a in arrays]
    return _comm_call(name, body, shapes, 6 * n, arrays)


def swap_halves(name, arrays):
    n = len(arrays)

    def body(*refs):
        ins, outs = refs[:n], refs[n:2 * n]
        send_sems, recv_sems = refs[2 * n:]
        x, y, c, _ = _place()
        sib = (x, y, 1 - c)
        remote = []
        for a in range(n):
            for k in range(N_CHIPS):
                remote.append(_rcopy(ins[a].at[1 - c, k], outs[a].at[k], send_sems, recv_sems, N_CHIPS * a + k, sib))
                remote[-1].start()
        for a in range(n):
            for k in range(N_CHIPS):
                _rcopy(ins[a].at[1 - c, k], outs[a].at[k], send_sems, recv_sems, N_CHIPS * a + k, sib).wait_recv()
        _finish(remote)

    shapes = [jax.ShapeDtypeStruct(a.shape[1:], a.dtype) for a in arrays]
    return _comm_call(name, body, shapes, N_CHIPS * n, arrays)


def scatter_chips(name, arrays):
    n = len(arrays)

    def body(*refs):
        ins, outs = refs[:n], refs[n:2 * n]
        send_sems, recv_sems = refs[2 * n:]
        x, y, c, chips = _place()
        remote = []
        for a in range(n):
            for j, chip in enumerate(chips):
                remote.append(_rcopy(ins[a].at[_chip_id(*chip)], outs[a].at[j], send_sems, recv_sems, 3 * a + j,
                                     (*chip, c)))
                remote[-1].start()
        for a in range(n):
            for j, chip in enumerate(chips):
                _rcopy(ins[a].at[0], outs[a].at[j], send_sems, recv_sems, 3 * a + j, (*chip, c)).wait_recv()
        _finish(remote)

    shapes = [jax.ShapeDtypeStruct((3,) + a.shape[1:], a.dtype) for a in arrays]
    return _comm_call(name, body, shapes, 3 * n, arrays)


def join_halves(name, arrays):
    n = len(arrays)
    offs = [sum(a.shape[0] for a in arrays[:i]) for i in range(n)]

    def body(*refs):
        ins, outs = refs[:n], refs[n:2 * n]
        send_sems, recv_sems = refs[2 * n:]
        x, y, c, _ = _place()
        sib = (x, y, 1 - c)
        remote = []
        for a in range(n):
            h = arrays[a].shape[0]
            for k in range(h):
                remote.append(_rcopy(ins[a].at[k], outs[a].at[c * h + k], send_sems, recv_sems, offs[a] + k, sib))
                remote[-1].start()
        for a in range(n):
            h = arrays[a].shape[0]
            for k in range(h):
                _rcopy(ins[a].at[k], outs[a].at[(1 - c) * h + k], send_sems, recv_sems, offs[a] + k, sib).wait_recv()
        _finish(remote)

    shapes = [jax.ShapeDtypeStruct((2 * a.shape[0],) + a.shape[1:], a.dtype) for a in arrays]
    return _comm_call(name, body, shapes, offs[-1] + arrays[-1].shape[0], arrays)


def add_pair(name, mine2, other, c):
    a3, o2 = _as2d(mine2, 1), _as2d(other)
    r, cols = o2.shape
    rb = _row_block(r, cols, 8)

    def body(c_ref, a_ref, o_ref, out_ref, out16_ref):
        tot = a_ref[0] + o_ref[...]
        out_ref[...] = tot
        out16_ref[...] = tot.astype(BF16)

    spec = pl.BlockSpec((rb, cols), lambda i, s: (i, 0))
    out, out16 = pl.pallas_call(
        body, name=name, out_shape=[jax.ShapeDtypeStruct((r, cols), F32), jax.ShapeDtypeStruct((r, cols), BF16)],
        grid_spec=pltpu.PrefetchScalarGridSpec(
            num_scalar_prefetch=1, grid=(r // rb,),
            in_specs=[pl.BlockSpec((1, rb, cols), lambda i, s: (s[0], i, 0)), spec], out_specs=[spec, spec]),
        compiler_params=_cparams(1),
    )(jnp.reshape(c, (1,)).astype(jnp.int32), a3, o2)
    return out.reshape(other.shape), out16.reshape(other.shape)


def add_chips(name, part, recv, me):
    p3, r3 = _as2d(part, 1), _as2d(recv, 1)
    _, r, cols = p3.shape
    rb = _row_block(r, cols, 10)

    def body(me_ref, p_ref, r_ref, out_ref):
        own, acc = p_ref[0], None
        got = [r_ref[j].astype(F32) for j in range(3)]
        for k in range(N_CHIPS):
            rel = jnp.full(own.shape, me_ref[0] ^ k, jnp.int32)
            term = jnp.where(rel == 0, own, jnp.where(rel == 2, got[0], jnp.where(rel == 1, got[1], got[2])))
            acc = term if acc is None else acc + term
        out_ref[...] = acc

    out = pl.pallas_call(
        body, name=name, out_shape=jax.ShapeDtypeStruct((r, cols), F32),
        grid_spec=pltpu.PrefetchScalarGridSpec(
            num_scalar_prefetch=1, grid=(r // rb,),
            in_specs=[pl.BlockSpec((1, rb, cols), lambda i, s: (s[0], i, 0)),
                      pl.BlockSpec((3, rb, cols), lambda i, s: (0, i, 0))],
            out_specs=pl.BlockSpec((rb, cols), lambda i, s: (i, 0))),
        compiler_params=_cparams(1),
    )(jnp.reshape(me, (1,)).astype(jnp.int32), p3, r3)
    return out.reshape(part.shape[1:])


def gather_all(name, slab):
    def body(x_ref, out_ref, send_sems, recv_sems):
        x, y, c, chips = _place()
        me, sib = (x, y, c), (x, y, 1 - c)

        def rows(px, py, pc):
            return out_ref.at[4 * px + 2 * py + pc]

        first = [_rcopy(x_ref, rows(*me), send_sems, recv_sems, 0, sib)]
        first += [_rcopy(x_ref, rows(*me), send_sems, recv_sems, 1 + j, (*chip, c)) for j, chip in enumerate(chips)]
        for cp in first:
            cp.start()
        passed = []
        for j, chip in enumerate(chips):
            landed = rows(*chip, c)
            _rcopy(x_ref, landed, send_sems, recv_sems, 1 + j, sib).wait_recv()
            passed.append(_rcopy(landed, landed, send_sems, recv_sems, 4 + j, sib))
            passed[-1].start()
        _rcopy(x_ref, rows(*sib), send_sems, recv_sems, 0, sib).wait_recv()
        for j, chip in enumerate(chips):
            _rcopy(x_ref, rows(*chip, 1 - c), send_sems, recv_sems, 4 + j, sib).wait_recv()
        _finish(first + passed)

    return _comm_call(name, body, [jax.ShapeDtypeStruct((8,) + slab.shape, slab.dtype)], 7, [slab])[0]


D, BW, NH_B, NG, NP_S5, CG = 1024, 512, 4, 32, 64, 16
P_S5 = NG * NP_S5
C_GATES, C_XA, C_GA, C_Q, C_K, C_V, C_Z, C_UC, C_BA, N_PROJ = 0, 3072, 3584, 4096, 4608, 5120, 5632, 6144, 6656, 7168
_IN_OFFS = (0, 512, 1024, 1536, 2048, 2560, 3072, 3076, 3080, 3592, 6664)


def _blockdiag2(w):
    z = jnp.zeros((w.shape[0] // 2, 2 * w.shape[1], 2 * w.shape[2]), w.dtype)
    return z.at[:, :64, :64].set(w[0::2]).at[:, 64:, 64:].set(w[1::2])


def _unblockdiag2(d):
    return jnp.stack([d[:, :64, :64], d[:, 64:, 64:]], axis=1).reshape(-1, 64, 64)


def _expand(t):
    eye = jnp.eye(t.shape[0], dtype=t.dtype)
    return (t[:, :, None, :] * eye[:, None, :, None]).reshape(t.shape[0] * t.shape[1], -1)


def _layer_params(w):
    p = {}
    wi = w["w_in"]
    xa, ga, q, k, v, z, beta, alpha, uc, gates = [wi[:, a:b] for a, b in zip(_IN_OFFS[:-1], _IN_OFFS[1:])]
    ba = jnp.zeros((D, NH_B, LANES), wi.dtype).at[:, :, 0].set(beta).at[:, :, 1].set(alpha).reshape(D, NH_B * LANES)
    p["w_in_pad"] = jnp.concatenate([gates, xa, ga, q, k, v, z, uc, ba], axis=1)
    row = lambda a: a.reshape(1, -1).astype(F32)
    acw, bcw, fcw = w["a_conv_w"], w["b_conv_w"], w["ffn_conv_w"]
    p["a_taps"] = [row(acw[i]) for i in range(4)]
    p["a_cb"], p["a_br"], p["a_bi"], p["a_lam"] = (row(w[n]) for n in ("a_conv_b", "a_b_r", "a_b_i", "a_lam"))
    p["a_wr"], p["a_wi"] = _blockdiag2(w["a_w_r"]), _blockdiag2(w["a_w_i"])
    p["b_taps"] = [row(bcw[i, j * BW:(j + 1) * BW]) for j in range(3) for i in range(4)]
    p["b_alog"] = row(jnp.repeat(w["b_a_log"], LANES))
    p["b_dtb"] = row(jnp.repeat(w["b_dt_bias"], LANES))
    p["b_nw"] = row(w["b_norm"])
    p["c_lr"], p["c_li"] = w["c_lam_re"][:, None, :], w["c_lam_im"][:, None, :]
    p["c_ldt"] = w["c_log_dt"][:, None, None]
    p["c_btr"], p["c_bti"] = w["c_b_re"].transpose(0, 2, 1), w["c_b_im"].transpose(0, 2, 1)
    p["c_cre"] = _expand(w["c_c_re"].transpose(0, 2, 1))
    p["c_cimn"] = _expand(-w["c_c_im"].transpose(0, 2, 1))
    p["c_d"], p["c_glu_b"] = row(w["c_d"]), row(w["c_glu_b"])
    p["c_glu_w"] = w["c_glu_w"].astype(F32)
    p["wb"] = [w["w_branch"][k].astype(F32) for k in range(3)]
    p["bg"] = [row(w["b_gate"][k * D:(k + 1) * D]) for k in range(3)]
    for n in ("w_out", "xa_w_q", "xa_w_kv", "xa_w_o", "ffn_w_up", "ffn_w_down"):
        p[n] = w[n]
    for n in ("mix_norm", "xa_norm", "mem_norm", "ffn_norm"):
        p[n] = row(w[n])
    dff = fcw.shape[1] // 2
    p["f_taps"] = [row(fcw[i, :dff]) for i in range(3)] + [row(fcw[i, dff:]) for i in range(3)]
    p["f_b"] = [row(w["ffn_conv_b"][:dff]), row(w["ffn_conv_b"][dff:])]
    return p


def _chan(arr, cb=LANES):
    return (arr, (1, cb), lambda j: (0, j))


def _a_params(p):
    blk = lambda a: (a, (1, LANES, LANES), lambda j: (j, 0, 0))
    return [_chan(t) for t in p["a_taps"]] + [_chan(p["a_cb"]), blk(p["a_wr"]), blk(p["a_wi"]),
                                              _chan(p["a_br"]), _chan(p["a_bi"]), _chan(p["a_lam"])]


def _f_params(p):
    return [_chan(t) for t in p["f_taps"]] + [_chan(b) for b in p["f_b"]]


def _layer_fwd(x, mem, p, nb, s, m):
    sv = {"x": x}
    h = rowop_fwd("norm_mix", _norm_f, [(x, 0, D)], [p["mix_norm"]], [(D, BF16)])[0]
    proj = mm("in_proj", h, p["w_in_pad"], "nn")
    a, b = seqop_fwd("rglru_pre", _rglru_pre_f, [(proj, C_XA)], _a_params(p), [F32, F32], nb=nb, s=s, nblk=BW // LANES)
    ha = rscan("rglru_scan", a, b, False, nb=nb, s=s)
    yb, states = deltanet_fwd(proj, (C_Q, C_K, C_V, C_Z, C_BA), p["b_taps"], p["b_alog"], p["b_dtb"], p["b_nw"],
                              nb=nb, s=s, nh=NH_B)
    ar, ai, bbr, bbi = s5_param_fwd(p["c_lr"], p["c_li"], p["c_ldt"], p["c_btr"], p["c_bti"])
    bd = jnp.concatenate([_expand(bbr), _expand(bbi)], axis=1)
    ar2, ai2 = ar.reshape(1, P_S5), ai.reshape(1, P_S5)
    s5b = dict(nblk=BW // LANES, bk=LANES, bn=P_S5 * LANES // BW)
    s5c = dict(nblk=BW // LANES, bk=P_S5 * LANES // BW, bn=LANES)
    bu = mm_bd("s5_bu", proj, bd, "nn", a_col0=C_UC, **s5b)
    hr, hi = cscan("s5_scan", bu, ar2, ai2, False, nb=nb, s=s)
    ys = mm_bd("s5_y_im", hi, p["c_cimn"], "nn", res=mm_bd("s5_y_re", hr, p["c_cre"], "nn", **s5c), **s5c)
    yc = rowop_fwd("s5_post", _s5_post_f, [(ys, 0, BW), (proj, C_UC, BW)], [p["c_d"], p["c_glu_w"], p["c_glu_b"]],
                   [(BW, F32)])[0]
    merge_rows = [(ha, 0, BW), (proj, C_GA, BW), (yb, 0, BW), (yc, 0, BW),
                  (proj, 0, D), (proj, D, D), (proj, 2 * D, D)]
    merged = rowop_fwd("merge", _merge_f, merge_rows, p["wb"] + p["bg"], [(D, BF16)])[0]
    x1 = mm("out_proj", merged, p["w_out"], "nn", res=x)
    hq = rowop_fwd("norm_xa", _norm_f, [(x1, 0, D)], [p["xa_norm"]], [(D, BF16)])[0]
    q = mm("xa_q", hq, p["xa_w_q"], "nn", out_dtype=BF16)
    memn = rowop_fwd("norm_mem", _norm_f, [(mem, 0, D)], [p["mem_norm"]], [(D, BF16)])[0]
    kv = mm("xa_kv", memn, p["xa_w_kv"], "nn")
    o = attn_fwd(q, kv, nb=nb, s=s, m=m, nh=4)
    x2 = mm("xa_o", o, p["xa_w_o"], "nn", res=x1)
    hf = rowop_fwd("norm_ffn", _norm_f, [(x2, 0, D)], [p["ffn_norm"]], [(D, BF16)])[0]
    up = mm("ffn_up", hf, p["ffn_w_up"], "nn")
    dff = up.shape[1] // 2
    act = seqop_fwd("ffn_act", _ffn_act_f, [(up, 0), (up, dff)], _f_params(p), [BF16], nb=nb, s=s,
                    nblk=dff // LANES)[0]
    x3 = mm("ffn_down", act, p["ffn_w_down"], "nn", res=x2)
    sv.update(h=h, proj=proj, a=a, ha=ha, yb=yb, states=states, ar=ar2, ai=ai2, bd=bd, hr=hr, hi=hi, ys=ys, yc=yc,
              merged=merged, x1=x1, hq=hq, q=q, memn=memn, kv=kv, o=o, x2=x2, hf=hf, up=up, act=act)
    return x3, sv


def _layer_bwd(dx3, mem, sv, p, nb, s, m):
    g = {}
    up, dff = sv["up"], sv["up"].shape[1] // 2
    dact = mm("d_ffn_down_x", dx3, p["ffn_w_down"], "nt")
    g["ffn_w_down"] = mm("d_ffn_down_w", sv["act"], dx3, "tn")
    (dug, duv), dfp = seqop_bwd("ffn_act_bwd", _ffn_act_f, [(up, 0), (up, dff)], _f_params(p), [(dact, 0)],
                                lambda t0, e: (e[SUBLANES:],), nb=nb, s=s, nblk=dff // LANES, n=128)
    dhf = mm("d_ffn_up_x1", duv, p["ffn_w_up"], "nt", b_cols=(dff, dff),
             res=mm("d_ffn_up_x0", dug, p["ffn_w_up"], "nt", b_cols=(0, dff)))
    g["ffn_w_up"] = jnp.concatenate([mm("d_ffn_up_w0", sv["hf"], dug, "tn"), mm("d_ffn_up_w1", sv["hf"], duv, "tn")],
                                    axis=1)
    g["ffn_conv_w"] = jnp.concatenate([jnp.concatenate(dfp[0:3], axis=0), jnp.concatenate(dfp[3:6], axis=0)], axis=1)
    g["ffn_conv_b"] = jnp.concatenate([dfp[6], dfp[7]], axis=1)[0]
    (dx2,), (gn,) = rowop_bwd("norm_ffn_bwd", _norm_f, [(sv["x2"], 0, D)], [p["ffn_norm"]], [(dhf, 0, D)], [True],
                              add=(dx3, 0, D))
    g["ffn_norm"] = gn[0]
    do = mm("d_xa_o_x", dx2, p["xa_w_o"], "nt")
    g["xa_w_o"] = mm("d_xa_o_w", sv["o"], dx2, "tn")
    dq, dk, dv = attn_bwd(sv["q"], sv["kv"], do, nb=nb, s=s, m=m, nh=4)
    dkv = jnp.concatenate([dk, dv], axis=1)
    dhq = mm("d_xa_q_x", dq, p["xa_w_q"], "nt")
    g["xa_w_q"] = mm("d_xa_q_w", sv["hq"], dq, "tn")
    dmemn = mm("d_xa_kv_x", dkv, p["xa_w_kv"], "nt")
    g["xa_w_kv"] = mm("d_xa_kv_w", sv["memn"], dkv, "tn")
    (dx1,), (gn,) = rowop_bwd("norm_xa_bwd", _norm_f, [(sv["x1"], 0, D)], [p["xa_norm"]], [(dhq, 0, D)], [True],
                              add=(dx2, 0, D))
    g["xa_norm"] = gn[0]
    _, (gn,) = rowop_bwd("norm_mem_bwd", _norm_f, [(mem, 0, D)], [p["mem_norm"]], [(dmemn, 0, D)], [False])
    g["mem_norm"] = gn[0]
    proj = sv["proj"]
    dmerged = mm("d_out_proj_x", dx1, p["w_out"], "nt")
    g["w_out"] = mm("d_out_proj_w", sv["merged"], dx1, "tn")
    merge_rows = [(sv["ha"], 0, BW), (proj, C_GA, BW), (sv["yb"], 0, BW), (sv["yc"], 0, BW),
                  (proj, 0, D), (proj, D, D), (proj, 2 * D, D)]
    (dha, dga, dyb, dyc, dg0, dg1, dg2), dmp = rowop_bwd("merge_bwd", _merge_f, merge_rows, p["wb"] + p["bg"],
                                                          [(dmerged, 0, D)], [True] * 7)
    g["w_branch"] = jnp.stack(dmp[0:3])
    g["b_gate"] = jnp.concatenate(dmp[3:6], axis=1)[0]
    (dys, duc), (gd, ggw, ggb) = rowop_bwd("s5_post_bwd", _s5_post_f, [(sv["ys"], 0, BW), (proj, C_UC, BW)],
                                           [p["c_d"], p["c_glu_w"], p["c_glu_b"]], [(dyc, 0, BW)], [True, True])
    g["c_d"], g["c_glu_w"], g["c_glu_b"] = gd[0], ggw, ggb[0]
    cd = jnp.concatenate([p["c_cre"], p["c_cimn"]], axis=0)
    s5b = dict(nblk=BW // LANES, bk=LANES, bn=P_S5 * LANES // BW)
    s5c = dict(nblk=BW // LANES, bk=P_S5 * LANES // BW, bn=LANES)
    dhs = mm_bd("d_s5_y_x", dys, cd, "nt", **s5c)
    dcre = mm_bd("d_s5_y_wre", sv["hr"], dys, "tn", **s5c)
    dcimn = mm_bd("d_s5_y_wim", sv["hi"], dys, "tn", **s5c)
    gr, gi = cscan("s5_scan_bwd", dhs, sv["ar"], sv["ai"], True, nb=nb, s=s)
    dar, dai = s5_dlam(gr, gi, sv["hr"], sv["hi"], nb=nb, s=s)
    duc = mm_bd("d_s5_bu_x1", gi, sv["bd"], "nt", b_col0=P_S5,
                res=mm_bd("d_s5_bu_x0", gr, sv["bd"], "nt", res=duc, **s5b), **s5b)
    dbd_re = mm_bd("d_s5_bu_wre", proj, gr, "tn", a_col0=C_UC, **s5b)
    dbd_im = mm_bd("d_s5_bu_wim", proj, gi, "tn", a_col0=C_UC, **s5b)
    gpb = NG * LANES // BW
    eye = jnp.eye(gpb, dtype=F32)
    pick = lambda dmat, r, c: jnp.einsum("jgrhc,gh->jgrc", dmat.reshape(BW // LANES, gpb, r, gpb, c),
                                         eye).reshape(NG, r, c)
    glr, gli, gldt, gbtr, gbti = s5_param_bwd(
        p["c_lr"], p["c_li"], p["c_ldt"], p["c_btr"], p["c_bti"],
        (dar.reshape(NG, 1, NP_S5), dai.reshape(NG, 1, NP_S5), pick(dbd_re, CG, NP_S5), pick(dbd_im, CG, NP_S5)))
    g["c_lam_re"], g["c_lam_im"], g["c_log_dt"] = glr[:, 0, :], gli[:, 0, :], gldt[:, 0, 0]
    g["c_b_re"], g["c_b_im"] = gbtr.transpose(0, 2, 1), gbti.transpose(0, 2, 1)
    g["c_c_re"] = pick(dcre, NP_S5, CG).transpose(0, 2, 1)
    g["c_c_im"] = -pick(dcimn, NP_S5, CG).transpose(0, 2, 1)
    (dq_b, dk_b, dv_b, dz_b, dba), dbp = deltanet_bwd(proj, (C_Q, C_K, C_V, C_Z, C_BA), p["b_taps"], p["b_alog"],
                                                      p["b_dtb"], p["b_nw"], sv["states"], dyb, nb=nb, s=s, nh=NH_B)
    g["b_conv_w"] = jnp.concatenate([jnp.concatenate(dbp[4 * j:4 * j + 4], axis=0) for j in range(3)], axis=1)
    g["b_a_log"] = dbp[12].reshape(NH_B, LANES)[:, 0]
    g["b_dt_bias"] = dbp[13].reshape(NH_B, LANES)[:, 0]
    g["b_norm"] = dbp[14][0]
    gsc = rscan("rglru_scan_bwd", sv["a"], dha, True, nb=nb, s=s)
    (dxa,), dap = seqop_bwd("rglru_pre_bwd", _rglru_pre_f, [(proj, C_XA)], _a_params(p), [(gsc, 0), (sv["ha"], 0)],
                            lambda t0, ge, he: (ge[SUBLANES:] * taps(he, 2)[1], ge[SUBLANES:]),
                            nb=nb, s=s, nblk=BW // LANES, n=128)
    g["a_conv_w"] = jnp.concatenate(dap[0:4], axis=0)
    g["a_conv_b"], g["a_b_r"], g["a_b_i"], g["a_lam"] = dap[4][0], dap[7][0], dap[8][0], dap[9][0]
    g["a_w_r"], g["a_w_i"] = _unblockdiag2(dap[5]), _unblockdiag2(dap[6])
    dproj = jnp.concatenate([dg0, dg1, dg2, dxa, dga, dq_b, dk_b, dv_b, dz_b, duc, dba], axis=1).astype(BF16)
    dh = mm("d_in_proj_x", dproj, p["w_in_pad"], "nt")
    dwp = mm("d_in_proj_w", sv["h"], dproj, "tn")
    dba_w = dwp[:, C_BA:].reshape(D, NH_B, LANES)
    pieces = [dwp[:, c0:c0 + BW] for c0 in (C_XA, C_GA, C_Q, C_K, C_V, C_Z)]
    g["w_in"] = jnp.concatenate(pieces + [dba_w[:, :, 0], dba_w[:, :, 1], dwp[:, C_UC:C_UC + BW], dwp[:, :3 * D]],
                                axis=1)
    (dx,), (gn,) = rowop_bwd("norm_mix_bwd", _norm_f, [(sv["x"], 0, D)], [p["mix_norm"]], [(dh, 0, D)], [True],
                             add=(dx1, 0, D))
    g["mix_norm"] = gn[0]
    return dx, g


def _local_step(x, mem, target, w):
    nb, s, _ = x.shape
    m = mem.shape[1]
    depth = w["mix_norm"].shape[0]
    x2d, mem2d = x.reshape(nb * s, D), mem.reshape(nb * m, D)
    stacked = jax.vmap(_layer_params)({n: a for n, a in w.items() if n != "final_norm"})
    params, saved = [], []
    for l in range(depth):
        params.append(jax.tree.map(lambda a: a[l], stacked))
        x2d, sv = _layer_fwd(x2d, mem2d, params[l], nb, s, m)
        saved.append(sv)
    loss_row, dx, gfn = loss_head(x2d, target.reshape(nb * s, D), w["final_norm"].reshape(1, D))
    grads = [None] * depth
    for l in reversed(range(depth)):
        dx, grads[l] = _layer_bwd(dx, mem2d, saved[l], params[l], nb, s, m)
    grads.append({"final_norm": gfn[0]})
    return loss_row, dx.reshape(x.shape), grads


def _stack_layers(grads, name):
    if name == "final_norm":
        return grads[-1][name]
    return jnp.stack([g[name] for g in grads[:-1]])


SLAB_UNIT = 2 * 1024 * LANES


def _slab(parts, dtype, unit):
    flat = jnp.concatenate([a.astype(dtype).reshape(-1) for a in parts])
    pad = (-flat.shape[0]) % unit
    return jnp.pad(flat, (0, pad)) if pad else flat


def _unslab(flat, like):
    out, off = [], 0
    for a in like:
        out.append(flat[off:off + a.size].reshape(a.shape))
        off += a.size
    return out


def _gather_weights(name, local, names_axes, dtype):
    mine = [local[n].astype(dtype) for n, _ in names_axes]
    me = _chip_id(lax.axis_index("x"), lax.axis_index("y"))
    got = [_set_slot(g, me, a) for g, a in zip(gather_chips(name, mine), mine)]
    return {n: jnp.concatenate([g[j] for j in range(N_CHIPS)], axis=ax) for (n, ax), g in zip(names_axes, got)}


def kernel(x, mem, mix_norm, w_in, b_gate, a_conv_w, a_conv_b, a_w_r, a_b_r, a_w_i, a_b_i, a_lam, b_conv_w, b_a_log, b_dt_bias, b_norm, c_lam_re, c_lam_im, c_log_dt, c_b_re, c_b_im, c_c_re, c_c_im, c_d, c_glu_w, c_glu_b, w_branch, w_out, xa_norm, mem_norm, xa_w_q, xa_w_kv, xa_w_o, ffn_norm, ffn_w_up, ffn_conv_w, ffn_conv_b, ffn_w_down, final_norm, loss_target, m_mix_norm, m_w_in, m_b_gate, m_a_conv_w, m_a_conv_b, m_a_w_r, m_a_b_r, m_a_w_i, m_a_b_i, m_a_lam, m_b_conv_w, m_b_a_log, m_b_dt_bias, m_b_norm, m_c_lam_re, m_c_lam_im, m_c_log_dt, m_c_b_re, m_c_b_im, m_c_c_re, m_c_c_im, m_c_d, m_c_glu_w, m_c_glu_b, m_w_branch, m_w_out, m_xa_norm, m_mem_norm, m_xa_w_q, m_xa_w_kv, m_xa_w_o, m_ffn_norm, m_ffn_w_up, m_ffn_conv_w, m_ffn_conv_b, m_ffn_w_down, m_final_norm, v_mix_norm, v_w_in, v_b_gate, v_a_conv_w, v_a_conv_b, v_a_w_r, v_a_b_r, v_a_w_i, v_a_b_i, v_a_lam, v_b_conv_w, v_b_a_log, v_b_dt_bias, v_b_norm, v_c_lam_re, v_c_lam_im, v_c_log_dt, v_c_b_re, v_c_b_im, v_c_c_re, v_c_c_im, v_c_d, v_c_glu_w, v_c_glu_b, v_w_branch, v_w_out, v_xa_norm, v_mem_norm, v_xa_w_q, v_xa_w_kv, v_xa_w_o, v_ffn_norm, v_ffn_w_up, v_ffn_conv_w, v_ffn_conv_b, v_ffn_w_down, v_final_norm):
    env = dict(locals())
    w = {n: env[n] for n in WEIGHTS}
    mom = {n: env["m_" + n] for n in WEIGHTS}
    var = {n: env["v_" + n] for n in WEIGHTS}

    full = dict(w)
    full.update(_gather_weights("gather_mats", w, SHARDED_MATS, BF16))
    full.update(_gather_weights("gather_convs", w, SHARDED_CONVS, F32))
    loss_row, grad_x, glayers = _local_step(x, mem, loss_target, full)

    depth = mix_norm.shape[0]
    assert depth % 2 == 0
    hl = depth // 2
    res = {}
    by_chip = []
    for n, ax in SHARDED:
        width = w[n].shape[ax]
        by_chip.append(jnp.stack([jnp.stack([jnp.stack([
            lax.slice_in_dim(glayers[l][n], j * width, (j + 1) * width, axis=ax - 1)
            for l in range(hf * hl, (hf + 1) * hl)]) for j in range(N_CHIPS)]) for hf in range(2)]))
    core = lax.axis_index("c")
    chip = _chip_id(lax.axis_index("x"), lax.axis_index("y"))
    sibs = swap_halves("swap_halves", by_chip)
    part = [add_pair("add_cores_" + n, g2, sb, core) for (n, _), g2, sb in zip(SHARDED, by_chip, sibs)]
    recv = scatter_chips("scatter_chips", [p16 for _, p16 in part])
    mine = [add_chips("add_chips_" + n, p32, r, chip) for (n, _), (p32, _), r in zip(SHARDED, part, recv)]
    for (n, _), own, joined in zip(SHARDED, mine, join_halves("join_halves", mine)):
        g = _set_slot(joined, core * hl, own)
        res["grad_" + n] = g
        res["delta_" + n], res["new_m_" + n], res["new_v_" + n] = adamw("adamw_" + n, w[n], g, mom[n], var[n])

    unit = 1024 * LANES
    small = _slab([_stack_layers(glayers, n) for n in REPLICATED] + [loss_row[:, :1]], F32, unit).reshape(-1, LANES)
    g_rp = sumk("add_devices", _set_slot(gather_all("gather_all", small), 2 * chip + core, small))
    slab1 = lambda d: _slab([d[n] for n in REPLICATED] + [jnp.zeros((1, 1), F32)], F32, unit).reshape(-1, LANES)
    d_rp, m_rp, v_rp = adamw("adamw_replicated", slab1(w), g_rp, slab1(mom), slab1(var))
    like_rp = [w[n] for n in REPLICATED] + [jnp.zeros((1, 1), F32)]
    for kind, rp in (("grad", g_rp), ("delta", d_rp), ("new_m", m_rp), ("new_v", v_rp)):
        for n, a in zip(list(REPLICATED) + ["loss"], _unslab(rp.reshape(-1), like_rp)):
            res[kind + "_" + n] = a
    loss = res["grad_loss"].reshape(())
    return (loss, grad_x, *[res["grad_" + n] for n in WEIGHTS], *[res["delta_" + n] for n in WEIGHTS],
            *[res["new_m_" + n] for n in WEIGHTS], *[res["new_v_" + n] for n in WEIGHTS])
```
